```python
import math
import jax
import jax.numpy as jnp
from jax import lax
import numpy as np

D_MODEL = 2048
BATCH = 16
SEQ = 256
DEPTH = 2
DEC_BATCH = 2
DEC_SEQ = 2048
PAST_LEN = 512

GRID_W = 64
EPS = 1e-6

N_HEADS_A = 8
DK_A = 64
DV_A = 2 * DK_A
QK_A = N_HEADS_A * 2 * DK_A
W_A = N_HEADS_A * DV_A
Q_BLOCK = 128
ROPE_BASE = 10000.0
ROPE_AXIS = DK_A // 2

N_HEADS_B = 4
DK_B = 128
DV_B = 128
W_B = N_HEADS_B * DV_B
CONV_K = 3
DELTA_CHUNK = 64

POOL_WINDOWS = (2, 4, 8, 16)
N_POOL_GROUPS = 4
POOL_GROUP_W = 128
W_C = N_POOL_GROUPS * POOL_GROUP_W

N_BRANCH = 3
SPLIT_SIZES = (QK_A, QK_A, W_A, 3 * W_B, W_B, 2 * N_HEADS_B, 2 * N_HEADS_B, W_C, N_BRANCH * D_MODEL)
D_IN = 2 * QK_A + W_A + 4 * W_B + 4 * N_HEADS_B + W_C + N_BRANCH * D_MODEL

N_EXPERTS = 16
N_EXPERT_GROUPS = 4
EXPERTS_PER_GROUP = N_EXPERTS // N_EXPERT_GROUPS
TOP_K = 2
D_FF = 1408
ROW_BLOCK = 128

kernel_name = 'hybrid_diffusion_prefix_trunk_step'


def rmsnorm(x, g):
    xf = x.astype(jnp.float32)
    y = xf * lax.rsqrt(jnp.mean(xf * xf, axis=-1, keepdims=True) + EPS)
    return (y * g.astype(jnp.float32)).astype(x.dtype)


def l2norm(x):
    xf = x.astype(jnp.float32)
    return xf * lax.rsqrt(jnp.sum(xf * xf, axis=-1, keepdims=True) + EPS)


def axial_rope(T):
    rows = T // GRID_W
    r = jnp.repeat(jnp.arange(rows), GRID_W).astype(jnp.float32)
    col = jnp.tile(jnp.arange(GRID_W), rows).astype(jnp.float32)
    inv = ROPE_BASE ** (-jnp.arange(0, ROPE_AXIS, 2, dtype=jnp.float32) / ROPE_AXIS)
    ang = jnp.stack([r[:, None] * inv, col[:, None] * inv], axis=1)
    return jnp.cos(ang), jnp.sin(ang)


def apply_axial_rope(x, cos, sin):
    B, T, H, M, _ = x.shape
    xr = x.astype(jnp.float32).reshape(B, T, H, M, 2, 2, ROPE_AXIS // 2)
    x1, x2 = xr[..., 0, :], xr[..., 1, :]
    c = cos[None, :, None, None]
    s = sin[None, :, None, None]
    out = jnp.stack([x1 * c - x2 * s, x2 * c + x1 * s], axis=-2)
    return out.reshape(x.shape).astype(x.dtype)


def diff_attention(q, k, v, lam):
    B, Tq = q.shape[:2]
    nb = Tq // Q_BLOCK
    qb = jnp.moveaxis(q.reshape((B, nb, Q_BLOCK) + q.shape[2:]), 1, 0)

    def one_block(qblk):
        s = jnp.einsum('bqhmd,bkhmd->bhmqk', qblk, k).astype(jnp.float32) * (DK_A ** -0.5)
        pr = jax.nn.softmax(s, axis=-1)
        pd = pr[:, :, 0] - lam * pr[:, :, 1]
        return jnp.einsum('bhqk,bkhd->bqhd', pd.astype(v.dtype), v)

    o = lax.map(one_block, qb)
    return jnp.moveaxis(o, 0, 1).reshape(B, Tq, q.shape[2], DV_A)


def centred_dwconv(x, w):
    return lax.conv_general_dilated(
        x, w[:, None, :].astype(x.dtype), window_strides=(1,),
        padding=[(CONV_K // 2, CONV_K // 2)],
        dimension_numbers=('NWC', 'WIO', 'NWC'), feature_group_count=x.shape[-1])


def gated_delta_chunked(q, k, v, g, beta, s0):
    B, T, H, dk = q.shape
    dv = v.shape[-1]
    C = DELTA_CHUNK
    n = T // C

    def to_chunks(a):
        a = a.astype(jnp.float32).reshape((B, n, C) + a.shape[2:])
        return jnp.moveaxis(a, 3, 1)

    q = to_chunks(q) * (dk ** -0.5)
    k = to_chunks(k)
    v = to_chunks(v)
    g = jnp.cumsum(to_chunks(g), axis=-1)
    beta = to_chunks(beta)
    causal = jnp.tril(jnp.ones((C, C), bool))
    strict = jnp.tril(jnp.ones((C, C), bool), -1)
    decay = jnp.exp(jnp.where(causal, g[..., :, None] - g[..., None, :], -jnp.inf))
    kb = k * beta[..., None]
    lmat = jnp.where(strict, jnp.einsum('bhnid,bhnjd->bhnij', kb, k) * decay, 0.0)
    eye = jnp.eye(C, dtype=jnp.float32)
    rhs = jnp.concatenate([v * beta[..., None], kb * jnp.exp(g)[..., None]], axis=-1)
    sol = lax.linalg.triangular_solve(eye + lmat, rhs, left_side=True, lower=True,
                                      unit_diagonal=True)
    u, w = sol[..., :dv], sol[..., dv:]
    attn = jnp.einsum('bhnid,bhnjd->bhnij', q, k) * decay
    xs = tuple(jnp.moveaxis(a, 2, 0) for a in (q, k, u, w, g, attn))

    def step(S, inp):
        qi, ki, ui, wi, gi, ai = inp
        v_new = ui - jnp.einsum('bhck,bhkv->bhcv', wi, S)
        o = (jnp.einsum('bhck,bhkv->bhcv', qi * jnp.exp(gi)[..., None], S)
             + jnp.einsum('bhij,bhjv->bhiv', ai, v_new))
        g_last = gi[..., -1:]
        S = (S * jnp.exp(g_last)[..., None]
             + jnp.einsum('bhck,bhcv->bhkv', ki * jnp.exp(g_last - gi)[..., None], v_new))
        return S, o

    s_final, o = lax.scan(step, s0.astype(jnp.float32), xs)
    o = jnp.moveaxis(jnp.moveaxis(o, 0, 2), 1, 3).reshape(B, T, H, dv)
    return o, s_final


def bidir_delta(q, k, v, g, beta, s0):
    o_f, s_f = gated_delta_chunked(q, k, v, g[:, :, 0], beta[:, :, 0], s0[:, 0])
    rev = lambda a: jnp.flip(a, axis=1)
    o_b, s_b = gated_delta_chunked(rev(q), rev(k), rev(v), rev(g[:, :, 1]), rev(beta[:, :, 1]), s0[:, 1])
    return o_f + rev(o_b), jnp.stack([s_f, s_b], axis=1)


def multiscale_pool(x, w_grp, scale):
    B, T, _ = x.shape
    xg = x.astype(jnp.float32).reshape(B, T, N_POOL_GROUPS, POOL_GROUP_W)
    cs = jnp.pad(jnp.cumsum(xg, axis=1), ((0, 0), (1, 0), (0, 0), (0, 0)))
    t = jnp.arange(T)
    means = []
    for gi, win in enumerate(POOL_WINDOWS):
        lo = jnp.clip(t - win // 2, 0, T)
        hi = jnp.clip(t + win // 2, 0, T)
        s = jnp.take(cs[:, :, gi], hi, axis=1) - jnp.take(cs[:, :, gi], lo, axis=1)
        means.append(s / (hi - lo).astype(jnp.float32)[None, :, None])
    pooled = jnp.stack(means, axis=2)
    y = jnp.einsum('btgc,gcd->btgd', pooled - xg, w_grp.astype(jnp.float32))
    return (y.reshape(B, T, W_C) * scale.astype(jnp.float32)).astype(x.dtype)


def route(h, w_router, b_router):
    N = h.shape[0]
    scores = jax.nn.sigmoid(h.astype(jnp.float32) @ w_router.astype(jnp.float32))
    sel = scores + b_router.astype(jnp.float32)
    group_score = lax.top_k(sel.reshape(N, N_EXPERT_GROUPS, EXPERTS_PER_GROUP), 2)[0].sum(-1)
    best = jnp.argmax(group_score, axis=-1)
    expert_mask = jnp.repeat(jnp.arange(N_EXPERT_GROUPS)[None, :] == best[:, None],
                             EXPERTS_PER_GROUP, axis=1)
    _, idx = lax.top_k(jnp.where(expert_mask, sel, -jnp.inf), TOP_K)
    wts = jnp.take_along_axis(scores, idx, axis=-1)
    return idx, wts / jnp.sum(wts, axis=-1, keepdims=True)


def moe_ffn(h, w_router, b_router, w_gate, w_up, w_down):
    N, D = h.shape
    idx, wts = route(h, w_router, b_router)
    NK = N * TOP_K
    flat_e = idx.reshape(NK)
    flat_w = wts.reshape(NK)
    flat_tok = jnp.arange(NK) // TOP_K
    order = jnp.argsort(flat_e)
    e_sorted = flat_e[order]
    counts = jnp.bincount(flat_e, length=N_EXPERTS)
    padded = (counts + ROW_BLOCK - 1) // ROW_BLOCK * ROW_BLOCK
    pad_end = jnp.cumsum(padded)
    pad_start = pad_end - padded
    start = jnp.cumsum(counts) - counts
    dest = pad_start[e_sorted] + jnp.arange(NK) - start[e_sorted]
    nb = -(-(NK + N_EXPERTS * (ROW_BLOCK - 1)) // ROW_BLOCK)
    P = nb * ROW_BLOCK
    row_tok = jnp.full((P,), N, jnp.int32).at[dest].set(flat_tok[order].astype(jnp.int32))
    row_w = jnp.zeros((P,), jnp.float32).at[dest].set(flat_w[order])
    blk_e = jnp.minimum(jnp.searchsorted(pad_end, jnp.arange(nb) * ROW_BLOCK, side='right'),
                        N_EXPERTS - 1)
    hp = jnp.concatenate([h, jnp.zeros((1, D), h.dtype)], axis=0)
    xb = hp[row_tok].reshape(nb, ROW_BLOCK, D)

    def expert_block(args):
        xblk, e = args
        return (jax.nn.silu(xblk @ w_gate[e]) * (xblk @ w_up[e])) @ w_down[e]

    yb = lax.map(expert_block, (xb, blk_e)).reshape(P, D)
    out = jnp.zeros((N + 1, D), h.dtype).at[row_tok].add(yb * row_w[:, None].astype(yb.dtype))
    return out[:N]


def trunk_layer(x, cond, layer, rope, ctx_k, ctx_v, s0, p, w_router, b_router):
    B, T, D = x.shape
    sh1, sc1, gt1, sh2, sc2, gt2 = jnp.split(jax.nn.silu(cond) @ p['w_mod'] + p['b_mod'], 6, axis=-1)
    h = rmsnorm(x, p['norm1']) * (1 + sc1) + sh1
    z = h @ p['w_in']
    q_a, k_a, v_a, qkv_b, gate_b, beta_b, alpha_b, pool_in, branch_g = jnp.split(
        z, np.cumsum(SPLIT_SIZES)[:-1].tolist(), axis=-1)

    q_a = q_a.reshape(B, T, N_HEADS_A, 2, DK_A)
    k_store = k_a.reshape(B, T, N_HEADS_A, 2 * DK_A)
    k_a = k_a.reshape(B, T, N_HEADS_A, 2, DK_A)
    v_a = v_a.reshape(B, T, N_HEADS_A, DV_A)
    if rope is None:
        keys, vals = k_a, v_a
    else:
        cos, sin = rope
        q_a = apply_axial_rope(q_a, cos, sin)
        keys = jnp.concatenate([apply_axial_rope(k_a, cos, sin),
                                ctx_k.reshape(B, -1, N_HEADS_A, 2, DK_A).astype(x.dtype)], axis=1)
        vals = jnp.concatenate([v_a, ctx_v.astype(x.dtype)], axis=1)
    lq = p['lambda_qk'].astype(jnp.float32)
    lam_init = 0.8 - 0.6 * math.exp(-0.3 * layer)
    lam = jnp.exp(jnp.sum(lq[0] * lq[1])) - jnp.exp(jnp.sum(lq[2] * lq[3])) + lam_init
    o_a = diff_attention(q_a, keys, vals, lam)
    o_a = (rmsnorm(o_a, p['subln']) * (1.0 - lam_init)).reshape(B, T, W_A)

    qkv_b = jax.nn.silu(centred_dwconv(qkv_b, p['conv_w']))
    q_b, k_b, v_b = jnp.split(qkv_b, 3, axis=-1)
    q_b = l2norm(q_b.reshape(B, T, N_HEADS_B, DK_B))
    k_b = l2norm(k_b.reshape(B, T, N_HEADS_B, DK_B))
    v_b = v_b.reshape(B, T, N_HEADS_B, DV_B)
    beta = jax.nn.sigmoid(beta_b.astype(jnp.float32)).reshape(B, T, 2, N_HEADS_B)
    g = -jnp.exp(p['a_log'].astype(jnp.float32)) * jax.nn.softplus(
        alpha_b.astype(jnp.float32).reshape(B, T, 2, N_HEADS_B) + p['dt_bias'].astype(jnp.float32))
    o_b, s_new = bidir_delta(q_b, k_b, v_b, g, beta, s0)
    o_b = rmsnorm(o_b, p['delta_norm']) * jax.nn.silu(gate_b.astype(jnp.float32)).reshape(B, T, N_HEADS_B, DV_B)
    o_b = o_b.astype(x.dtype).reshape(B, T, W_B)

    o_c = multiscale_pool(pool_in, p['pool_w'], p['pool_scale'])

    gates = jax.nn.sigmoid(branch_g).reshape(B, T, N_BRANCH, D)
    mixed = (gates[:, :, 0] * (o_a @ p['w_pa']) + gates[:, :, 1] * (o_b @ p['w_pb'])
             + gates[:, :, 2] * (o_c @ p['w_pc']))
    x = x + gt1 * (mixed @ p['w_out'])

    h2 = rmsnorm(x, p['norm2']) * (1 + sc2) + sh2
    f = moe_ffn(h2.reshape(B * T, D), w_router, b_router, p['w_gate'], p['w_up'], p['w_down'])
    x = x + gt2 * f.reshape(B, T, D)
    return x, k_store, v_a, s_new


def setup_inputs(seed: int = 0) -> dict:
    key = jax.random.key(seed)
    ks = iter(jax.random.split(key, 40))
    D = D_MODEL

    def nrm(shape, scale=1.0):
        return jax.random.normal(next(ks), shape, jnp.float32) * scale

    def gain(shape):
        return 1.0 + 0.1 * jax.random.normal(next(ks), shape, jnp.float32)

    a_log = jnp.log(jax.random.uniform(next(ks), (DEPTH, 2, N_HEADS_B), jnp.float32, 1.0, 16.0))
    dt = jnp.exp(jax.random.uniform(next(ks), (DEPTH, 2, N_HEADS_B), jnp.float32,
                                    math.log(1e-3), math.log(1e-1)))
    dt_bias = dt + jnp.log(-jnp.expm1(-dt))
    return {
        'x_prompt': nrm((BATCH, SEQ, D)),
        'x_sample': nrm((DEC_BATCH, DEC_SEQ, D)),
        'cache_k': nrm((DEC_BATCH, DEPTH, PAST_LEN, N_HEADS_A, 2 * DK_A)),
        'cache_v': nrm((DEC_BATCH, DEPTH, PAST_LEN, N_HEADS_A, DV_A)),
        'state_delta': nrm((DEC_BATCH, DEPTH, 2, N_HEADS_B, DK_B, DV_B), 0.1),
        'c': nrm((DEC_BATCH, D)),
        'c_ctx': nrm((D,)),
        'w_mod': nrm((DEPTH, D, 6 * D), 0.5 * D ** -0.5),
        'b_mod': nrm((DEPTH, 6 * D), 0.02),
        'norm1': gain((DEPTH, D)),
        'norm2': gain((DEPTH, D)),
        'w_in': nrm((DEPTH, D, D_IN), D ** -0.5),
        'lambda_qk': nrm((DEPTH, 4, DK_A), 0.1),
        'subln': gain((DEPTH, DV_A)),
        'conv_w': nrm((DEPTH, CONV_K, 3 * W_B), CONV_K ** -0.5),
        'a_log': a_log,
        'dt_bias': dt_bias,
        'delta_norm': gain((DEPTH, DV_B)),
        'pool_w': nrm((DEPTH, N_POOL_GROUPS, POOL_GROUP_W, POOL_GROUP_W), POOL_GROUP_W ** -0.5),
        'pool_scale': gain((DEPTH, W_C)),
        'w_pa': nrm((DEPTH, W_A, D), W_A ** -0.5),
        'w_pb': nrm((DEPTH, W_B, D), W_B ** -0.5),
        'w_pc': nrm((DEPTH, W_C, D), W_C ** -0.5),
        'w_out': nrm((DEPTH, D, D), D ** -0.5),
        'w_router': nrm((D, N_EXPERTS), D ** -0.5),
        'b_router': nrm((N_EXPERTS,), 0.01),
        'w_gate': nrm((DEPTH, N_EXPERTS, D, D_FF), D ** -0.5),
        'w_up': nrm((DEPTH, N_EXPERTS, D, D_FF), D ** -0.5),
        'w_down': nrm((DEPTH, N_EXPERTS, D_FF, D), D_FF ** -0.5),
        'norm_final': gain((D,)),
    }


def reference(x_prompt, x_sample, cache_k, cache_v, state_delta, c, c_ctx, w_mod, b_mod,
              norm1, norm2, w_in, lambda_qk, subln, conv_w, a_log, dt_bias, delta_norm,
              pool_w, pool_scale, w_pa, w_pb, w_pc, w_out, w_router, b_router,
              w_gate, w_up, w_down, norm_final):
    rope = axial_rope(x_sample.shape[1])
    xp, xs = x_prompt, x_sample
    n_prompt = x_prompt.shape[0]
    new_k, new_v, new_s = [], [], []
    for l in range(DEPTH):
        p = {
            'w_mod': w_mod[l], 'b_mod': b_mod[l], 'norm1': norm1[l], 'norm2': norm2[l],
            'w_in': w_in[l], 'lambda_qk': lambda_qk[l], 'subln': subln[l], 'conv_w': conv_w[l],
            'a_log': a_log[l], 'dt_bias': dt_bias[l], 'delta_norm': delta_norm[l],
            'pool_w': pool_w[l], 'pool_scale': pool_scale[l], 'w_pa': w_pa[l], 'w_pb': w_pb[l],
            'w_pc': w_pc[l], 'w_out': w_out[l], 'w_gate': w_gate[l], 'w_up': w_up[l],
            'w_down': w_down[l],
        }
        zero_state = jnp.zeros((n_prompt, 2, N_HEADS_B, DK_B, DV_B), jnp.float32)
        xp, k_l, v_l, s_l = trunk_layer(xp, c_ctx[None, None, :], l, None, None, None,
                                        zero_state, p, w_router, b_router)
        new_k.append(k_l)
        new_v.append(v_l)
        new_s.append(s_l)
        xs = trunk_layer(xs, c[:, None, :], l, rope, cache_k[:, l], cache_v[:, l],
                         state_delta[:, l], p, w_router, b_router)[0]
    y_prompt = rmsnorm(xp, norm_final)
    y_sample = rmsnorm(xs, norm_final)
    new_cache_k = jnp.stack(new_k, axis=1)
    new_cache_v = jnp.stack(new_v, axis=1)
    new_state_delta = jnp.stack(new_s, axis=1)
    return (y_prompt, y_sample, new_cache_k, new_cache_v, new_state_delta)
```

```python
import functools
import math

import jax
import jax.numpy as jnp
from jax import lax
from jax.experimental import pallas as pl
from jax.experimental.pallas import tpu as pltpu

F32 = jnp.float32
BF16 = jnp.bfloat16
HIGHEST = lax.Precision.HIGHEST

D_MODEL = 2048
GRID_W = 64
EPS = 1e-6
N_HEADS_A = 8
DK_A = 64
DV_A = 128
QK_A = N_HEADS_A * 2 * DK_A
W_A = N_HEADS_A * DV_A
ROPE_BASE = 10000.0
ROPE_AXIS = DK_A // 2
N_HEADS_B = 4
DK_B = 128
DV_B = 128
W_B = N_HEADS_B * DV_B
DELTA_CHUNK = 64
POOL_WINDOWS = (2, 4, 8, 16)
POOL_GROUP_W = 128
W_C = len(POOL_WINDOWS) * POOL_GROUP_W
N_BRANCH = 3
N_EXPERTS = 16
N_EXPERT_GROUPS = 4
EXPERTS_PER_GROUP = N_EXPERTS // N_EXPERT_GROUPS
TOP_K = 2
D_FF = 1408

SRC_Q, SRC_K, SRC_V = 0, QK_A, 2 * QK_A
SRC_QKV_B = 2 * QK_A + W_A
SRC_GATE_B = SRC_QKV_B + 3 * W_B
SRC_BA = SRC_GATE_B + W_B
SRC_POOL = SRC_BA + 4 * N_HEADS_B
SRC_BRANCH = SRC_POOL + W_C
D_IN = SRC_BRANCH + N_BRANCH * D_MODEL
LANES = 128
SUBLANES = 8
OFF_BRANCH = 0
OFF_Q = N_BRANCH * D_MODEL
OFF_K = OFF_Q + QK_A
OFF_V = OFF_K + QK_A
OFF_QKV_B = OFF_V + W_A
OFF_GATE_B = OFF_QKV_B + 3 * W_B
OFF_POOL = OFF_GATE_B + W_B
OFF_BA = OFF_POOL + W_C
IN_TILE_N = 1024
NP_IN = -(-(OFF_BA + LANES) // IN_TILE_N) * IN_TILE_N

ROW_TILE = 256
MOE_ROWS = 256
VMEM_LIMIT = 56 * 1024 * 1024


def _cparams(n_axes, vmem=VMEM_LIMIT):
    return pltpu.CompilerParams(dimension_semantics=("arbitrary",) * n_axes, vmem_limit_bytes=vmem)


def _dot(a, b, precision=None):
    return jnp.dot(a, b, preferred_element_type=F32, precision=precision)


def _dot_nt(a, b, precision=None):
    return lax.dot_general(a, b, (((1,), (1,)), ((), ())), preferred_element_type=F32, precision=precision)


def _dot_tn(a, b, precision=None):
    return lax.dot_general(a, b, (((0,), (0,)), ((), ())), preferred_element_type=F32, precision=precision)


def _sigmoid(x):
    return 1.0 / (1.0 + jnp.exp(-x))


def _silu(x):
    return x * _sigmoid(x)


def _rms(x, eps=EPS):
    return x * lax.rsqrt(jnp.mean(x * x, axis=-1, keepdims=True) + eps)


def _pick_tile(cands, *dims):
    for t in cands:
        if all(d % t == 0 for d in dims):
            return t
    raise ValueError(f"no tile in {cands} divides {dims}")


def _mod_kernel(c_ref, w_ref, b_ref, o_ref):
    a = _silu(c_ref[...]).astype(BF16)
    o_ref[...] = _dot(a, w_ref[...].astype(BF16)) + b_ref[...]


def _modulation(cond8, w_mod, b_mod):
    depth, d, n6 = w_mod.shape
    tn = 1024
    return pl.pallas_call(
        _mod_kernel,
        grid=(depth, n6 // tn),
        in_specs=[
            pl.BlockSpec((SUBLANES, d), lambda l, n: (0, 0)),
            pl.BlockSpec((None, d, tn), lambda l, n: (l, 0, n)),
            pl.BlockSpec((None, 1, tn), lambda l, n: (l, 0, n)),
        ],
        out_specs=pl.BlockSpec((None, SUBLANES, tn), lambda l, n: (l, 0, n)),
        out_shape=jax.ShapeDtypeStruct((depth, SUBLANES, n6), F32),
        compiler_params=_cparams(2),
        name="adaln_mod",
    )(cond8, w_mod, b_mod.reshape(depth, 1, n6))


def _in_kernel(x_ref, mod_ref, g_ref, w_ref, z_ref, h_scr):
    @pl.when(pl.program_id(1) == 0)
    def _():
        d = x_ref.shape[1]
        y = _rms(x_ref[...]) * g_ref[...]
        shift = mod_ref[:, 0:d]
        scale = mod_ref[:, d:2 * d]
        h_scr[...] = (y * (1.0 + scale) + shift).astype(BF16)

    z_ref[...] = _dot(h_scr[...], w_ref[...])


def _in_proj(x, mod_l, norm1_l, w_in_p, cond_of_row):
    m, d = x.shape
    npad = w_in_p.shape[1]
    tm, tn = cond_of_row.tile_in, IN_TILE_N
    return pl.pallas_call(
        _in_kernel,
        grid=(m // tm, npad // tn),
        in_specs=[
            pl.BlockSpec((tm, d), lambda i, n: (i, 0)),
            pl.BlockSpec((None, 1, 6 * d), lambda i, n: (cond_of_row(i * tm), 0, 0)),
            pl.BlockSpec((1, d), lambda i, n: (0, 0)),
            pl.BlockSpec((d, tn), lambda i, n: (0, n)),
        ],
        out_specs=pl.BlockSpec((tm, tn), lambda i, n: (i, n)),
        out_shape=jax.ShapeDtypeStruct((m, npad), F32),
        scratch_shapes=[pltpu.VMEM((tm, d), BF16)],
        compiler_params=_cparams(2),
        name="in_proj",
    )(x, mod_l, norm1_l.reshape(1, d), w_in_p)


def _lambda(lq_ref, lam_init):
    lq = lq_ref[...]
    a = jnp.sum(lq[0:1] * lq[1:2], axis=-1, keepdims=True)
    b = jnp.sum(lq[2:3] * lq[3:4], axis=-1, keepdims=True)
    return jnp.exp(a) - jnp.exp(b) + lam_init


def _diff_attn(qb, kb, vb, lam):
    probs = []
    for mp in range(2):
        s = _dot_nt(qb[:, mp * DK_A:(mp + 1) * DK_A], kb[:, mp * DK_A:(mp + 1) * DK_A])
        e = jnp.exp(s - jnp.max(s, axis=-1, keepdims=True))
        probs.append(e / jnp.sum(e, axis=-1, keepdims=True))
    pd = probs[0] - lam * probs[1]
    return _dot(pd.astype(BF16), vb)


def _attn_ctx_kernel(lam_init, lq_ref, sub_ref, q_ref, k_ref, v_ref, o_ref):
    lam = _lambda(lq_ref, lam_init)
    qb = (q_ref[...] * DK_A ** -0.5).astype(BF16)
    o = _diff_attn(qb, k_ref[...].astype(BF16), v_ref[...].astype(BF16), lam)
    o_ref[...] = (_rms(o) * sub_ref[...] * (1.0 - lam_init)).astype(BF16)


def _rope(x, cos, sin_signed):
    lane = lax.broadcasted_iota(jnp.int32, x.shape, 1)
    first_half = (lane % (ROPE_AXIS)) < (ROPE_AXIS // 2)
    partner = jnp.where(first_half, pltpu.roll(x, LANES - ROPE_AXIS // 2, 1), pltpu.roll(x, ROPE_AXIS // 2, 1))
    return x * cos + partner * sin_signed


def _attn_lat_kernel(lam_init, t_lat, lq_ref, sub_ref, q_ref, k_ref, v_ref, ck_ref, cv_ref,
                     cq_ref, sq_ref, ckk_ref, skk_ref, prev_ref, o_ref, k_scr, v_scr):
    del prev_ref

    @pl.when(pl.program_id(2) == 0)
    def _():
        k_scr[0:t_lat, :] = _rope(k_ref[...], ckk_ref[...], skk_ref[...]).astype(BF16)
        k_scr[t_lat:, :] = ck_ref[...].astype(BF16)
        v_scr[0:t_lat, :] = v_ref[...].astype(BF16)
        v_scr[t_lat:, :] = cv_ref[...].astype(BF16)

    lam = _lambda(lq_ref, lam_init)
    q = _rope(q_ref[...], cq_ref[...], sq_ref[...]) * DK_A ** -0.5
    o = _diff_attn(q.astype(BF16), k_scr[...], v_scr[...], lam)
    o_ref[...] = (_rms(o) * sub_ref[...] * (1.0 - lam_init)).astype(BF16)


def _attention(z, lq_l, sub_l, cache_k_l, cache_v_l, rope_tabs, lam_init, dims):
    m = z.shape[0]
    b_ctx, t_ctx, b_lat, t_lat = dims
    n_ctx = b_ctx * t_ctx
    past = cache_k_l.shape[1]
    cq, ck, cv = OFF_Q // LANES, OFF_K // LANES, OFF_V // LANES
    small = [pl.BlockSpec((4, DK_A), lambda *_: (0, 0)), pl.BlockSpec((1, DV_A), lambda *_: (0, 0))]
    sub2 = sub_l.reshape(1, DV_A)

    oa = pl.pallas_call(
        functools.partial(_attn_ctx_kernel, lam_init),
        grid=(b_ctx, N_HEADS_A),
        in_specs=small + [
            pl.BlockSpec((t_ctx, LANES), lambda b, h: (b, cq + h)),
            pl.BlockSpec((t_ctx, LANES), lambda b, h: (b, ck + h)),
            pl.BlockSpec((t_ctx, LANES), lambda b, h: (b, cv + h)),
        ],
        out_specs=pl.BlockSpec((t_ctx, LANES), lambda b, h: (b, h)),
        out_shape=jax.ShapeDtypeStruct((m, W_A), BF16),
        compiler_params=_cparams(2),
        name="attn_ctx",
    )(lq_l, sub2, z, z, z)

    tq = _pick_tile((256, 128), t_lat)
    nq = t_lat // tq
    rb_q = n_ctx // tq
    rb_k = n_ctx // t_lat
    cos_t, sin_t = rope_tabs
    return pl.pallas_call(
        functools.partial(_attn_lat_kernel, lam_init, t_lat),
        grid=(b_lat, N_HEADS_A, nq),
        in_specs=small + [
            pl.BlockSpec((tq, LANES), lambda b, h, i: (rb_q + b * nq + i, cq + h)),
            pl.BlockSpec((t_lat, LANES), lambda b, h, i: (rb_k + b, ck + h)),
            pl.BlockSpec((t_lat, LANES), lambda b, h, i: (rb_k + b, cv + h)),
            pl.BlockSpec((None, past, LANES), lambda b, h, i: (b, 0, h)),
            pl.BlockSpec((None, past, LANES), lambda b, h, i: (b, 0, h)),
            pl.BlockSpec((tq, LANES), lambda b, h, i: (i, 0)),
            pl.BlockSpec((tq, LANES), lambda b, h, i: (i, 0)),
            pl.BlockSpec((t_lat, LANES), lambda b, h, i: (0, 0)),
            pl.BlockSpec((t_lat, LANES), lambda b, h, i: (0, 0)),
            pl.BlockSpec(memory_space=pl.ANY),
        ],
        out_specs=pl.BlockSpec((tq, LANES), lambda b, h, i: (rb_q + b * nq + i, h)),
        out_shape=jax.ShapeDtypeStruct((m, W_A), BF16),
        scratch_shapes=[pltpu.VMEM((t_lat + past, LANES), BF16), pltpu.VMEM((t_lat + past, LANES), BF16)],
        input_output_aliases={11: 0},
        compiler_params=_cparams(3),
        name="attn_lat",
    )(lq_l, sub2, z, z, z, cache_k_l, cache_v_l, cos_t, sin_t, cos_t, sin_t, oa)


def _rope_tables(t_lat):
    rows = t_lat // GRID_W
    r = jnp.repeat(jnp.arange(rows), GRID_W).astype(F32)
    col = jnp.tile(jnp.arange(GRID_W), rows).astype(F32)
    inv = ROPE_BASE ** (-jnp.arange(0, ROPE_AXIS, 2, dtype=F32) / ROPE_AXIS)
    ang_r, ang_c = r[:, None] * inv, col[:, None] * inv
    cos64 = jnp.concatenate([jnp.cos(ang_r)] * 2 + [jnp.cos(ang_c)] * 2, axis=-1)
    sin64 = jnp.concatenate([-jnp.sin(ang_r), jnp.sin(ang_r), -jnp.sin(ang_c), jnp.sin(ang_c)], axis=-1)
    return jnp.tile(cos64, (1, 2)), jnp.tile(sin64, (1, 2))


def _local_kernel(seq, cur_ref, prev_ref, next_ref, pcur_ref, pprev_ref, pnext_ref,
                  cw_ref, pw_ref, ps_ref, q_ref, k_ref, v_ref, oc_ref):
    rows_n = cur_ref.shape[0]
    pos0, t_seq = seq(pl.program_id(0) * rows_n)
    has_prev = pos0 > 0
    has_next = pos0 + rows_n < t_seq
    rows = lax.broadcasted_iota(jnp.int32, (rows_n, 1), 0)

    x = cur_ref[...]
    prev_row = jnp.where(has_prev, prev_ref[SUBLANES - 1:SUBLANES, :], 0.0)
    next_row = jnp.where(has_next, next_ref[0:1, :], 0.0)
    xm1 = jnp.where(rows == 0, prev_row, pltpu.roll(x, 1, 0))
    xp1 = jnp.where(rows == rows_n - 1, next_row, pltpu.roll(x, rows_n - 1, 0))
    y = _silu(cw_ref[0:1, :] * xm1 + cw_ref[1:2, :] * x + cw_ref[2:3, :] * xp1)
    for h in range(N_HEADS_B):
        for j, ref in enumerate((q_ref, k_ref)):
            a = y[:, j * W_B + h * DK_B:j * W_B + (h + 1) * DK_B]
            ref[:, h * DK_B:(h + 1) * DK_B] = a * lax.rsqrt(jnp.sum(a * a, axis=-1, keepdims=True) + EPS)
    v_ref[...] = y[:, 2 * W_B:]

    pz = pcur_ref[...]
    ext = jnp.concatenate([jnp.where(has_prev, pprev_ref[...], 0.0), pz,
                           jnp.where(has_next, pnext_ref[...], 0.0)], axis=0)
    n_ext = rows_n + 2 * SUBLANES
    tpos = pos0 + rows
    for gi, win in enumerate(POOL_WINDOWS):
        sl = slice(gi * POOL_GROUP_W, (gi + 1) * POOL_GROUP_W)
        xg = ext[:, sl]
        acc = xg + pltpu.roll(xg, 1, 0)
        s = 1
        while 2 * s < win:
            acc = pltpu.roll(acc, s, 0) + pltpu.roll(acc, n_ext - s, 0)
            s *= 2
        acc = acc[SUBLANES:SUBLANES + rows_n]
        cnt = (jnp.minimum(tpos + win // 2, t_seq) - jnp.maximum(tpos - win // 2, 0)).astype(F32)
        d = acc / cnt - pz[:, sl]
        yg = _dot(d.astype(BF16), pw_ref[gi].astype(BF16)) * ps_ref[:, sl]
        oc_ref[:, sl] = yg.astype(BF16)


def _local(z, conv_w_l, pool_w_l, pool_scale_l, seq):
    m = z.shape[0]
    r = ROW_TILE
    hb = r // SUBLANES
    last_hb = m // SUBLANES - 1
    c_qkv, c_pool = OFF_QKV_B // (3 * W_B), OFF_POOL // W_C
    prev_map = lambda c: (lambda i: (jnp.maximum(i * hb - 1, 0), c))
    next_map = lambda c: (lambda i: (jnp.minimum((i + 1) * hb, last_hb), c))
    out_b = jax.ShapeDtypeStruct((m, W_B), F32)
    return pl.pallas_call(
        functools.partial(_local_kernel, seq),
        grid=(m // r,),
        in_specs=[
            pl.BlockSpec((r, 3 * W_B), lambda i: (i, c_qkv)),
            pl.BlockSpec((SUBLANES, 3 * W_B), prev_map(c_qkv)),
            pl.BlockSpec((SUBLANES, 3 * W_B), next_map(c_qkv)),
            pl.BlockSpec((r, W_C), lambda i: (i, c_pool)),
            pl.BlockSpec((SUBLANES, W_C), prev_map(c_pool)),
            pl.BlockSpec((SUBLANES, W_C), next_map(c_pool)),
            pl.BlockSpec((3, 3 * W_B), lambda i: (0, 0)),
            pl.BlockSpec((len(POOL_WINDOWS), POOL_GROUP_W, POOL_GROUP_W), lambda i: (0, 0, 0)),
            pl.BlockSpec((1, W_C), lambda i: (0, 0)),
        ],
        out_specs=[pl.BlockSpec((r, W_B), lambda i: (i, 0))] * 3 + [pl.BlockSpec((r, W_C), lambda i: (i, 0))],
        out_shape=[out_b, out_b, out_b, jax.ShapeDtypeStruct((m, W_C), BF16)],
        compiler_params=_cparams(1),
        name="local_conv_pool",
    )(z, z, z, z, z, z, conv_w_l, pool_w_l, pool_scale_l.reshape(1, W_C))


def _delta_direction(rev, q_ref, k_ref, v_ref, ba_ref, bat_ref, al_row, al_col, dt_row, dt_col,
                     s_scr, o_ref):
    c = DELTA_CHUNK
    nh = N_HEADS_B
    d = 1 if rev else 0
    ri = lax.broadcasted_iota(jnp.int32, (c, c), 0)
    ci = lax.broadcasted_iota(jnp.int32, (c, c), 1)
    incl = (ri <= ci) if rev else (ri >= ci)
    strict = (ri < ci) if rev else (ri > ci)
    tri = incl.astype(F32)
    tri_t = ((ri >= ci) if rev else (ri <= ci)).astype(F32)
    eye = (ri == ci).astype(F32)

    ba = ba_ref[...]
    bat = bat_ref[...]
    beta_all = _sigmoid(ba[:, 0:2 * nh])
    x = ba[:, 2 * nh:4 * nh] + dt_row[...]
    g_all = -jnp.exp(al_row[...]) * (jnp.maximum(x, 0.0) + jnp.log(1.0 + jnp.exp(-jnp.abs(x))))
    xt = bat[2 * nh:4 * nh, :] + dt_col[...]
    gt_all = -jnp.exp(al_col[...]) * (jnp.maximum(xt, 0.0) + jnp.log(1.0 + jnp.exp(-jnp.abs(xt))))
    gc_all = _dot(tri, g_all, HIGHEST)
    gct_all = _dot(gt_all, tri_t, HIGHEST)
    last = 0 if rev else c - 1

    for h in range(nh):
        idx = d * nh + h
        sl = slice(h * DK_B, (h + 1) * DK_B)
        qh = q_ref[:, sl] * DK_B ** -0.5
        kh = k_ref[:, sl]
        vh = v_ref[:, sl]
        b_col = beta_all[:, idx:idx + 1]
        gc = gc_all[:, idx:idx + 1]
        gr = gct_all[idx:idx + 1, :]
        decay = jnp.where(incl, jnp.exp(jnp.where(incl, gc - gr, 0.0)), 0.0)
        kb = kh * b_col
        a = jnp.where(strict, -(_dot_nt(kb, kh, HIGHEST) * decay), 0.0)
        p = eye + a
        apow = a
        for _ in range(int(math.log2(c)) - 1):
            apow = _dot(apow, apow, HIGHEST)
            p = p + _dot(p, apow, HIGHEST)
        rhs = jnp.concatenate([vh * b_col, kb * jnp.exp(gc)], axis=-1)
        sol = _dot(p, rhs, HIGHEST)
        u, w = sol[:, :DV_B], sol[:, DV_B:]
        attn = _dot_nt(qh, kh, HIGHEST) * decay
        s = s_scr[idx]
        v_new = u - _dot(w, s, HIGHEST)
        o_ref[:, sl] = _dot(qh * jnp.exp(gc), s, HIGHEST) + _dot(attn, v_new, HIGHEST)
        g_last = gc[last:last + 1, :]
        s_scr[idx] = s * jnp.exp(g_last) + _dot_tn(kh * jnp.exp(g_last - gc), v_new, HIGHEST)


def _delta_kernel(zero_init, has_prev, emit_state, *refs):
    fwd, bwd = refs[0:5], refs[5:10]
    al_row, al_col, dt_row, dt_col = refs[10:14]
    pos = 14
    if not zero_init:
        s0_ref = refs[pos]
        pos += 1
    if has_prev:
        pos += 2
    of_ref, ob_ref = refs[pos:pos + 2]
    pos += 2
    if emit_state:
        sfin_ref = refs[pos]
        pos += 1
    s_scr = refs[pos]
    ci = pl.program_id(1)

    @pl.when(ci == 0)
    def _():
        if zero_init:
            s_scr[...] = jnp.zeros_like(s_scr)
        else:
            s_scr[...] = s0_ref[...]

    _delta_direction(False, *fwd, al_row, al_col, dt_row, dt_col, s_scr, of_ref)
    _delta_direction(True, *bwd, al_row, al_col, dt_row, dt_col, s_scr, ob_ref)

    if emit_state:
        @pl.when(ci == pl.num_programs(1) - 1)
        def _():
            sfin_ref[...] = s_scr[...]


def _delta_call(z, qn, kn, vv, bat, a_log_l, dt_l, s0, prev_of, prev_ob, row0, nb, t_seq, emit_state):
    m = z.shape[0]
    c = DELTA_CHUNK
    n = t_seq // c
    cb0 = row0 // c
    c_ba = OFF_BA // LANES
    nh2 = 2 * N_HEADS_B
    fmap = lambda b, i: cb0 + b * n + i
    bmap = lambda b, i: cb0 + b * n + (n - 1 - i)

    def dir_specs(cm):
        return [
            pl.BlockSpec((c, W_B), lambda b, i: (cm(b, i), 0)),
            pl.BlockSpec((c, W_B), lambda b, i: (cm(b, i), 0)),
            pl.BlockSpec((c, W_B), lambda b, i: (cm(b, i), 0)),
            pl.BlockSpec((c, LANES), lambda b, i: (cm(b, i), c_ba)),
            pl.BlockSpec((None, 2 * nh2, c), lambda b, i: (cm(b, i), 0, 0)),
        ]

    row = pl.BlockSpec((1, nh2), lambda b, i: (0, 0))
    col = pl.BlockSpec((nh2, 1), lambda b, i: (0, 0))
    in_specs = dir_specs(fmap) + dir_specs(bmap) + [row, col, row, col]
    args = [qn, kn, vv, z, bat] * 2 + [a_log_l.reshape(1, nh2), a_log_l.reshape(nh2, 1),
                                       dt_l.reshape(1, nh2), dt_l.reshape(nh2, 1)]
    if s0 is not None:
        in_specs.append(pl.BlockSpec((None, nh2, DK_B, DV_B), lambda b, i: (b, 0, 0, 0)))
        args.append(s0)
    out_shape = [jax.ShapeDtypeStruct((m, W_B), F32)] * 2
    out_specs = [pl.BlockSpec((c, W_B), lambda b, i: (fmap(b, i), 0)),
                 pl.BlockSpec((c, W_B), lambda b, i: (bmap(b, i), 0))]
    aliases = {}
    if prev_of is not None:
        aliases = {len(args): 0, len(args) + 1: 1}
        in_specs += [pl.BlockSpec(memory_space=pl.ANY)] * 2
        args += [prev_of, prev_ob]
    if emit_state:
        out_shape.append(jax.ShapeDtypeStruct((nb, nh2, DK_B, DV_B), F32))
        out_specs.append(pl.BlockSpec((None, nh2, DK_B, DV_B), lambda b, i: (b, 0, 0, 0)))
    kern = functools.partial(_delta_kernel, s0 is None, prev_of is not None, emit_state)
    return pl.pallas_call(
        kern,
        grid=(nb, n),
        in_specs=in_specs,
        out_specs=out_specs,
        out_shape=out_shape,
        scratch_shapes=[pltpu.VMEM((nh2, DK_B, DV_B), F32)],
        input_output_aliases=aliases,
        compiler_params=_cparams(2),
        name="delta_scan",
    )(*args)


def _route_rows(sel, scores):
    g = EXPERTS_PER_GROUP
    gscore = []
    for gi in range(N_EXPERT_GROUPS):
        a, b, c, d = sel[gi * g:(gi + 1) * g]
        hi1, lo1 = jnp.maximum(a, b), jnp.minimum(a, b)
        hi2, lo2 = jnp.maximum(c, d), jnp.minimum(c, d)
        gscore.append(jnp.maximum(hi1, hi2) + jnp.maximum(jnp.minimum(hi1, hi2), jnp.maximum(lo1, lo2)))
    best = jnp.zeros_like(gscore[0], dtype=jnp.int32)
    bestv = gscore[0]
    for gi in range(1, N_EXPERT_GROUPS):
        upd = gscore[gi] > bestv
        best = jnp.where(upd, gi, best)
        bestv = jnp.where(upd, gscore[gi], bestv)

    def in_best(rows, j):
        out = rows[j]
        for gi in range(1, N_EXPERT_GROUPS):
            out = jnp.where(best == gi, rows[gi * g + j], out)
        return out

    e_sel = [in_best(sel, j) for j in range(g)]
    e_sc = [in_best(scores, j) for j in range(g)]

    def first_argmax(vals):
        bi, bv, bs = jnp.zeros_like(best), vals[0], e_sc[0]
        for j in range(1, g):
            upd = vals[j] > bv
            bi = jnp.where(upd, j, bi)
            bv = jnp.where(upd, vals[j], bv)
            bs = jnp.where(upd, e_sc[j], bs)
        return bi, bs

    i0, s0 = first_argmax(e_sel)
    i1, s1 = first_argmax([jnp.where(i0 == j, -jnp.inf, e_sel[j]) for j in range(g)])
    tot = s0 + s1
    return best * g + i0, best * g + i1, s0 / tot, s1 / tot


def _mix_kernel(x_ref, mod_ref, oa_ref, of_ref, ob_ref, gb_ref, oc_ref, g0_ref, g1_ref, g2_ref,
                dn_ref, n2_ref, wpa_ref, wpb_ref, wpc_ref, wout_ref, wr_ref, br_ref,
                x1_ref, h2_ref, idx_ref, wt_ref):
    d = x_ref.shape[1]
    dn = dn_ref[...]
    o = of_ref[...] + ob_ref[...]
    gate = _silu(gb_ref[...])
    parts = []
    for h in range(N_HEADS_B):
        sl = slice(h * DV_B, (h + 1) * DV_B)
        parts.append((_rms(o[:, sl]) * dn * gate[:, sl]).astype(BF16))
    o_b = jnp.concatenate(parts, axis=-1)
    mixed = (_sigmoid(g0_ref[...]) * _dot(oa_ref[...], wpa_ref[...])
             + _sigmoid(g1_ref[...]) * _dot(o_b, wpb_ref[...])
             + _sigmoid(g2_ref[...]) * _dot(oc_ref[...], wpc_ref[...]))
    gate1 = mod_ref[:, 2 * d:3 * d]
    x1 = x_ref[...] + gate1 * _dot(mixed.astype(BF16), wout_ref[...])
    x1_ref[...] = x1
    shift2 = mod_ref[:, 3 * d:4 * d]
    scale2 = mod_ref[:, 4 * d:5 * d]
    h2 = _rms(x1) * n2_ref[...] * (1.0 + scale2) + shift2
    h2_ref[...] = h2
    sc = _sigmoid(_dot_nt(wr_ref[...], h2, HIGHEST))
    sel = sc + br_ref[...]
    i0, i1, w0, w1 = _route_rows([sel[e:e + 1] for e in range(N_EXPERTS)],
                                 [sc[e:e + 1] for e in range(N_EXPERTS)])
    idx_ref[0:1, :] = i0
    idx_ref[1:2, :] = i1
    wt_ref[0:1, :] = w0
    wt_ref[1:2, :] = w1


def _mix(x, mod_l, oa, o_f, o_b, z, oc, delta_norm_l, norm2_l, wpa, wpb, wpc, wout, wr_t, br, cond_of_row):
    m, d = x.shape
    r = ROW_TILE
    c_gb = OFF_GATE_B // W_B
    once = dict(pipeline_mode=pl.Buffered(1))
    full = lambda shape: pl.BlockSpec(shape, lambda i: (0,) * len(shape), **once)
    return pl.pallas_call(
        _mix_kernel,
        grid=(m // r,),
        in_specs=[
            pl.BlockSpec((r, d), lambda i: (i, 0)),
            pl.BlockSpec((None, 1, 6 * d), lambda i: (cond_of_row(i * r), 0, 0)),
            pl.BlockSpec((r, W_A), lambda i: (i, 0)),
            pl.BlockSpec((r, W_B), lambda i: (i, 0)),
            pl.BlockSpec((r, W_B), lambda i: (i, 0)),
            pl.BlockSpec((r, W_B), lambda i: (i, c_gb)),
            pl.BlockSpec((r, W_C), lambda i: (i, 0)),
            pl.BlockSpec((r, d), lambda i: (i, 0)),
            pl.BlockSpec((r, d), lambda i: (i, 1)),
            pl.BlockSpec((r, d), lambda i: (i, 2)),
            full((1, DV_B)), full((1, d)),
            full((W_A, d)), full((W_B, d)), full((W_C, d)), full((d, d)),
            full((N_EXPERTS, d)), full((N_EXPERTS, 1)),
        ],
        out_specs=[
            pl.BlockSpec((r, d), lambda i: (i, 0)),
            pl.BlockSpec((r, d), lambda i: (i, 0)),
            pl.BlockSpec((TOP_K, r), lambda i: (0, i)),
            pl.BlockSpec((TOP_K, r), lambda i: (0, i)),
        ],
        out_shape=[
            jax.ShapeDtypeStruct((m, d), F32),
            jax.ShapeDtypeStruct((m, d), F32),
            jax.ShapeDtypeStruct((TOP_K, m), jnp.int32),
            jax.ShapeDtypeStruct((TOP_K, m), F32),
        ],
        compiler_params=_cparams(1),
        name="mix_route",
    )(x, mod_l, oa, o_f, o_b, z, oc, z, z, z, delta_norm_l.reshape(1, DV_B), norm2_l.reshape(1, d),
      wpa, wpb, wpc, wout, wr_t, br)


def _dispatch_meta(idx_t, bm):
    m = idx_t.shape[1]
    nk = m * TOP_K
    flat_e = idx_t.T.reshape(nk)
    onehot = (flat_e[:, None] == jnp.arange(N_EXPERTS, dtype=jnp.int32)[None, :]).astype(jnp.int32)
    csum = jnp.cumsum(onehot, axis=0)
    rank = jnp.sum((csum - onehot) * onehot, axis=1)
    counts = csum[-1]
    padded = (counts + bm - 1) // bm * bm
    pad_end = jnp.cumsum(padded)
    pad_start = pad_end - padded
    dest = jnp.sum(onehot * pad_start[None, :], axis=1) + rank
    nb = -(-(nk + N_EXPERTS * (bm - 1)) // bm)
    row_tok = jnp.zeros((nb * bm,), jnp.int32).at[dest].set(jnp.arange(nk, dtype=jnp.int32) // TOP_K)
    n_used = (pad_end[-1] // bm).astype(jnp.int32)
    blk_e = jnp.minimum(jnp.searchsorted(pad_end, jnp.arange(nb, dtype=jnp.int32) * bm, side="right"),
                        N_EXPERTS - 1).astype(jnp.int32)
    return blk_e, n_used.reshape(1), row_tok, dest.astype(jnp.int32)


def _moe_kernel(blk_e_ref, n_used_ref, row_tok_ref, h2_hbm, wg_ref, wu_ref, wd_ref, y_ref, xbuf, sem):
    del blk_e_ref
    i = pl.program_id(0)
    bm = xbuf.shape[1]
    n_used = n_used_ref[0]

    def gather(blk, slot):
        def body(r, carry):
            tok = row_tok_ref[blk * bm + r]
            pltpu.make_async_copy(h2_hbm.at[pl.ds(tok, 1)], xbuf.at[slot, pl.ds(r, 1)], sem.at[slot]).start()
            return carry
        lax.fori_loop(0, bm, body, 0, unroll=8)

    @pl.when((i == 0) & (n_used > 0))
    def _():
        gather(0, 0)

    @pl.when(i + 1 < n_used)
    def _():
        gather(i + 1, (i + 1) % 2)

    @pl.when(i < n_used)
    def _():
        slot = i % 2
        pltpu.make_async_copy(h2_hbm.at[pl.ds(0, bm)], xbuf.at[slot], sem.at[slot]).wait()
        xb = xbuf[slot].astype(BF16)
        act = _silu(_dot(xb, wg_ref[...])) * _dot(xb, wu_ref[...])
        y_ref[...] = _dot(act.astype(BF16), wd_ref[...])

    @pl.when(i >= n_used)
    def _():
        y_ref[...] = jnp.zeros_like(y_ref)


def _moe(h2, blk_e, n_used, row_tok, wg, wu, wd):
    m, d = h2.shape
    bm = MOE_ROWS
    nb = row_tok.shape[0] // bm
    f = wg.shape[2]
    return pl.pallas_call(
        _moe_kernel,
        grid_spec=pltpu.PrefetchScalarGridSpec(
            num_scalar_prefetch=3,
            grid=(nb,),
            in_specs=[
                pl.BlockSpec(memory_space=pl.ANY),
                pl.BlockSpec((None, d, f), lambda i, be, nu, rt: (be[i], 0, 0)),
                pl.BlockSpec((None, d, f), lambda i, be, nu, rt: (be[i], 0, 0)),
                pl.BlockSpec((None, f, d), lambda i, be, nu, rt: (be[i], 0, 0)),
            ],
            out_specs=pl.BlockSpec((bm, d), lambda i, be, nu, rt: (i, 0)),
            scratch_shapes=[pltpu.VMEM((2, bm, d), F32), pltpu.SemaphoreType.DMA((2,))],
        ),
        out_shape=jax.ShapeDtypeStruct((nb * bm, d), F32),
        compiler_params=_cparams(1),
        name="moe_experts",
    )(blk_e, n_used, row_tok, h2, wg, wu, wd)


def _combine_kernel(final, pos_ref, yb_hbm, x1_ref, mod_ref, w_ref, nf_ref, o_ref, buf, sem):
    i = pl.program_id(0)
    n = pl.num_programs(0)
    r = x1_ref.shape[0]
    d = x1_ref.shape[1]

    def gather(tile, slot):
        def body(j, carry):
            for k in range(TOP_K):
                p = pos_ref[(tile * r + j) * TOP_K + k]
                pltpu.make_async_copy(yb_hbm.at[pl.ds(p, 1)], buf.at[slot, k, pl.ds(j, 1)], sem.at[slot]).start()
            return carry
        lax.fori_loop(0, r, body, 0, unroll=4)

    @pl.when(i == 0)
    def _():
        gather(0, 0)

    @pl.when(i + 1 < n)
    def _():
        gather(i + 1, (i + 1) % 2)

    slot = i % 2
    for k in range(TOP_K):
        pltpu.make_async_copy(yb_hbm.at[pl.ds(0, r)], buf.at[slot, k], sem.at[slot]).wait()
    f =w_ref[:, 0:1] * buf[slot, 0] + w_ref[:, 1:2] * buf[slot, 1]
    x2 = x1_ref[...] + mod_ref[:, 5 * d:6 * d] * f
    if final:
        x2 = _rms(x2) * nf_ref[...]
    o_ref[...] = x2


def _combine(yb, x1, mod_l, w_col, pos, norm_final, final, cond_of_row):
    m, d = x1.shape
    r = ROW_TILE
    return pl.pallas_call(
        functools.partial(_combine_kernel, final),
        grid_spec=pltpu.PrefetchScalarGridSpec(
            num_scalar_prefetch=1,
            grid=(m // r,),
            in_specs=[
                pl.BlockSpec(memory_space=pl.ANY),
                pl.BlockSpec((r, d), lambda i, p: (i, 0)),
                pl.BlockSpec((None, 1, 6 * d), lambda i, p: (cond_of_row(i * r), 0, 0)),
                pl.BlockSpec((r, TOP_K), lambda i, p: (i, 0)),
                pl.BlockSpec((1, d), lambda i, p: (0, 0)),
            ],
            out_specs=pl.BlockSpec((r, d), lambda i, p: (i, 0)),
            scratch_shapes=[pltpu.VMEM((2, TOP_K, r, d), F32), pltpu.SemaphoreType.DMA((2,))],
        ),
        out_shape=jax.ShapeDtypeStruct((m, d), F32),
        compiler_params=_cparams(1),
        name="moe_combine",
    )(pos, yb, x1, mod_l, w_col, norm_final.reshape(1, d))


class _CondOfRow:
    def __init__(self, n_ctx, t_lat):
        self.n_ctx, self.t_lat = n_ctx, t_lat
        self.tile_in = _pick_tile((1024, 512, 256), n_ctx, t_lat)

    def __call__(self, row0):
        return jnp.where(row0 < self.n_ctx, 0, 1 + (row0 - self.n_ctx) // self.t_lat)


def _prep_w_in(w):
    d = w.shape[0]
    cols = [w[:, SRC_BRANCH:], w[:, SRC_Q:SRC_QKV_B], w[:, SRC_QKV_B:SRC_BA], w[:, SRC_POOL:SRC_BRANCH],
            w[:, SRC_BA:SRC_POOL]]
    used = OFF_BA + (SRC_POOL - SRC_BA)
    cols.append(jnp.zeros((d, NP_IN - used), w.dtype))
    return jnp.concatenate(cols, axis=1).astype(BF16)


def kernel(x_prompt, x_sample, cache_k, cache_v, state_delta, c, c_ctx, w_mod, b_mod, norm1, norm2, w_in,
           lambda_qk, subln, conv_w, a_log, dt_bias, delta_norm, pool_w, pool_scale, w_pa, w_pb, w_pc,
           w_out, w_router, b_router, w_gate, w_up, w_down, norm_final):
    b_ctx, t_ctx, d = x_prompt.shape
    b_lat, t_lat, _ = x_sample.shape
    depth = w_in.shape[0]
    past = cache_k.shape[2]
    n_ctx, n_lat = b_ctx * t_ctx, b_lat * t_lat
    m = n_ctx + n_lat
    assert d == D_MODEL and w_in.shape[2] == D_IN and 1 + b_lat <= SUBLANES
    assert t_ctx % ROW_TILE == 0 and t_lat % ROW_TILE == 0 and n_ctx % t_lat == 0
    cond_of_row = _CondOfRow(n_ctx, t_lat)
    dims = (b_ctx, t_ctx, b_lat, t_lat)

    def seq(row0):
        is_ctx = row0 < n_ctx
        return (jnp.where(is_ctx, row0 % t_ctx, (row0 - n_ctx) % t_lat), jnp.where(is_ctx, t_ctx, t_lat))

    x = jnp.concatenate([x_prompt.reshape(n_ctx, d), x_sample.reshape(n_lat, d)], axis=0)
    cond8 = jnp.zeros((SUBLANES, d), F32).at[0].set(c_ctx).at[1:1 + b_lat].set(c)
    mod = _modulation(cond8, w_mod, b_mod).reshape(depth, SUBLANES, 1, 6 * d)
    rope_tabs = _rope_tables(t_lat)
    wr_t = w_router.T
    br = b_router.reshape(N_EXPERTS, 1)
    ck = cache_k.reshape(b_lat, depth, past, QK_A)
    cv = cache_v.reshape(b_lat, depth, past, W_A)
    s_lat = state_delta.reshape(b_lat, depth, 2 * N_HEADS_B, DK_B, DV_B)

    new_k, new_v, new_s = [], [], []
    for l in range(depth):
        lam_init = 0.8 - 0.6 * math.exp(-0.3 * l)
        z = _in_proj(x, mod[l], norm1[l], _prep_w_in(w_in[l]), cond_of_row)
        new_k.append(z[:n_ctx, OFF_K:OFF_K + QK_A].reshape(b_ctx, t_ctx, N_HEADS_A, 2 * DK_A))
        new_v.append(z[:n_ctx, OFF_V:OFF_V + W_A].reshape(b_ctx, t_ctx, N_HEADS_A, DV_A))

        oa = _attention(z, lambda_qk[l], subln[l], ck[:, l], cv[:, l], rope_tabs, lam_init, dims)
        qn, kn, vv, oc = _local(z, conv_w[l], pool_w[l], pool_scale[l], seq)
        nba = 4 * N_HEADS_B
        bat = z[:, OFF_BA:OFF_BA + nba].reshape(m // DELTA_CHUNK, DELTA_CHUNK, nba).transpose(0, 2, 1)
        o_f, o_b, s_ctx = _delta_call(z, qn, kn, vv, bat, a_log[l], dt_bias[l], None, None, None,
                                      0, b_ctx, t_ctx, True)
        o_f, o_b = _delta_call(z, qn, kn, vv, bat, a_log[l], dt_bias[l], s_lat[:, l], o_f, o_b,
                               n_ctx, b_lat, t_lat, False)
        new_s.append(s_ctx.reshape(b_ctx, 2, N_HEADS_B, DK_B, DV_B))

        x1, h2, idx_t, wt_t = _mix(x, mod[l], oa, o_f, o_b, z, oc, delta_norm[l], norm2[l],
                                   w_pa[l].astype(BF16), w_pb[l].astype(BF16), w_pc[l].astype(BF16),
                                   w_out[l].astype(BF16), wr_t, br, cond_of_row)
        blk_e, n_used, row_tok, pos = _dispatch_meta(idx_t, MOE_ROWS)
        yb = _moe(h2, blk_e, n_used, row_tok, w_gate[l].astype(BF16), w_up[l].astype(BF16),
                  w_down[l].astype(BF16))
        x = _combine(yb, x1, mod[l], wt_t.T, pos, norm_final, l == depth - 1, cond_of_row)

    y_prompt = x[:n_ctx].reshape(b_ctx, t_ctx, d)
    y_sample = x[n_ctx:].reshape(b_lat, t_lat, d)
    return (y_prompt, y_sample, jnp.stack(new_k, axis=1), jnp.stack(new_v, axis=1), jnp.stack(new_s, axis=1))
```

```python
import functools
import math

import jax
import jax.numpy as jnp
from jax import lax
from jax.experimental import pallas as pl
from jax.experimental.pallas import tpu as pltpu

F32 = jnp.float32
BF16 = jnp.bfloat16
HIGHEST = lax.Precision.HIGHEST

D_MODEL = 2048
GRID_W = 64
EPS = 1e-6
N_HEADS_A = 8
DK_A = 64
DV_A = 128
QK_A = N_HEADS_A * 2 * DK_A
W_A = N_HEADS_A * DV_A
ROPE_BASE = 10000.0
ROPE_AXIS = DK_A // 2
N_HEADS_B = 4
DK_B = 128
DV_B = 128
W_B = N_HEADS_B * DV_B
DELTA_CHUNK = 64
POOL_WINDOWS = (2, 4, 8, 16)
POOL_GROUP_W = 128
W_C = len(POOL_WINDOWS) * POOL_GROUP_W
N_BRANCH = 3
N_EXPERTS = 16
N_EXPERT_GROUPS = 4
EXPERTS_PER_GROUP = N_EXPERTS // N_EXPERT_GROUPS
TOP_K = 2
D_FF = 1408

SRC_Q, SRC_K, SRC_V = 0, QK_A, 2 * QK_A
SRC_QKV_B = 2 * QK_A + W_A
SRC_GATE_B = SRC_QKV_B + 3 * W_B
SRC_BA = SRC_GATE_B + W_B
SRC_POOL = SRC_BA + 4 * N_HEADS_B
SRC_BRANCH = SRC_POOL + W_C
D_IN = SRC_BRANCH + N_BRANCH * D_MODEL
LANES = 128
SUBLANES = 8
OFF_BRANCH = 0
OFF_Q = N_BRANCH * D_MODEL
OFF_K = OFF_Q + QK_A
OFF_V = OFF_K + QK_A
OFF_QKV_B = OFF_V + W_A
OFF_GATE_B = OFF_QKV_B + 3 * W_B
OFF_POOL = OFF_GATE_B + W_B
OFF_BA = OFF_POOL + W_C
IN_TILE_N = 1024
NP_IN = -(-(OFF_BA + LANES) // IN_TILE_N) * IN_TILE_N

ROW_TILE = 256
MOE_ROWS = 256
VMEM_LIMIT = 56 * 1024 * 1024


def _cparams(n_axes, vmem=VMEM_LIMIT):
    return pltpu.CompilerParams(dimension_semantics=("arbitrary",) * n_axes, vmem_limit_bytes=vmem)


def _dot(a, b, precision=None):
    return jnp.dot(a, b, preferred_element_type=F32, precision=precision)


def _dot_nt(a, b, precision=None):
    return lax.dot_general(a, b, (((1,), (1,)), ((), ())), preferred_element_type=F32, precision=precision)


def _bdot(a, b):
    return _dot(a.astype(BF16), b.astype(BF16))


def _bdot_nt(a, b):
    return _dot_nt(a.astype(BF16), b.astype(BF16))


def _sigmoid(x):
    return 1.0 / (1.0 + jnp.exp(-x))


def _silu(x):
    return x * _sigmoid(x)


def _rms(x, eps=EPS):
    return x * lax.rsqrt(jnp.mean(x * x, axis=-1, keepdims=True) + eps)


def _pick_tile(cands, *dims):
    for t in cands:
        if all(d % t == 0 for d in dims):
            return t
    raise ValueError(f"no tile in {cands} divides {dims}")


def _mod_kernel(c_ref, w_ref, b_ref, o_ref):
    a = _silu(c_ref[...]).astype(BF16)
    o_ref[...] = _dot(a, w_ref[...].astype(BF16)) + b_ref[...]


def _modulation(cond8, w_mod, b_mod):
    depth, d, n6 = w_mod.shape
    tn = 1024
    return pl.pallas_call(
        _mod_kernel,
        grid=(depth, n6 // tn),
        in_specs=[
            pl.BlockSpec((SUBLANES, d), lambda l, n: (0, 0)),
            pl.BlockSpec((None, d, tn), lambda l, n: (l, 0, n)),
            pl.BlockSpec((None, 1, tn), lambda l, n: (l, 0, n)),
        ],
        out_specs=pl.BlockSpec((None, SUBLANES, tn), lambda l, n: (l, 0, n)),
        out_shape=jax.ShapeDtypeStruct((depth, SUBLANES, n6), F32),
        compiler_params=_cparams(2),
        name="adaln_mod",
    )(cond8, w_mod, b_mod.reshape(depth, 1, n6))


def _in_kernel(x_ref, mod_ref, g_ref, w_ref, z_ref, h_scr):
    @pl.when(pl.program_id(1) == 0)
    def _():
        d = x_ref.shape[1]
        y = _rms(x_ref[...]) * g_ref[...]
        shift = mod_ref[:, 0:d]
        scale = mod_ref[:, d:2 * d]
        h_scr[...] = (y * (1.0 + scale) + shift).astype(BF16)

    z_ref[...] = _dot(h_scr[...], w_ref[...])


def _in_proj(x, mod_l, norm1_l, w_in_p, cond_of_row):
    m, d = x.shape
    npad = w_in_p.shape[1]
    tm, tn = cond_of_row.tile_in, IN_TILE_N
    return pl.pallas_call(
        _in_kernel,
        grid=(m // tm, npad // tn),
        in_specs=[
            pl.BlockSpec((tm, d), lambda i, n: (i, 0)),
            pl.BlockSpec((None, 1, 6 * d), lambda i, n: (cond_of_row(i * tm), 0, 0)),
            pl.BlockSpec((1, d), lambda i, n: (0, 0)),
            pl.BlockSpec((d, tn), lambda i, n: (0, n)),
        ],
        out_specs=pl.BlockSpec((tm, tn), lambda i, n: (i, n)),
        out_shape=jax.ShapeDtypeStruct((m, npad), F32),
        scratch_shapes=[pltpu.VMEM((tm, d), BF16)],
        compiler_params=_cparams(2),
        name="in_proj",
    )(x, mod_l, norm1_l.reshape(1, d), w_in_p)


def _lambda(lq_ref, lam_init):
    lq = lq_ref[...]
    a = jnp.sum(lq[0:1] * lq[1:2], axis=-1, keepdims=True)
    b = jnp.sum(lq[2:3] * lq[3:4], axis=-1, keepdims=True)
    return jnp.exp(a) - jnp.exp(b) + lam_init


def _diff_attn(qb, kb, vb, lam):
    probs = []
    for mp in range(2):
        s = _dot_nt(qb[:, mp * DK_A:(mp + 1) * DK_A], kb[:, mp * DK_A:(mp + 1) * DK_A])
        e = jnp.exp(s - jnp.max(s, axis=-1, keepdims=True))
        probs.append(e / jnp.sum(e, axis=-1, keepdims=True))
    pd = probs[0] - lam * probs[1]
    return _dot(pd.astype(BF16), vb)


def _attn_ctx_kernel(lam_init, lq_ref, sub_ref, q_ref, k_ref, v_ref, o_ref):
    lam = _lambda(lq_ref, lam_init)
    qb = (q_ref[...] * DK_A ** -0.5).astype(BF16)
    o = _diff_attn(qb, k_ref[...].astype(BF16), v_ref[...].astype(BF16), lam)
    o_ref[...] = (_rms(o) * sub_ref[...] * (1.0 - lam_init)).astype(BF16)


def _rope(x, cos, sin_signed):
    lane = lax.broadcasted_iota(jnp.int32, x.shape, 1)
    first_half = (lane % (ROPE_AXIS)) < (ROPE_AXIS // 2)
    partner = jnp.where(first_half, pltpu.roll(x, LANES - ROPE_AXIS // 2, 1), pltpu.roll(x, ROPE_AXIS // 2, 1))
    return x * cos + partner * sin_signed


def _attn_lat_kernel(lam_init, t_lat, lq_ref, sub_ref, q_ref, k_ref, v_ref, ck_ref, cv_ref,
                     cq_ref, sq_ref, ckk_ref, skk_ref, o_ref, k_scr, v_scr):
    @pl.when(pl.program_id(2) == 0)
    def _():
        k_scr[0:t_lat, :] = _rope(k_ref[...], ckk_ref[...], skk_ref[...]).astype(BF16)
        k_scr[t_lat:, :] = ck_ref[...].astype(BF16)
        v_scr[0:t_lat, :] = v_ref[...].astype(BF16)
        v_scr[t_lat:, :] = cv_ref[...].astype(BF16)

    lam = _lambda(lq_ref, lam_init)
    q = _rope(q_ref[...], cq_ref[...], sq_ref[...]) * DK_A ** -0.5
    o = _diff_attn(q.astype(BF16), k_scr[...], v_scr[...], lam)
    o_ref[...] = (_rms(o) * sub_ref[...] * (1.0 - lam_init)).astype(BF16)


def _attention(z, lq_l, sub_l, cache_k_l, cache_v_l, rope_tabs, lam_init, dims):
    b_ctx, t_ctx, b_lat, t_lat = dims
    n_ctx = b_ctx * t_ctx
    past = cache_k_l.shape[1]
    cq, ck, cv = OFF_Q // LANES, OFF_K // LANES, OFF_V // LANES
    small = [pl.BlockSpec((4, DK_A), lambda *_: (0, 0)), pl.BlockSpec((1, DV_A), lambda *_: (0, 0))]
    sub2 = sub_l.reshape(1, DV_A)

    oa_ctx = pl.pallas_call(
        functools.partial(_attn_ctx_kernel, lam_init),
        grid=(b_ctx, N_HEADS_A),
        in_specs=small + [
            pl.BlockSpec((t_ctx, LANES), lambda b, h: (b, cq + h)),
            pl.BlockSpec((t_ctx, LANES), lambda b, h: (b, ck + h)),
            pl.BlockSpec((t_ctx, LANES), lambda b, h: (b, cv + h)),
        ],
        out_specs=pl.BlockSpec((t_ctx, LANES), lambda b, h: (b, h)),
        out_shape=jax.ShapeDtypeStruct((n_ctx, W_A), BF16),
        compiler_params=_cparams(2),
        name="attn_ctx",
    )(lq_l, sub2, z, z, z)

    tq = _pick_tile((256, 128), t_lat)
    nq = t_lat // tq
    rb_q = n_ctx // tq
    rb_k = n_ctx // t_lat
    cos_t, sin_t = rope_tabs
    oa_lat = pl.pallas_call(
        functools.partial(_attn_lat_kernel, lam_init, t_lat),
        grid=(b_lat, N_HEADS_A, nq),
        in_specs=small + [
            pl.BlockSpec((tq, LANES), lambda b, h, i: (rb_q + b * nq + i, cq + h)),
            pl.BlockSpec((t_lat, LANES), lambda b, h, i: (rb_k + b, ck + h)),
            pl.BlockSpec((t_lat, LANES), lambda b, h, i: (rb_k + b, cv + h)),
            pl.BlockSpec((None, past, LANES), lambda b, h, i: (b, 0, h)),
            pl.BlockSpec((None, past, LANES), lambda b, h, i: (b, 0, h)),
            pl.BlockSpec((tq, LANES), lambda b, h, i: (i, 0)),
            pl.BlockSpec((tq, LANES), lambda b, h, i: (i, 0)),
            pl.BlockSpec((t_lat, LANES), lambda b, h, i: (0, 0)),
            pl.BlockSpec((t_lat, LANES), lambda b, h, i: (0, 0)),
        ],
        out_specs=pl.BlockSpec((tq, LANES), lambda b, h, i: (b * nq + i, h)),
        out_shape=jax.ShapeDtypeStruct((b_lat * t_lat, W_A), BF16),
        scratch_shapes=[pltpu.VMEM((t_lat + past, LANES), BF16), pltpu.VMEM((t_lat + past, LANES), BF16)],
        compiler_params=_cparams(3),
        name="attn_lat",
    )(lq_l, sub2, z, z, z, cache_k_l, cache_v_l, cos_t, sin_t, cos_t, sin_t)
    return oa_ctx, oa_lat


def _rope_tables(t_lat):
    rows = t_lat // GRID_W
    r = jnp.repeat(jnp.arange(rows), GRID_W).astype(F32)
    col = jnp.tile(jnp.arange(GRID_W), rows).astype(F32)
    inv = ROPE_BASE ** (-jnp.arange(0, ROPE_AXIS, 2, dtype=F32) / ROPE_AXIS)
    ang_r, ang_c = r[:, None] * inv, col[:, None] * inv
    cos64 = jnp.concatenate([jnp.cos(ang_r)] * 2 + [jnp.cos(ang_c)] * 2, axis=-1)
    sin64 = jnp.concatenate([-jnp.sin(ang_r), jnp.sin(ang_r), -jnp.sin(ang_c), jnp.sin(ang_c)], axis=-1)
    return jnp.tile(cos64, (1, 2)), jnp.tile(sin64, (1, 2))


def _log_decay(alpha, a_log, dt):
    x = alpha + dt
    return -jnp.exp(a_log) * (jnp.maximum(x, 0.0) + jnp.log(1.0 + jnp.exp(-jnp.abs(x))))


def _chunk_cumsum(g, axis):
    n = g.shape[axis]
    pos = lax.broadcasted_iota(jnp.int32, g.shape, axis) % DELTA_CHUNK
    f, b = g, g
    s = 1
    while s < DELTA_CHUNK:
        f = f + jnp.where(pos >= s, pltpu.roll(f, s, axis), 0.0)
        b = b + jnp.where(pos < DELTA_CHUNK - s, pltpu.roll(b, n - s, axis), 0.0)
        s *= 2
    return f, b


def _delta_prepare(q_scr, k_scr, v_scr, ba_ref, bat_ref, al_row, al_col, dt_row, dt_col,
                   u_ref, l1_ref, l2_ref, e_ref):
    c, nh = DELTA_CHUNK, N_HEADS_B
    rows_n = q_scr.shape[0]
    ba = ba_ref[...]
    bat = bat_ref[...]
    beta_all = _sigmoid(ba[:, 0:2 * nh])
    gf, gb = _chunk_cumsum(_log_decay(ba[:, 2 * nh:4 * nh], al_row[...], dt_row[...]), 0)
    gc_all = jnp.where(lax.broadcasted_iota(jnp.int32, gf.shape, 1) < nh, gf, gb)
    gtf, gtb = _chunk_cumsum(_log_decay(bat[2 * nh:4 * nh, :], al_col[...], dt_col[...]), 1)
    gct_all = jnp.where(lax.broadcasted_iota(jnp.int32, gtf.shape, 0) < nh, gtf, gtb)

    ri = lax.broadcasted_iota(jnp.int32, (c, c), 0)
    ci = lax.broadcasted_iota(jnp.int32, (c, c), 1)
    masks = ((ri >= ci, ri > ci), (ri <= ci, ri < ci))
    eye = (ri == ci).astype(F32)
    n_sq = int(math.log2(c)) - 1

    for j in range(rows_n // c):
        rs = slice(j * c, (j + 1) * c)
        chains = []
        for d in range(2):
            incl, strict = masks[d]
            last = 0 if d else c - 1
            for h in range(nh):
                idx = d * nh + h
                hs = slice(h * DK_B, (h + 1) * DK_B)
                gc = gc_all[rs, idx:idx + 1]
                gr = gct_all[idx:idx + 1, rs]
                ch = dict(d=d, h=h, idx=idx, hs=hs, strict=strict, gc=gc, g_last=gc[last:last + 1, :],
                          q=q_scr[rs, hs] * DK_B ** -0.5, k=k_scr[rs, hs], v=v_scr[rs, hs],
                          beta=beta_all[rs, idx:idx + 1],
                          decay=jnp.where(incl, jnp.exp(jnp.where(incl, gc - gr, 0.0)), 0.0))
                ch["kb"] = ch["k"] * ch["beta"]
                chains.append(ch)
        for ch in chains:
            kbf = ch["k"].astype(BF16)
            a = jnp.where(ch["strict"], -(_dot_nt(ch["kb"].astype(BF16), kbf) * ch["decay"]), 0.0)
            ch["attn"] = _dot_nt(ch["q"].astype(BF16), kbf) * ch["decay"]
            ch["p"], ch["apow"] = eye + a, a
        for _ in range(n_sq):
            for ch in chains:
                ch["apow"] = _bdot(ch["apow"], ch["apow"])
            for ch in chains:
                ch["p"] = ch["p"] + _bdot(ch["p"], ch["apow"])
        for ch in chains:
            rhs = jnp.concatenate([ch["v"] * ch["beta"], ch["kb"] * jnp.exp(ch["gc"])], axis=-1)
            ch["sol"] = _bdot(ch["p"], rhs)
        for ch in chains:
            d, h, hs = ch["d"], ch["h"], ch["hs"]
            u_ref[d, rs, hs] = ch["sol"][:, :DV_B]
            l1_ref[d, j, h, 0:c, :] = ch["sol"][:, DV_B:].astype(BF16)
            l1_ref[d, j, h, c:2 * c, :] = (ch["q"] * jnp.exp(ch["gc"])).astype(BF16)
            l2_ref[d, j, h, 0:c, :] = ch["attn"].astype(BF16)
            l2_ref[d, j, h, c:, :] = (ch["k"] * jnp.exp(ch["g_last"] - ch["gc"])).T.astype(BF16)
            e_ref[j, ch["idx"]:ch["idx"] + 1, :] = jnp.broadcast_to(jnp.exp(ch["g_last"]), (1, LANES))


def _local_kernel(seq, cur_ref, prev_ref, next_ref, pcur_ref, pprev_ref, pnext_ref, ba_ref, bat_ref,
                  cw_ref, pw_ref, ps_ref, al_row, al_col, dt_row, dt_col,
                  oc_ref, u_ref, l1_ref, l2_ref, e_ref, q_scr, k_scr, v_scr):
    rows_n = cur_ref.shape[0]
    pos0, t_seq = seq(pl.program_id(0) * rows_n)
    has_prev = pos0 > 0
    has_next = pos0 + rows_n < t_seq
    rows = lax.broadcasted_iota(jnp.int32, (rows_n, 1), 0)

    x = cur_ref[...]
    prev_row = jnp.where(has_prev, prev_ref[SUBLANES - 1:SUBLANES, :], 0.0)
    next_row = jnp.where(has_next, next_ref[0:1, :], 0.0)
    xm1 = jnp.where(rows == 0, prev_row, pltpu.roll(x, 1, 0))
    xp1 = jnp.where(rows == rows_n - 1, next_row, pltpu.roll(x, rows_n - 1, 0))
    y = _silu(cw_ref[0:1, :] * xm1 + cw_ref[1:2, :] * x + cw_ref[2:3, :] * xp1)
    for h in range(N_HEADS_B):
        for j, ref in enumerate((q_scr, k_scr)):
            a = y[:, j * W_B + h * DK_B:j * W_B + (h + 1) * DK_B]
            ref[:, h * DK_B:(h + 1) * DK_B] = a * lax.rsqrt(jnp.sum(a * a, axis=-1, keepdims=True) + EPS)
    v_scr[...] = y[:, 2 * W_B:]
    _delta_prepare(q_scr, k_scr, v_scr, ba_ref, bat_ref, al_row, al_col, dt_row, dt_col,
                   u_ref, l1_ref, l2_ref, e_ref)

    pz = pcur_ref[...]
    ext = jnp.concatenate([jnp.where(has_prev, pprev_ref[...], 0.0), pz,
                           jnp.where(has_next, pnext_ref[...], 0.0)], axis=0)
    n_ext = rows_n + 2 * SUBLANES
    tpos = pos0 + rows
    for gi, win in enumerate(POOL_WINDOWS):
        sl = slice(gi * POOL_GROUP_W, (gi + 1) * POOL_GROUP_W)
        xg = ext[:, sl]
        acc = xg + pltpu.roll(xg, 1, 0)
        s = 1
        while 2 * s < win:
            acc = pltpu.roll(acc, s, 0) + pltpu.roll(acc, n_ext - s, 0)
            s *= 2
        acc = acc[SUBLANES:SUBLANES + rows_n]
        cnt = (jnp.minimum(tpos + win // 2, t_seq) - jnp.maximum(tpos - win // 2, 0)).astype(F32)
        d = acc / cnt - pz[:, sl]
        yg = _dot(d.astype(BF16), pw_ref[gi].astype(BF16)) * ps_ref[:, sl]
        oc_ref[:, sl] = yg.astype(BF16)


def _local(z, bat, conv_w_l, pool_w_l, pool_scale_l, a_log_l, dt_l, seq):
    m = z.shape[0]
    r, c = ROW_TILE, DELTA_CHUNK
    hb = r // SUBLANES
    last_hb = m // SUBLANES - 1
    c_qkv, c_pool, c_ba = OFF_QKV_B // (3 * W_B), OFF_POOL // W_C, OFF_BA // LANES
    nh2 = 2 * N_HEADS_B
    mc, cpt = m // c, r // c
    prev_map = lambda cb: (lambda i: (jnp.maximum(i * hb - 1, 0), cb))
    next_map = lambda cb: (lambda i: (jnp.minimum((i + 1) * hb, last_hb), cb))
    row = pl.BlockSpec((1, nh2), lambda i: (0, 0))
    col = pl.BlockSpec((nh2, 1), lambda i: (0, 0))
    return pl.pallas_call(
        functools.partial(_local_kernel, seq),
        grid=(m // r,),
        in_specs=[
            pl.BlockSpec((r, 3 * W_B), lambda i: (i, c_qkv)),
            pl.BlockSpec((SUBLANES, 3 * W_B), prev_map(c_qkv)),
            pl.BlockSpec((SUBLANES, 3 * W_B), next_map(c_qkv)),
            pl.BlockSpec((r, W_C), lambda i: (i, c_pool)),
            pl.BlockSpec((SUBLANES, W_C), prev_map(c_pool)),
            pl.BlockSpec((SUBLANES, W_C), next_map(c_pool)),
            pl.BlockSpec((r, LANES), lambda i: (i, c_ba)),
            pl.BlockSpec((2 * nh2, r), lambda i: (0, i)),
            pl.BlockSpec((3, 3 * W_B), lambda i: (0, 0)),
            pl.BlockSpec((len(POOL_WINDOWS), POOL_GROUP_W, POOL_GROUP_W), lambda i: (0, 0, 0)),
            pl.BlockSpec((1, W_C), lambda i: (0, 0)),
            row, col, row, col,
        ],
        out_specs=[
            pl.BlockSpec((r, W_C), lambda i: (i, 0)),
            pl.BlockSpec((2, r, W_B), lambda i: (0, i, 0)),
            pl.BlockSpec((2, cpt, N_HEADS_B, 2 * c, DV_B), lambda i: (0, i, 0, 0, 0)),
            pl.BlockSpec((2, cpt, N_HEADS_B, c + DK_B, c), lambda i: (0, i, 0, 0, 0)),
            pl.BlockSpec((cpt, nh2, LANES), lambda i: (i, 0, 0)),
        ],
        out_shape=[
            jax.ShapeDtypeStruct((m, W_C), BF16),
            jax.ShapeDtypeStruct((2, m, W_B), F32),
            jax.ShapeDtypeStruct((2, mc, N_HEADS_B, 2 * c, DV_B), BF16),
            jax.ShapeDtypeStruct((2, mc, N_HEADS_B, c + DK_B, c), BF16),
            jax.ShapeDtypeStruct((mc, nh2, LANES), F32),
        ],
        scratch_shapes=[pltpu.VMEM((r, W_B), F32)] * 3,
        compiler_params=_cparams(1),
        name="local_conv_pool",
    )(z, z, z, z, z, z, z, bat, conv_w_l, pool_w_l, pool_scale_l.reshape(1, W_C),
      a_log_l.reshape(1, nh2), a_log_l.reshape(nh2, 1), dt_l.reshape(1, nh2), dt_l.reshape(nh2, 1))


def _scan_kernel(par, b_lat, n_ctx_ch, *refs):
    c, nh = DELTA_CHUNK, N_HEADS_B
    n_slot = par + b_lat
    ins = refs[:n_slot * 8]
    s0_ref = refs[n_slot * 8]
    of_ctx, ob_ctx, of_lat, ob_lat, sfin_ref, s_scr = refs[n_slot * 8 + 1:]
    step = pl.program_id(0)
    cc = step % n_ctx_ch

    @pl.when(cc == 0)
    def _():
        s_scr[0:par] = jnp.zeros((par,) + s_scr.shape[1:], F32)

    @pl.when(step == 0)
    def _():
        s_scr[par:] = s0_ref[...]

    for slot in range(n_slot):
        outs = (of_ctx, ob_ctx, slot) if slot < par else (of_lat, ob_lat, slot - par)
        chains = []
        for d in range(2):
            l1_ref, l2_ref, u_ref, e_ref = ins[(slot * 2 + d) * 4:(slot * 2 + d) * 4 + 4]
            for h in range(nh):
                chains.append(dict(d=d, h=h, idx=d * nh + h, l1=l1_ref, l2=l2_ref, u=u_ref, e=e_ref))
        for ch in chains:
            ch["s"] = s_scr[slot, ch["idx"]]
            ch["r1"] = _dot(ch["l1"][ch["h"]], ch["s"].astype(BF16))
        for ch in chains:
            hs = slice(ch["h"] * DV_B, (ch["h"] + 1) * DV_B)
            v_new = ch["u"][:, hs] - ch["r1"][:c]
            ch["r2"] = _dot(ch["l2"][ch["h"]], v_new.astype(BF16))
        for ch in chains:
            hs = slice(ch["h"] * DV_B, (ch["h"] + 1) * DV_B)
            outs[ch["d"]][outs[2], :, hs] = ch["r1"][c:] + ch["r2"][:c]
            e = ch["e"][ch["idx"]:ch["idx"] + 1, :]
            s_scr[slot, ch["idx"]] = ch["s"] * e + ch["r2"][c:]

    @pl.when(cc == n_ctx_ch - 1)
    def _():
        sfin_ref[...] = s_scr[0:par]


def _delta_scan(u, l1, l2, e, s0_lat, dims):
    b_ctx, t_ctx, b_lat, t_lat = dims
    c, nh = DELTA_CHUNK, N_HEADS_B
    n_ctx_ch, n_lat_ch = t_ctx // c, t_lat // c
    par = b_ctx * n_ctx_ch // n_lat_ch
    assert par >= 1 and par * n_lat_ch == b_ctx * n_ctx_ch and b_ctx % par == 0
    mc = l1.shape[1]
    ctx_chunks = b_ctx * n_ctx_ch
    u4 = u.reshape(2, mc, c, W_B)

    def chunk_of(slot, d):
        if slot < par:
            return lambda s: ((s // n_ctx_ch) * par + slot) * n_ctx_ch + (
                (n_ctx_ch - 1 - s % n_ctx_ch) if d else s % n_ctx_ch)
        q = slot - par
        return lambda s: ctx_chunks + q * n_lat_ch + ((n_lat_ch - 1 - s) if d else s)

    in_specs, args = [], []
    for slot in range(par + b_lat):
        for d in range(2):
            cg = chunk_of(slot, d)
            in_specs += [
                pl.BlockSpec((None, None, nh, 2 * c, DV_B), lambda s, cg=cg, d=d: (d, cg(s), 0, 0, 0)),
                pl.BlockSpec((None, None, nh, c + DK_B, c), lambda s, cg=cg, d=d: (d, cg(s), 0, 0, 0)),
                pl.BlockSpec((None, None, c, W_B), lambda s, cg=cg, d=d: (d, cg(s), 0, 0)),
                pl.BlockSpec((None, 2 * nh, LANES), lambda s, cg=cg: (cg(s), 0, 0)),
            ]
            args += [l1, l2, u4, e]
    in_specs.append(pl.BlockSpec((b_lat, 2 * nh, DK_B, DV_B), lambda s: (0, 0, 0, 0)))
    args.append(s0_lat)

    ctx_o = jax.ShapeDtypeStruct((b_ctx // par, par, n_ctx_ch, c, W_B), F32)
    lat_o = jax.ShapeDtypeStruct((b_lat, n_lat_ch, c, W_B), F32)
    cf = lambda s: s % n_ctx_ch
    of_ctx, ob_ctx, of_lat, ob_lat, s_fin = pl.pallas_call(
        functools.partial(_scan_kernel, par, b_lat, n_ctx_ch),
        grid=(n_lat_ch,),
        in_specs=in_specs,
        out_specs=[
            pl.BlockSpec((None, par, None, c, W_B), lambda s: (s // n_ctx_ch, 0, cf(s), 0, 0)),
            pl.BlockSpec((None, par, None, c, W_B), lambda s: (s // n_ctx_ch, 0, n_ctx_ch - 1 - cf(s), 0, 0)),
            pl.BlockSpec((b_lat, None, c, W_B), lambda s: (0, s, 0, 0)),
            pl.BlockSpec((b_lat, None, c, W_B), lambda s: (0, n_lat_ch - 1 - s, 0, 0)),
            pl.BlockSpec((par, 2 * nh, DK_B, DV_B), lambda s: (s // n_ctx_ch, 0, 0, 0)),
        ],
        out_shape=[ctx_o, ctx_o, lat_o, lat_o, jax.ShapeDtypeStruct((b_ctx, 2 * nh, DK_B, DV_B), F32)],
        scratch_shapes=[pltpu.VMEM((par + b_lat, 2 * nh, DK_B, DV_B), F32)],
        compiler_params=_cparams(1),
        name="delta_scan",
    )(*args)
    n_ctx, n_lat = b_ctx * t_ctx, b_lat * t_lat
    return (of_ctx.reshape(n_ctx, W_B), ob_ctx.reshape(n_ctx, W_B), of_lat.reshape(n_lat, W_B),
            ob_lat.reshape(n_lat, W_B), s_fin)


def _route_rows(sel, scores):
    g = EXPERTS_PER_GROUP
    gscore = []
    for gi in range(N_EXPERT_GROUPS):
        a, b, c, d = sel[gi * g:(gi + 1) * g]
        hi1, lo1 = jnp.maximum(a, b), jnp.minimum(a, b)
        hi2, lo2 = jnp.maximum(c, d), jnp.minimum(c, d)
        gscore.append(jnp.maximum(hi1, hi2) + jnp.maximum(jnp.minimum(hi1, hi2), jnp.maximum(lo1, lo2)))
    best = jnp.zeros_like(gscore[0], dtype=jnp.int32)
    bestv = gscore[0]
    for gi in range(1, N_EXPERT_GROUPS):
        upd = gscore[gi] > bestv
        best = jnp.where(upd, gi, best)
        bestv = jnp.where(upd, gscore[gi], bestv)

    def in_best(rows, j):
        out = rows[j]
        for gi in range(1, N_EXPERT_GROUPS):
            out = jnp.where(best == gi, rows[gi * g + j], out)
        return out

    e_sel = [in_best(sel, j) for j in range(g)]
    e_sc = [in_best(scores, j) for j in range(g)]

    def first_argmax(vals):
        bi, bv, bs = jnp.zeros_like(best), vals[0], e_sc[0]
        for j in range(1, g):
            upd = vals[j] > bv
            bi = jnp.where(upd, j, bi)
            bv = jnp.where(upd, vals[j], bv)
            bs = jnp.where(upd, e_sc[j], bs)
        return bi, bs

    i0, s0 = first_argmax(e_sel)
    i1, s1 = first_argmax([jnp.where(i0 == j, -jnp.inf, e_sel[j]) for j in range(g)])
    tot = s0 + s1
    return best * g + i0, best * g + i1, s0 / tot, s1 / tot


def _mix_kernel(n_ctx, x_ref, mod_ref, oac_ref, oal_ref, ofc_ref, obc_ref, ofl_ref, obl_ref,
                gb_ref, oc_ref, g0_ref, g1_ref, g2_ref,
                dn_ref, n2_ref, wpa_ref, wpb_ref, wpc_ref, wout_ref, wr_ref, br_ref,
                x1_ref, h2_ref, idx_ref, wt_ref):
    d = x_ref.shape[1]
    is_ctx = pl.program_id(0) * x_ref.shape[0] < n_ctx
    dn = dn_ref[...]
    o = jnp.where(is_ctx, ofc_ref[...] + obc_ref[...], ofl_ref[...] + obl_ref[...])
    o_a = jnp.where(is_ctx, oac_ref[...], oal_ref[...])
    gate = _silu(gb_ref[...])
    parts = []
    for h in range(N_HEADS_B):
        sl = slice(h * DV_B, (h + 1) * DV_B)
        parts.append((_rms(o[:, sl]) * dn * gate[:, sl]).astype(BF16))
    o_b = jnp.concatenate(parts, axis=-1)
    mixed = (_sigmoid(g0_ref[...]) * _dot(o_a, wpa_ref[...])
             + _sigmoid(g1_ref[...]) * _dot(o_b, wpb_ref[...])
             + _sigmoid(g2_ref[...]) * _dot(oc_ref[...], wpc_ref[...]))
    gate1 = mod_ref[:, 2 * d:3 * d]
    x1 = x_ref[...] + gate1 * _dot(mixed.astype(BF16), wout_ref[...])
    x1_ref[...] = x1
    shift2 = mod_ref[:, 3 * d:4 * d]
    scale2 = mod_ref[:, 4 * d:5 * d]
    h2 = _rms(x1) * n2_ref[...] * (1.0 + scale2) + shift2
    h2_ref[...] = h2
    sc = _sigmoid(_dot_nt(wr_ref[...], h2, HIGHEST))
    sel = sc + br_ref[...]
    i0, i1, w0, w1 = _route_rows([sel[e:e + 1] for e in range(N_EXPERTS)],
                                 [sc[e:e + 1] for e in range(N_EXPERTS)])
    idx_ref[0:1, :] = i0
    idx_ref[1:2, :] = i1
    wt_ref[0:1, :] = w0
    wt_ref[1:2, :] = w1


def _mix(x, mod_l, oa, delta_o, z, oc, delta_norm_l, norm2_l, wpa, wpb, wpc, wout, wr_t, br, cond_of_row):
    m, d = x.shape
    r = ROW_TILE
    n_ctx = cond_of_row.n_ctx
    nct = n_ctx // r
    c_gb = OFF_GATE_B // W_B
    once = dict(pipeline_mode=pl.Buffered(1))
    full = lambda shape: pl.BlockSpec(shape, lambda i: (0,) * len(shape), **once)
    ctx_rows = lambda w: pl.BlockSpec((r, w), lambda i: (jnp.minimum(i, nct - 1), 0))
    lat_rows = lambda w: pl.BlockSpec((r, w), lambda i: (jnp.maximum(i - nct, 0), 0))
    of_ctx, ob_ctx, of_lat, ob_lat = delta_o
    return pl.pallas_call(
        functools.partial(_mix_kernel, n_ctx),
        grid=(m // r,),
        in_specs=[
            pl.BlockSpec((r, d), lambda i: (i, 0)),
            pl.BlockSpec((None, 1, 6 * d), lambda i: (cond_of_row(i * r), 0, 0)),
            ctx_rows(W_A), lat_rows(W_A),
            ctx_rows(W_B), ctx_rows(W_B), lat_rows(W_B), lat_rows(W_B),
            pl.BlockSpec((r, W_B), lambda i: (i, c_gb)),
            pl.BlockSpec((r, W_C), lambda i: (i, 0)),
            pl.BlockSpec((r, d), lambda i: (i, 0)),
            pl.BlockSpec((r, d), lambda i: (i, 1)),
            pl.BlockSpec((r, d), lambda i: (i, 2)),
            full((1, DV_B)), full((1, d)),
            full((W_A, d)), full((W_B, d)), full((W_C, d)), full((d, d)),
            full((N_EXPERTS, d)), full((N_EXPERTS, 1)),
        ],
        out_specs=[
            pl.BlockSpec((r, d), lambda i: (i, 0)),
            pl.BlockSpec((r, d), lambda i: (i, 0)),
            pl.BlockSpec((TOP_K, r), lambda i: (0, i)),
            pl.BlockSpec((TOP_K, r), lambda i: (0, i)),
        ],
        out_shape=[
            jax.ShapeDtypeStruct((m, d), F32),
            jax.ShapeDtypeStruct((m, d), F32),
            jax.ShapeDtypeStruct((TOP_K, m), jnp.int32),
            jax.ShapeDtypeStruct((TOP_K, m), F32),
        ],
        compiler_params=_cparams(1),
        name="mix_route",
    )(x, mod_l, oa[0], oa[1], of_ctx, ob_ctx, of_lat, ob_lat, z, oc, z, z, z,
      delta_norm_l.reshape(1, DV_B), norm2_l.reshape(1, d), wpa, wpb, wpc, wout, wr_t, br)


def _dispatch_meta(idx_t, bm):
    m = idx_t.shape[1]
    nk = m * TOP_K
    flat_e = idx_t.T.reshape(nk)
    onehot = (flat_e[:, None] == jnp.arange(N_EXPERTS, dtype=jnp.int32)[None, :]).astype(jnp.int32)
    csum = jnp.cumsum(onehot, axis=0)
    rank = jnp.sum((csum - onehot) * onehot, axis=1)
    counts = csum[-1]
    padded = (counts + bm - 1) // bm * bm
    pad_end = jnp.cumsum(padded)
    pad_start = pad_end - padded
    dest = jnp.sum(onehot * pad_start[None, :], axis=1) + rank
    nb = -(-(nk + N_EXPERTS * (bm - 1)) // bm)
    row_tok = jnp.zeros((nb * bm,), jnp.int32).at[dest].set(jnp.arange(nk, dtype=jnp.int32) // TOP_K)
    n_used = (pad_end[-1] // bm).astype(jnp.int32)
    blk_e = jnp.minimum(jnp.searchsorted(pad_end, jnp.arange(nb, dtype=jnp.int32) * bm, side="right"),
                        N_EXPERTS - 1).astype(jnp.int32)
    return blk_e, n_used.reshape(1), row_tok, dest.astype(jnp.int32)


def _moe_kernel(blk_e_ref, n_used_ref, row_tok_ref, h2_hbm, wg_ref, wu_ref, wd_ref, y_ref, xbuf, sem):
    del blk_e_ref
    i = pl.program_id(0)
    bm = xbuf.shape[1]
    n_used = n_used_ref[0]

    def gather(blk, slot):
        def body(r, carry):
            tok = row_tok_ref[blk * bm + r]
            pltpu.make_async_copy(h2_hbm.at[pl.ds(tok, 1)], xbuf.at[slot, pl.ds(r, 1)], sem.at[slot]).start()
            return carry
        lax.fori_loop(0, bm, body, 0, unroll=8)

    @pl.when((i == 0) & (n_used > 0))
    def _():
        gather(0, 0)

    @pl.when(i + 1 < n_used)
    def _():
        gather(i + 1, (i + 1) % 2)

    @pl.when(i < n_used)
    def _():
        slot = i % 2
        pltpu.make_async_copy(h2_hbm.at[pl.ds(0, bm)], xbuf.at[slot], sem.at[slot]).wait()
        xb = xbuf[slot].astype(BF16)
        act = _silu(_dot(xb, wg_ref[...])) * _dot(xb, wu_ref[...])
        y_ref[...] = _dot(act.astype(BF16), wd_ref[...])

    @pl.when(i >= n_used)
    def _():
        y_ref[...] = jnp.zeros_like(y_ref)


def _moe(h2, blk_e, n_used, row_tok, wg, wu, wd):
    m, d = h2.shape
    bm = MOE_ROWS
    nb = row_tok.shape[0] // bm
    f = wg.shape[2]
    return pl.pallas_call(
        _moe_kernel,
        grid_spec=pltpu.PrefetchScalarGridSpec(
            num_scalar_prefetch=3,
            grid=(nb,),
            in_specs=[
                pl.BlockSpec(memory_space=pl.ANY),
                pl.BlockSpec((None, d, f), lambda i, be, nu, rt: (be[i], 0, 0)),
                pl.BlockSpec((None, d, f), lambda i, be, nu, rt: (be[i], 0, 0)),
                pl.BlockSpec((None, f, d), lambda i, be, nu, rt: (be[i], 0, 0)),
            ],
            out_specs=pl.BlockSpec((bm, d), lambda i, be, nu, rt: (i, 0)),
            scratch_shapes=[pltpu.VMEM((2, bm, d), F32), pltpu.SemaphoreType.DMA((2,))],
        ),
        out_shape=jax.ShapeDtypeStruct((nb * bm, d), F32),
        compiler_params=_cparams(1),
        name="moe_experts",
    )(blk_e, n_used, row_tok, h2, wg, wu, wd)


def _combine_kernel(final, pos_ref, yb_hbm, x1_ref, mod_ref, w_ref, nf_ref, o_ref, buf, sem):
    i = pl.program_id(0)
    n = pl.num_programs(0)
    r = x1_ref.shape[0]
    d = x1_ref.shape[1]

    def gather(tile, slot):
        def body(j, carry):
            for k in range(TOP_K):
                p = pos_ref[(tile * r + j) * TOP_K + k]
                pltpu.make_async_copy(yb_hbm.at[pl.ds(p, 1)], buf.at[slot, k, pl.ds(j, 1)], sem.at[slot]).start()
            return carry
        lax.fori_loop(0, r, body, 0, unroll=4)

    @pl.when(i == 0)
    def _():
        gather(0, 0)

    @pl.when(i + 1 < n)
    def _():
        gather(i + 1, (i + 1) % 2)

    slot = i % 2
    for k in range(TOP_K):
        pltpu.make_async_copy(yb_hbm.at[pl.ds(0, r)], buf.at[slot, k], sem.at[slot]).wait()
    f = w_ref[:, 0:1] * buf[slot, 0] + w_ref[:, 1:2] * buf[slot, 1]
    x2 = x1_ref[...] + mod_ref[:, 5 * d:6 * d] * f
    if final:
        x2 = _rms(x2) * nf_ref[...]
    o_ref[...] = x2


def _combine(yb, x1, mod_l, w_col, pos, norm_final, final, cond_of_row):
    m, d = x1.shape
    r = ROW_TILE
    return pl.pallas_call(
        functools.partial(_combine_kernel, final),
        grid_spec=pltpu.PrefetchScalarGridSpec(
            num_scalar_prefetch=1,
            grid=(m // r,),
            in_specs=[
                pl.BlockSpec(memory_space=pl.ANY),
                pl.BlockSpec((r, d), lambda i, p: (i, 0)),
                pl.BlockSpec((None, 1, 6 * d), lambda i, p: (cond_of_row(i * r), 0, 0)),
                pl.BlockSpec((r, TOP_K), lambda i, p: (i, 0)),
                pl.BlockSpec((1, d), lambda i, p: (0, 0)),
            ],
            out_specs=pl.BlockSpec((r, d), lambda i, p: (i, 0)),
            scratch_shapes=[pltpu.VMEM((2, TOP_K, r, d), F32), pltpu.SemaphoreType.DMA((2,))],
        ),
        out_shape=jax.ShapeDtypeStruct((m, d), F32),
        compiler_params=_cparams(1),
        name="moe_combine",
    )(pos, yb, x1, mod_l, w_col, norm_final.reshape(1, d))


class _CondOfRow:
    def __init__(self, n_ctx, t_lat):
        self.n_ctx, self.t_lat = n_ctx, t_lat
        self.tile_in = _pick_tile((1024, 512, 256), n_ctx, t_lat)

    def __call__(self, row0):
        return jnp.where(row0 < self.n_ctx, 0, 1 + (row0 - self.n_ctx) // self.t_lat)


def _prep_w_in(w):
    d = w.shape[0]
    cols = [w[:, SRC_BRANCH:], w[:, SRC_Q:SRC_QKV_B], w[:, SRC_QKV_B:SRC_BA], w[:, SRC_POOL:SRC_BRANCH],
            w[:, SRC_BA:SRC_POOL]]
    used = OFF_BA + (SRC_POOL - SRC_BA)
    cols.append(jnp.zeros((d, NP_IN - used), w.dtype))
    return jnp.concatenate(cols, axis=1).astype(BF16)


def kernel(x_prompt, x_sample, cache_k, cache_v, state_delta, c, c_ctx, w_mod, b_mod, norm1, norm2, w_in,
           lambda_qk, subln, conv_w, a_log, dt_bias, delta_norm, pool_w, pool_scale, w_pa, w_pb, w_pc,
           w_out, w_router, b_router, w_gate, w_up, w_down, norm_final):
    b_ctx, t_ctx, d = x_prompt.shape
    b_lat, t_lat, _ = x_sample.shape
    depth = w_in.shape[0]
    past = cache_k.shape[2]
    n_ctx, n_lat = b_ctx * t_ctx, b_lat * t_lat
    m = n_ctx + n_lat
    assert d == D_MODEL and w_in.shape[2] == D_IN and 1 + b_lat <= SUBLANES
    assert t_ctx % ROW_TILE == 0 and t_lat % ROW_TILE == 0 and n_ctx % t_lat == 0
    cond_of_row = _CondOfRow(n_ctx, t_lat)
    dims = (b_ctx, t_ctx, b_lat, t_lat)

    def seq(row0):
        is_ctx = row0 < n_ctx
        return (jnp.where(is_ctx, row0 % t_ctx, (row0 - n_ctx) % t_lat), jnp.where(is_ctx, t_ctx, t_lat))

    x = jnp.concatenate([x_prompt.reshape(n_ctx, d), x_sample.reshape(n_lat, d)], axis=0)
    cond8 = jnp.zeros((SUBLANES, d), F32).at[0].set(c_ctx).at[1:1 + b_lat].set(c)
    mod = _modulation(cond8, w_mod, b_mod).reshape(depth, SUBLANES, 1, 6 * d)
    rope_tabs = _rope_tables(t_lat)
    wr_t = w_router.T
    br = b_router.reshape(N_EXPERTS, 1)
    ck = cache_k.reshape(b_lat, depth, past, QK_A)
    cv = cache_v.reshape(b_lat, depth, past, W_A)
    s_lat = state_delta.reshape(b_lat, depth, 2 * N_HEADS_B, DK_B, DV_B)

    new_k, new_v, new_s = [], [], []
    for l in range(depth):
        lam_init = 0.8 - 0.6 * math.exp(-0.3 * l)
        z = _in_proj(x, mod[l], norm1[l], _prep_w_in(w_in[l]), cond_of_row)
        new_k.append(z[:n_ctx, OFF_K:OFF_K + QK_A].reshape(b_ctx, t_ctx, N_HEADS_A, 2 * DK_A))
        new_v.append(z[:n_ctx, OFF_V:OFF_V + W_A].reshape(b_ctx, t_ctx, N_HEADS_A, DV_A))

        oa = _attention(z, lambda_qk[l], subln[l], ck[:, l], cv[:, l], rope_tabs, lam_init, dims)
        bat = z[:, OFF_BA:OFF_BA + 4 * N_HEADS_B].T
        oc, u, l1, l2, e = _local(z, bat, conv_w[l], pool_w[l], pool_scale[l], a_log[l], dt_bias[l], seq)
        *delta_o, s_ctx = _delta_scan(u, l1, l2, e, s_lat[:, l], dims)
        new_s.append(s_ctx.reshape(b_ctx, 2, N_HEADS_B, DK_B, DV_B))

        x1, h2, idx_t, wt_t = _mix(x, mod[l], oa, delta_o, z, oc, delta_norm[l], norm2[l],
                                   w_pa[l].astype(BF16), w_pb[l].astype(BF16), w_pc[l].astype(BF16),
                                   w_out[l].astype(BF16), wr_t, br, cond_of_row)
        blk_e, n_used, row_tok, pos = _dispatch_meta(idx_t, MOE_ROWS)
        yb = _moe(h2, blk_e, n_used, row_tok, w_gate[l].astype(BF16), w_up[l].astype(BF16),
                  w_down[l].astype(BF16))
        x = _combine(yb, x1, mod[l], wt_t.T, pos, norm_final, l == depth - 1, cond_of_row)

    y_prompt = x[:n_ctx].reshape(b_ctx, t_ctx, d)
    y_sample = x[n_ctx:].reshape(b_lat, t_lat, d)
    return (y_prompt, y_sample, jnp.stack(new_k, axis=1), jnp.stack(new_v, axis=1), jnp.stack(new_s, axis=1))
```

```python
import functools
import math

import jax
import jax.numpy as jnp
from jax import lax
from jax.experimental import pallas as pl
from jax.experimental.pallas import tpu as pltpu

F32 = jnp.float32
BF16 = jnp.bfloat16
HIGHEST = lax.Precision.HIGHEST

D_MODEL = 2048
GRID_W = 64
EPS = 1e-6
N_HEADS_A = 8
DK_A = 64
DV_A = 128
QK_A = N_HEADS_A * 2 * DK_A
W_A = N_HEADS_A * DV_A
ROPE_BASE = 10000.0
ROPE_AXIS = DK_A // 2
N_HEADS_B = 4
DK_B = 128
DV_B = 128
W_B = N_HEADS_B * DV_B
DELTA_CHUNK = 64
POOL_WINDOWS = (2, 4, 8, 16)
POOL_GROUP_W = 128
W_C = len(POOL_WINDOWS) * POOL_GROUP_W
N_BRANCH = 3
N_EXPERTS = 16
N_EXPERT_GROUPS = 4
EXPERTS_PER_GROUP = N_EXPERTS // N_EXPERT_GROUPS
TOP_K = 2
D_FF = 1408

SRC_QKV_B = 2 * QK_A + W_A
SRC_GATE_B = SRC_QKV_B + 3 * W_B
SRC_BA = SRC_GATE_B + W_B
SRC_POOL = SRC_BA + 4 * N_HEADS_B
SRC_BRANCH = SRC_POOL + W_C
D_IN = SRC_BRANCH + N_BRANCH * D_MODEL
LANES = 128
SUBLANES = 8
IN_TILE_N = 1024
IN_HALF_N = IN_TILE_N // 2
OFF_RAW = N_BRANCH * D_MODEL
RAW_W = -(-SRC_BRANCH // IN_TILE_N) * IN_TILE_N
NP_IN = OFF_RAW + RAW_W
OFF_Q = OFF_RAW
OFF_K = OFF_Q + QK_A
OFF_V = OFF_K + QK_A
OFF_QKV_B = OFF_RAW + SRC_QKV_B
OFF_GATE_B = OFF_RAW + SRC_GATE_B
OFF_BA = OFF_RAW + SRC_BA
BRANCH_SHIFT = SRC_BRANCH % LANES
assert SRC_BA % IN_TILE_N == 0 and (SRC_BRANCH - BRANCH_SHIFT) % IN_HALF_N == 0
assert SRC_POOL - SRC_BA == BRANCH_SHIFT and OFF_RAW % IN_TILE_N == 0

ROW_TILE = 256
MOE_ROWS = 256
INV_BASE_BLOCK = 16
VMEM_LIMIT = 56 * 1024 * 1024


def _cparams(n_axes, vmem=VMEM_LIMIT):
    return pltpu.CompilerParams(dimension_semantics=("arbitrary",) * n_axes, vmem_limit_bytes=vmem)


def _dot(a, b, precision=None):
    return jnp.dot(a, b, preferred_element_type=F32, precision=precision)


def _dot_nt(a, b, precision=None):
    return lax.dot_general(a, b, (((1,), (1,)), ((), ())), preferred_element_type=F32, precision=precision)


def _bdot(a, b):
    return _dot(a.astype(BF16), b.astype(BF16))


def _bdot_nt(a, b):
    return _dot_nt(a.astype(BF16), b.astype(BF16))


def _sigmoid(x):
    return 1.0 / (1.0 + jnp.exp(-x))


def _silu(x):
    return x * _sigmoid(x)


def _rms(x, eps=EPS):
    return x * lax.rsqrt(jnp.mean(x * x, axis=-1, keepdims=True) + eps)


def _pick_tile(cands, *dims):
    for t in cands:
        if all(d % t == 0 for d in dims):
            return t
    raise ValueError(f"no tile in {cands} divides {dims}")


def _mod_kernel(c_ref, w_ref, b_ref, o_ref):
    a = _silu(c_ref[...]).astype(BF16)
    o_ref[...] = _dot(a, w_ref[...].astype(BF16)) + b_ref[...]


def _modulation(cond8, w_mod, b_mod):
    depth, d, n6 = w_mod.shape
    tn = 1024
    return pl.pallas_call(
        _mod_kernel,
        grid=(depth, n6 // tn),
        in_specs=[
            pl.BlockSpec((SUBLANES, d), lambda l, n: (0, 0)),
            pl.BlockSpec((None, d, tn), lambda l, n: (l, 0, n)),
            pl.BlockSpec((None, 1, tn), lambda l, n: (l, 0, n)),
        ],
        out_specs=pl.BlockSpec((None, SUBLANES, tn), lambda l, n: (l, 0, n)),
        out_shape=jax.ShapeDtypeStruct((depth, SUBLANES, n6), F32),
        compiler_params=_cparams(2),
        name="adaln_mod",
    )(cond8, w_mod, b_mod.reshape(depth, 1, n6))


def _modulated_norm(x, g, mod, d):
    return _rms(x) * g * (1.0 + mod[:, d:2 * d]) + mod[:, 0:d]


def _norm_kernel(x_ref, mod_ref, g_ref, h_ref):
    h_ref[...] = _modulated_norm(x_ref[...], g_ref[...], mod_ref[...], x_ref.shape[1]).astype(BF16)


def _norm_mod(x, mod_l, norm_l, cond_of_row):
    m, d = x.shape
    r = ROW_TILE
    return pl.pallas_call(
        _norm_kernel,
        grid=(m // r,),
        in_specs=[
            pl.BlockSpec((r, d), lambda i: (i, 0)),
            pl.BlockSpec((None, 1, 6 * d), lambda i: (cond_of_row(i * r), 0, 0)),
            pl.BlockSpec((1, d), lambda i: (0, 0)),
        ],
        out_specs=pl.BlockSpec((r, d), lambda i: (i, 0)),
        out_shape=jax.ShapeDtypeStruct((m, d), BF16),
        compiler_params=_cparams(1),
        name="norm_mod",
    )(x, mod_l, norm_l.reshape(1, d))


IN_CAST_ROWS = 256


def _in_kernel(h_ref, wa_ref, wb_ref, wc_ref, z_ref, w_scr):
    n = pl.program_id(0)
    first_row_tile = pl.program_id(1) == 0
    n_branch_tiles = OFF_RAW // IN_TILE_N

    @pl.when(first_row_tile & (n < n_branch_tiles))
    def _():
        def body(c, carry):
            rs = pl.ds(pl.multiple_of(c * IN_CAST_ROWS, IN_CAST_ROWS), IN_CAST_ROWS)
            win = jnp.concatenate([wa_ref[rs, :], wb_ref[rs, :], wc_ref[rs, :]], axis=1)
            w_scr[rs, :] = pltpu.roll(win, win.shape[1] - BRANCH_SHIFT, 1)[:, :IN_TILE_N].astype(BF16)
            return carry
        lax.fori_loop(0, w_scr.shape[0] // IN_CAST_ROWS, body, 0)

    @pl.when(first_row_tile & (n >= n_branch_tiles))
    def _():
        w_scr[:, :IN_HALF_N] = wa_ref[...].astype(BF16)
        w_scr[:, IN_HALF_N:] = wb_ref[...].astype(BF16)

    z_ref[...] = _dot(h_ref[...], w_scr[...])


def _in_proj(h, w_in_l, tm):
    m, d = h.shape
    n_branch_tiles = OFF_RAW // IN_TILE_N
    half0 = (SRC_BRANCH - BRANCH_SHIFT) // IN_HALF_N
    lane0 = (SRC_BRANCH - BRANCH_SHIFT) // LANES
    per_tile = IN_TILE_N // LANES
    a_of = lambda n: jnp.where(n < n_branch_tiles, half0 + 2 * n, 2 * (n - n_branch_tiles))
    c_of = lambda n: jnp.where(n < n_branch_tiles, lane0 + per_tile * (n + 1), 0)
    return pl.pallas_call(
        _in_kernel,
        grid=(NP_IN // IN_TILE_N, m // tm),
        in_specs=[
            pl.BlockSpec((tm, d), lambda n, i: (i, 0)),
            pl.BlockSpec((d, IN_HALF_N), lambda n, i: (0, a_of(n))),
            pl.BlockSpec((d, IN_HALF_N), lambda n, i: (0, a_of(n) + 1)),
            pl.BlockSpec((d, LANES), lambda n, i: (0, c_of(n))),
        ],
        out_specs=pl.BlockSpec((tm, IN_TILE_N), lambda n, i: (i, n)),
        out_shape=jax.ShapeDtypeStruct((m, NP_IN), F32),
        scratch_shapes=[pltpu.VMEM((d, IN_TILE_N), BF16)],
        compiler_params=_cparams(2),
        name="in_proj",
    )(h, w_in_l, w_in_l, w_in_l)


def _lambda(lq_ref, lam_init):
    lq = lq_ref[...]
    a = jnp.sum(lq[0:1] * lq[1:2], axis=-1, keepdims=True)
    b = jnp.sum(lq[2:3] * lq[3:4], axis=-1, keepdims=True)
    return jnp.exp(a) - jnp.exp(b) + lam_init


def _diff_attn(qb, kb, vb, lam):
    probs = []
    for mp in range(2):
        s = _dot_nt(qb[:, mp * DK_A:(mp + 1) * DK_A], kb[:, mp * DK_A:(mp + 1) * DK_A])
        e = jnp.exp(s - jnp.max(s, axis=-1, keepdims=True))
        probs.append(e / jnp.sum(e, axis=-1, keepdims=True))
    pd = probs[0] - lam * probs[1]
    return _dot(pd.astype(BF16), vb)


def _attn_ctx_kernel(lam_init, lq_ref, sub_ref, q_ref, k_ref, v_ref, o_ref):
    lam = _lambda(lq_ref, lam_init)
    qb = (q_ref[...] * DK_A ** -0.5).astype(BF16)
    o = _diff_attn(qb, k_ref[...].astype(BF16), v_ref[...].astype(BF16), lam)
    o_ref[...] = (_rms(o) * sub_ref[...] * (1.0 - lam_init)).astype(BF16)


def _rope(x, cos, sin_signed):
    lane = lax.broadcasted_iota(jnp.int32, x.shape, 1)
    first_half = (lane % (ROPE_AXIS)) < (ROPE_AXIS // 2)
    partner = jnp.where(first_half, pltpu.roll(x, LANES - ROPE_AXIS // 2, 1), pltpu.roll(x, ROPE_AXIS // 2, 1))
    return x * cos + partner * sin_signed


def _attn_lat_kernel(lam_init, t_lat, lq_ref, sub_ref, q_ref, k_ref, v_ref, ck_ref, cv_ref,
                     cq_ref, sq_ref, ckk_ref, skk_ref, o_ref, k_scr, v_scr):
    @pl.when(pl.program_id(2) == 0)
    def _():
        k_scr[0:t_lat, :] = _rope(k_ref[...], ckk_ref[...], skk_ref[...]).astype(BF16)
        k_scr[t_lat:, :] = ck_ref[...].astype(BF16)
        v_scr[0:t_lat, :] = v_ref[...].astype(BF16)
        v_scr[t_lat:, :] = cv_ref[...].astype(BF16)

    lam = _lambda(lq_ref, lam_init)
    q = _rope(q_ref[...], cq_ref[...], sq_ref[...]) * DK_A ** -0.5
    o = _diff_attn(q.astype(BF16), k_scr[...], v_scr[...], lam)
    o_ref[...] = (_rms(o) * sub_ref[...] * (1.0 - lam_init)).astype(BF16)


def _attention(z, lq_l, sub_l, cache_k_l, cache_v_l, rope_tabs, lam_init, dims):
    b_ctx, t_ctx, b_lat, t_lat = dims
    n_ctx = b_ctx * t_ctx
    past = cache_k_l.shape[1]
    cq, ck, cv = OFF_Q // LANES, OFF_K // LANES, OFF_V // LANES
    small = [pl.BlockSpec((4, DK_A), lambda *_: (0, 0)), pl.BlockSpec((1, DV_A), lambda *_: (0, 0))]
    sub2 = sub_l.reshape(1, DV_A)

    oa_ctx = pl.pallas_call(
        functools.partial(_attn_ctx_kernel, lam_init),
        grid=(b_ctx, N_HEADS_A),
        in_specs=small + [
            pl.BlockSpec((t_ctx, LANES), lambda b, h: (b, cq + h)),
            pl.BlockSpec((t_ctx, LANES), lambda b, h: (b, ck + h)),
            pl.BlockSpec((t_ctx, LANES), lambda b, h: (b, cv + h)),
        ],
        out_specs=pl.BlockSpec((t_ctx, LANES), lambda b, h: (b, h)),
        out_shape=jax.ShapeDtypeStruct((n_ctx, W_A), BF16),
        compiler_params=_cparams(2),
        name="attn_ctx",
    )(lq_l, sub2, z, z, z)

    tq = _pick_tile((256, 128), t_lat)
    nq = t_lat // tq
    rb_q = n_ctx // tq
    rb_k = n_ctx // t_lat
    cos_t, sin_t = rope_tabs
    oa_lat = pl.pallas_call(
        functools.partial(_attn_lat_kernel, lam_init, t_lat),
        grid=(b_lat, N_HEADS_A, nq),
        in_specs=small + [
            pl.BlockSpec((tq, LANES), lambda b, h, i: (rb_q + b * nq + i, cq + h)),
            pl.BlockSpec((t_lat, LANES), lambda b, h, i: (rb_k + b, ck + h)),
            pl.BlockSpec((t_lat, LANES), lambda b, h, i: (rb_k + b, cv + h)),
            pl.BlockSpec((None, past, LANES), lambda b, h, i: (b, 0, h)),
            pl.BlockSpec((None, past, LANES), lambda b, h, i: (b, 0, h)),
            pl.BlockSpec((tq, LANES), lambda b, h, i: (i, 0)),
            pl.BlockSpec((tq, LANES), lambda b, h, i: (i, 0)),
            pl.BlockSpec((t_lat, LANES), lambda b, h, i: (0, 0)),
            pl.BlockSpec((t_lat, LANES), lambda b, h, i: (0, 0)),
        ],
        out_specs=pl.BlockSpec((tq, LANES), lambda b, h, i: (b * nq + i, h)),
        out_shape=jax.ShapeDtypeStruct((b_lat * t_lat, W_A), BF16),
        scratch_shapes=[pltpu.VMEM((t_lat + past, LANES), BF16), pltpu.VMEM((t_lat + past, LANES), BF16)],
        compiler_params=_cparams(3),
        name="attn_lat",
    )(lq_l, sub2, z, z, z, cache_k_l, cache_v_l, cos_t, sin_t, cos_t, sin_t)
    return oa_ctx, oa_lat


def _rope_tables(t_lat):
    rows = t_lat // GRID_W
    r = jnp.repeat(jnp.arange(rows), GRID_W).astype(F32)
    col = jnp.tile(jnp.arange(GRID_W), rows).astype(F32)
    inv = ROPE_BASE ** (-jnp.arange(0, ROPE_AXIS, 2, dtype=F32) / ROPE_AXIS)
    ang_r, ang_c = r[:, None] * inv, col[:, None] * inv
    cos64 = jnp.concatenate([jnp.cos(ang_r)] * 2 + [jnp.cos(ang_c)] * 2, axis=-1)
    sin64 = jnp.concatenate([-jnp.sin(ang_r), jnp.sin(ang_r), -jnp.sin(ang_c), jnp.sin(ang_c)], axis=-1)
    return jnp.tile(cos64, (1, 2)), jnp.tile(sin64, (1, 2))


def _log_decay(alpha, a_log, dt):
    x = alpha + dt
    return -jnp.exp(a_log) * (jnp.maximum(x, 0.0) + jnp.log(1.0 + jnp.exp(-jnp.abs(x))))


def _chunk_cumsum(g, axis):
    n = g.shape[axis]
    pos = lax.broadcasted_iota(jnp.int32, g.shape, axis) % DELTA_CHUNK
    f, b = g, g
    s = 1
    while s < DELTA_CHUNK:
        f = f + jnp.where(pos >= s, pltpu.roll(f, s, axis), 0.0)
        b = b + jnp.where(pos < DELTA_CHUNK - s, pltpu.roll(b, n - s, axis), 0.0)
        s *= 2
    return f, b


def _delta_prepare(q_scr, k_scr, v_scr, ba, bat, al_row, al_col, dt_row, dt_col,
                   u_ref, l1_ref, l2_ref, e_ref):
    c, nh = DELTA_CHUNK, N_HEADS_B
    rows_n = q_scr.shape[0]
    beta_all = _sigmoid(ba[:, 0:2 * nh])
    gf, gb = _chunk_cumsum(_log_decay(ba[:, 2 * nh:4 * nh], al_row[...], dt_row[...]), 0)
    gc_all = jnp.where(lax.broadcasted_iota(jnp.int32, gf.shape, 1) < nh, gf, gb)
    gtf, gtb = _chunk_cumsum(_log_decay(bat[2 * nh:4 * nh, :], al_col[...], dt_col[...]), 1)
    gct_all = jnp.where(lax.broadcasted_iota(jnp.int32, gtf.shape, 0) < nh, gtf, gtb)

    ri = lax.broadcasted_iota(jnp.int32, (c, c), 0)
    ci = lax.broadcasted_iota(jnp.int32, (c, c), 1)
    masks = ((ri >= ci, ri > ci), (ri <= ci, ri < ci))
    eye = (ri == ci).astype(F32)
    blk_sizes = [INV_BASE_BLOCK << i for i in range(int(math.log2(c // INV_BASE_BLOCK)) + 1)]
    same_blk = [(ri // s) == (ci // s) for s in blk_sizes]
    n_sq = int(math.log2(INV_BASE_BLOCK)) - 1

    for j in range(rows_n // c):
        rs = slice(j * c, (j + 1) * c)
        chains = []
        for d in range(2):
            incl, strict = masks[d]
            last = 0 if d else c - 1
            for h in range(nh):
                idx = d * nh + h
                hs = slice(h * DK_B, (h + 1) * DK_B)
                gc = gc_all[rs, idx:idx + 1]
                gr = gct_all[idx:idx + 1, rs]
                ch = dict(d=d, h=h, idx=idx, hs=hs, strict=strict, gc=gc, g_last=gc[last:last + 1, :],
                          q=q_scr[rs, hs] * DK_B ** -0.5, k=k_scr[rs, hs], v=v_scr[rs, hs],
                          beta=beta_all[rs, idx:idx + 1],
                          decay=jnp.where(incl, jnp.exp(jnp.where(incl, gc - gr, 0.0)), 0.0))
                ch["kb"] = ch["k"] * ch["beta"]
                chains.append(ch)
        for ch in chains:
            kbf = ch["k"].astype(BF16)
            a = jnp.where(ch["strict"], -(_dot_nt(ch["kb"].astype(BF16), kbf) * ch["decay"]), 0.0)
            ch["attn"] = _dot_nt(ch["q"].astype(BF16), kbf) * ch["decay"]
            ch["a"] = a
            ch["apow"] = jnp.where(same_blk[0], a, 0.0)
            ch["p"] = eye + ch["apow"]
        for _ in range(n_sq):
            for ch in chains:
                ch["apow"] = _bdot(ch["apow"], ch["apow"])
            for ch in chains:
                ch["p"] = ch["p"] + _bdot(ch["p"], ch["apow"])
        for lvl in range(1, len(same_blk)):
            for ch in chains:
                a_off = jnp.where(same_blk[lvl] & jnp.logical_not(same_blk[lvl - 1]), ch["a"], 0.0)
                ch["t"] = _bdot(ch["p"], a_off)
            for ch in chains:
                ch["p"] = ch["p"] + _bdot(ch["t"], ch["p"])
        for ch in chains:
            rhs = jnp.concatenate([ch["v"] * ch["beta"], ch["kb"] * jnp.exp(ch["gc"])], axis=-1)
            ch["sol"] = _bdot(ch["p"], rhs)
        for ch in chains:
            d, h, hs = ch["d"], ch["h"], ch["hs"]
            u_ref[d, rs, hs] = ch["sol"][:, :DV_B]
            l1_ref[d, j, h, 0:c, :] = ch["sol"][:, DV_B:].astype(BF16)
            l1_ref[d, j, h, c:2 * c, :] = (ch["q"] * jnp.exp(ch["gc"])).astype(BF16)
            l2_ref[d, j, h, 0:c, :] = ch["attn"].astype(BF16)
            l2_ref[d, j, h, c:, :] = (ch["k"] * jnp.exp(ch["g_last"] - ch["gc"])).T.astype(BF16)
            e_ref[j, ch["idx"]:ch["idx"] + 1, :] = jnp.broadcast_to(jnp.exp(ch["g_last"]), (1, LANES))


def _local_kernel(seq, cur_ref, prev_ref, next_ref, pcur_ref, pprev_ref, pnext_ref, bat_ref,
                  cw_ref, pw_ref, ps_ref, al_row, al_col, dt_row, dt_col,
                  oc_ref, u_ref, l1_ref, l2_ref, e_ref, q_scr, k_scr, v_scr):
    rows_n = cur_ref.shape[0]
    pos0, t_seq = seq(pl.program_id(0) * rows_n)
    has_prev = pos0 > 0
    has_next = pos0 + rows_n < t_seq
    rows = lax.broadcasted_iota(jnp.int32, (rows_n, 1), 0)

    x = cur_ref[...]
    prev_row = jnp.where(has_prev, prev_ref[SUBLANES - 1:SUBLANES, :], 0.0)
    next_row = jnp.where(has_next, next_ref[0:1, :], 0.0)
    xm1 = jnp.where(rows == 0, prev_row, pltpu.roll(x, 1, 0))
    xp1 = jnp.where(rows == rows_n - 1, next_row, pltpu.roll(x, rows_n - 1, 0))
    y = _silu(cw_ref[0:1, :] * xm1 + cw_ref[1:2, :] * x + cw_ref[2:3, :] * xp1)
    for h in range(N_HEADS_B):
        for j, ref in enumerate((q_scr, k_scr)):
            a = y[:, j * W_B + h * DK_B:j * W_B + (h + 1) * DK_B]
            ref[:, h * DK_B:(h + 1) * DK_B] = a * lax.rsqrt(jnp.sum(a * a, axis=-1, keepdims=True) + EPS)
    v_scr[...] = y[:, 2 * W_B:]
    _delta_prepare(q_scr, k_scr, v_scr, pcur_ref[:, 0:LANES], bat_ref[...], al_row, al_col, dt_row, dt_col,
                   u_ref, l1_ref, l2_ref, e_ref)

    def pool_cols(ref):
        t = ref[...]
        return pltpu.roll(t, t.shape[1] - BRANCH_SHIFT, 1)[:, :W_C]

    pz = pool_cols(pcur_ref)
    ext = jnp.concatenate([jnp.where(has_prev, pool_cols(pprev_ref), 0.0), pz,
                           jnp.where(has_next, pool_cols(pnext_ref), 0.0)], axis=0)
    n_ext = rows_n + 2 * SUBLANES
    tpos = pos0 + rows
    for gi, win in enumerate(POOL_WINDOWS):
        sl = slice(gi * POOL_GROUP_W, (gi + 1) * POOL_GROUP_W)
        xg = ext[:, sl]
        acc = xg + pltpu.roll(xg, 1, 0)
        s = 1
        while 2 * s < win:
            acc = pltpu.roll(acc, s, 0) + pltpu.roll(acc, n_ext - s, 0)
            s *= 2
        acc = acc[SUBLANES:SUBLANES + rows_n]
        cnt = (jnp.minimum(tpos + win // 2, t_seq) - jnp.maximum(tpos - win // 2, 0)).astype(F32)
        d = acc / cnt - pz[:, sl]
        yg = _dot(d.astype(BF16), pw_ref[gi].astype(BF16)) * ps_ref[:, sl]
        oc_ref[:, sl] = yg.astype(BF16)


def _local(z, bat, conv_w_l, pool_w_l, pool_scale_l, a_log_l, dt_l, seq):
    m = z.shape[0]
    r, c = ROW_TILE, DELTA_CHUNK
    hb = r // SUBLANES
    last_hb = m // SUBLANES - 1
    c_qkv, c_pool = OFF_QKV_B // (3 * W_B), OFF_BA // IN_TILE_N
    nh2 = 2 * N_HEADS_B
    mc, cpt = m // c, r // c
    prev_map = lambda cb: (lambda i: (jnp.maximum(i * hb - 1, 0), cb))
    next_map = lambda cb: (lambda i: (jnp.minimum((i + 1) * hb, last_hb), cb))
    row = pl.BlockSpec((1, nh2), lambda i: (0, 0))
    col = pl.BlockSpec((nh2, 1), lambda i: (0, 0))
    return pl.pallas_call(
        functools.partial(_local_kernel, seq),
        grid=(m // r,),
        in_specs=[
            pl.BlockSpec((r, 3 * W_B), lambda i: (i, c_qkv)),
            pl.BlockSpec((SUBLANES, 3 * W_B), prev_map(c_qkv)),
            pl.BlockSpec((SUBLANES, 3 * W_B), next_map(c_qkv)),
            pl.BlockSpec((r, IN_TILE_N), lambda i: (i, c_pool)),
            pl.BlockSpec((SUBLANES, IN_TILE_N), prev_map(c_pool)),
            pl.BlockSpec((SUBLANES, IN_TILE_N), next_map(c_pool)),
            pl.BlockSpec((2 * nh2, r), lambda i: (0, i)),
            pl.BlockSpec((3, 3 * W_B), lambda i: (0, 0)),
            pl.BlockSpec((len(POOL_WINDOWS), POOL_GROUP_W, POOL_GROUP_W), lambda i: (0, 0, 0)),
            pl.BlockSpec((1, W_C), lambda i: (0, 0)),
            row, col, row, col,
        ],
        out_specs=[
            pl.BlockSpec((r, W_C), lambda i: (i, 0)),
            pl.BlockSpec((2, r, W_B), lambda i: (0, i, 0)),
            pl.BlockSpec((2, cpt, N_HEADS_B, 2 * c, DV_B), lambda i: (0, i, 0, 0, 0)),
            pl.BlockSpec((2, cpt, N_HEADS_B, c + DK_B, c), lambda i: (0, i, 0, 0, 0)),
            pl.BlockSpec((cpt, nh2, LANES), lambda i: (i, 0, 0)),
        ],
        out_shape=[
            jax.ShapeDtypeStruct((m, W_C), BF16),
            jax.ShapeDtypeStruct((2, m, W_B), F32),
            jax.ShapeDtypeStruct((2, mc, N_HEADS_B, 2 * c, DV_B), BF16),
            jax.ShapeDtypeStruct((2, mc, N_HEADS_B, c + DK_B, c), BF16),
            jax.ShapeDtypeStruct((mc, nh2, LANES), F32),
        ],
        scratch_shapes=[pltpu.VMEM((r, W_B), F32)] * 3,
        compiler_params=_cparams(1),
        name="local_conv_pool",
    )(z, z, z, z, z, z, bat, conv_w_l, pool_w_l, pool_scale_l.reshape(1, W_C),
      a_log_l.reshape(1, nh2), a_log_l.reshape(nh2, 1), dt_l.reshape(1, nh2), dt_l.reshape(nh2, 1))


def _scan_kernel(par, b_lat, n_ctx_ch, *refs):
    c, nh = DELTA_CHUNK, N_HEADS_B
    n_slot = par + b_lat
    ins = refs[:n_slot * 8]
    s0_ref = refs[n_slot * 8]
    of_ctx, ob_ctx, of_lat, ob_lat, sfin_ref, s_scr = refs[n_slot * 8 + 1:]
    step = pl.program_id(0)
    cc = step % n_ctx_ch

    @pl.when(cc == 0)
    def _():
        s_scr[0:par] = jnp.zeros((par,) + s_scr.shape[1:], F32)

    @pl.when(step == 0)
    def _():
        s_scr[par:] = s0_ref[...]

    for slot in range(n_slot):
        outs = (of_ctx, ob_ctx, slot) if slot < par else (of_lat, ob_lat, slot - par)
        chains = []
        for d in range(2):
            l1_ref, l2_ref, u_ref, e_ref = ins[(slot * 2 + d) * 4:(slot * 2 + d) * 4 + 4]
            for h in range(nh):
                chains.append(dict(d=d, h=h, idx=d * nh + h, l1=l1_ref, l2=l2_ref, u=u_ref, e=e_ref))
        for ch in chains:
            ch["s"] = s_scr[slot, ch["idx"]]
            ch["r1"] = _dot(ch["l1"][ch["h"]], ch["s"].astype(BF16))
        for ch in chains:
            hs = slice(ch["h"] * DV_B, (ch["h"] + 1) * DV_B)
            v_new = ch["u"][:, hs] - ch["r1"][:c]
            ch["r2"] = _dot(ch["l2"][ch["h"]], v_new.astype(BF16))
        for ch in chains:
            hs = slice(ch["h"] * DV_B, (ch["h"] + 1) * DV_B)
            outs[ch["d"]][outs[2], :, hs] = ch["r1"][c:] + ch["r2"][:c]
            e = ch["e"][ch["idx"]:ch["idx"] + 1, :]
            s_scr[slot, ch["idx"]] = ch["s"] * e + ch["r2"][c:]

    @pl.when(cc == n_ctx_ch - 1)
    def _():
        sfin_ref[...] = s_scr[0:par]


def _delta_scan(u, l1, l2, e, s0_lat, dims):
    b_ctx, t_ctx, b_lat, t_lat = dims
    c, nh = DELTA_CHUNK, N_HEADS_B
    n_ctx_ch, n_lat_ch = t_ctx // c, t_lat // c
    par = b_ctx * n_ctx_ch // n_lat_ch
    assert par >= 1 and par * n_lat_ch == b_ctx * n_ctx_ch and b_ctx % par == 0
    mc = l1.shape[1]
    ctx_chunks = b_ctx * n_ctx_ch
    u4 = u.reshape(2, mc, c, W_B)

    def chunk_of(slot, d):
        if slot < par:
            return lambda s: ((s // n_ctx_ch) * par + slot) * n_ctx_ch + (
                (n_ctx_ch - 1 - s % n_ctx_ch) if d else s % n_ctx_ch)
        q = slot - par
        return lambda s: ctx_chunks + q * n_lat_ch + ((n_lat_ch - 1 - s) if d else s)

    in_specs, args = [], []
    for slot in range(par + b_lat):
        for d in range(2):
            cg = chunk_of(slot, d)
            in_specs += [
                pl.BlockSpec((None, None, nh, 2 * c, DV_B), lambda s, cg=cg, d=d: (d, cg(s), 0, 0, 0)),
                pl.BlockSpec((None, None, nh, c + DK_B, c), lambda s, cg=cg, d=d: (d, cg(s), 0, 0, 0)),
                pl.BlockSpec((None, None, c, W_B), lambda s, cg=cg, d=d: (d, cg(s), 0, 0)),
                pl.BlockSpec((None, 2 * nh, LANES), lambda s, cg=cg: (cg(s), 0, 0)),
            ]
            args += [l1, l2, u4, e]
    in_specs.append(pl.BlockSpec((b_lat, 2 * nh, DK_B, DV_B), lambda s: (0, 0, 0, 0)))
    args.append(s0_lat)

    ctx_o = jax.ShapeDtypeStruct((b_ctx // par, par, n_ctx_ch, c, W_B), F32)
    lat_o = jax.ShapeDtypeStruct((b_lat, n_lat_ch, c, W_B), F32)
    cf = lambda s: s % n_ctx_ch
    of_ctx, ob_ctx, of_lat, ob_lat, s_fin = pl.pallas_call(
        functools.partial(_scan_kernel, par, b_lat, n_ctx_ch),
        grid=(n_lat_ch,),
        in_specs=in_specs,
        out_specs=[
            pl.BlockSpec((None, par, None, c, W_B), lambda s: (s // n_ctx_ch, 0, cf(s), 0, 0)),
            pl.BlockSpec((None, par, None, c, W_B), lambda s: (s // n_ctx_ch, 0, n_ctx_ch - 1 - cf(s), 0, 0)),
            pl.BlockSpec((b_lat, None, c, W_B), lambda s: (0, s, 0, 0)),
            pl.BlockSpec((b_lat, None, c, W_B), lambda s: (0, n_lat_ch - 1 - s, 0, 0)),
            pl.BlockSpec((par, 2 * nh, DK_B, DV_B), lambda s: (s // n_ctx_ch, 0, 0, 0)),
        ],
        out_shape=[ctx_o, ctx_o, lat_o, lat_o, jax.ShapeDtypeStruct((b_ctx, 2 * nh, DK_B, DV_B), F32)],
        scratch_shapes=[pltpu.VMEM((par + b_lat, 2 * nh, DK_B, DV_B), F32)],
        compiler_params=_cparams(1),
        name="delta_scan",
    )(*args)
    n_ctx, n_lat = b_ctx * t_ctx, b_lat * t_lat
    return (of_ctx.reshape(n_ctx, W_B), ob_ctx.reshape(n_ctx, W_B), of_lat.reshape(n_lat, W_B),
            ob_lat.reshape(n_lat, W_B), s_fin)


def _route_rows(sel, scores):
    g = EXPERTS_PER_GROUP
    gscore = []
    for gi in range(N_EXPERT_GROUPS):
        a, b, c, d = sel[gi * g:(gi + 1) * g]
        hi1, lo1 = jnp.maximum(a, b), jnp.minimum(a, b)
        hi2, lo2 = jnp.maximum(c, d), jnp.minimum(c, d)
        gscore.append(jnp.maximum(hi1, hi2) + jnp.maximum(jnp.minimum(hi1, hi2), jnp.maximum(lo1, lo2)))
    best = jnp.zeros_like(gscore[0], dtype=jnp.int32)
    bestv = gscore[0]
    for gi in range(1, N_EXPERT_GROUPS):
        upd = gscore[gi] > bestv
        best = jnp.where(upd, gi, best)
        bestv = jnp.where(upd, gscore[gi], bestv)

    def in_best(rows, j):
        out = rows[j]
        for gi in range(1, N_EXPERT_GROUPS):
            out = jnp.where(best == gi, rows[gi * g + j], out)
        return out

    e_sel = [in_best(sel, j) for j in range(g)]
    e_sc = [in_best(scores, j) for j in range(g)]

    def first_argmax(vals):
        bi, bv, bs = jnp.zeros_like(best), vals[0], e_sc[0]
        for j in range(1, g):
            upd = vals[j] > bv
            bi = jnp.where(upd, j, bi)
            bv = jnp.where(upd, vals[j], bv)
            bs = jnp.where(upd, e_sc[j], bs)
        return bi, bs

    i0, s0 = first_argmax(e_sel)
    i1, s1 = first_argmax([jnp.where(i0 == j, -jnp.inf, e_sel[j]) for j in range(g)])
    tot = s0 + s1
    return best * g + i0, best * g + i1, s0 / tot, s1 / tot


def _mix_kernel(n_ctx, x_ref, mod_ref, oac_ref, oal_ref, ofc_ref, obc_ref, ofl_ref, obl_ref,
                gb_ref, oc_ref, g0_ref, g1_ref, g2_ref,
                dn_ref, n2_ref, wpa_ref, wpb_ref, wpc_ref, wout_ref, wr_ref, br_ref,
                x1_ref, h2_ref, idx_ref, wt_ref):
    d = x_ref.shape[1]
    is_ctx = pl.program_id(0) * x_ref.shape[0] < n_ctx
    dn = dn_ref[...]
    o = jnp.where(is_ctx, ofc_ref[...] + obc_ref[...], ofl_ref[...] + obl_ref[...])
    o_a = jnp.where(is_ctx, oac_ref[...], oal_ref[...])
    gate = _silu(gb_ref[...])
    parts = []
    for h in range(N_HEADS_B):
        sl = slice(h * DV_B, (h + 1) * DV_B)
        parts.append((_rms(o[:, sl]) * dn * gate[:, sl]).astype(BF16))
    o_b = jnp.concatenate(parts, axis=-1)
    mixed = (_sigmoid(g0_ref[...]) * _dot(o_a, wpa_ref[...])
             + _sigmoid(g1_ref[...]) * _dot(o_b, wpb_ref[...])
             + _sigmoid(g2_ref[...]) * _dot(oc_ref[...], wpc_ref[...]))
    gate1 = mod_ref[:, 2 * d:3 * d]
    x1 = x_ref[...] + gate1 * _dot(mixed.astype(BF16), wout_ref[...])
    x1_ref[...] = x1
    shift2 = mod_ref[:, 3 * d:4 * d]
    scale2 = mod_ref[:, 4 * d:5 * d]
    h2 = _rms(x1) * n2_ref[...] * (1.0 + scale2) + shift2
    h2_ref[...] = h2
    sc = _sigmoid(_dot_nt(wr_ref[...], h2, HIGHEST))
    sel = sc + br_ref[...]
    i0, i1, w0, w1 = _route_rows([sel[e:e + 1] for e in range(N_EXPERTS)],
                                 [sc[e:e + 1] for e in range(N_EXPERTS)])
    idx_ref[0:1, :] = i0
    idx_ref[1:2, :] = i1
    wt_ref[0:1, :] = w0
    wt_ref[1:2, :] = w1


def _mix(x, mod_l, oa, delta_o, z, oc, delta_norm_l, norm2_l, wpa, wpb, wpc, wout, wr_t, br, cond_of_row):
    m, d = x.shape
    r = ROW_TILE
    n_ctx = cond_of_row.n_ctx
    nct = n_ctx // r
    c_gb = OFF_GATE_B // W_B
    once = dict(pipeline_mode=pl.Buffered(1))
    full = lambda shape: pl.BlockSpec(shape, lambda i: (0,) * len(shape), **once)
    ctx_rows = lambda w: pl.BlockSpec((r, w), lambda i: (jnp.minimum(i, nct - 1), 0))
    lat_rows = lambda w: pl.BlockSpec((r, w), lambda i: (jnp.maximum(i - nct, 0), 0))
    of_ctx, ob_ctx, of_lat, ob_lat = delta_o
    return pl.pallas_call(
        functools.partial(_mix_kernel, n_ctx),
        grid=(m // r,),
        in_specs=[
            pl.BlockSpec((r, d), lambda i: (i, 0)),
            pl.BlockSpec((None, 1, 6 * d), lambda i: (cond_of_row(i * r), 0, 0)),
            ctx_rows(W_A), lat_rows(W_A),
            ctx_rows(W_B), ctx_rows(W_B), lat_rows(W_B), lat_rows(W_B),
            pl.BlockSpec((r, W_B), lambda i: (i, c_gb)),
            pl.BlockSpec((r, W_C), lambda i: (i, 0)),
            pl.BlockSpec((r, d), lambda i: (i, 0)),
            pl.BlockSpec((r, d), lambda i: (i, 1)),
            pl.BlockSpec((r, d), lambda i: (i, 2)),
            full((1, DV_B)), full((1, d)),
            full((W_A, d)), full((W_B, d)), full((W_C, d)), full((d, d)),
            full((N_EXPERTS, d)), full((N_EXPERTS, 1)),
        ],
        out_specs=[
            pl.BlockSpec((r, d), lambda i: (i, 0)),
            pl.BlockSpec((r, d), lambda i: (i, 0)),
            pl.BlockSpec((TOP_K, r), lambda i: (0, i)),
            pl.BlockSpec((TOP_K, r), lambda i: (0, i)),
        ],
        out_shape=[
            jax.ShapeDtypeStruct((m, d), F32),
            jax.ShapeDtypeStruct((m, d), F32),
            jax.ShapeDtypeStruct((TOP_K, m), jnp.int32),
            jax.ShapeDtypeStruct((TOP_K, m), F32),
        ],
        compiler_params=_cparams(1),
        name="mix_route",
    )(x, mod_l, oa[0], oa[1], of_ctx, ob_ctx, of_lat, ob_lat, z, oc, z, z, z,
      delta_norm_l.reshape(1, DV_B), norm2_l.reshape(1, d), wpa, wpb, wpc, wout, wr_t, br)


def _dispatch_meta(idx_t, bm):
    m = idx_t.shape[1]
    nk = m * TOP_K
    flat_e = idx_t.T.reshape(nk)
    onehot = (flat_e[:, None] == jnp.arange(N_EXPERTS, dtype=jnp.int32)[None, :]).astype(jnp.int32)
    csum = jnp.cumsum(onehot, axis=0)
    rank = jnp.sum((csum - onehot) * onehot, axis=1)
    counts = csum[-1]
    padded = (counts + bm - 1) // bm * bm
    pad_end = jnp.cumsum(padded)
    pad_start = pad_end - padded
    dest = jnp.sum(onehot * pad_start[None, :], axis=1) + rank
    nb = -(-(nk + N_EXPERTS * (bm - 1)) // bm)
    assign = jnp.arange(nk, dtype=jnp.int32)
    row_tok = jnp.zeros((nb * bm,), jnp.int32).at[dest].set(assign // TOP_K)
    p = jnp.arange(nb * bm, dtype=jnp.int32)
    spare = nk + ((p // bm) % 2) * bm + p % bm
    row_dst = spare.at[dest].set((assign % TOP_K) * m + assign // TOP_K)
    prime = nk + jnp.arange(2 * bm, dtype=jnp.int32)
    n_used = (pad_end[-1] // bm).astype(jnp.int32)
    blk_e = jnp.minimum(jnp.searchsorted(pad_end, jnp.arange(nb, dtype=jnp.int32) * bm, side="right"),
                        N_EXPERTS - 1).astype(jnp.int32)
    return blk_e, n_used.reshape(1), row_tok, jnp.concatenate([prime, row_dst])


def _moe_kernel(blk_e_ref, n_used_ref, row_tok_ref, row_dst_ref, h2_hbm, wg_ref, wu_ref, wd_ref, out_hbm,
                xbuf, ybuf, gsem, ssem):
    del blk_e_ref
    i = pl.program_id(0)
    bm = xbuf.shape[1]
    n_used = n_used_ref[0]

    def gather_row(blk, slot, r):
        tok = row_tok_ref[blk * bm + r]
        pltpu.make_async_copy(h2_hbm.at[pl.ds(tok, 1)], xbuf.at[slot, pl.ds(r, 1)], gsem.at[slot]).start()

    def scatter_row(tab_blk, slot, r):
        dst = row_dst_ref[tab_blk * bm + r]
        pltpu.make_async_copy(ybuf.at[slot, pl.ds(r, 1)], out_hbm.at[pl.ds(dst, 1)], ssem.at[slot]).start()

    def looped(fn, *args):
        def body(r, carry):
            fn(*args, r)
            return carry
        lax.fori_loop(0, bm, body, 0, unroll=8)

    def wait_gather(slot):
        pltpu.make_async_copy(h2_hbm.at[pl.ds(0, bm)], xbuf.at[slot], gsem.at[slot]).wait()

    def wait_scatter(slot):
        pltpu.make_async_copy(ybuf.at[slot], out_hbm.at[pl.ds(0, bm)], ssem.at[slot]).wait()

    @pl.when(i == 0)
    def _():
        ybuf[...] = jnp.zeros_like(ybuf)
        looped(gather_row, 0, 0)
        looped(scatter_row, 0, 0)

    def compute_block(slot):
        other = 1 - slot
        nxt = jnp.minimum(i + 1, n_used - 1)
        wait_gather(slot)
        for r in range(bm):
            gather_row(nxt, other, r)
            scatter_row(i + 1, other, r)
        xb = xbuf[slot].astype(BF16)
        act = _silu(_dot(xb, wg_ref[...])) * _dot(xb, wu_ref[...])
        y = _dot(act.astype(BF16), wd_ref[...])
        wait_scatter(slot)
        ybuf[slot] = y

    for parity in range(2):
        pl.when((i < n_used) & (i % 2 == parity))(functools.partial(compute_block, parity))

    @pl.when(i == n_used)
    def _():
        looped(scatter_row, i + 1, (i - 1) % 2)
        wait_gather(i % 2)
        wait_scatter(0)
        wait_scatter(1)


def _moe(h2, blk_e, n_used, row_tok, row_dst, wg, wu, wd):
    m, d = h2.shape
    bm = MOE_ROWS
    nb = row_tok.shape[0] // bm
    f = wg.shape[2]
    w_idx = lambda i, be, nu, rt, rd: (be[jnp.minimum(i, nb - 1)], 0, 0)
    return pl.pallas_call(
        _moe_kernel,
        grid_spec=pltpu.PrefetchScalarGridSpec(
            num_scalar_prefetch=4,
            grid=(nb + 1,),
            in_specs=[
                pl.BlockSpec(memory_space=pl.ANY),
                pl.BlockSpec((None, d, f), w_idx),
                pl.BlockSpec((None, d, f), w_idx),
                pl.BlockSpec((None, f, d), w_idx),
            ],
            out_specs=pl.BlockSpec(memory_space=pl.ANY),
            scratch_shapes=[pltpu.VMEM((2, bm, d), F32), pltpu.VMEM((2, bm, d), F32),
                            pltpu.SemaphoreType.DMA((2,)), pltpu.SemaphoreType.DMA((2,))],
        ),
        out_shape=jax.ShapeDtypeStruct((TOP_K * m + 2 * bm, d), F32),
        compiler_params=_cparams(1),
        name="moe_experts",
    )(blk_e, n_used, row_tok, row_dst, h2, wg, wu, wd)


def _combine_kernel(final, y0_ref, y1_ref, x1_ref, mod_ref, w_ref, g_ref, modn_ref, o_ref, *h_ref):
    d = x1_ref.shape[1]
    f = w_ref[:, 0:1] * y0_ref[...] + w_ref[:, 1:2] * y1_ref[...]
    x2 = x1_ref[...] + mod_ref[:, 5 * d:6 * d] * f
    if final:
        o_ref[...] = _rms(x2) * g_ref[...]
    else:
        o_ref[...] = x2
        h_ref[0][...] = _modulated_norm(x2, g_ref[...], modn_ref[...], d).astype(BF16)


def _combine(y2, x1, mod_l, w_col, gain, mod_next, final, cond_of_row):
    m, d = x1.shape
    r = ROW_TILE
    k1 = m // r
    cond_row = pl.BlockSpec((None, 1, 6 * d), lambda i: (cond_of_row(i * r), 0, 0))
    tile = pl.BlockSpec((r, d), lambda i: (i, 0))
    out_shape = [jax.ShapeDtypeStruct((m, d), F32)] + ([] if final else [jax.ShapeDtypeStruct((m, d), BF16)])
    return pl.pallas_call(
        functools.partial(_combine_kernel, final),
        grid=(m // r,),
        in_specs=[
            tile,
            pl.BlockSpec((r, d), lambda i: (k1 + i, 0)),
            tile, cond_row,
            pl.BlockSpec((r, TOP_K), lambda i: (i, 0)),
            pl.BlockSpec((1, d), lambda i: (0, 0)),
            cond_row,
        ],
        out_specs=[tile] * len(out_shape),
        out_shape=out_shape,
        compiler_params=_cparams(1),
        name="moe_combine",
    )(y2, y2, x1, mod_l, w_col, gain.reshape(1, d), mod_next)


class _CondOfRow:
    def __init__(self, n_ctx, t_lat):
        self.n_ctx, self.t_lat = n_ctx, t_lat
        self.tile_in = _pick_tile((1024, 512, 256), n_ctx, t_lat)

    def __call__(self, row0):
        return jnp.where(row0 < self.n_ctx, 0, 1 + (row0 - self.n_ctx) // self.t_lat)


def kernel(x_prompt, x_sample, cache_k, cache_v, state_delta, c, c_ctx, w_mod, b_mod, norm1, norm2, w_in,
           lambda_qk, subln, conv_w, a_log, dt_bias, delta_norm, pool_w, pool_scale, w_pa, w_pb, w_pc,
           w_out, w_router, b_router, w_gate, w_up, w_down, norm_final):
    b_ctx, t_ctx, d = x_prompt.shape
    b_lat, t_lat, _ = x_sample.shape
    depth = w_in.shape[0]
    past = cache_k.shape[2]
    n_ctx, n_lat = b_ctx * t_ctx, b_lat * t_lat
    m = n_ctx + n_lat
    assert d == D_MODEL and w_in.shape[2] == D_IN and 1 + b_lat <= SUBLANES
    assert t_ctx % ROW_TILE == 0 and t_lat % ROW_TILE == 0 and n_ctx % t_lat == 0
    cond_of_row = _CondOfRow(n_ctx, t_lat)
    dims = (b_ctx, t_ctx, b_lat, t_lat)

    def seq(row0):
        is_ctx = row0 < n_ctx
        return (jnp.where(is_ctx, row0 % t_ctx, (row0 - n_ctx) % t_lat), jnp.where(is_ctx, t_ctx, t_lat))

    x = jnp.concatenate([x_prompt.reshape(n_ctx, d), x_sample.reshape(n_lat, d)], axis=0)
    cond8 = jnp.zeros((SUBLANES, d), F32).at[0].set(c_ctx).at[1:1 + b_lat].set(c)
    mod = _modulation(cond8, w_mod, b_mod).reshape(depth, SUBLANES, 1, 6 * d)
    rope_tabs = _rope_tables(t_lat)
    wr_t = w_router.T
    br = b_router.reshape(N_EXPERTS, 1)
    ck = cache_k.reshape(b_lat, depth, past, QK_A)
    cv = cache_v.reshape(b_lat, depth, past, W_A)
    s_lat = state_delta.reshape(b_lat, depth, 2 * N_HEADS_B, DK_B, DV_B)

    new_k, new_v, new_s = [], [], []
    h = _norm_mod(x, mod[0], norm1[0], cond_of_row)
    for l in range(depth):
        lam_init = 0.8 - 0.6 * math.exp(-0.3 * l)
        last = l == depth - 1
        z = _in_proj(h, w_in[l], cond_of_row.tile_in)
        new_k.append(z[:n_ctx, OFF_K:OFF_K + QK_A].reshape(b_ctx, t_ctx, N_HEADS_A, 2 * DK_A))
        new_v.append(z[:n_ctx, OFF_V:OFF_V + W_A].reshape(b_ctx, t_ctx, N_HEADS_A, DV_A))

        oa = _attention(z, lambda_qk[l], subln[l], ck[:, l], cv[:, l], rope_tabs, lam_init, dims)
        bat = z[:, OFF_BA:OFF_BA + 4 * N_HEADS_B].T
        oc, u, l1, l2, e = _local(z, bat, conv_w[l], pool_w[l], pool_scale[l], a_log[l], dt_bias[l], seq)
        *delta_o, s_ctx = _delta_scan(u, l1, l2, e, s_lat[:, l], dims)
        new_s.append(s_ctx.reshape(b_ctx, 2, N_HEADS_B, DK_B, DV_B))

        x1, h2, idx_t, wt_t = _mix(x, mod[l], oa, delta_o, z, oc, delta_norm[l], norm2[l],
                                   w_pa[l].astype(BF16), w_pb[l].astype(BF16), w_pc[l].astype(BF16),
                                   w_out[l].astype(BF16), wr_t, br, cond_of_row)
        blk_e, n_used, row_tok, row_dst = _dispatch_meta(idx_t, MOE_ROWS)
        y2 = _moe(h2, blk_e, n_used, row_tok, row_dst, w_gate[l].astype(BF16), w_up[l].astype(BF16),
                  w_down[l].astype(BF16))
        if last:
            x, = _combine(y2, x1, mod[l], wt_t.T, norm_final, mod[l], True, cond_of_row)
        else:
            x, h = _combine(y2, x1, mod[l], wt_t.T, norm1[l + 1], mod[l + 1], False, cond_of_row)

    y_prompt = x[:n_ctx].reshape(b_ctx, t_ctx, d)
    y_sample = x[n_ctx:].reshape(b_lat, t_lat, d)
    return (y_prompt, y_sample, jnp.stack(new_k, axis=1), jnp.stack(new_v, axis=1), jnp.stack(new_s, axis=1))
```

```python
import functools
import math

import jax
import jax.numpy as jnp
from jax import lax
from jax.experimental import pallas as pl
from jax.experimental.pallas import tpu as pltpu

F32 = jnp.float32
BF16 = jnp.bfloat16
HIGHEST = lax.Precision.HIGHEST

D_MODEL = 2048
GRID_W = 64
EPS = 1e-6
N_HEADS_A = 8
DK_A = 64
DV_A = 128
QK_A = N_HEADS_A * 2 * DK_A
W_A = N_HEADS_A * DV_A
ROPE_BASE = 10000.0
ROPE_AXIS = DK_A // 2
N_HEADS_B = 4
DK_B = 128
DV_B = 128
W_B = N_HEADS_B * DV_B
DELTA_CHUNK = 64
POOL_WINDOWS = (2, 4, 8, 16)
POOL_GROUP_W = 128
W_C = len(POOL_WINDOWS) * POOL_GROUP_W
N_BRANCH = 3
N_EXPERTS = 16
N_EXPERT_GROUPS = 4
EXPERTS_PER_GROUP = N_EXPERTS // N_EXPERT_GROUPS
TOP_K = 2
D_FF = 1408

SRC_QKV_B = 2 * QK_A + W_A
SRC_GATE_B = SRC_QKV_B + 3 * W_B
SRC_BA = SRC_GATE_B + W_B
SRC_POOL = SRC_BA + 4 * N_HEADS_B
SRC_BRANCH = SRC_POOL + W_C
D_IN = SRC_BRANCH + N_BRANCH * D_MODEL
LANES = 128
SUBLANES = 8
IN_TILE_N = 1024
IN_HALF_N = IN_TILE_N // 2
OFF_RAW = N_BRANCH * D_MODEL
RAW_W = -(-SRC_BRANCH // IN_TILE_N) * IN_TILE_N
NP_IN = OFF_RAW + RAW_W
OFF_Q = OFF_RAW
OFF_K = OFF_Q + QK_A
OFF_V = OFF_K + QK_A
OFF_QKV_B = OFF_RAW + SRC_QKV_B
OFF_GATE_B = OFF_RAW + SRC_GATE_B
OFF_BA = OFF_RAW + SRC_BA
BRANCH_SHIFT = SRC_BRANCH % LANES
assert SRC_BA % IN_TILE_N == 0 and (SRC_BRANCH - BRANCH_SHIFT) % IN_HALF_N == 0
assert SRC_POOL - SRC_BA == BRANCH_SHIFT and OFF_RAW % IN_TILE_N == 0

ROW_TILE = 256
MOE_ROWS = 256
INV_BASE_BLOCK = 16
VMEM_LIMIT = 56 * 1024 * 1024


def _cparams(n_axes, vmem=VMEM_LIMIT):
    return pltpu.CompilerParams(dimension_semantics=("arbitrary",) * n_axes, vmem_limit_bytes=vmem)


def _dot(a, b, precision=None):
    return jnp.dot(a, b, preferred_element_type=F32, precision=precision)


def _dot_nt(a, b, precision=None):
    return lax.dot_general(a, b, (((1,), (1,)), ((), ())), preferred_element_type=F32, precision=precision)


def _bdot(a, b):
    return _dot(a.astype(BF16), b.astype(BF16))


def _bdot_nt(a, b):
    return _dot_nt(a.astype(BF16), b.astype(BF16))


def _sigmoid(x):
    return 1.0 / (1.0 + jnp.exp(-x))


def _silu(x):
    return x * _sigmoid(x)


def _rms(x, eps=EPS):
    return x * lax.rsqrt(jnp.mean(x * x, axis=-1, keepdims=True) + eps)


def _pick_tile(cands, *dims):
    for t in cands:
        if all(d % t == 0 for d in dims):
            return t
    raise ValueError(f"no tile in {cands} divides {dims}")


def _mod_kernel(c_ref, w_ref, b_ref, o_ref):
    a = _silu(c_ref[...]).astype(BF16)
    o_ref[...] = _dot(a, w_ref[...].astype(BF16)) + b_ref[...]


def _modulation(cond8, w_mod, b_mod):
    depth, d, n6 = w_mod.shape
    tn = 1024
    return pl.pallas_call(
        _mod_kernel,
        grid=(depth, n6 // tn),
        in_specs=[
            pl.BlockSpec((SUBLANES, d), lambda l, n: (0, 0)),
            pl.BlockSpec((None, d, tn), lambda l, n: (l, 0, n)),
            pl.BlockSpec((None, 1, tn), lambda l, n: (l, 0, n)),
        ],
        out_specs=pl.BlockSpec((None, SUBLANES, tn), lambda l, n: (l, 0, n)),
        out_shape=jax.ShapeDtypeStruct((depth, SUBLANES, n6), F32),
        compiler_params=_cparams(2),
        name="adaln_mod",
    )(cond8, w_mod, b_mod.reshape(depth, 1, n6))


def _modulated_norm(x, g, mod, d):
    return _rms(x) * g * (1.0 + mod[:, d:2 * d]) + mod[:, 0:d]


def _norm_kernel(x_ref, mod_ref, g_ref, h_ref):
    h_ref[...] = _modulated_norm(x_ref[...], g_ref[...], mod_ref[...], x_ref.shape[1]).astype(BF16)


def _norm_mod(x, mod_l, norm_l, cond_of_row):
    m, d = x.shape
    r = ROW_TILE
    return pl.pallas_call(
        _norm_kernel,
        grid=(m // r,),
        in_specs=[
            pl.BlockSpec((r, d), lambda i: (i, 0)),
            pl.BlockSpec((None, 1, 6 * d), lambda i: (cond_of_row(i * r), 0, 0)),
            pl.BlockSpec((1, d), lambda i: (0, 0)),
        ],
        out_specs=pl.BlockSpec((r, d), lambda i: (i, 0)),
        out_shape=jax.ShapeDtypeStruct((m, d), BF16),
        compiler_params=_cparams(1),
        name="norm_mod",
    )(x, mod_l, norm_l.reshape(1, d))


IN_CAST_ROWS = 256


def _in_kernel(h_ref, wa_ref, wb_ref, wc_ref, z_ref, w_scr):
    n = pl.program_id(0)
    first_row_tile = pl.program_id(1) == 0
    n_branch_tiles = OFF_RAW // IN_TILE_N

    @pl.when(first_row_tile & (n < n_branch_tiles))
    def _():
        def body(c, carry):
            rs = pl.ds(pl.multiple_of(c * IN_CAST_ROWS, IN_CAST_ROWS), IN_CAST_ROWS)
            win = jnp.concatenate([wa_ref[rs, :], wb_ref[rs, :], wc_ref[rs, :]], axis=1)
            w_scr[rs, :] = pltpu.roll(win, win.shape[1] - BRANCH_SHIFT, 1)[:, :IN_TILE_N].astype(BF16)
            return carry
        lax.fori_loop(0, w_scr.shape[0] // IN_CAST_ROWS, body, 0)

    @pl.when(first_row_tile & (n >= n_branch_tiles))
    def _():
        w_scr[:, :IN_HALF_N] = wa_ref[...].astype(BF16)
        w_scr[:, IN_HALF_N:] = wb_ref[...].astype(BF16)

    z_ref[...] = _dot(h_ref[...], w_scr[...])


def _in_proj(h, w_in, layer, tm):
    m, d = h.shape
    n_branch_tiles = OFF_RAW // IN_TILE_N
    half0 = (SRC_BRANCH - BRANCH_SHIFT) // IN_HALF_N
    lane0 = (SRC_BRANCH - BRANCH_SHIFT) // LANES
    per_tile = IN_TILE_N // LANES
    a_of = lambda n: jnp.where(n < n_branch_tiles, half0 + 2 * n, 2 * (n - n_branch_tiles))
    c_of = lambda n: jnp.where(n < n_branch_tiles, lane0 + per_tile * (n + 1), 0)
    return pl.pallas_call(
        _in_kernel,
        grid=(NP_IN // IN_TILE_N, m // tm),
        in_specs=[
            pl.BlockSpec((tm, d), lambda n, i: (i, 0)),
            pl.BlockSpec((None, d, IN_HALF_N), lambda n, i: (layer, 0, a_of(n))),
            pl.BlockSpec((None, d, IN_HALF_N), lambda n, i: (layer, 0, a_of(n) + 1)),
            pl.BlockSpec((None, d, LANES), lambda n, i: (layer, 0, c_of(n))),
        ],
        out_specs=pl.BlockSpec((tm, IN_TILE_N), lambda n, i: (i, n)),
        out_shape=jax.ShapeDtypeStruct((m, NP_IN), F32),
        scratch_shapes=[pltpu.VMEM((d, IN_TILE_N), BF16)],
        compiler_params=_cparams(2),
        name="in_proj",
    )(h, w_in, w_in, w_in)


def _lambda(lq_ref, lam_init):
    lq = lq_ref[...]
    a = jnp.sum(lq[0:1] * lq[1:2], axis=-1, keepdims=True)
    b = jnp.sum(lq[2:3] * lq[3:4], axis=-1, keepdims=True)
    return jnp.exp(a) - jnp.exp(b) + lam_init


def _diff_attn(qb, kb, vb, lam):
    probs = []
    for mp in range(2):
        s = _dot_nt(qb[:, mp * DK_A:(mp + 1) * DK_A], kb[:, mp * DK_A:(mp + 1) * DK_A])
        e = jnp.exp(s - jnp.max(s, axis=-1, keepdims=True))
        probs.append(e / jnp.sum(e, axis=-1, keepdims=True))
    pd = probs[0] - lam * probs[1]
    return _dot(pd.astype(BF16), vb)


def _attn_ctx_kernel(lam_init, lq_ref, sub_ref, q_ref, k_ref, v_ref, o_ref, knew_ref, vnew_ref):
    lam = _lambda(lq_ref, lam_init)
    qb = (q_ref[...] * DK_A ** -0.5).astype(BF16)
    k, v = k_ref[...], v_ref[...]
    knew_ref[...] = k
    vnew_ref[...] = v
    o = _diff_attn(qb, k.astype(BF16), v.astype(BF16), lam)
    o_ref[...] = (_rms(o) * sub_ref[...] * (1.0 - lam_init)).astype(BF16)


def _rope(x, cos, sin_signed):
    lane = lax.broadcasted_iota(jnp.int32, x.shape, 1)
    first_half = (lane % (ROPE_AXIS)) < (ROPE_AXIS // 2)
    partner = jnp.where(first_half, pltpu.roll(x, LANES - ROPE_AXIS // 2, 1), pltpu.roll(x, ROPE_AXIS // 2, 1))
    return x * cos + partner * sin_signed


def _attn_lat_kernel(lam_init, t_lat, lq_ref, sub_ref, q_ref, k_ref, v_ref, ck_ref, cv_ref,
                     cq_ref, sq_ref, ckk_ref, skk_ref, o_ref, k_scr, v_scr):
    @pl.when(pl.program_id(2) == 0)
    def _():
        k_scr[0:t_lat, :] = _rope(k_ref[...], ckk_ref[...], skk_ref[...]).astype(BF16)
        k_scr[t_lat:, :] = ck_ref[...].astype(BF16)
        v_scr[0:t_lat, :] = v_ref[...].astype(BF16)
        v_scr[t_lat:, :] = cv_ref[...].astype(BF16)

    lam = _lambda(lq_ref, lam_init)
    q = _rope(q_ref[...], cq_ref[...], sq_ref[...]) * DK_A ** -0.5
    o = _diff_attn(q.astype(BF16), k_scr[...], v_scr[...], lam)
    o_ref[...] = (_rms(o) * sub_ref[...] * (1.0 - lam_init)).astype(BF16)


def _attention(z, lq_l, sub_l, cache_k_l, cache_v_l, rope_tabs, lam_init, dims):
    b_ctx, t_ctx, b_lat, t_lat = dims
    n_ctx = b_ctx * t_ctx
    past = cache_k_l.shape[1]
    cq, ck, cv = OFF_Q // LANES, OFF_K // LANES, OFF_V // LANES
    small = [pl.BlockSpec((4, DK_A), lambda *_: (0, 0)), pl.BlockSpec((1, DV_A), lambda *_: (0, 0))]
    sub2 = sub_l.reshape(1, DV_A)

    head_blk = pl.BlockSpec((t_ctx, LANES), lambda b, h: (b, h))
    oa_ctx, k_new, v_new = pl.pallas_call(
        functools.partial(_attn_ctx_kernel, lam_init),
        grid=(b_ctx, N_HEADS_A),
        in_specs=small + [
            pl.BlockSpec((t_ctx, LANES), lambda b, h: (b, cq + h)),
            pl.BlockSpec((t_ctx, LANES), lambda b, h: (b, ck + h)),
            pl.BlockSpec((t_ctx, LANES), lambda b, h: (b, cv + h)),
        ],
        out_specs=[head_blk] * 3,
        out_shape=[jax.ShapeDtypeStruct((n_ctx, W_A), BF16), jax.ShapeDtypeStruct((n_ctx, QK_A), F32),
                   jax.ShapeDtypeStruct((n_ctx, W_A), F32)],
        compiler_params=_cparams(2),
        name="attn_ctx",
    )(lq_l, sub2, z, z, z)

    tq = _pick_tile((256, 128), t_lat)
    nq = t_lat // tq
    rb_q = n_ctx // tq
    rb_k = n_ctx // t_lat
    cos_t, sin_t = rope_tabs
    oa_lat = pl.pallas_call(
        functools.partial(_attn_lat_kernel, lam_init, t_lat),
        grid=(b_lat, N_HEADS_A, nq),
        in_specs=small + [
            pl.BlockSpec((tq, LANES), lambda b, h, i: (rb_q + b * nq + i, cq + h)),
            pl.BlockSpec((t_lat, LANES), lambda b, h, i: (rb_k + b, ck + h)),
            pl.BlockSpec((t_lat, LANES), lambda b, h, i: (rb_k + b, cv + h)),
            pl.BlockSpec((None, past, LANES), lambda b, h, i: (b, 0, h)),
            pl.BlockSpec((None, past, LANES), lambda b, h, i: (b, 0, h)),
            pl.BlockSpec((tq, LANES), lambda b, h, i: (i, 0)),
            pl.BlockSpec((tq, LANES), lambda b, h, i: (i, 0)),
            pl.BlockSpec((t_lat, LANES), lambda b, h, i: (0, 0)),
            pl.BlockSpec((t_lat, LANES), lambda b, h, i: (0, 0)),
        ],
        out_specs=pl.BlockSpec((tq, LANES), lambda b, h, i: (b * nq + i, h)),
        out_shape=jax.ShapeDtypeStruct((b_lat * t_lat, W_A), BF16),
        scratch_shapes=[pltpu.VMEM((t_lat + past, LANES), BF16), pltpu.VMEM((t_lat + past, LANES), BF16)],
        compiler_params=_cparams(3),
        name="attn_lat",
    )(lq_l, sub2, z, z, z, cache_k_l, cache_v_l, cos_t, sin_t, cos_t, sin_t)
    return (oa_ctx, oa_lat), k_new, v_new


def _rope_tables(t_lat):
    rows = t_lat // GRID_W
    r = jnp.repeat(jnp.arange(rows), GRID_W).astype(F32)
    col = jnp.tile(jnp.arange(GRID_W), rows).astype(F32)
    inv = ROPE_BASE ** (-jnp.arange(0, ROPE_AXIS, 2, dtype=F32) / ROPE_AXIS)
    ang_r, ang_c = r[:, None] * inv, col[:, None] * inv
    cos64 = jnp.concatenate([jnp.cos(ang_r)] * 2 + [jnp.cos(ang_c)] * 2, axis=-1)
    sin64 = jnp.concatenate([-jnp.sin(ang_r), jnp.sin(ang_r), -jnp.sin(ang_c), jnp.sin(ang_c)], axis=-1)
    return jnp.tile(cos64, (1, 2)), jnp.tile(sin64, (1, 2))


def _log_decay(alpha, a_log, dt):
    x = alpha + dt
    return -jnp.exp(a_log) * (jnp.maximum(x, 0.0) + jnp.log(1.0 + jnp.exp(-jnp.abs(x))))


def _chunk_cumsum(g, axis):
    n = g.shape[axis]
    pos = lax.broadcasted_iota(jnp.int32, g.shape, axis) % DELTA_CHUNK
    f, b = g, g
    s = 1
    while s < DELTA_CHUNK:
        f = f + jnp.where(pos >= s, pltpu.roll(f, s, axis), 0.0)
        b = b + jnp.where(pos < DELTA_CHUNK - s, pltpu.roll(b, n - s, axis), 0.0)
        s *= 2
    return f, b


def _delta_prepare(q_scr, k_scr, v_scr, ba, bat, al_row, al_col, dt_row, dt_col,
                   u_ref, l1_ref, l2_ref, e_ref):
    c, nh = DELTA_CHUNK, N_HEADS_B
    rows_n = q_scr.shape[0]
    beta_all = _sigmoid(ba[:, 0:2 * nh])
    gf, gb = _chunk_cumsum(_log_decay(ba[:, 2 * nh:4 * nh], al_row[...], dt_row[...]), 0)
    gc_all = jnp.where(lax.broadcasted_iota(jnp.int32, gf.shape, 1) < nh, gf, gb)
    gtf, gtb = _chunk_cumsum(_log_decay(bat[2 * nh:4 * nh, :], al_col[...], dt_col[...]), 1)
    gct_all = jnp.where(lax.broadcasted_iota(jnp.int32, gtf.shape, 0) < nh, gtf, gtb)

    ri = lax.broadcasted_iota(jnp.int32, (c, c), 0)
    ci = lax.broadcasted_iota(jnp.int32, (c, c), 1)
    masks = ((ri >= ci, ri > ci), (ri <= ci, ri < ci))
    eye = (ri == ci).astype(F32)
    blk_sizes = [INV_BASE_BLOCK << i for i in range(int(math.log2(c // INV_BASE_BLOCK)) + 1)]
    same_blk = [(ri // s) == (ci // s) for s in blk_sizes]
    n_sq = int(math.log2(INV_BASE_BLOCK)) - 1

    for j in range(rows_n // c):
        rs = slice(j * c, (j + 1) * c)
        chains = []
        for d in range(2):
            incl, strict = masks[d]
            last = 0 if d else c - 1
            for h in range(nh):
                idx = d * nh + h
                hs = slice(h * DK_B, (h + 1) * DK_B)
                gc = gc_all[rs, idx:idx + 1]
                gr = gct_all[idx:idx + 1, rs]
                ch = dict(d=d, h=h, idx=idx, hs=hs, strict=strict, gc=gc, g_last=gc[last:last + 1, :],
                          q=q_scr[rs, hs] * DK_B ** -0.5, k=k_scr[rs, hs], v=v_scr[rs, hs],
                          beta=beta_all[rs, idx:idx + 1],
                          decay=jnp.where(incl, jnp.exp(jnp.where(incl, gc - gr, 0.0)), 0.0))
                ch["kb"] = ch["k"] * ch["beta"]
                chains.append(ch)
        for ch in chains:
            kbf = ch["k"].astype(BF16)
            a = jnp.where(ch["strict"], -(_dot_nt(ch["kb"].astype(BF16), kbf) * ch["decay"]), 0.0)
            ch["attn"] = _dot_nt(ch["q"].astype(BF16), kbf) * ch["decay"]
            ch["a"] = a
            ch["apow"] = jnp.where(same_blk[0], a, 0.0)
            ch["p"] = eye + ch["apow"]
        for _ in range(n_sq):
            for ch in chains:
                ch["apow"] = _bdot(ch["apow"], ch["apow"])
            for ch in chains:
                ch["p"] = ch["p"] + _bdot(ch["p"], ch["apow"])
        for lvl in range(1, len(same_blk)):
            for ch in chains:
                a_off = jnp.where(same_blk[lvl] & jnp.logical_not(same_blk[lvl - 1]), ch["a"], 0.0)
                ch["t"] = _bdot(ch["p"], a_off)
            for ch in chains:
                ch["p"] = ch["p"] + _bdot(ch["t"], ch["p"])
        for ch in chains:
            rhs = jnp.concatenate([ch["v"] * ch["beta"], ch["kb"] * jnp.exp(ch["gc"])], axis=-1)
            ch["sol"] = _bdot(ch["p"], rhs)
        for ch in chains:
            d, h, hs = ch["d"], ch["h"], ch["hs"]
            u_ref[d, rs, hs] = ch["sol"][:, :DV_B]
            l1_ref[d, j, h, 0:c, :] = ch["sol"][:, DV_B:].astype(BF16)
            l1_ref[d, j, h, c:2 * c, :] = (ch["q"] * jnp.exp(ch["gc"])).astype(BF16)
            l2_ref[d, j, h, 0:c, :] = ch["attn"].astype(BF16)
            l2_ref[d, j, h, c:, :] = (ch["k"] * jnp.exp(ch["g_last"] - ch["gc"])).T.astype(BF16)
            e_ref[j, ch["idx"]:ch["idx"] + 1, :] = jnp.broadcast_to(jnp.exp(ch["g_last"]), (1, LANES))


def _local_kernel(seq, cur_ref, prev_ref, next_ref, pcur_ref, pprev_ref, pnext_ref, bat_ref,
                  cw_ref, pw_ref, ps_ref, al_row, al_col, dt_row, dt_col,
                  oc_ref, u_ref, l1_ref, l2_ref, e_ref, q_scr, k_scr, v_scr):
    rows_n = cur_ref.shape[0]
    pos0, t_seq = seq(pl.program_id(0) * rows_n)
    has_prev = pos0 > 0
    has_next = pos0 + rows_n < t_seq
    rows = lax.broadcasted_iota(jnp.int32, (rows_n, 1), 0)

    x = cur_ref[...]
    prev_row = jnp.where(has_prev, prev_ref[SUBLANES - 1:SUBLANES, :], 0.0)
    next_row = jnp.where(has_next, next_ref[0:1, :], 0.0)
    xm1 = jnp.where(rows == 0, prev_row, pltpu.roll(x, 1, 0))
    xp1 = jnp.where(rows == rows_n - 1, next_row, pltpu.roll(x, rows_n - 1, 0))
    y = _silu(cw_ref[0:1, :] * xm1 + cw_ref[1:2, :] * x + cw_ref[2:3, :] * xp1)
    for h in range(N_HEADS_B):
        for j, ref in enumerate((q_scr, k_scr)):
            a = y[:, j * W_B + h * DK_B:j * W_B + (h + 1) * DK_B]
            ref[:, h * DK_B:(h + 1) * DK_B] = a * lax.rsqrt(jnp.sum(a * a, axis=-1, keepdims=True) + EPS)
    v_scr[...] = y[:, 2 * W_B:]
    _delta_prepare(q_scr, k_scr, v_scr, pcur_ref[:, 0:LANES], bat_ref[...], al_row, al_col, dt_row, dt_col,
                   u_ref, l1_ref, l2_ref, e_ref)

    def pool_cols(ref):
        t = ref[...]
        return pltpu.roll(t, t.shape[1] - BRANCH_SHIFT, 1)[:, :W_C]

    pz = pool_cols(pcur_ref)
    ext = jnp.concatenate([jnp.where(has_prev, pool_cols(pprev_ref), 0.0), pz,
                           jnp.where(has_next, pool_cols(pnext_ref), 0.0)], axis=0)
    n_ext = rows_n + 2 * SUBLANES
    tpos = pos0 + rows
    for gi, win in enumerate(POOL_WINDOWS):
        sl = slice(gi * POOL_GROUP_W, (gi + 1) * POOL_GROUP_W)
        xg = ext[:, sl]
        acc = xg + pltpu.roll(xg, 1, 0)
        s = 1
        while 2 * s < win:
            acc = pltpu.roll(acc, s, 0) + pltpu.roll(acc, n_ext - s, 0)
            s *= 2
        acc = acc[SUBLANES:SUBLANES + rows_n]
        cnt = (jnp.minimum(tpos + win // 2, t_seq) - jnp.maximum(tpos - win // 2, 0)).astype(F32)
        d = acc / cnt - pz[:, sl]
        yg = _dot(d.astype(BF16), pw_ref[gi].astype(BF16)) * ps_ref[:, sl]
        oc_ref[:, sl] = yg.astype(BF16)


def _local(z, bat, conv_w_l, pool_w_l, pool_scale_l, a_log_l, dt_l, seq):
    m = z.shape[0]
    r, c = ROW_TILE, DELTA_CHUNK
    hb = r // SUBLANES
    last_hb = m // SUBLANES - 1
    c_qkv, c_pool = OFF_QKV_B // (3 * W_B), OFF_BA // IN_TILE_N
    nh2 = 2 * N_HEADS_B
    mc, cpt = m // c, r // c
    prev_map = lambda cb: (lambda i: (jnp.maximum(i * hb - 1, 0), cb))
    next_map = lambda cb: (lambda i: (jnp.minimum((i + 1) * hb, last_hb), cb))
    row = pl.BlockSpec((1, nh2), lambda i: (0, 0))
    col = pl.BlockSpec((nh2, 1), lambda i: (0, 0))
    return pl.pallas_call(
        functools.partial(_local_kernel, seq),
        grid=(m // r,),
        in_specs=[
            pl.BlockSpec((r, 3 * W_B), lambda i: (i, c_qkv)),
            pl.BlockSpec((SUBLANES, 3 * W_B), prev_map(c_qkv)),
            pl.BlockSpec((SUBLANES, 3 * W_B), next_map(c_qkv)),
            pl.BlockSpec((r, IN_TILE_N), lambda i: (i, c_pool)),
            pl.BlockSpec((SUBLANES, IN_TILE_N), prev_map(c_pool)),
            pl.BlockSpec((SUBLANES, IN_TILE_N), next_map(c_pool)),
            pl.BlockSpec((2 * nh2, r), lambda i: (0, i)),
            pl.BlockSpec((3, 3 * W_B), lambda i: (0, 0)),
            pl.BlockSpec((len(POOL_WINDOWS), POOL_GROUP_W, POOL_GROUP_W), lambda i: (0, 0, 0)),
            pl.BlockSpec((1, W_C), lambda i: (0, 0)),
            row, col, row, col,
        ],
        out_specs=[
            pl.BlockSpec((r, W_C), lambda i: (i, 0)),
            pl.BlockSpec((2, r, W_B), lambda i: (0, i, 0)),
            pl.BlockSpec((2, cpt, N_HEADS_B, 2 * c, DV_B), lambda i: (0, i, 0, 0, 0)),
            pl.BlockSpec((2, cpt, N_HEADS_B, c + DK_B, c), lambda i: (0, i, 0, 0, 0)),
            pl.BlockSpec((cpt, nh2, LANES), lambda i: (i, 0, 0)),
        ],
        out_shape=[
            jax.ShapeDtypeStruct((m, W_C), BF16),
            jax.ShapeDtypeStruct((2, m, W_B), F32),
            jax.ShapeDtypeStruct((2, mc, N_HEADS_B, 2 * c, DV_B), BF16),
            jax.ShapeDtypeStruct((2, mc, N_HEADS_B, c + DK_B, c), BF16),
            jax.ShapeDtypeStruct((mc, nh2, LANES), F32),
        ],
        scratch_shapes=[pltpu.VMEM((r, W_B), F32)] * 3,
        compiler_params=_cparams(1),
        name="local_conv_pool",
    )(z, z, z, z, z, z, bat, conv_w_l, pool_w_l, pool_scale_l.reshape(1, W_C),
      a_log_l.reshape(1, nh2), a_log_l.reshape(nh2, 1), dt_l.reshape(1, nh2), dt_l.reshape(nh2, 1))


def _scan_kernel(par, b_lat, n_ctx_ch, *refs):
    c, nh = DELTA_CHUNK, N_HEADS_B
    n_slot = par + b_lat
    ins = refs[:n_slot * 8]
    s0_ref = refs[n_slot * 8]
    of_ctx, ob_ctx, of_lat, ob_lat, sfin_ref, s_scr = refs[n_slot * 8 + 1:]
    step = pl.program_id(0)
    cc = step % n_ctx_ch

    @pl.when(cc == 0)
    def _():
        s_scr[0:par] = jnp.zeros((par,) + s_scr.shape[1:], F32)

    @pl.when(step == 0)
    def _():
        s_scr[par:] = s0_ref[...]

    for slot in range(n_slot):
        outs = (of_ctx, ob_ctx, slot) if slot < par else (of_lat, ob_lat, slot - par)
        chains = []
        for d in range(2):
            l1_ref, l2_ref, u_ref, e_ref = ins[(slot * 2 + d) * 4:(slot * 2 + d) * 4 + 4]
            for h in range(nh):
                chains.append(dict(d=d, h=h, idx=d * nh + h, l1=l1_ref, l2=l2_ref, u=u_ref, e=e_ref))
        for ch in chains:
            ch["s"] = s_scr[slot, ch["idx"]]
            ch["r1"] = _dot(ch["l1"][ch["h"]], ch["s"].astype(BF16))
        for ch in chains:
            hs = slice(ch["h"] * DV_B, (ch["h"] + 1) * DV_B)
            v_new = ch["u"][:, hs] - ch["r1"][:c]
            ch["r2"] = _dot(ch["l2"][ch["h"]], v_new.astype(BF16))
        for ch in chains:
            hs = slice(ch["h"] * DV_B, (ch["h"] + 1) * DV_B)
            outs[ch["d"]][outs[2], :, hs] = ch["r1"][c:] + ch["r2"][:c]
            e = ch["e"][ch["idx"]:ch["idx"] + 1, :]
            s_scr[slot, ch["idx"]] = ch["s"] * e + ch["r2"][c:]

    @pl.when(cc == n_ctx_ch - 1)
    def _():
        sfin_ref[...] = s_scr[0:par]


def _delta_scan(u, l1, l2, e, s0_lat, dims):
    b_ctx, t_ctx, b_lat, t_lat = dims
    c, nh = DELTA_CHUNK, N_HEADS_B
    n_ctx_ch, n_lat_ch = t_ctx // c, t_lat // c
    par = b_ctx * n_ctx_ch // n_lat_ch
    assert par >= 1 and par * n_lat_ch == b_ctx * n_ctx_ch and b_ctx % par == 0
    mc = l1.shape[1]
    ctx_chunks = b_ctx * n_ctx_ch
    u4 = u.reshape(2, mc, c, W_B)

    def chunk_of(slot, d):
        if slot < par:
            return lambda s: ((s // n_ctx_ch) * par + slot) * n_ctx_ch + (
                (n_ctx_ch - 1 - s % n_ctx_ch) if d else s % n_ctx_ch)
        q = slot - par
        return lambda s: ctx_chunks + q * n_lat_ch + ((n_lat_ch - 1 - s) if d else s)

    in_specs, args = [], []
    for slot in range(par + b_lat):
        for d in range(2):
            cg = chunk_of(slot, d)
            in_specs += [
                pl.BlockSpec((None, None, nh, 2 * c, DV_B), lambda s, cg=cg, d=d: (d, cg(s), 0, 0, 0)),
                pl.BlockSpec((None, None, nh, c + DK_B, c), lambda s, cg=cg, d=d: (d, cg(s), 0, 0, 0)),
                pl.BlockSpec((None, None, c, W_B), lambda s, cg=cg, d=d: (d, cg(s), 0, 0)),
                pl.BlockSpec((None, 2 * nh, LANES), lambda s, cg=cg: (cg(s), 0, 0)),
            ]
            args += [l1, l2, u4, e]
    in_specs.append(pl.BlockSpec((b_lat, 2 * nh, DK_B, DV_B), lambda s: (0, 0, 0, 0)))
    args.append(s0_lat)

    ctx_o = jax.ShapeDtypeStruct((b_ctx // par, par, n_ctx_ch, c, W_B), F32)
    lat_o = jax.ShapeDtypeStruct((b_lat, n_lat_ch, c, W_B), F32)
    cf = lambda s: s % n_ctx_ch
    of_ctx, ob_ctx, of_lat, ob_lat, s_fin = pl.pallas_call(
        functools.partial(_scan_kernel, par, b_lat, n_ctx_ch),
        grid=(n_lat_ch,),
        in_specs=in_specs,
        out_specs=[
            pl.BlockSpec((None, par, None, c, W_B), lambda s: (s // n_ctx_ch, 0, cf(s), 0, 0)),
            pl.BlockSpec((None, par, None, c, W_B), lambda s: (s // n_ctx_ch, 0, n_ctx_ch - 1 - cf(s), 0, 0)),
            pl.BlockSpec((b_lat, None, c, W_B), lambda s: (0, s, 0, 0)),
            pl.BlockSpec((b_lat, None, c, W_B), lambda s: (0, n_lat_ch - 1 - s, 0, 0)),
            pl.BlockSpec((par, 2 * nh, DK_B, DV_B), lambda s: (s // n_ctx_ch, 0, 0, 0)),
        ],
        out_shape=[ctx_o, ctx_o, lat_o, lat_o, jax.ShapeDtypeStruct((b_ctx, 2 * nh, DK_B, DV_B), F32)],
        scratch_shapes=[pltpu.VMEM((par + b_lat, 2 * nh, DK_B, DV_B), F32)],
        compiler_params=_cparams(1),
        name="delta_scan",
    )(*args)
    n_ctx, n_lat = b_ctx * t_ctx, b_lat * t_lat
    return (of_ctx.reshape(n_ctx, W_B), ob_ctx.reshape(n_ctx, W_B), of_lat.reshape(n_lat, W_B),
            ob_lat.reshape(n_lat, W_B), s_fin)


def _route_rows(sel, scores):
    g = EXPERTS_PER_GROUP
    gscore = []
    for gi in range(N_EXPERT_GROUPS):
        a, b, c, d = sel[gi * g:(gi + 1) * g]
        hi1, lo1 = jnp.maximum(a, b), jnp.minimum(a, b)
        hi2, lo2 = jnp.maximum(c, d), jnp.minimum(c, d)
        gscore.append(jnp.maximum(hi1, hi2) + jnp.maximum(jnp.minimum(hi1, hi2), jnp.maximum(lo1, lo2)))
    best = jnp.zeros_like(gscore[0], dtype=jnp.int32)
    bestv = gscore[0]
    for gi in range(1, N_EXPERT_GROUPS):
        upd = gscore[gi] > bestv
        best = jnp.where(upd, gi, best)
        bestv = jnp.where(upd, gscore[gi], bestv)

    def in_best(rows, j):
        out = rows[j]
        for gi in range(1, N_EXPERT_GROUPS):
            out = jnp.where(best == gi, rows[gi * g + j], out)
        return out

    e_sel = [in_best(sel, j) for j in range(g)]
    e_sc = [in_best(scores, j) for j in range(g)]

    def first_argmax(vals):
        bi, bv, bs = jnp.zeros_like(best), vals[0], e_sc[0]
        for j in range(1, g):
            upd = vals[j] > bv
            bi = jnp.where(upd, j, bi)
            bv = jnp.where(upd, vals[j], bv)
            bs = jnp.where(upd, e_sc[j], bs)
        return bi, bs

    i0, s0 = first_argmax(e_sel)
    i1, s1 = first_argmax([jnp.where(i0 == j, -jnp.inf, e_sel[j]) for j in range(g)])
    tot = s0 + s1
    return best * g + i0, best * g + i1, s0 / tot, s1 / tot


def _mix_kernel(n_ctx, x_ref, mod_ref, oac_ref, oal_ref, ofc_ref, obc_ref, ofl_ref, obl_ref,
                gb_ref, oc_ref, g0_ref, g1_ref, g2_ref,
                dn_ref, n2_ref, wpa_ref, wpb_ref, wpc_ref, wout_ref, wr_ref, br_ref,
                x1_ref, h2_ref, idx_ref, wt_ref):
    d = x_ref.shape[1]
    is_ctx = pl.program_id(0) * x_ref.shape[0] < n_ctx
    dn = dn_ref[...]
    o = jnp.where(is_ctx, ofc_ref[...] + obc_ref[...], ofl_ref[...] + obl_ref[...])
    o_a = jnp.where(is_ctx, oac_ref[...], oal_ref[...])
    gate = _silu(gb_ref[...])
    parts = []
    for h in range(N_HEADS_B):
        sl = slice(h * DV_B, (h + 1) * DV_B)
        parts.append((_rms(o[:, sl]) * dn * gate[:, sl]).astype(BF16))
    o_b = jnp.concatenate(parts, axis=-1)
    mixed = (_sigmoid(g0_ref[...]) * _dot(o_a, wpa_ref[...])
             + _sigmoid(g1_ref[...]) * _dot(o_b, wpb_ref[...])
             + _sigmoid(g2_ref[...]) * _dot(oc_ref[...], wpc_ref[...]))
    gate1 = mod_ref[:, 2 * d:3 * d]
    x1 = x_ref[...] + gate1 * _dot(mixed.astype(BF16), wout_ref[...])
    x1_ref[...] = x1
    shift2 = mod_ref[:, 3 * d:4 * d]
    scale2 = mod_ref[:, 4 * d:5 * d]
    h2 = _rms(x1) * n2_ref[...] * (1.0 + scale2) + shift2
    h2_ref[...] = h2
    sc = _sigmoid(_dot_nt(wr_ref[...], h2, HIGHEST))
    sel = sc + br_ref[...]
    i0, i1, w0, w1 = _route_rows([sel[e:e + 1] for e in range(N_EXPERTS)],
                                 [sc[e:e + 1] for e in range(N_EXPERTS)])
    idx_ref[0:1, :] = i0
    idx_ref[1:2, :] = i1
    wt_ref[0:1, :] = w0
    wt_ref[1:2, :] = w1


def _mix(x, mod_l, oa, delta_o, z, oc, delta_norm_l, norm2_l, wpa, wpb, wpc, wout, layer, wr_t, br,
         cond_of_row):
    m, d = x.shape
    r = ROW_TILE
    n_ctx = cond_of_row.n_ctx
    nct = n_ctx // r
    c_gb = OFF_GATE_B // W_B
    once = dict(pipeline_mode=pl.Buffered(1))
    full = lambda shape: pl.BlockSpec(shape, lambda i: (0,) * len(shape), **once)
    of_layer = lambda shape: pl.BlockSpec((None,) + shape, lambda i: (layer,) + (0,) * len(shape), **once)
    ctx_rows = lambda w: pl.BlockSpec((r, w), lambda i: (jnp.minimum(i, nct - 1), 0))
    lat_rows = lambda w: pl.BlockSpec((r, w), lambda i: (jnp.maximum(i - nct, 0), 0))
    of_ctx, ob_ctx, of_lat, ob_lat = delta_o
    return pl.pallas_call(
        functools.partial(_mix_kernel, n_ctx),
        grid=(m // r,),
        in_specs=[
            pl.BlockSpec((r, d), lambda i: (i, 0)),
            pl.BlockSpec((None, 1, 6 * d), lambda i: (cond_of_row(i * r), 0, 0)),
            ctx_rows(W_A), lat_rows(W_A),
            ctx_rows(W_B), ctx_rows(W_B), lat_rows(W_B), lat_rows(W_B),
            pl.BlockSpec((r, W_B), lambda i: (i, c_gb)),
            pl.BlockSpec((r, W_C), lambda i: (i, 0)),
            pl.BlockSpec((r, d), lambda i: (i, 0)),
            pl.BlockSpec((r, d), lambda i: (i, 1)),
            pl.BlockSpec((r, d), lambda i: (i, 2)),
            full((1, DV_B)), full((1, d)),
            of_layer((W_A, d)), of_layer((W_B, d)), of_layer((W_C, d)), of_layer((d, d)),
            full((N_EXPERTS, d)), full((N_EXPERTS, 1)),
        ],
        out_specs=[
            pl.BlockSpec((r, d), lambda i: (i, 0)),
            pl.BlockSpec((r, d), lambda i: (i, 0)),
            pl.BlockSpec((TOP_K, r), lambda i: (0, i)),
            pl.BlockSpec((TOP_K, r), lambda i: (0, i)),
        ],
        out_shape=[
            jax.ShapeDtypeStruct((m, d), F32),
            jax.ShapeDtypeStruct((m, d), F32),
            jax.ShapeDtypeStruct((TOP_K, m), jnp.int32),
            jax.ShapeDtypeStruct((TOP_K, m), F32),
        ],
        compiler_params=_cparams(1),
        name="mix_route",
    )(x, mod_l, oa[0], oa[1], of_ctx, ob_ctx, of_lat, ob_lat, z, oc, z, z, z,
      delta_norm_l.reshape(1, DV_B), norm2_l.reshape(1, d), wpa, wpb, wpc, wout, wr_t, br)


def _dispatch_meta(idx_t, bm):
    m = idx_t.shape[1]
    nk = m * TOP_K
    flat_e = idx_t.T.reshape(nk)
    experts = jnp.arange(N_EXPERTS, dtype=jnp.int32)[None, :]
    counts = jnp.sum((flat_e[:, None] == experts).astype(jnp.int32), axis=0)
    order = jnp.argsort(flat_e, stable=True).astype(jnp.int32)
    start = jnp.cumsum(counts) - counts
    padded = (counts + bm - 1) // bm * bm
    pad_end = jnp.cumsum(padded)
    pad_start = pad_end - padded
    nb = -(-(nk + N_EXPERTS * (bm - 1)) // bm)
    p = jnp.arange(nb * bm, dtype=jnp.int32)
    e_of_p = jnp.minimum(jnp.sum((pad_end[None, :] <= p[:, None]).astype(jnp.int32), axis=1), N_EXPERTS - 1)
    pick = (e_of_p[:, None] == experts).astype(jnp.int32)
    rank = p - jnp.sum(pick * pad_start[None, :], axis=1)
    real = rank < jnp.sum(pick * counts[None, :], axis=1)
    assign = order[jnp.clip(jnp.sum(pick * start[None, :], axis=1) + rank, 0, nk - 1)]
    row_tok = jnp.where(real, assign // TOP_K, 0)
    spare = nk + ((p // bm) % 2) * bm + p % bm
    row_dst = jnp.where(real, (assign % TOP_K) * m + assign // TOP_K, spare)
    prime = nk + jnp.arange(2 * bm, dtype=jnp.int32)
    n_used = (pad_end[-1] // bm).astype(jnp.int32)
    blk_e = e_of_p[::bm]
    return blk_e, n_used.reshape(1), row_tok, jnp.concatenate([prime, row_dst])


def _moe_kernel(blk_e_ref, n_used_ref, row_tok_ref, row_dst_ref, h2_hbm, wg_ref, wu_ref, wd_ref, out_hbm,
                xbuf, ybuf, gsem, ssem):
    del blk_e_ref
    i = pl.program_id(0)
    bm = xbuf.shape[1]
    n_used = n_used_ref[0]

    def gather_row(blk, slot, r):
        tok = row_tok_ref[blk * bm + r]
        pltpu.make_async_copy(h2_hbm.at[pl.ds(tok, 1)], xbuf.at[slot, pl.ds(r, 1)], gsem.at[slot]).start()

    def scatter_row(tab_blk, slot, r):
        dst = row_dst_ref[tab_blk * bm + r]
        pltpu.make_async_copy(ybuf.at[slot, pl.ds(r, 1)], out_hbm.at[pl.ds(dst, 1)], ssem.at[slot]).start()

    def looped(fn, *args):
        def body(r, carry):
            fn(*args, r)
            return carry
        lax.fori_loop(0, bm, body, 0, unroll=8)

    def wait_gather(slot):
        pltpu.make_async_copy(h2_hbm.at[pl.ds(0, bm)], xbuf.at[slot], gsem.at[slot]).wait()

    def wait_scatter(slot):
        pltpu.make_async_copy(ybuf.at[slot], out_hbm.at[pl.ds(0, bm)], ssem.at[slot]).wait()

    @pl.when(i == 0)
    def _():
        ybuf[...] = jnp.zeros_like(ybuf)
        looped(gather_row, 0, 0)
        looped(scatter_row, 0, 0)

    def compute_block(slot):
        other = 1 - slot
        nxt = jnp.minimum(i + 1, n_used - 1)
        wait_gather(slot)
        for r in range(bm):
            gather_row(nxt, other, r)
            scatter_row(i + 1, other, r)
        xb = xbuf[slot].astype(BF16)
        act = _silu(_dot(xb, wg_ref[...])) * _dot(xb, wu_ref[...])
        y = _dot(act.astype(BF16), wd_ref[...])
        wait_scatter(slot)
        ybuf[slot] = y

    for parity in range(2):
        pl.when((i < n_used) & (i % 2 == parity))(functools.partial(compute_block, parity))

    @pl.when(i == n_used)
    def _():
        looped(scatter_row, i + 1, (i - 1) % 2)
        wait_gather(i % 2)
        wait_scatter(0)
        wait_scatter(1)


def _moe(h2, blk_e, n_used, row_tok, row_dst, wg, wu, wd, layer):
    m, d = h2.shape
    bm = MOE_ROWS
    nb = row_tok.shape[0] // bm
    f = wg.shape[3]
    w_idx = lambda i, be, nu, rt, rd: (layer, be[jnp.minimum(i, nb - 1)], 0, 0)
    return pl.pallas_call(
        _moe_kernel,
        grid_spec=pltpu.PrefetchScalarGridSpec(
            num_scalar_prefetch=4,
            grid=(nb + 1,),
            in_specs=[
                pl.BlockSpec(memory_space=pl.ANY),
                pl.BlockSpec((None, None, d, f), w_idx),
                pl.BlockSpec((None, None, d, f), w_idx),
                pl.BlockSpec((None, None, f, d), w_idx),
            ],
            out_specs=pl.BlockSpec(memory_space=pl.ANY),
            scratch_shapes=[pltpu.VMEM((2, bm, d), F32), pltpu.VMEM((2, bm, d), F32),
                            pltpu.SemaphoreType.DMA((2,)), pltpu.SemaphoreType.DMA((2,))],
        ),
        out_shape=jax.ShapeDtypeStruct((TOP_K * m + 2 * bm, d), F32),
        compiler_params=_cparams(1),
        name="moe_experts",
    )(blk_e, n_used, row_tok, row_dst, h2, wg, wu, wd)


def _combine_kernel(n_ctx, final, y0_ref, y1_ref, x1_ref, mod_ref, w_ref, g_ref, modn_ref, o_ref, o2_ref):
    d = x1_ref.shape[1]
    f = w_ref[:, 0:1] * y0_ref[...] + w_ref[:, 1:2] * y1_ref[...]
    x2 = x1_ref[...] + mod_ref[:, 5 * d:6 * d] * f
    if final:
        y = _rms(x2) * g_ref[...]
        is_ctx = pl.program_id(0) * x1_ref.shape[0] < n_ctx

        @pl.when(is_ctx)
        def _():
            o_ref[...] = y

        @pl.when(jnp.logical_not(is_ctx))
        def _():
            o2_ref[...] = y
    else:
        o_ref[...] = x2
        o2_ref[...] = _modulated_norm(x2, g_ref[...], modn_ref[...], d).astype(BF16)


def _combine(y2, x1, mod_l, w_col, gain, mod_next, final, cond_of_row):
    m, d = x1.shape
    r = ROW_TILE
    k1 = m // r
    n_ctx = cond_of_row.n_ctx
    nct = n_ctx // r
    cond_row = pl.BlockSpec((None, 1, 6 * d), lambda i: (cond_of_row(i * r), 0, 0))
    tile = pl.BlockSpec((r, d), lambda i: (i, 0))
    if final:
        out_specs = [pl.BlockSpec((r, d), lambda i: (jnp.minimum(i, nct - 1), 0)),
                     pl.BlockSpec((r, d), lambda i: (jnp.maximum(i - nct, 0), 0))]
        out_shape = [jax.ShapeDtypeStruct((n_ctx, d), F32), jax.ShapeDtypeStruct((m - n_ctx, d), F32)]
    else:
        out_specs = [tile, tile]
        out_shape = [jax.ShapeDtypeStruct((m, d), F32), jax.ShapeDtypeStruct((m, d), BF16)]
    return pl.pallas_call(
        functools.partial(_combine_kernel, n_ctx, final),
        grid=(m // r,),
        in_specs=[
            tile,
            pl.BlockSpec((r, d), lambda i: (k1 + i, 0)),
            tile, cond_row,
            pl.BlockSpec((r, TOP_K), lambda i: (i, 0)),
            pl.BlockSpec((1, d), lambda i: (0, 0)),
            cond_row,
        ],
        out_specs=out_specs,
        out_shape=out_shape,
        compiler_params=_cparams(1),
        name="moe_combine",
    )(y2, y2, x1, mod_l, w_col, gain.reshape(1, d), mod_next)


class _CondOfRow:
    def __init__(self, n_ctx, t_lat):
        self.n_ctx, self.t_lat = n_ctx, t_lat
        self.tile_in = _pick_tile((1024, 512, 256), n_ctx, t_lat)

    def __call__(self, row0):
        return jnp.where(row0 < self.n_ctx, 0, 1 + (row0 - self.n_ctx) // self.t_lat)


def kernel(x_prompt, x_sample, cache_k, cache_v, state_delta, c, c_ctx, w_mod, b_mod, norm1, norm2, w_in,
           lambda_qk, subln, conv_w, a_log, dt_bias, delta_norm, pool_w, pool_scale, w_pa, w_pb, w_pc,
           w_out, w_router, b_router, w_gate, w_up, w_down, norm_final):
    b_ctx, t_ctx, d = x_prompt.shape
    b_lat, t_lat, _ = x_sample.shape
    depth = w_in.shape[0]
    past = cache_k.shape[2]
    n_ctx, n_lat = b_ctx * t_ctx, b_lat * t_lat
    m = n_ctx + n_lat
    assert d == D_MODEL and w_in.shape[2] == D_IN and 1 + b_lat <= SUBLANES
    assert t_ctx % ROW_TILE == 0 and t_lat % ROW_TILE == 0 and n_ctx % t_lat == 0
    cond_of_row = _CondOfRow(n_ctx, t_lat)
    dims = (b_ctx, t_ctx, b_lat, t_lat)

    def seq(row0):
        is_ctx = row0 < n_ctx
        return (jnp.where(is_ctx, row0 % t_ctx, (row0 - n_ctx) % t_lat), jnp.where(is_ctx, t_ctx, t_lat))

    x = jnp.concatenate([x_prompt.reshape(n_ctx, d), x_sample.reshape(n_lat, d)], axis=0)
    cond8 = jnp.zeros((SUBLANES, d), F32).at[0].set(c_ctx).at[1:1 + b_lat].set(c)
    mod = _modulation(cond8, w_mod, b_mod).reshape(depth, SUBLANES, 1, 6 * d)
    rope_tabs = _rope_tables(t_lat)
    wr_t = w_router.T
    br = b_router.reshape(N_EXPERTS, 1)
    ck = cache_k.reshape(b_lat, depth, past, QK_A)
    cv = cache_v.reshape(b_lat, depth, past, W_A)
    s_lat = state_delta.reshape(b_lat, depth, 2 * N_HEADS_B, DK_B, DV_B)

    new_k, new_v, new_s = [], [], []
    h = _norm_mod(x, mod[0], norm1[0], cond_of_row)
    w_proj = [w.astype(BF16) for w in (w_pa, w_pb, w_pc, w_out)]
    w_experts = [w.astype(BF16) for w in (w_gate, w_up, w_down)]
    for l in range(depth):
        lam_init = 0.8 - 0.6 * math.exp(-0.3 * l)
        last = l == depth - 1
        z = _in_proj(h, w_in, l, cond_of_row.tile_in)
        oa, k_l, v_l = _attention(z, lambda_qk[l], subln[l], ck[:, l], cv[:, l], rope_tabs, lam_init, dims)
        new_k.append(k_l.reshape(b_ctx, t_ctx, N_HEADS_A, 2 * DK_A))
        new_v.append(v_l.reshape(b_ctx, t_ctx, N_HEADS_A, DV_A))
        bat = z[:, OFF_BA:OFF_BA + 4 * N_HEADS_B].T
        oc, u, l1, l2, e = _local(z, bat, conv_w[l], pool_w[l], pool_scale[l], a_log[l], dt_bias[l], seq)
        *delta_o, s_ctx = _delta_scan(u, l1, l2, e, s_lat[:, l], dims)
        new_s.append(s_ctx.reshape(b_ctx, 2, N_HEADS_B, DK_B, DV_B))

        x1, h2, idx_t, wt_t = _mix(x, mod[l], oa, delta_o, z, oc, delta_norm[l], norm2[l], *w_proj, l,
                                   wr_t, br, cond_of_row)
        blk_e, n_used, row_tok, row_dst = _dispatch_meta(idx_t, MOE_ROWS)
        y2 = _moe(h2, blk_e, n_used, row_tok, row_dst, *w_experts, l)
        if last:
            y_ctx, y_lat = _combine(y2, x1, mod[l], wt_t.T, norm_final, mod[l], True, cond_of_row)
        else:
            x, h = _combine(y2, x1, mod[l], wt_t.T, norm1[l + 1], mod[l + 1], False, cond_of_row)

    y_prompt = y_ctx.reshape(b_ctx, t_ctx, d)
    y_sample = y_lat.reshape(b_lat, t_lat, d)
    return (y_prompt, y_sample, jnp.stack(new_k, axis=1), jnp.stack(new_v, axis=1), jnp.stack(new_s, axis=1))
```

```python
import functools
import math

import jax
import jax.numpy as jnp
from jax import lax
from jax.experimental import pallas as pl
from jax.experimental.pallas import tpu as pltpu

F32 = jnp.float32
BF16 = jnp.bfloat16
HIGHEST = lax.Precision.HIGHEST

D_MODEL = 2048
GRID_W = 64
EPS = 1e-6
N_HEADS_A = 8
DK_A = 64
DV_A = 128
QK_A = N_HEADS_A * 2 * DK_A
W_A = N_HEADS_A * DV_A
ROPE_BASE = 10000.0
ROPE_AXIS = DK_A // 2
N_HEADS_B = 4
DK_B = 128
DV_B = 128
W_B = N_HEADS_B * DV_B
DELTA_CHUNK = 64
POOL_WINDOWS = (2, 4, 8, 16)
POOL_GROUP_W = 128
W_C = len(POOL_WINDOWS) * POOL_GROUP_W
N_BRANCH = 3
N_EXPERTS = 16
N_EXPERT_GROUPS = 4
EXPERTS_PER_GROUP = N_EXPERTS // N_EXPERT_GROUPS
TOP_K = 2
D_FF = 1408

SRC_QKV_B = 2 * QK_A + W_A
SRC_GATE_B = SRC_QKV_B + 3 * W_B
SRC_BA = SRC_GATE_B + W_B
SRC_POOL = SRC_BA + 4 * N_HEADS_B
SRC_BRANCH = SRC_POOL + W_C
D_IN = SRC_BRANCH + N_BRANCH * D_MODEL
LANES = 128
SUBLANES = 8
IN_TILE_N = 1024
IN_HALF_N = IN_TILE_N // 2
OFF_RAW = N_BRANCH * D_MODEL
RAW_W = -(-SRC_BRANCH // IN_TILE_N) * IN_TILE_N
NP_IN = OFF_RAW + RAW_W
OFF_Q = OFF_RAW
OFF_K = OFF_Q + QK_A
OFF_V = OFF_K + QK_A
OFF_QKV_B = OFF_RAW + SRC_QKV_B
OFF_GATE_B = OFF_RAW + SRC_GATE_B
OFF_BA = OFF_RAW + SRC_BA
BRANCH_SHIFT = SRC_BRANCH % LANES
assert SRC_BA % IN_TILE_N == 0 and (SRC_BRANCH - BRANCH_SHIFT) % IN_HALF_N == 0
assert SRC_POOL - SRC_BA == BRANCH_SHIFT and OFF_RAW % IN_TILE_N == 0

ROW_TILE = 256
MOE_ROWS = 256
INV_BASE_BLOCK = 16
VMEM_LIMIT = 56 * 1024 * 1024


def _cparams(n_axes, vmem=VMEM_LIMIT):
    return pltpu.CompilerParams(dimension_semantics=("arbitrary",) * n_axes, vmem_limit_bytes=vmem)


def _dot(a, b, precision=None):
    return jnp.dot(a, b, preferred_element_type=F32, precision=precision)


def _dot_nt(a, b, precision=None):
    return lax.dot_general(a, b, (((1,), (1,)), ((), ())), preferred_element_type=F32, precision=precision)


def _bdot(a, b):
    return _dot(a.astype(BF16), b.astype(BF16))


def _bdot_nt(a, b):
    return _dot_nt(a.astype(BF16), b.astype(BF16))


def _sigmoid(x):
    return 1.0 / (1.0 + jnp.exp(-x))


def _silu(x):
    return x * _sigmoid(x)


def _rms(x, eps=EPS):
    return x * lax.rsqrt(jnp.mean(x * x, axis=-1, keepdims=True) + eps)


def _pick_tile(cands, *dims):
    for t in cands:
        if all(d % t == 0 for d in dims):
            return t
    raise ValueError(f"no tile in {cands} divides {dims}")


def _mod_kernel(c_ref, w_ref, b_ref, o_ref):
    a = _silu(c_ref[...]).astype(BF16)
    o_ref[...] = _dot(a, w_ref[...].astype(BF16)) + b_ref[...]


def _modulation(cond8, w_mod, b_mod):
    depth, d, n6 = w_mod.shape
    tn = 1024
    return pl.pallas_call(
        _mod_kernel,
        grid=(depth, n6 // tn),
        in_specs=[
            pl.BlockSpec((SUBLANES, d), lambda l, n: (0, 0)),
            pl.BlockSpec((None, d, tn), lambda l, n: (l, 0, n)),
            pl.BlockSpec((None, 1, tn), lambda l, n: (l, 0, n)),
        ],
        out_specs=pl.BlockSpec((None, SUBLANES, tn), lambda l, n: (l, 0, n)),
        out_shape=jax.ShapeDtypeStruct((depth, SUBLANES, n6), F32),
        compiler_params=_cparams(2),
        name="adaln_mod",
    )(cond8, w_mod, b_mod.reshape(depth, 1, n6))


def _modulated_norm(x, g, mod, d):
    return _rms(x) * g * (1.0 + mod[:, d:2 * d]) + mod[:, 0:d]


def _norm_kernel(x_ref, mod_ref, g_ref, h_ref):
    h_ref[...] = _modulated_norm(x_ref[...], g_ref[...], mod_ref[...], x_ref.shape[1]).astype(BF16)


def _norm_mod(x, mod_l, norm_l, cond_of_row):
    m, d = x.shape
    r = ROW_TILE
    return pl.pallas_call(
        _norm_kernel,
        grid=(m // r,),
        in_specs=[
            pl.BlockSpec((r, d), lambda i: (i, 0)),
            pl.BlockSpec((None, 1, 6 * d), lambda i: (cond_of_row(i * r), 0, 0)),
            pl.BlockSpec((1, d), lambda i: (0, 0)),
        ],
        out_specs=pl.BlockSpec((r, d), lambda i: (i, 0)),
        out_shape=jax.ShapeDtypeStruct((m, d), BF16),
        compiler_params=_cparams(1),
        name="norm_mod",
    )(x, mod_l, norm_l.reshape(1, d))


def _in_kernel(h_ref, wt_ref, z_ref, w_scr):
    @pl.when(pl.program_id(1) == 0)
    def _():
        w_scr[...] = wt_ref[0].astype(BF16)

    z_ref[...] = _dot_nt(h_ref[...], w_scr[...])


def _in_proj(h, w_in_t, layer, tm):
    m, d = h.shape
    n_branch_tiles = OFF_RAW // IN_TILE_N
    assert SRC_BRANCH % SUBLANES == 0
    tile8 = IN_TILE_N // SUBLANES
    row0 = lambda n: SUBLANES * jnp.where(n < n_branch_tiles, SRC_BRANCH // SUBLANES + tile8 * n,
                                          tile8 * (n - n_branch_tiles))
    return pl.pallas_call(
        _in_kernel,
        grid=(NP_IN // IN_TILE_N, m // tm),
        in_specs=[
            pl.BlockSpec((tm, d), lambda n, i: (i, 0)),
            pl.BlockSpec((pl.Element(1), pl.Element(IN_TILE_N), pl.Element(d)),
                         lambda n, i: (layer, row0(n), 0)),
        ],
        out_specs=pl.BlockSpec((tm, IN_TILE_N), lambda n, i: (i, n)),
        out_shape=jax.ShapeDtypeStruct((m, NP_IN), F32),
        scratch_shapes=[pltpu.VMEM((IN_TILE_N, d), BF16)],
        compiler_params=_cparams(2),
        name="in_proj",
    )(h, w_in_t)


def _lambda(lq_ref, lam_init):
    lq = lq_ref[...]
    a = jnp.sum(lq[0:1] * lq[1:2], axis=-1, keepdims=True)
    b = jnp.sum(lq[2:3] * lq[3:4], axis=-1, keepdims=True)
    return jnp.exp(a) - jnp.exp(b) + lam_init


def _diff_attn(qb, kb, vb, lam):
    probs = []
    for mp in range(2):
        s = _dot_nt(qb[:, mp * DK_A:(mp + 1) * DK_A], kb[:, mp * DK_A:(mp + 1) * DK_A])
        e = jnp.exp(s - jnp.max(s, axis=-1, keepdims=True))
        probs.append(e / jnp.sum(e, axis=-1, keepdims=True))
    pd = probs[0] - lam * probs[1]
    return _dot(pd.astype(BF16), vb)


def _attn_ctx_kernel(lam_init, lq_ref, sub_ref, q_ref, k_ref, v_ref, o_ref, knew_ref, vnew_ref):
    lam = _lambda(lq_ref, lam_init)
    qb = (q_ref[...] * DK_A ** -0.5).astype(BF16)
    k, v = k_ref[...], v_ref[...]
    knew_ref[...] = k
    vnew_ref[...] = v
    o = _diff_attn(qb, k.astype(BF16), v.astype(BF16), lam)
    o_ref[...] = (_rms(o) * sub_ref[...] * (1.0 - lam_init)).astype(BF16)


def _rope(x, cos, sin_signed):
    lane = lax.broadcasted_iota(jnp.int32, x.shape, 1)
    first_half = (lane % (ROPE_AXIS)) < (ROPE_AXIS // 2)
    partner = jnp.where(first_half, pltpu.roll(x, LANES - ROPE_AXIS // 2, 1), pltpu.roll(x, ROPE_AXIS // 2, 1))
    return x * cos + partner * sin_signed


def _attn_lat_kernel(lam_init, t_lat, lq_ref, sub_ref, q_ref, k_ref, v_ref, ck_ref, cv_ref,
                     cq_ref, sq_ref, ckk_ref, skk_ref, o_ref, k_scr, v_scr):
    @pl.when(pl.program_id(2) == 0)
    def _():
        k_scr[0:t_lat, :] = _rope(k_ref[...], ckk_ref[...], skk_ref[...]).astype(BF16)
        k_scr[t_lat:, :] = ck_ref[...].astype(BF16)
        v_scr[0:t_lat, :] = v_ref[...].astype(BF16)
        v_scr[t_lat:, :] = cv_ref[...].astype(BF16)

    lam = _lambda(lq_ref, lam_init)
    q = _rope(q_ref[...], cq_ref[...], sq_ref[...]) * DK_A ** -0.5
    o = _diff_attn(q.astype(BF16), k_scr[...], v_scr[...], lam)
    o_ref[...] = (_rms(o) * sub_ref[...] * (1.0 - lam_init)).astype(BF16)


def _attention(z, lq_l, sub_l, cache_k_l, cache_v_l, rope_tabs, lam_init, dims):
    b_ctx, t_ctx, b_lat, t_lat = dims
    n_ctx = b_ctx * t_ctx
    past = cache_k_l.shape[1]
    cq, ck, cv = OFF_Q // LANES, OFF_K // LANES, OFF_V // LANES
    small = [pl.BlockSpec((4, DK_A), lambda *_: (0, 0)), pl.BlockSpec((1, DV_A), lambda *_: (0, 0))]
    sub2 = sub_l.reshape(1, DV_A)

    head_blk = pl.BlockSpec((t_ctx, LANES), lambda b, h: (b, h))
    oa_ctx, k_new, v_new = pl.pallas_call(
        functools.partial(_attn_ctx_kernel, lam_init),
        grid=(b_ctx, N_HEADS_A),
        in_specs=small + [
            pl.BlockSpec((t_ctx, LANES), lambda b, h: (b, cq + h)),
            pl.BlockSpec((t_ctx, LANES), lambda b, h: (b, ck + h)),
            pl.BlockSpec((t_ctx, LANES), lambda b, h: (b, cv + h)),
        ],
        out_specs=[head_blk] * 3,
        out_shape=[jax.ShapeDtypeStruct((n_ctx, W_A), BF16), jax.ShapeDtypeStruct((n_ctx, QK_A), F32),
                   jax.ShapeDtypeStruct((n_ctx, W_A), F32)],
        compiler_params=_cparams(2),
        name="attn_ctx",
    )(lq_l, sub2, z, z, z)

    tq = _pick_tile((256, 128), t_lat)
    nq = t_lat // tq
    rb_q = n_ctx // tq
    rb_k = n_ctx // t_lat
    cos_t, sin_t = rope_tabs
    oa_lat = pl.pallas_call(
        functools.partial(_attn_lat_kernel, lam_init, t_lat),
        grid=(b_lat, N_HEADS_A, nq),
        in_specs=small + [
            pl.BlockSpec((tq, LANES), lambda b, h, i: (rb_q + b * nq + i, cq + h)),
            pl.BlockSpec((t_lat, LANES), lambda b, h, i: (rb_k + b, ck + h)),
            pl.BlockSpec((t_lat, LANES), lambda b, h, i: (rb_k + b, cv + h)),
            pl.BlockSpec((None, past, LANES), lambda b, h, i: (b, 0, h)),
            pl.BlockSpec((None, past, LANES), lambda b, h, i: (b, 0, h)),
            pl.BlockSpec((tq, LANES), lambda b, h, i: (i, 0)),
            pl.BlockSpec((tq, LANES), lambda b, h, i: (i, 0)),
            pl.BlockSpec((t_lat, LANES), lambda b, h, i: (0, 0)),
            pl.BlockSpec((t_lat, LANES), lambda b, h, i: (0, 0)),
        ],
        out_specs=pl.BlockSpec((tq, LANES), lambda b, h, i: (b * nq + i, h)),
        out_shape=jax.ShapeDtypeStruct((b_lat * t_lat, W_A), BF16),
        scratch_shapes=[pltpu.VMEM((t_lat + past, LANES), BF16), pltpu.VMEM((t_lat + past, LANES), BF16)],
        compiler_params=_cparams(3),
        name="attn_lat",
    )(lq_l, sub2, z, z, z, cache_k_l, cache_v_l, cos_t, sin_t, cos_t, sin_t)
    return (oa_ctx, oa_lat), k_new, v_new


def _rope_tables(t_lat):
    rows = t_lat // GRID_W
    r = jnp.repeat(jnp.arange(rows), GRID_W).astype(F32)
    col = jnp.tile(jnp.arange(GRID_W), rows).astype(F32)
    inv = ROPE_BASE ** (-jnp.arange(0, ROPE_AXIS, 2, dtype=F32) / ROPE_AXIS)
    ang_r, ang_c = r[:, None] * inv, col[:, None] * inv
    cos64 = jnp.concatenate([jnp.cos(ang_r)] * 2 + [jnp.cos(ang_c)] * 2, axis=-1)
    sin64 = jnp.concatenate([-jnp.sin(ang_r), jnp.sin(ang_r), -jnp.sin(ang_c), jnp.sin(ang_c)], axis=-1)
    return jnp.tile(cos64, (1, 2)), jnp.tile(sin64, (1, 2))


def _log_decay(alpha, a_log, dt):
    x = alpha + dt
    return -jnp.exp(a_log) * (jnp.maximum(x, 0.0) + jnp.log(1.0 + jnp.exp(-jnp.abs(x))))


def _chunk_cumsum(g, axis):
    n = g.shape[axis]
    pos = lax.broadcasted_iota(jnp.int32, g.shape, axis) % DELTA_CHUNK
    f, b = g, g
    s = 1
    while s < DELTA_CHUNK:
        f = f + jnp.where(pos >= s, pltpu.roll(f, s, axis), 0.0)
        b = b + jnp.where(pos < DELTA_CHUNK - s, pltpu.roll(b, n - s, axis), 0.0)
        s *= 2
    return f, b


def _delta_prepare(q_scr, k_scr, v_scr, ba, bat, al_row, al_col, dt_row, dt_col,
                   u_ref, l1_ref, l2_ref, e_ref):
    c, nh = DELTA_CHUNK, N_HEADS_B
    rows_n = q_scr.shape[0]
    beta_all = _sigmoid(ba[:, 0:2 * nh])
    gf, gb = _chunk_cumsum(_log_decay(ba[:, 2 * nh:4 * nh], al_row[...], dt_row[...]), 0)
    gc_all = jnp.where(lax.broadcasted_iota(jnp.int32, gf.shape, 1) < nh, gf, gb)
    gtf, gtb = _chunk_cumsum(_log_decay(bat[2 * nh:4 * nh, :], al_col[...], dt_col[...]), 1)
    gct_all = jnp.where(lax.broadcasted_iota(jnp.int32, gtf.shape, 0) < nh, gtf, gtb)

    ri = lax.broadcasted_iota(jnp.int32, (c, c), 0)
    ci = lax.broadcasted_iota(jnp.int32, (c, c), 1)
    masks = ((ri >= ci, ri > ci), (ri <= ci, ri < ci))
    eye = (ri == ci).astype(F32)
    blk_sizes = [INV_BASE_BLOCK << i for i in range(int(math.log2(c // INV_BASE_BLOCK)) + 1)]
    same_blk = [(ri // s) == (ci // s) for s in blk_sizes]
    n_sq = int(math.log2(INV_BASE_BLOCK)) - 1

    for j in range(rows_n // c):
        rs = slice(j * c, (j + 1) * c)
        chains = []
        for d in range(2):
            incl, strict = masks[d]
            last = 0 if d else c - 1
            for h in range(nh):
                idx = d * nh + h
                hs = slice(h * DK_B, (h + 1) * DK_B)
                gc = gc_all[rs, idx:idx + 1]
                gr = gct_all[idx:idx + 1, rs]
                ch = dict(d=d, h=h, idx=idx, hs=hs, strict=strict, gc=gc, g_last=gc[last:last + 1, :],
                          q=q_scr[rs, hs] * DK_B ** -0.5, k=k_scr[rs, hs], v=v_scr[rs, hs],
                          beta=beta_all[rs, idx:idx + 1],
                          decay=jnp.where(incl, jnp.exp(jnp.where(incl, gc - gr, 0.0)), 0.0))
                ch["kb"] = ch["k"] * ch["beta"]
                chains.append(ch)
        for ch in chains:
            kbf = ch["k"].astype(BF16)
            a = jnp.where(ch["strict"], -(_dot_nt(ch["kb"].astype(BF16), kbf) * ch["decay"]), 0.0)
            ch["attn"] = _dot_nt(ch["q"].astype(BF16), kbf) * ch["decay"]
            ch["a"] = a
            ch["apow"] = jnp.where(same_blk[0], a, 0.0)
            ch["p"] = eye + ch["apow"]
        for _ in range(n_sq):
            for ch in chains:
                ch["apow"] = _bdot(ch["apow"], ch["apow"])
            for ch in chains:
                ch["p"] = ch["p"] + _bdot(ch["p"], ch["apow"])
        for lvl in range(1, len(same_blk)):
            for ch in chains:
                a_off = jnp.where(same_blk[lvl] & jnp.logical_not(same_blk[lvl - 1]), ch["a"], 0.0)
                ch["t"] = _bdot(ch["p"], a_off)
            for ch in chains:
                ch["p"] = ch["p"] + _bdot(ch["t"], ch["p"])
        for ch in chains:
            rhs = jnp.concatenate([ch["v"] * ch["beta"], ch["kb"] * jnp.exp(ch["gc"])], axis=-1)
            ch["sol"] = _bdot(ch["p"], rhs)
        for ch in chains:
            d, h, hs = ch["d"], ch["h"], ch["hs"]
            u_ref[d, rs, hs] = ch["sol"][:, :DV_B]
            l1_ref[d, j, h, 0:c, :] = ch["sol"][:, DV_B:].astype(BF16)
            l1_ref[d, j, h, c:2 * c, :] = (ch["q"] * jnp.exp(ch["gc"])).astype(BF16)
            l2_ref[d, j, h, 0:c, :] = ch["attn"].astype(BF16)
            l2_ref[d, j, h, c:, :] = (ch["k"] * jnp.exp(ch["g_last"] - ch["gc"])).T.astype(BF16)
            e_ref[j, ch["idx"]:ch["idx"] + 1, :] = jnp.broadcast_to(jnp.exp(ch["g_last"]), (1, LANES))


def _local_kernel(seq, cur_ref, prev_ref, next_ref, pcur_ref, pprev_ref, pnext_ref, bat_ref,
                  cw_ref, pw_ref, ps_ref, al_row, al_col, dt_row, dt_col,
                  oc_ref, u_ref, l1_ref, l2_ref, e_ref, q_scr, k_scr, v_scr):
    rows_n = cur_ref.shape[0]
    pos0, t_seq = seq(pl.program_id(0) * rows_n)
    has_prev = pos0 > 0
    has_next = pos0 + rows_n < t_seq
    rows = lax.broadcasted_iota(jnp.int32, (rows_n, 1), 0)

    x = cur_ref[...]
    prev_row = jnp.where(has_prev, prev_ref[SUBLANES - 1:SUBLANES, :], 0.0)
    next_row = jnp.where(has_next, next_ref[0:1, :], 0.0)
    xm1 = jnp.where(rows == 0, prev_row, pltpu.roll(x, 1, 0))
    xp1 = jnp.where(rows == rows_n - 1, next_row, pltpu.roll(x, rows_n - 1, 0))
    y = _silu(cw_ref[0:1, :] * xm1 + cw_ref[1:2, :] * x + cw_ref[2:3, :] * xp1)
    for h in range(N_HEADS_B):
        for j, ref in enumerate((q_scr, k_scr)):
            a = y[:, j * W_B + h * DK_B:j * W_B + (h + 1) * DK_B]
            ref[:, h * DK_B:(h + 1) * DK_B] = a * lax.rsqrt(jnp.sum(a * a, axis=-1, keepdims=True) + EPS)
    v_scr[...] = y[:, 2 * W_B:]
    _delta_prepare(q_scr, k_scr, v_scr, pcur_ref[:, 0:LANES], bat_ref[...], al_row, al_col, dt_row, dt_col,
                   u_ref, l1_ref, l2_ref, e_ref)

    def pool_cols(ref):
        t = ref[...]
        return pltpu.roll(t, t.shape[1] - BRANCH_SHIFT, 1)[:, :W_C]

    pz = pool_cols(pcur_ref)
    ext = jnp.concatenate([jnp.where(has_prev, pool_cols(pprev_ref), 0.0), pz,
                           jnp.where(has_next, pool_cols(pnext_ref), 0.0)], axis=0)
    n_ext = rows_n + 2 * SUBLANES
    tpos = pos0 + rows
    for gi, win in enumerate(POOL_WINDOWS):
        sl = slice(gi * POOL_GROUP_W, (gi + 1) * POOL_GROUP_W)
        xg = ext[:, sl]
        acc = xg + pltpu.roll(xg, 1, 0)
        s = 1
        while 2 * s < win:
            acc = pltpu.roll(acc, s, 0) + pltpu.roll(acc, n_ext - s, 0)
            s *= 2
        acc = acc[SUBLANES:SUBLANES + rows_n]
        cnt = (jnp.minimum(tpos + win // 2, t_seq) - jnp.maximum(tpos - win // 2, 0)).astype(F32)
        d = acc / cnt - pz[:, sl]
        yg = _dot(d.astype(BF16), pw_ref[gi].astype(BF16)) * ps_ref[:, sl]
        oc_ref[:, sl] = yg.astype(BF16)


def _local(z, bat, conv_w_l, pool_w_l, pool_scale_l, a_log_l, dt_l, seq):
    m = z.shape[0]
    r, c = ROW_TILE, DELTA_CHUNK
    hb = r // SUBLANES
    last_hb = m // SUBLANES - 1
    c_qkv, c_pool = OFF_QKV_B // (3 * W_B), OFF_BA // IN_TILE_N
    nh2 = 2 * N_HEADS_B
    mc, cpt = m // c, r // c
    prev_map = lambda cb: (lambda i: (jnp.maximum(i * hb - 1, 0), cb))
    next_map = lambda cb: (lambda i: (jnp.minimum((i + 1) * hb, last_hb), cb))
    row = pl.BlockSpec((1, nh2), lambda i: (0, 0))
    col = pl.BlockSpec((nh2, 1), lambda i: (0, 0))
    return pl.pallas_call(
        functools.partial(_local_kernel, seq),
        grid=(m // r,),
        in_specs=[
            pl.BlockSpec((r, 3 * W_B), lambda i: (i, c_qkv)),
            pl.BlockSpec((SUBLANES, 3 * W_B), prev_map(c_qkv)),
            pl.BlockSpec((SUBLANES, 3 * W_B), next_map(c_qkv)),
            pl.BlockSpec((r, IN_TILE_N), lambda i: (i, c_pool)),
            pl.BlockSpec((SUBLANES, IN_TILE_N), prev_map(c_pool)),
            pl.BlockSpec((SUBLANES, IN_TILE_N), next_map(c_pool)),
            pl.BlockSpec((2 * nh2, r), lambda i: (0, i)),
            pl.BlockSpec((3, 3 * W_B), lambda i: (0, 0)),
            pl.BlockSpec((len(POOL_WINDOWS), POOL_GROUP_W, POOL_GROUP_W), lambda i: (0, 0, 0)),
            pl.BlockSpec((1, W_C), lambda i: (0, 0)),
            row, col, row, col,
        ],
        out_specs=[
            pl.BlockSpec((r, W_C), lambda i: (i, 0)),
            pl.BlockSpec((2, r, W_B), lambda i: (0, i, 0)),
            pl.BlockSpec((2, cpt, N_HEADS_B, 2 * c, DV_B), lambda i: (0, i, 0, 0, 0)),
            pl.BlockSpec((2, cpt, N_HEADS_B, c + DK_B, c), lambda i: (0, i, 0, 0, 0)),
            pl.BlockSpec((cpt, nh2, LANES), lambda i: (i, 0, 0)),
        ],
        out_shape=[
            jax.ShapeDtypeStruct((m, W_C), BF16),
            jax.ShapeDtypeStruct((2, m, W_B), F32),
            jax.ShapeDtypeStruct((2, mc, N_HEADS_B, 2 * c, DV_B), BF16),
            jax.ShapeDtypeStruct((2, mc, N_HEADS_B, c + DK_B, c), BF16),
            jax.ShapeDtypeStruct((mc, nh2, LANES), F32),
        ],
        scratch_shapes=[pltpu.VMEM((r, W_B), F32)] * 3,
        compiler_params=_cparams(1),
        name="local_conv_pool",
    )(z, z, z, z, z, z, bat, conv_w_l, pool_w_l, pool_scale_l.reshape(1, W_C),
      a_log_l.reshape(1, nh2), a_log_l.reshape(nh2, 1), dt_l.reshape(1, nh2), dt_l.reshape(nh2, 1))


def _scan_kernel(par, b_lat, n_ctx_ch, *refs):
    c, nh = DELTA_CHUNK, N_HEADS_B
    n_slot = par + b_lat
    ins = refs[:n_slot * 8]
    s0_ref = refs[n_slot * 8]
    of_ctx, ob_ctx, of_lat, ob_lat, sfin_ref, s_scr = refs[n_slot * 8 + 1:]
    step = pl.program_id(0)
    cc = step % n_ctx_ch

    @pl.when(cc == 0)
    def _():
        s_scr[0:par] = jnp.zeros((par,) + s_scr.shape[1:], F32)

    @pl.when(step == 0)
    def _():
        s_scr[par:] = s0_ref[...]

    for slot in range(n_slot):
        outs = (of_ctx, ob_ctx, slot) if slot < par else (of_lat, ob_lat, slot - par)
        chains = []
        for d in range(2):
            l1_ref, l2_ref, u_ref, e_ref = ins[(slot * 2 + d) * 4:(slot * 2 + d) * 4 + 4]
            for h in range(nh):
                chains.append(dict(d=d, h=h, idx=d * nh + h, l1=l1_ref, l2=l2_ref, u=u_ref, e=e_ref))
        for ch in chains:
            ch["s"] = s_scr[slot, ch["idx"]]
            ch["r1"] = _dot(ch["l1"][ch["h"]], ch["s"].astype(BF16))
        for ch in chains:
            hs = slice(ch["h"] * DV_B, (ch["h"] + 1) * DV_B)
            v_new = ch["u"][:, hs] - ch["r1"][:c]
            ch["r2"] = _dot(ch["l2"][ch["h"]], v_new.astype(BF16))
        for ch in chains:
            hs = slice(ch["h"] * DV_B, (ch["h"] + 1) * DV_B)
            outs[ch["d"]][outs[2], :, hs] = ch["r1"][c:] + ch["r2"][:c]
            e = ch["e"][ch["idx"]:ch["idx"] + 1, :]
            s_scr[slot, ch["idx"]] = ch["s"] * e + ch["r2"][c:]

    @pl.when(cc == n_ctx_ch - 1)
    def _():
        sfin_ref[...] = s_scr[0:par]


def _delta_scan(u, l1, l2, e, s0_lat, dims):
    b_ctx, t_ctx, b_lat, t_lat = dims
    c, nh = DELTA_CHUNK, N_HEADS_B
    n_ctx_ch, n_lat_ch = t_ctx // c, t_lat // c
    par = b_ctx * n_ctx_ch // n_lat_ch
    assert par >= 1 and par * n_lat_ch == b_ctx * n_ctx_ch and b_ctx % par == 0
    mc = l1.shape[1]
    ctx_chunks = b_ctx * n_ctx_ch
    u4 = u.reshape(2, mc, c, W_B)

    def chunk_of(slot, d):
        if slot < par:
            return lambda s: ((s // n_ctx_ch) * par + slot) * n_ctx_ch + (
                (n_ctx_ch - 1 - s % n_ctx_ch) if d else s % n_ctx_ch)
        q = slot - par
        return lambda s: ctx_chunks + q * n_lat_ch + ((n_lat_ch - 1 - s) if d else s)

    in_specs, args = [], []
    for slot in range(par + b_lat):
        for d in range(2):
            cg = chunk_of(slot, d)
            in_specs += [
                pl.BlockSpec((None, None, nh, 2 * c, DV_B), lambda s, cg=cg, d=d: (d, cg(s), 0, 0, 0)),
                pl.BlockSpec((None, None, nh, c + DK_B, c), lambda s, cg=cg, d=d: (d, cg(s), 0, 0, 0)),
                pl.BlockSpec((None, None, c, W_B), lambda s, cg=cg, d=d: (d, cg(s), 0, 0)),
                pl.BlockSpec((None, 2 * nh, LANES), lambda s, cg=cg: (cg(s), 0, 0)),
            ]
            args += [l1, l2, u4, e]
    in_specs.append(pl.BlockSpec((b_lat, 2 * nh, DK_B, DV_B), lambda s: (0, 0, 0, 0)))
    args.append(s0_lat)

    ctx_o = jax.ShapeDtypeStruct((b_ctx // par, par, n_ctx_ch, c, W_B), F32)
    lat_o = jax.ShapeDtypeStruct((b_lat, n_lat_ch, c, W_B), F32)
    cf = lambda s: s % n_ctx_ch
    of_ctx, ob_ctx, of_lat, ob_lat, s_fin = pl.pallas_call(
        functools.partial(_scan_kernel, par, b_lat, n_ctx_ch),
        grid=(n_lat_ch,),
        in_specs=in_specs,
        out_specs=[
            pl.BlockSpec((None, par, None, c, W_B), lambda s: (s // n_ctx_ch, 0, cf(s), 0, 0)),
            pl.BlockSpec((None, par, None, c, W_B), lambda s: (s // n_ctx_ch, 0, n_ctx_ch - 1 - cf(s), 0, 0)),
            pl.BlockSpec((b_lat, None, c, W_B), lambda s: (0, s, 0, 0)),
            pl.BlockSpec((b_lat, None, c, W_B), lambda s: (0, n_lat_ch - 1 - s, 0, 0)),
            pl.BlockSpec((par, 2 * nh, DK_B, DV_B), lambda s: (s // n_ctx_ch, 0, 0, 0)),
        ],
        out_shape=[ctx_o, ctx_o, lat_o, lat_o, jax.ShapeDtypeStruct((b_ctx, 2 * nh, DK_B, DV_B), F32)],
        scratch_shapes=[pltpu.VMEM((par + b_lat, 2 * nh, DK_B, DV_B), F32)],
        compiler_params=_cparams(1),
        name="delta_scan",
    )(*args)
    n_ctx, n_lat = b_ctx * t_ctx, b_lat * t_lat
    return (of_ctx.reshape(n_ctx, W_B), ob_ctx.reshape(n_ctx, W_B), of_lat.reshape(n_lat, W_B),
            ob_lat.reshape(n_lat, W_B), s_fin)


def _route_rows(sel, scores):
    g = EXPERTS_PER_GROUP
    gscore = []
    for gi in range(N_EXPERT_GROUPS):
        a, b, c, d = sel[gi * g:(gi + 1) * g]
        hi1, lo1 = jnp.maximum(a, b), jnp.minimum(a, b)
        hi2, lo2 = jnp.maximum(c, d), jnp.minimum(c, d)
        gscore.append(jnp.maximum(hi1, hi2) + jnp.maximum(jnp.minimum(hi1, hi2), jnp.maximum(lo1, lo2)))
    best = jnp.zeros_like(gscore[0], dtype=jnp.int32)
    bestv = gscore[0]
    for gi in range(1, N_EXPERT_GROUPS):
        upd = gscore[gi] > bestv
        best = jnp.where(upd, gi, best)
        bestv = jnp.where(upd, gscore[gi], bestv)

    def in_best(rows, j):
        out = rows[j]
        for gi in range(1, N_EXPERT_GROUPS):
            out = jnp.where(best == gi, rows[gi * g + j], out)
        return out

    e_sel = [in_best(sel, j) for j in range(g)]
    e_sc = [in_best(scores, j) for j in range(g)]

    def first_argmax(vals):
        bi, bv, bs = jnp.zeros_like(best), vals[0], e_sc[0]
        for j in range(1, g):
            upd = vals[j] > bv
            bi = jnp.where(upd, j, bi)
            bv = jnp.where(upd, vals[j], bv)
            bs = jnp.where(upd, e_sc[j], bs)
        return bi, bs

    i0, s0 = first_argmax(e_sel)
    i1, s1 = first_argmax([jnp.where(i0 == j, -jnp.inf, e_sel[j]) for j in range(g)])
    tot = s0 + s1
    return best * g + i0, best * g + i1, s0 / tot, s1 / tot


def _mix_kernel(n_ctx, x_ref, mod_ref, oac_ref, oal_ref, ofc_ref, obc_ref, ofl_ref, obl_ref,
                gb_ref, oc_ref, g0_ref, g1_ref, g2_ref,
                dn_ref, n2_ref, wpa_ref, wpb_ref, wpc_ref, wout_ref, wr_ref, br_ref,
                x1_ref, h2_ref, idx_ref, wt_ref):
    d = x_ref.shape[1]
    is_ctx = pl.program_id(0) * x_ref.shape[0] < n_ctx
    dn = dn_ref[...]
    o = jnp.where(is_ctx, ofc_ref[...] + obc_ref[...], ofl_ref[...] + obl_ref[...])
    o_a = jnp.where(is_ctx, oac_ref[...], oal_ref[...])
    gate = _silu(gb_ref[...])
    parts = []
    for h in range(N_HEADS_B):
        sl = slice(h * DV_B, (h + 1) * DV_B)
        parts.append((_rms(o[:, sl]) * dn * gate[:, sl]).astype(BF16))
    o_b = jnp.concatenate(parts, axis=-1)
    mixed = (_sigmoid(g0_ref[...]) * _dot(o_a, wpa_ref[...])
             + _sigmoid(g1_ref[...]) * _dot(o_b, wpb_ref[...])
             + _sigmoid(g2_ref[...]) * _dot(oc_ref[...], wpc_ref[...]))
    gate1 = mod_ref[:, 2 * d:3 * d]
    x1 = x_ref[...] + gate1 * _dot(mixed.astype(BF16), wout_ref[...])
    x1_ref[...] = x1
    shift2 = mod_ref[:, 3 * d:4 * d]
    scale2 = mod_ref[:, 4 * d:5 * d]
    h2 = _rms(x1) * n2_ref[...] * (1.0 + scale2) + shift2
    h2_ref[...] = h2
    sc = _sigmoid(_dot_nt(wr_ref[...], h2, HIGHEST))
    sel = sc + br_ref[...]
    i0, i1, w0, w1 = _route_rows([sel[e:e + 1] for e in range(N_EXPERTS)],
                                 [sc[e:e + 1] for e in range(N_EXPERTS)])
    idx_ref[0:1, :] = i0
    idx_ref[1:2, :] = i1
    wt_ref[0:1, :] = w0
    wt_ref[1:2, :] = w1


def _mix(x, mod_l, oa, delta_o, z, oc, delta_norm_l, norm2_l, wpa, wpb, wpc, wout, layer, wr_t, br,
         cond_of_row):
    m, d = x.shape
    r = ROW_TILE
    n_ctx = cond_of_row.n_ctx
    nct = n_ctx // r
    c_gb = OFF_GATE_B // W_B
    once = dict(pipeline_mode=pl.Buffered(1))
    full = lambda shape: pl.BlockSpec(shape, lambda i: (0,) * len(shape), **once)
    of_layer = lambda shape: pl.BlockSpec((None,) + shape, lambda i: (layer,) + (0,) * len(shape), **once)
    ctx_rows = lambda w: pl.BlockSpec((r, w), lambda i: (jnp.minimum(i, nct - 1), 0))
    lat_rows = lambda w: pl.BlockSpec((r, w), lambda i: (jnp.maximum(i - nct, 0), 0))
    of_ctx, ob_ctx, of_lat, ob_lat = delta_o
    return pl.pallas_call(
        functools.partial(_mix_kernel, n_ctx),
        grid=(m // r,),
        in_specs=[
            pl.BlockSpec((r, d), lambda i: (i, 0)),
            pl.BlockSpec((None, 1, 6 * d), lambda i: (cond_of_row(i * r), 0, 0)),
            ctx_rows(W_A), lat_rows(W_A),
            ctx_rows(W_B), ctx_rows(W_B), lat_rows(W_B), lat_rows(W_B),
            pl.BlockSpec((r, W_B), lambda i: (i, c_gb)),
            pl.BlockSpec((r, W_C), lambda i: (i, 0)),
            pl.BlockSpec((r, d), lambda i: (i, 0)),
            pl.BlockSpec((r, d), lambda i: (i, 1)),
            pl.BlockSpec((r, d), lambda i: (i, 2)),
            full((1, DV_B)), full((1, d)),
            of_layer((W_A, d)), of_layer((W_B, d)), of_layer((W_C, d)), of_layer((d, d)),
            full((N_EXPERTS, d)), full((N_EXPERTS, 1)),
        ],
        out_specs=[
            pl.BlockSpec((r, d), lambda i: (i, 0)),
            pl.BlockSpec((r, d), lambda i: (i, 0)),
            pl.BlockSpec((TOP_K, r), lambda i: (0, i)),
            pl.BlockSpec((TOP_K, r), lambda i: (0, i)),
        ],
        out_shape=[
            jax.ShapeDtypeStruct((m, d), F32),
            jax.ShapeDtypeStruct((m, d), F32),
            jax.ShapeDtypeStruct((TOP_K, m), jnp.int32),
            jax.ShapeDtypeStruct((TOP_K, m), F32),
        ],
        compiler_params=_cparams(1),
        name="mix_route",
    )(x, mod_l, oa[0], oa[1], of_ctx, ob_ctx, of_lat, ob_lat, z, oc, z, z, z,
      delta_norm_l.reshape(1, DV_B), norm2_l.reshape(1, d), wpa, wpb, wpc, wout, wr_t, br)


def _dispatch_meta(idx_t, bm):
    m = idx_t.shape[1]
    nk = m * TOP_K
    flat_e = idx_t.T.reshape(nk)
    experts = jnp.arange(N_EXPERTS, dtype=jnp.int32)[None, :]
    counts = jnp.sum((flat_e[:, None] == experts).astype(jnp.int32), axis=0)
    order = jnp.argsort(flat_e, stable=True).astype(jnp.int32)
    start = jnp.cumsum(counts) - counts
    padded = (counts + bm - 1) // bm * bm
    pad_end = jnp.cumsum(padded)
    pad_start = pad_end - padded
    nb = -(-(nk + N_EXPERTS * (bm - 1)) // bm)
    p = jnp.arange(nb * bm, dtype=jnp.int32)
    e_of_p = jnp.minimum(jnp.sum((pad_end[None, :] <= p[:, None]).astype(jnp.int32), axis=1), N_EXPERTS - 1)
    pick = (e_of_p[:, None] == experts).astype(jnp.int32)
    rank = p - jnp.sum(pick * pad_start[None, :], axis=1)
    real = rank < jnp.sum(pick * counts[None, :], axis=1)
    assign = order[jnp.clip(jnp.sum(pick * start[None, :], axis=1) + rank, 0, nk - 1)]
    row_tok = jnp.where(real, assign // TOP_K, 0)
    spare = nk + ((p // bm) % 2) * bm + p % bm
    row_dst = jnp.where(real, (assign % TOP_K) * m + assign // TOP_K, spare)
    prime = nk + jnp.arange(2 * bm, dtype=jnp.int32)
    n_used = (pad_end[-1] // bm).astype(jnp.int32)
    blk_e = e_of_p[::bm]
    return blk_e, n_used.reshape(1), row_tok, jnp.concatenate([prime, row_dst])


def _moe_kernel(blk_e_ref, n_used_ref, row_tok_ref, row_dst_ref, h2_hbm, wg_ref, wu_ref, wd_ref, out_hbm,
                xbuf, ybuf, gsem, ssem):
    del blk_e_ref
    i = pl.program_id(0)
    bm = xbuf.shape[1]
    n_used = n_used_ref[0]

    def gather_row(blk, slot, r):
        tok = row_tok_ref[blk * bm + r]
        pltpu.make_async_copy(h2_hbm.at[pl.ds(tok, 1)], xbuf.at[slot, pl.ds(r, 1)], gsem.at[slot]).start()

    def scatter_row(tab_blk, slot, r):
        dst = row_dst_ref[tab_blk * bm + r]
        pltpu.make_async_copy(ybuf.at[slot, pl.ds(r, 1)], out_hbm.at[pl.ds(dst, 1)], ssem.at[slot]).start()

    def looped(fn, *args):
        def body(r, carry):
            fn(*args, r)
            return carry
        lax.fori_loop(0, bm, body, 0, unroll=8)

    def wait_gather(slot):
        pltpu.make_async_copy(h2_hbm.at[pl.ds(0, bm)], xbuf.at[slot], gsem.at[slot]).wait()

    def wait_scatter(slot):
        pltpu.make_async_copy(ybuf.at[slot], out_hbm.at[pl.ds(0, bm)], ssem.at[slot]).wait()

    @pl.when(i == 0)
    def _():
        ybuf[...] = jnp.zeros_like(ybuf)
        looped(gather_row, 0, 0)
        looped(scatter_row, 0, 0)

    def compute_block(slot):
        other = 1 - slot
        nxt = jnp.minimum(i + 1, n_used - 1)
        wait_gather(slot)
        for r in range(bm):
            gather_row(nxt, other, r)
            scatter_row(i + 1, other, r)
        xb = xbuf[slot].astype(BF16)
        act = _silu(_dot(xb, wg_ref[...])) * _dot(xb, wu_ref[...])
        y = _dot(act.astype(BF16), wd_ref[...])
        wait_scatter(slot)
        ybuf[slot] = y

    for parity in range(2):
        pl.when((i < n_used) & (i % 2 == parity))(functools.partial(compute_block, parity))

    @pl.when(i == n_used)
    def _():
        looped(scatter_row, i + 1, (i - 1) % 2)
        wait_gather(i % 2)
        wait_scatter(0)
        wait_scatter(1)


def _moe(h2, blk_e, n_used, row_tok, row_dst, wg, wu, wd, layer):
    m, d = h2.shape
    bm = MOE_ROWS
    nb = row_tok.shape[0] // bm
    f = wg.shape[3]
    w_idx = lambda i, be, nu, rt, rd: (layer, be[jnp.minimum(i, nb - 1)], 0, 0)
    return pl.pallas_call(
        _moe_kernel,
        grid_spec=pltpu.PrefetchScalarGridSpec(
            num_scalar_prefetch=4,
            grid=(nb + 1,),
            in_specs=[
                pl.BlockSpec(memory_space=pl.ANY),
                pl.BlockSpec((None, None, d, f), w_idx),
                pl.BlockSpec((None, None, d, f), w_idx),
                pl.BlockSpec((None, None, f, d), w_idx),
            ],
            out_specs=pl.BlockSpec(memory_space=pl.ANY),
            scratch_shapes=[pltpu.VMEM((2, bm, d), F32), pltpu.VMEM((2, bm, d), F32),
                            pltpu.SemaphoreType.DMA((2,)), pltpu.SemaphoreType.DMA((2,))],
        ),
        out_shape=jax.ShapeDtypeStruct((TOP_K * m + 2 * bm, d), F32),
        compiler_params=_cparams(1),
        name="moe_experts",
    )(blk_e, n_used, row_tok, row_dst, h2, wg, wu, wd)


def _combine_kernel(n_ctx, final, y0_ref, y1_ref, x1_ref, mod_ref, w_ref, g_ref, modn_ref, o_ref, o2_ref):
    d = x1_ref.shape[1]
    f = w_ref[:, 0:1] * y0_ref[...] + w_ref[:, 1:2] * y1_ref[...]
    x2 = x1_ref[...] + mod_ref[:, 5 * d:6 * d] * f
    if final:
        y = _rms(x2) * g_ref[...]
        is_ctx = pl.program_id(0) * x1_ref.shape[0] < n_ctx

        @pl.when(is_ctx)
        def _():
            o_ref[...] = y

        @pl.when(jnp.logical_not(is_ctx))
        def _():
            o2_ref[...] = y
    else:
        o_ref[...] = x2
        o2_ref[...] = _modulated_norm(x2, g_ref[...], modn_ref[...], d).astype(BF16)


def _combine(y2, x1, mod_l, w_col, gain, mod_next, final, cond_of_row):
    m, d = x1.shape
    r = ROW_TILE
    k1 = m // r
    n_ctx = cond_of_row.n_ctx
    nct = n_ctx // r
    cond_row = pl.BlockSpec((None, 1, 6 * d), lambda i: (cond_of_row(i * r), 0, 0))
    tile = pl.BlockSpec((r, d), lambda i: (i, 0))
    if final:
        out_specs = [pl.BlockSpec((r, d), lambda i: (jnp.minimum(i, nct - 1), 0)),
                     pl.BlockSpec((r, d), lambda i: (jnp.maximum(i - nct, 0), 0))]
        out_shape = [jax.ShapeDtypeStruct((n_ctx, d), F32), jax.ShapeDtypeStruct((m - n_ctx, d), F32)]
    else:
        out_specs = [tile, tile]
        out_shape = [jax.ShapeDtypeStruct((m, d), F32), jax.ShapeDtypeStruct((m, d), BF16)]
    return pl.pallas_call(
        functools.partial(_combine_kernel, n_ctx, final),
        grid=(m // r,),
        in_specs=[
            tile,
            pl.BlockSpec((r, d), lambda i: (k1 + i, 0)),
            tile, cond_row,
            pl.BlockSpec((r, TOP_K), lambda i: (i, 0)),
            pl.BlockSpec((1, d), lambda i: (0, 0)),
            cond_row,
        ],
        out_specs=out_specs,
        out_shape=out_shape,
        compiler_params=_cparams(1),
        name="moe_combine",
    )(y2, y2, x1, mod_l, w_col, gain.reshape(1, d), mod_next)


class _CondOfRow:
    def __init__(self, n_ctx, t_lat):
        self.n_ctx, self.t_lat = n_ctx, t_lat
        self.tile_in = _pick_tile((1024, 512, 256), n_ctx, t_lat)

    def __call__(self, row0):
        return jnp.where(row0 < self.n_ctx, 0, 1 + (row0 - self.n_ctx) // self.t_lat)


def kernel(x_prompt, x_sample, cache_k, cache_v, state_delta, c, c_ctx, w_mod, b_mod, norm1, norm2, w_in,
           lambda_qk, subln, conv_w, a_log, dt_bias, delta_norm, pool_w, pool_scale, w_pa, w_pb, w_pc,
           w_out, w_router, b_router, w_gate, w_up, w_down, norm_final):
    b_ctx, t_ctx, d = x_prompt.shape
    b_lat, t_lat, _ = x_sample.shape
    depth = w_in.shape[0]
    past = cache_k.shape[2]
    n_ctx, n_lat = b_ctx * t_ctx, b_lat * t_lat
    m = n_ctx + n_lat
    assert d == D_MODEL and w_in.shape[2] == D_IN and 1 + b_lat <= SUBLANES
    assert t_ctx % ROW_TILE == 0 and t_lat % ROW_TILE == 0 and n_ctx % t_lat == 0
    cond_of_row = _CondOfRow(n_ctx, t_lat)
    dims = (b_ctx, t_ctx, b_lat, t_lat)

    def seq(row0):
        is_ctx = row0 < n_ctx
        return (jnp.where(is_ctx, row0 % t_ctx, (row0 - n_ctx) % t_lat), jnp.where(is_ctx, t_ctx, t_lat))

    x = jnp.concatenate([x_prompt.reshape(n_ctx, d), x_sample.reshape(n_lat, d)], axis=0)
    cond8 = jnp.zeros((SUBLANES, d), F32).at[0].set(c_ctx).at[1:1 + b_lat].set(c)
    mod = _modulation(cond8, w_mod, b_mod).reshape(depth, SUBLANES, 1, 6 * d)
    rope_tabs = _rope_tables(t_lat)
    wr_t = w_router.T
    br = b_router.reshape(N_EXPERTS, 1)
    ck = cache_k.reshape(b_lat, depth, past, QK_A)
    cv = cache_v.reshape(b_lat, depth, past, W_A)
    s_lat = state_delta.reshape(b_lat, depth, 2 * N_HEADS_B, DK_B, DV_B)

    new_k, new_v, new_s = [], [], []
    h = _norm_mod(x, mod[0], norm1[0], cond_of_row)
    w_in_t = jnp.swapaxes(w_in, 1, 2)
    w_proj = [w.astype(BF16) for w in (w_pa, w_pb, w_pc, w_out)]
    w_experts = [w.astype(BF16) for w in (w_gate, w_up, w_down)]
    for l in range(depth):
        lam_init = 0.8 - 0.6 * math.exp(-0.3 * l)
        last = l == depth - 1
        z = _in_proj(h, w_in_t, l, cond_of_row.tile_in)
        oa, k_l, v_l = _attention(z, lambda_qk[l], subln[l], ck[:, l], cv[:, l], rope_tabs, lam_init, dims)
        new_k.append(k_l.reshape(b_ctx, t_ctx, N_HEADS_A, 2 * DK_A))
        new_v.append(v_l.reshape(b_ctx, t_ctx, N_HEADS_A, DV_A))
        bat = z[:, OFF_BA:OFF_BA + 4 * N_HEADS_B].T
        oc, u, l1, l2, e = _local(z, bat, conv_w[l], pool_w[l], pool_scale[l], a_log[l], dt_bias[l], seq)
        *delta_o, s_ctx = _delta_scan(u, l1, l2, e, s_lat[:, l], dims)
        new_s.append(s_ctx.reshape(b_ctx, 2, N_HEADS_B, DK_B, DV_B))

        x1, h2, idx_t, wt_t = _mix(x, mod[l], oa, delta_o, z, oc, delta_norm[l], norm2[l], *w_proj, l,
                                   wr_t, br, cond_of_row)
        blk_e, n_used, row_tok, row_dst = _dispatch_meta(idx_t, MOE_ROWS)
        y2 = _moe(h2, blk_e, n_used, row_tok, row_dst, *w_experts, l)
        if last:
            y_ctx, y_lat = _combine(y2, x1, mod[l], wt_t.T, norm_final, mod[l], True, cond_of_row)
        else:
            x, h = _combine(y2, x1, mod[l], wt_t.T, norm1[l + 1], mod[l + 1], False, cond_of_row)

    y_prompt = y_ctx.reshape(b_ctx, t_ctx, d)
    y_sample = y_lat.reshape(b_lat, t_lat, d)
    return (y_prompt, y_sample, jnp.stack(new_k, axis=1), jnp.stack(new_v, axis=1), jnp.stack(new_s, axis=1))
```

```python
import functools
import math

import jax
import jax.numpy as jnp
from jax import lax
from jax.experimental import pallas as pl
from jax.experimental.pallas import tpu as pltpu

F32 = jnp.float32
BF16 = jnp.bfloat16
HIGHEST = lax.Precision.HIGHEST

D_MODEL = 2048
GRID_W = 64
EPS = 1e-6
N_HEADS_A = 8
DK_A = 64
DV_A = 128
QK_A = N_HEADS_A * 2 * DK_A
W_A = N_HEADS_A * DV_A
ROPE_BASE = 10000.0
ROPE_AXIS = DK_A // 2
N_HEADS_B = 4
DK_B = 128
DV_B = 128
W_B = N_HEADS_B * DV_B
DELTA_CHUNK = 64
POOL_WINDOWS = (2, 4, 8, 16)
POOL_GROUP_W = 128
W_C = len(POOL_WINDOWS) * POOL_GROUP_W
N_BRANCH = 3
N_EXPERTS = 16
N_EXPERT_GROUPS = 4
EXPERTS_PER_GROUP = N_EXPERTS // N_EXPERT_GROUPS
TOP_K = 2
D_FF = 1408

SRC_QKV_B = 2 * QK_A + W_A
SRC_GATE_B = SRC_QKV_B + 3 * W_B
SRC_BA = SRC_GATE_B + W_B
SRC_POOL = SRC_BA + 4 * N_HEADS_B
SRC_BRANCH = SRC_POOL + W_C
D_IN = SRC_BRANCH + N_BRANCH * D_MODEL
LANES = 128
SUBLANES = 8
IN_TILE_N = 1024
IN_HALF_N = IN_TILE_N // 2
OFF_RAW = N_BRANCH * D_MODEL
RAW_W = -(-SRC_BRANCH // IN_TILE_N) * IN_TILE_N
NP_IN = OFF_RAW + RAW_W
OFF_Q = OFF_RAW
OFF_K = OFF_Q + QK_A
OFF_V = OFF_K + QK_A
OFF_QKV_B = OFF_RAW + SRC_QKV_B
OFF_GATE_B = OFF_RAW + SRC_GATE_B
OFF_BA = OFF_RAW + SRC_BA
BRANCH_SHIFT = SRC_BRANCH % LANES
assert SRC_BA % IN_TILE_N == 0 and (SRC_BRANCH - BRANCH_SHIFT) % IN_HALF_N == 0
assert SRC_POOL - SRC_BA == BRANCH_SHIFT and OFF_RAW % IN_TILE_N == 0

ROW_TILE = 256
MOE_ROWS = 256
INV_BASE_BLOCK = 16
MOE_MAT_CHUNKS = 8
MOE_STAGE_SLOTS = 2
MOE_VMEM_LIMIT = 60 * 1024 * 1024
VMEM_LIMIT = 56 * 1024 * 1024


def _cparams(n_axes, vmem=VMEM_LIMIT):
    return pltpu.CompilerParams(dimension_semantics=("arbitrary",) * n_axes, vmem_limit_bytes=vmem)


def _dot(a, b, precision=None):
    return jnp.dot(a, b, preferred_element_type=F32, precision=precision)


def _dot_nt(a, b, precision=None):
    return lax.dot_general(a, b, (((1,), (1,)), ((), ())), preferred_element_type=F32, precision=precision)


def _bdot(a, b):
    return _dot(a.astype(BF16), b.astype(BF16))


def _bdot_nt(a, b):
    return _dot_nt(a.astype(BF16), b.astype(BF16))


def _sigmoid(x):
    return 1.0 / (1.0 + jnp.exp(-x))


def _silu(x):
    return x * _sigmoid(x)


def _rms(x, eps=EPS):
    return x * lax.rsqrt(jnp.mean(x * x, axis=-1, keepdims=True) + eps)


def _pick_tile(cands, *dims):
    for t in cands:
        if all(d % t == 0 for d in dims):
            return t
    raise ValueError(f"no tile in {cands} divides {dims}")


def _mod_kernel(c_ref, w_ref, b_ref, o_ref):
    a = _silu(c_ref[...]).astype(BF16)
    o_ref[...] = _dot(a, w_ref[...].astype(BF16)) + b_ref[...]


def _modulation(cond8, w_mod, b_mod):
    depth, d, n6 = w_mod.shape
    tn = 1024
    return pl.pallas_call(
        _mod_kernel,
        grid=(depth, n6 // tn),
        in_specs=[
            pl.BlockSpec((SUBLANES, d), lambda l, n: (0, 0)),
            pl.BlockSpec((None, d, tn), lambda l, n: (l, 0, n)),
            pl.BlockSpec((None, 1, tn), lambda l, n: (l, 0, n)),
        ],
        out_specs=pl.BlockSpec((None, SUBLANES, tn), lambda l, n: (l, 0, n)),
        out_shape=jax.ShapeDtypeStruct((depth, SUBLANES, n6), F32),
        compiler_params=_cparams(2),
        name="adaln_mod",
    )(cond8, w_mod, b_mod.reshape(depth, 1, n6))


def _modulated_norm(x, g, mod, d):
    return _rms(x) * g * (1.0 + mod[:, d:2 * d]) + mod[:, 0:d]


def _norm_kernel(x_ref, mod_ref, g_ref, h_ref):
    h_ref[...] = _modulated_norm(x_ref[...], g_ref[...], mod_ref[...], x_ref.shape[1]).astype(BF16)


def _norm_mod(x, mod_l, norm_l, cond_of_row):
    m, d = x.shape
    r = ROW_TILE
    return pl.pallas_call(
        _norm_kernel,
        grid=(m // r,),
        in_specs=[
            pl.BlockSpec((r, d), lambda i: (i, 0)),
            pl.BlockSpec((None, 1, 6 * d), lambda i: (cond_of_row(i * r), 0, 0)),
            pl.BlockSpec((1, d), lambda i: (0, 0)),
        ],
        out_specs=pl.BlockSpec((r, d), lambda i: (i, 0)),
        out_shape=jax.ShapeDtypeStruct((m, d), BF16),
        compiler_params=_cparams(1),
        name="norm_mod",
    )(x, mod_l, norm_l.reshape(1, d))


def _in_kernel(h_ref, wt_ref, z_ref, w_scr):
    @pl.when(pl.program_id(1) == 0)
    def _():
        w_scr[...] = wt_ref[0].astype(BF16)

    z_ref[...] = _dot_nt(h_ref[...], w_scr[...])


def _in_proj(h, w_in_t, layer, tm):
    m, d = h.shape
    n_branch_tiles = OFF_RAW // IN_TILE_N
    assert SRC_BRANCH % SUBLANES == 0
    tile8 = IN_TILE_N // SUBLANES
    row0 = lambda n: SUBLANES * jnp.where(n < n_branch_tiles, SRC_BRANCH // SUBLANES + tile8 * n,
                                          tile8 * (n - n_branch_tiles))
    return pl.pallas_call(
        _in_kernel,
        grid=(NP_IN // IN_TILE_N, m // tm),
        in_specs=[
            pl.BlockSpec((tm, d), lambda n, i: (i, 0)),
            pl.BlockSpec((pl.Element(1), pl.Element(IN_TILE_N), pl.Element(d)),
                         lambda n, i: (layer, row0(n), 0)),
        ],
        out_specs=pl.BlockSpec((tm, IN_TILE_N), lambda n, i: (i, n)),
        out_shape=jax.ShapeDtypeStruct((m, NP_IN), F32),
        scratch_shapes=[pltpu.VMEM((IN_TILE_N, d), BF16)],
        compiler_params=_cparams(2),
        name="in_proj",
    )(h, w_in_t)


def _lambda(lq_ref, lam_init):
    lq = lq_ref[...]
    a = jnp.sum(lq[0:1] * lq[1:2], axis=-1, keepdims=True)
    b = jnp.sum(lq[2:3] * lq[3:4], axis=-1, keepdims=True)
    return jnp.exp(a) - jnp.exp(b) + lam_init


def _diff_attn(qb, kb, vb, lam):
    probs = []
    for mp in range(2):
        s = _dot_nt(qb[:, mp * DK_A:(mp + 1) * DK_A], kb[:, mp * DK_A:(mp + 1) * DK_A])
        e = jnp.exp(s - jnp.max(s, axis=-1, keepdims=True))
        probs.append(e / jnp.sum(e, axis=-1, keepdims=True))
    pd = probs[0] - lam * probs[1]
    return _dot(pd.astype(BF16), vb)


def _attn_ctx_kernel(lam_init, lq_ref, sub_ref, q_ref, k_ref, v_ref, o_ref, knew_ref, vnew_ref):
    lam = _lambda(lq_ref, lam_init)
    qb = (q_ref[...] * DK_A ** -0.5).astype(BF16)
    k, v = k_ref[...], v_ref[...]
    knew_ref[...] = k
    vnew_ref[...] = v
    o = _diff_attn(qb, k.astype(BF16), v.astype(BF16), lam)
    o_ref[...] = (_rms(o) * sub_ref[...] * (1.0 - lam_init)).astype(BF16)


def _rope(x, cos, sin_signed):
    lane = lax.broadcasted_iota(jnp.int32, x.shape, 1)
    first_half = (lane % (ROPE_AXIS)) < (ROPE_AXIS // 2)
    partner = jnp.where(first_half, pltpu.roll(x, LANES - ROPE_AXIS // 2, 1), pltpu.roll(x, ROPE_AXIS // 2, 1))
    return x * cos + partner * sin_signed


def _attn_lat_kernel(lam_init, t_lat, lq_ref, sub_ref, q_ref, k_ref, v_ref, ck_ref, cv_ref,
                     cq_ref, sq_ref, ckk_ref, skk_ref, o_ref, k_scr, v_scr):
    @pl.when(pl.program_id(2) == 0)
    def _():
        k_scr[0:t_lat, :] = _rope(k_ref[...], ckk_ref[...], skk_ref[...]).astype(BF16)
        k_scr[t_lat:, :] = ck_ref[...].astype(BF16)
        v_scr[0:t_lat, :] = v_ref[...].astype(BF16)
        v_scr[t_lat:, :] = cv_ref[...].astype(BF16)

    lam = _lambda(lq_ref, lam_init)
    q = _rope(q_ref[...], cq_ref[...], sq_ref[...]) * DK_A ** -0.5
    o = _diff_attn(q.astype(BF16), k_scr[...], v_scr[...], lam)
    o_ref[...] = (_rms(o) * sub_ref[...] * (1.0 - lam_init)).astype(BF16)


def _attention(z, lq_l, sub_l, cache_k_l, cache_v_l, rope_tabs, lam_init, dims):
    b_ctx, t_ctx, b_lat, t_lat = dims
    n_ctx = b_ctx * t_ctx
    past = cache_k_l.shape[1]
    cq, ck, cv = OFF_Q // LANES, OFF_K // LANES, OFF_V // LANES
    small = [pl.BlockSpec((4, DK_A), lambda *_: (0, 0)), pl.BlockSpec((1, DV_A), lambda *_: (0, 0))]
    sub2 = sub_l.reshape(1, DV_A)

    head_blk = pl.BlockSpec((t_ctx, LANES), lambda b, h: (b, h))
    oa_ctx, k_new, v_new = pl.pallas_call(
        functools.partial(_attn_ctx_kernel, lam_init),
        grid=(b_ctx, N_HEADS_A),
        in_specs=small + [
            pl.BlockSpec((t_ctx, LANES), lambda b, h: (b, cq + h)),
            pl.BlockSpec((t_ctx, LANES), lambda b, h: (b, ck + h)),
            pl.BlockSpec((t_ctx, LANES), lambda b, h: (b, cv + h)),
        ],
        out_specs=[head_blk] * 3,
        out_shape=[jax.ShapeDtypeStruct((n_ctx, W_A), BF16), jax.ShapeDtypeStruct((n_ctx, QK_A), F32),
                   jax.ShapeDtypeStruct((n_ctx, W_A), F32)],
        compiler_params=_cparams(2),
        name="attn_ctx",
    )(lq_l, sub2, z, z, z)

    tq = _pick_tile((256, 128), t_lat)
    nq = t_lat // tq
    rb_q = n_ctx // tq
    rb_k = n_ctx // t_lat
    cos_t, sin_t = rope_tabs
    oa_lat = pl.pallas_call(
        functools.partial(_attn_lat_kernel, lam_init, t_lat),
        grid=(b_lat, N_HEADS_A, nq),
        in_specs=small + [
            pl.BlockSpec((tq, LANES), lambda b, h, i: (rb_q + b * nq + i, cq + h)),
            pl.BlockSpec((t_lat, LANES), lambda b, h, i: (rb_k + b, ck + h)),
            pl.BlockSpec((t_lat, LANES), lambda b, h, i: (rb_k + b, cv + h)),
            pl.BlockSpec((None, past, LANES), lambda b, h, i: (b, 0, h)),
            pl.BlockSpec((None, past, LANES), lambda b, h, i: (b, 0, h)),
            pl.BlockSpec((tq, LANES), lambda b, h, i: (i, 0)),
            pl.BlockSpec((tq, LANES), lambda b, h, i: (i, 0)),
            pl.BlockSpec((t_lat, LANES), lambda b, h, i: (0, 0)),
            pl.BlockSpec((t_lat, LANES), lambda b, h, i: (0, 0)),
        ],
        out_specs=pl.BlockSpec((tq, LANES), lambda b, h, i: (b * nq + i, h)),
        out_shape=jax.ShapeDtypeStruct((b_lat * t_lat, W_A), BF16),
        scratch_shapes=[pltpu.VMEM((t_lat + past, LANES), BF16), pltpu.VMEM((t_lat + past, LANES), BF16)],
        compiler_params=_cparams(3),
        name="attn_lat",
    )(lq_l, sub2, z, z, z, cache_k_l, cache_v_l, cos_t, sin_t, cos_t, sin_t)
    return (oa_ctx, oa_lat), k_new, v_new


def _rope_tables(t_lat):
    rows = t_lat // GRID_W
    r = jnp.repeat(jnp.arange(rows), GRID_W).astype(F32)
    col = jnp.tile(jnp.arange(GRID_W), rows).astype(F32)
    inv = ROPE_BASE ** (-jnp.arange(0, ROPE_AXIS, 2, dtype=F32) / ROPE_AXIS)
    ang_r, ang_c = r[:, None] * inv, col[:, None] * inv
    cos64 = jnp.concatenate([jnp.cos(ang_r)] * 2 + [jnp.cos(ang_c)] * 2, axis=-1)
    sin64 = jnp.concatenate([-jnp.sin(ang_r), jnp.sin(ang_r), -jnp.sin(ang_c), jnp.sin(ang_c)], axis=-1)
    return jnp.tile(cos64, (1, 2)), jnp.tile(sin64, (1, 2))


def _log_decay(alpha, a_log, dt):
    x = alpha + dt
    return -jnp.exp(a_log) * (jnp.maximum(x, 0.0) + jnp.log(1.0 + jnp.exp(-jnp.abs(x))))


def _chunk_cumsum(g, axis):
    n = g.shape[axis]
    pos = lax.broadcasted_iota(jnp.int32, g.shape, axis) % DELTA_CHUNK
    f, b = g, g
    s = 1
    while s < DELTA_CHUNK:
        f = f + jnp.where(pos >= s, pltpu.roll(f, s, axis), 0.0)
        b = b + jnp.where(pos < DELTA_CHUNK - s, pltpu.roll(b, n - s, axis), 0.0)
        s *= 2
    return f, b


def _delta_prepare(q_scr, k_scr, v_scr, ba, bat, al_row, al_col, dt_row, dt_col,
                   u_ref, l1_ref, l2_ref, e_ref):
    c, nh = DELTA_CHUNK, N_HEADS_B
    rows_n = q_scr.shape[0]
    beta_all = _sigmoid(ba[:, 0:2 * nh])
    gf, gb = _chunk_cumsum(_log_decay(ba[:, 2 * nh:4 * nh], al_row[...], dt_row[...]), 0)
    gc_all = jnp.where(lax.broadcasted_iota(jnp.int32, gf.shape, 1) < nh, gf, gb)
    gtf, gtb = _chunk_cumsum(_log_decay(bat[2 * nh:4 * nh, :], al_col[...], dt_col[...]), 1)
    gct_all = jnp.where(lax.broadcasted_iota(jnp.int32, gtf.shape, 0) < nh, gtf, gtb)

    ri = lax.broadcasted_iota(jnp.int32, (c, c), 0)
    ci = lax.broadcasted_iota(jnp.int32, (c, c), 1)
    masks = ((ri >= ci, ri > ci), (ri <= ci, ri < ci))
    eye = (ri == ci).astype(F32)
    blk_sizes = [INV_BASE_BLOCK << i for i in range(int(math.log2(c // INV_BASE_BLOCK)) + 1)]
    same_blk = [(ri // s) == (ci // s) for s in blk_sizes]
    n_sq = int(math.log2(INV_BASE_BLOCK)) - 1

    for j in range(rows_n // c):
        rs = slice(j * c, (j + 1) * c)
        chains = []
        for d in range(2):
            incl, strict = masks[d]
            last = 0 if d else c - 1
            for h in range(nh):
                idx = d * nh + h
                hs = slice(h * DK_B, (h + 1) * DK_B)
                gc = gc_all[rs, idx:idx + 1]
                gr = gct_all[idx:idx + 1, rs]
                ch = dict(d=d, h=h, idx=idx, hs=hs, strict=strict, gc=gc, g_last=gc[last:last + 1, :],
                          q=q_scr[rs, hs] * DK_B ** -0.5, k=k_scr[rs, hs], v=v_scr[rs, hs],
                          beta=beta_all[rs, idx:idx + 1],
                          decay=jnp.where(incl, jnp.exp(jnp.where(incl, gc - gr, 0.0)), 0.0))
                ch["kb"] = ch["k"] * ch["beta"]
                chains.append(ch)
        for ch in chains:
            kbf = ch["k"].astype(BF16)
            a = jnp.where(ch["strict"], -(_dot_nt(ch["kb"].astype(BF16), kbf) * ch["decay"]), 0.0)
            ch["attn"] = _dot_nt(ch["q"].astype(BF16), kbf) * ch["decay"]
            ch["a"] = a
            ch["apow"] = jnp.where(same_blk[0], a, 0.0)
            ch["p"] = eye + ch["apow"]
        for _ in range(n_sq):
            for ch in chains:
                ch["apow"] = _bdot(ch["apow"], ch["apow"])
            for ch in chains:
                ch["p"] = ch["p"] + _bdot(ch["p"], ch["apow"])
        for lvl in range(1, len(same_blk)):
            for ch in chains:
                a_off = jnp.where(same_blk[lvl] & jnp.logical_not(same_blk[lvl - 1]), ch["a"], 0.0)
                ch["t"] = _bdot(ch["p"], a_off)
            for ch in chains:
                ch["p"] = ch["p"] + _bdot(ch["t"], ch["p"])
        for ch in chains:
            rhs = jnp.concatenate([ch["v"] * ch["beta"], ch["kb"] * jnp.exp(ch["gc"])], axis=-1)
            ch["sol"] = _bdot(ch["p"], rhs)
        for ch in chains:
            d, h, hs = ch["d"], ch["h"], ch["hs"]
            u_ref[d, rs, hs] = ch["sol"][:, :DV_B]
            l1_ref[d, j, h, 0:c, :] = ch["sol"][:, DV_B:].astype(BF16)
            l1_ref[d, j, h, c:2 * c, :] = (ch["q"] * jnp.exp(ch["gc"])).astype(BF16)
            l2_ref[d, j, h, 0:c, :] = ch["attn"].astype(BF16)
            l2_ref[d, j, h, c:, :] = (ch["k"] * jnp.exp(ch["g_last"] - ch["gc"])).T.astype(BF16)
            e_ref[j, ch["idx"]:ch["idx"] + 1, :] = jnp.broadcast_to(jnp.exp(ch["g_last"]), (1, LANES))


def _local_kernel(seq, cur_ref, prev_ref, next_ref, pcur_ref, pprev_ref, pnext_ref, bat_ref,
                  cw_ref, pw_ref, ps_ref, al_row, al_col, dt_row, dt_col,
                  oc_ref, u_ref, l1_ref, l2_ref, e_ref, q_scr, k_scr, v_scr):
    rows_n = cur_ref.shape[0]
    pos0, t_seq = seq(pl.program_id(0) * rows_n)
    has_prev = pos0 > 0
    has_next = pos0 + rows_n < t_seq
    rows = lax.broadcasted_iota(jnp.int32, (rows_n, 1), 0)

    x = cur_ref[...]
    prev_row = jnp.where(has_prev, prev_ref[SUBLANES - 1:SUBLANES, :], 0.0)
    next_row = jnp.where(has_next, next_ref[0:1, :], 0.0)
    xm1 = jnp.where(rows == 0, prev_row, pltpu.roll(x, 1, 0))
    xp1 = jnp.where(rows == rows_n - 1, next_row, pltpu.roll(x, rows_n - 1, 0))
    y = _silu(cw_ref[0:1, :] * xm1 + cw_ref[1:2, :] * x + cw_ref[2:3, :] * xp1)
    for h in range(N_HEADS_B):
        for j, ref in enumerate((q_scr, k_scr)):
            a = y[:, j * W_B + h * DK_B:j * W_B + (h + 1) * DK_B]
            ref[:, h * DK_B:(h + 1) * DK_B] = a * lax.rsqrt(jnp.sum(a * a, axis=-1, keepdims=True) + EPS)
    v_scr[...] = y[:, 2 * W_B:]
    _delta_prepare(q_scr, k_scr, v_scr, pcur_ref[:, 0:LANES], bat_ref[...], al_row, al_col, dt_row, dt_col,
                   u_ref, l1_ref, l2_ref, e_ref)

    def pool_cols(ref):
        t = ref[...]
        return pltpu.roll(t, t.shape[1] - BRANCH_SHIFT, 1)[:, :W_C]

    pz = pool_cols(pcur_ref)
    ext = jnp.concatenate([jnp.where(has_prev, pool_cols(pprev_ref), 0.0), pz,
                           jnp.where(has_next, pool_cols(pnext_ref), 0.0)], axis=0)
    n_ext = rows_n + 2 * SUBLANES
    tpos = pos0 + rows
    for gi, win in enumerate(POOL_WINDOWS):
        sl = slice(gi * POOL_GROUP_W, (gi + 1) * POOL_GROUP_W)
        xg = ext[:, sl]
        acc = xg + pltpu.roll(xg, 1, 0)
        s = 1
        while 2 * s < win:
            acc = pltpu.roll(acc, s, 0) + pltpu.roll(acc, n_ext - s, 0)
            s *= 2
        acc = acc[SUBLANES:SUBLANES + rows_n]
        cnt = (jnp.minimum(tpos + win // 2, t_seq) - jnp.maximum(tpos - win // 2, 0)).astype(F32)
        d = acc / cnt - pz[:, sl]
        yg = _dot(d.astype(BF16), pw_ref[gi].astype(BF16)) * ps_ref[:, sl]
        oc_ref[:, sl] = yg.astype(BF16)


def _local(z, bat, conv_w_l, pool_w_l, pool_scale_l, a_log_l, dt_l, seq):
    m = z.shape[0]
    r, c = ROW_TILE, DELTA_CHUNK
    hb = r // SUBLANES
    last_hb = m // SUBLANES - 1
    c_qkv, c_pool = OFF_QKV_B // (3 * W_B), OFF_BA // IN_TILE_N
    nh2 = 2 * N_HEADS_B
    mc, cpt = m // c, r // c
    prev_map = lambda cb: (lambda i: (jnp.maximum(i * hb - 1, 0), cb))
    next_map = lambda cb: (lambda i: (jnp.minimum((i + 1) * hb, last_hb), cb))
    row = pl.BlockSpec((1, nh2), lambda i: (0, 0))
    col = pl.BlockSpec((nh2, 1), lambda i: (0, 0))
    return pl.pallas_call(
        functools.partial(_local_kernel, seq),
        grid=(m // r,),
        in_specs=[
            pl.BlockSpec((r, 3 * W_B), lambda i: (i, c_qkv)),
            pl.BlockSpec((SUBLANES, 3 * W_B), prev_map(c_qkv)),
            pl.BlockSpec((SUBLANES, 3 * W_B), next_map(c_qkv)),
            pl.BlockSpec((r, IN_TILE_N), lambda i: (i, c_pool)),
            pl.BlockSpec((SUBLANES, IN_TILE_N), prev_map(c_pool)),
            pl.BlockSpec((SUBLANES, IN_TILE_N), next_map(c_pool)),
            pl.BlockSpec((2 * nh2, r), lambda i: (0, i)),
            pl.BlockSpec((3, 3 * W_B), lambda i: (0, 0)),
            pl.BlockSpec((len(POOL_WINDOWS), POOL_GROUP_W, POOL_GROUP_W), lambda i: (0, 0, 0)),
            pl.BlockSpec((1, W_C), lambda i: (0, 0)),
            row, col, row, col,
        ],
        out_specs=[
            pl.BlockSpec((r, W_C), lambda i: (i, 0)),
            pl.BlockSpec((2, r, W_B), lambda i: (0, i, 0)),
            pl.BlockSpec((2, cpt, N_HEADS_B, 2 * c, DV_B), lambda i: (0, i, 0, 0, 0)),
            pl.BlockSpec((2, cpt, N_HEADS_B, c + DK_B, c), lambda i: (0, i, 0, 0, 0)),
            pl.BlockSpec((cpt, nh2, LANES), lambda i: (i, 0, 0)),
        ],
        out_shape=[
            jax.ShapeDtypeStruct((m, W_C), BF16),
            jax.ShapeDtypeStruct((2, m, W_B), F32),
            jax.ShapeDtypeStruct((2, mc, N_HEADS_B, 2 * c, DV_B), BF16),
            jax.ShapeDtypeStruct((2, mc, N_HEADS_B, c + DK_B, c), BF16),
            jax.ShapeDtypeStruct((mc, nh2, LANES), F32),
        ],
        scratch_shapes=[pltpu.VMEM((r, W_B), F32)] * 3,
        compiler_params=_cparams(1),
        name="local_conv_pool",
    )(z, z, z, z, z, z, bat, conv_w_l, pool_w_l, pool_scale_l.reshape(1, W_C),
      a_log_l.reshape(1, nh2), a_log_l.reshape(nh2, 1), dt_l.reshape(1, nh2), dt_l.reshape(nh2, 1))


def _scan_kernel(par, b_lat, n_ctx_ch, *refs):
    c, nh = DELTA_CHUNK, N_HEADS_B
    n_slot = par + b_lat
    ins = refs[:n_slot * 8]
    s0_ref = refs[n_slot * 8]
    of_ctx, ob_ctx, of_lat, ob_lat, sfin_ref, s_scr = refs[n_slot * 8 + 1:]
    step = pl.program_id(0)
    cc = step % n_ctx_ch

    @pl.when(cc == 0)
    def _():
        s_scr[0:par] = jnp.zeros((par,) + s_scr.shape[1:], F32)

    @pl.when(step == 0)
    def _():
        s_scr[par:] = s0_ref[...]

    for slot in range(n_slot):
        outs = (of_ctx, ob_ctx, slot) if slot < par else (of_lat, ob_lat, slot - par)
        chains = []
        for d in range(2):
            l1_ref, l2_ref, u_ref, e_ref = ins[(slot * 2 + d) * 4:(slot * 2 + d) * 4 + 4]
            for h in range(nh):
                chains.append(dict(d=d, h=h, idx=d * nh + h, l1=l1_ref, l2=l2_ref, u=u_ref, e=e_ref))
        for ch in chains:
            ch["s"] = s_scr[slot, ch["idx"]]
            ch["r1"] = _dot(ch["l1"][ch["h"]], ch["s"].astype(BF16))
        for ch in chains:
            hs = slice(ch["h"] * DV_B, (ch["h"] + 1) * DV_B)
            v_new = ch["u"][:, hs] - ch["r1"][:c]
            ch["r2"] = _dot(ch["l2"][ch["h"]], v_new.astype(BF16))
        for ch in chains:
            hs = slice(ch["h"] * DV_B, (ch["h"] + 1) * DV_B)
            outs[ch["d"]][outs[2], :, hs] = ch["r1"][c:] + ch["r2"][:c]
            e = ch["e"][ch["idx"]:ch["idx"] + 1, :]
            s_scr[slot, ch["idx"]] = ch["s"] * e + ch["r2"][c:]

    @pl.when(cc == n_ctx_ch - 1)
    def _():
        sfin_ref[...] = s_scr[0:par]


def _delta_scan(u, l1, l2, e, s0_lat, dims):
    b_ctx, t_ctx, b_lat, t_lat = dims
    c, nh = DELTA_CHUNK, N_HEADS_B
    n_ctx_ch, n_lat_ch = t_ctx // c, t_lat // c
    par = b_ctx * n_ctx_ch // n_lat_ch
    assert par >= 1 and par * n_lat_ch == b_ctx * n_ctx_ch and b_ctx % par == 0
    mc = l1.shape[1]
    ctx_chunks = b_ctx * n_ctx_ch
    u4 = u.reshape(2, mc, c, W_B)

    def chunk_of(slot, d):
        if slot < par:
            return lambda s: ((s // n_ctx_ch) * par + slot) * n_ctx_ch + (
                (n_ctx_ch - 1 - s % n_ctx_ch) if d else s % n_ctx_ch)
        q = slot - par
        return lambda s: ctx_chunks + q * n_lat_ch + ((n_lat_ch - 1 - s) if d else s)

    in_specs, args = [], []
    for slot in range(par + b_lat):
        for d in range(2):
            cg = chunk_of(slot, d)
            in_specs += [
                pl.BlockSpec((None, None, nh, 2 * c, DV_B), lambda s, cg=cg, d=d: (d, cg(s), 0, 0, 0)),
                pl.BlockSpec((None, None, nh, c + DK_B, c), lambda s, cg=cg, d=d: (d, cg(s), 0, 0, 0)),
                pl.BlockSpec((None, None, c, W_B), lambda s, cg=cg, d=d: (d, cg(s), 0, 0)),
                pl.BlockSpec((None, 2 * nh, LANES), lambda s, cg=cg: (cg(s), 0, 0)),
            ]
            args += [l1, l2, u4, e]
    in_specs.append(pl.BlockSpec((b_lat, 2 * nh, DK_B, DV_B), lambda s: (0, 0, 0, 0)))
    args.append(s0_lat)

    ctx_o = jax.ShapeDtypeStruct((b_ctx // par, par, n_ctx_ch, c, W_B), F32)
    lat_o = jax.ShapeDtypeStruct((b_lat, n_lat_ch, c, W_B), F32)
    cf = lambda s: s % n_ctx_ch
    of_ctx, ob_ctx, of_lat, ob_lat, s_fin = pl.pallas_call(
        functools.partial(_scan_kernel, par, b_lat, n_ctx_ch),
        grid=(n_lat_ch,),
        in_specs=in_specs,
        out_specs=[
            pl.BlockSpec((None, par, None, c, W_B), lambda s: (s // n_ctx_ch, 0, cf(s), 0, 0)),
            pl.BlockSpec((None, par, None, c, W_B), lambda s: (s // n_ctx_ch, 0, n_ctx_ch - 1 - cf(s), 0, 0)),
            pl.BlockSpec((b_lat, None, c, W_B), lambda s: (0, s, 0, 0)),
            pl.BlockSpec((b_lat, None, c, W_B), lambda s: (0, n_lat_ch - 1 - s, 0, 0)),
            pl.BlockSpec((par, 2 * nh, DK_B, DV_B), lambda s: (s // n_ctx_ch, 0, 0, 0)),
        ],
        out_shape=[ctx_o, ctx_o, lat_o, lat_o, jax.ShapeDtypeStruct((b_ctx, 2 * nh, DK_B, DV_B), F32)],
        scratch_shapes=[pltpu.VMEM((par + b_lat, 2 * nh, DK_B, DV_B), F32)],
        compiler_params=_cparams(1),
        name="delta_scan",
    )(*args)
    n_ctx, n_lat = b_ctx * t_ctx, b_lat * t_lat
    return (of_ctx.reshape(n_ctx, W_B), ob_ctx.reshape(n_ctx, W_B), of_lat.reshape(n_lat, W_B),
            ob_lat.reshape(n_lat, W_B), s_fin)


def _route_rows(sel, scores):
    g = EXPERTS_PER_GROUP
    gscore = []
    for gi in range(N_EXPERT_GROUPS):
        a, b, c, d = sel[gi * g:(gi + 1) * g]
        hi1, lo1 = jnp.maximum(a, b), jnp.minimum(a, b)
        hi2, lo2 = jnp.maximum(c, d), jnp.minimum(c, d)
        gscore.append(jnp.maximum(hi1, hi2) + jnp.maximum(jnp.minimum(hi1, hi2), jnp.maximum(lo1, lo2)))
    best = jnp.zeros_like(gscore[0], dtype=jnp.int32)
    bestv = gscore[0]
    for gi in range(1, N_EXPERT_GROUPS):
        upd = gscore[gi] > bestv
        best = jnp.where(upd, gi, best)
        bestv = jnp.where(upd, gscore[gi], bestv)

    def in_best(rows, j):
        out = rows[j]
        for gi in range(1, N_EXPERT_GROUPS):
            out = jnp.where(best == gi, rows[gi * g + j], out)
        return out

    e_sel = [in_best(sel, j) for j in range(g)]
    e_sc = [in_best(scores, j) for j in range(g)]

    def first_argmax(vals):
        bi, bv, bs = jnp.zeros_like(best), vals[0], e_sc[0]
        for j in range(1, g):
            upd = vals[j] > bv
            bi = jnp.where(upd, j, bi)
            bv = jnp.where(upd, vals[j], bv)
            bs = jnp.where(upd, e_sc[j], bs)
        return bi, bs

    i0, s0 = first_argmax(e_sel)
    i1, s1 = first_argmax([jnp.where(i0 == j, -jnp.inf, e_sel[j]) for j in range(g)])
    tot = s0 + s1
    return best * g + i0, best * g + i1, s0 / tot, s1 / tot


def _mix_kernel(n_ctx, x_ref, mod_ref, oac_ref, oal_ref, ofc_ref, obc_ref, ofl_ref, obl_ref,
                gb_ref, oc_ref, g0_ref, g1_ref, g2_ref,
                dn_ref, n2_ref, wpa_ref, wpb_ref, wpc_ref, wout_ref, wr_ref, br_ref,
                x1_ref, h2_ref, idx_ref, wt_ref):
    d = x_ref.shape[1]
    is_ctx = pl.program_id(0) * x_ref.shape[0] < n_ctx
    dn = dn_ref[...]
    o = jnp.where(is_ctx, ofc_ref[...] + obc_ref[...], ofl_ref[...] + obl_ref[...])
    o_a = jnp.where(is_ctx, oac_ref[...], oal_ref[...])
    gate = _silu(gb_ref[...])
    parts = []
    for h in range(N_HEADS_B):
        sl = slice(h * DV_B, (h + 1) * DV_B)
        parts.append((_rms(o[:, sl]) * dn * gate[:, sl]).astype(BF16))
    o_b = jnp.concatenate(parts, axis=-1)
    mixed = (_sigmoid(g0_ref[...]) * _dot(o_a, wpa_ref[...])
             + _sigmoid(g1_ref[...]) * _dot(o_b, wpb_ref[...])
             + _sigmoid(g2_ref[...]) * _dot(oc_ref[...], wpc_ref[...]))
    gate1 = mod_ref[:, 2 * d:3 * d]
    x1 = x_ref[...] + gate1 * _dot(mixed.astype(BF16), wout_ref[...])
    x1_ref[...] = x1
    shift2 = mod_ref[:, 3 * d:4 * d]
    scale2 = mod_ref[:, 4 * d:5 * d]
    h2 = _rms(x1) * n2_ref[...] * (1.0 + scale2) + shift2
    h2_ref[...] = h2
    sc = _sigmoid(_dot_nt(wr_ref[...], h2, HIGHEST))
    sel = sc + br_ref[...]
    i0, i1, w0, w1 = _route_rows([sel[e:e + 1] for e in range(N_EXPERTS)],
                                 [sc[e:e + 1] for e in range(N_EXPERTS)])
    idx_ref[0:1, :] = i0
    idx_ref[1:2, :] = i1
    wt_ref[0:1, :] = w0
    wt_ref[1:2, :] = w1


def _mix(x, mod_l, oa, delta_o, z, oc, delta_norm_l, norm2_l, wpa, wpb, wpc, wout, layer, wr_t, br,
         cond_of_row):
    m, d = x.shape
    r = ROW_TILE
    n_ctx = cond_of_row.n_ctx
    nct = n_ctx // r
    c_gb = OFF_GATE_B // W_B
    once = dict(pipeline_mode=pl.Buffered(1))
    full = lambda shape: pl.BlockSpec(shape, lambda i: (0,) * len(shape), **once)
    of_layer = lambda shape: pl.BlockSpec((None,) + shape, lambda i: (layer,) + (0,) * len(shape), **once)
    ctx_rows = lambda w: pl.BlockSpec((r, w), lambda i: (jnp.minimum(i, nct - 1), 0))
    lat_rows = lambda w: pl.BlockSpec((r, w), lambda i: (jnp.maximum(i - nct, 0), 0))
    of_ctx, ob_ctx, of_lat, ob_lat = delta_o
    return pl.pallas_call(
        functools.partial(_mix_kernel, n_ctx),
        grid=(m // r,),
        in_specs=[
            pl.BlockSpec((r, d), lambda i: (i, 0)),
            pl.BlockSpec((None, 1, 6 * d), lambda i: (cond_of_row(i * r), 0, 0)),
            ctx_rows(W_A), lat_rows(W_A),
            ctx_rows(W_B), ctx_rows(W_B), lat_rows(W_B), lat_rows(W_B),
            pl.BlockSpec((r, W_B), lambda i: (i, c_gb)),
            pl.BlockSpec((r, W_C), lambda i: (i, 0)),
            pl.BlockSpec((r, d), lambda i: (i, 0)),
            pl.BlockSpec((r, d), lambda i: (i, 1)),
            pl.BlockSpec((r, d), lambda i: (i, 2)),
            full((1, DV_B)), full((1, d)),
            of_layer((W_A, d)), of_layer((W_B, d)), of_layer((W_C, d)), of_layer((d, d)),
            full((N_EXPERTS, d)), full((N_EXPERTS, 1)),
        ],
        out_specs=[
            pl.BlockSpec((r, d), lambda i: (i, 0)),
            pl.BlockSpec((r, d), lambda i: (i, 0)),
            pl.BlockSpec((TOP_K, r), lambda i: (0, i)),
            pl.BlockSpec((TOP_K, r), lambda i: (0, i)),
        ],
        out_shape=[
            jax.ShapeDtypeStruct((m, d), F32),
            jax.ShapeDtypeStruct((m, d), F32),
            jax.ShapeDtypeStruct((TOP_K, m), jnp.int32),
            jax.ShapeDtypeStruct((TOP_K, m), F32),
        ],
        compiler_params=_cparams(1),
        name="mix_route",
    )(x, mod_l, oa[0], oa[1], of_ctx, ob_ctx, of_lat, ob_lat, z, oc, z, z, z,
      delta_norm_l.reshape(1, DV_B), norm2_l.reshape(1, d), wpa, wpb, wpc, wout, wr_t, br)


def _dispatch_meta(idx_t, bm):
    m = idx_t.shape[1]
    nk = m * TOP_K
    flat_e = idx_t.T.reshape(nk)
    experts = jnp.arange(N_EXPERTS, dtype=jnp.int32)[None, :]
    counts = jnp.sum((flat_e[:, None] == experts).astype(jnp.int32), axis=0)
    order = jnp.argsort(flat_e, stable=True).astype(jnp.int32)
    start = jnp.cumsum(counts) - counts
    padded = (counts + bm - 1) // bm * bm
    pad_end = jnp.cumsum(padded)
    pad_start = pad_end - padded
    nb = -(-(nk + N_EXPERTS * (bm - 1)) // bm)
    p = jnp.arange(nb * bm, dtype=jnp.int32)
    e_of_p = jnp.minimum(jnp.sum((pad_end[None, :] <= p[:, None]).astype(jnp.int32), axis=1), N_EXPERTS - 1)
    pick = (e_of_p[:, None] == experts).astype(jnp.int32)
    rank = p - jnp.sum(pick * pad_start[None, :], axis=1)
    real = rank < jnp.sum(pick * counts[None, :], axis=1)
    assign = order[jnp.clip(jnp.sum(pick * start[None, :], axis=1) + rank, 0, nk - 1)]
    row_tok = jnp.where(real, assign // TOP_K, 0)
    spare = nk + ((p // bm) % 2) * bm + p % bm
    row_dst = jnp.where(real, (assign % TOP_K) * m + assign // TOP_K, spare)
    prime = nk + jnp.arange(2 * bm, dtype=jnp.int32)
    n_used = (pad_end[-1] // bm).astype(jnp.int32)
    blk_e = e_of_p[::bm]
    n_chunks = 3 * MOE_MAT_CHUNKS
    e_ids = experts[0]
    n_blk = padded // bm
    has = n_blk > 0
    later = lax.cummin(jnp.where(has, e_ids, N_EXPERTS)[::-1])[::-1]
    nxt_of_e = jnp.concatenate([later[1:], jnp.full((1,), N_EXPERTS, jnp.int32)])
    set_of_e = (jnp.cumsum(has.astype(jnp.int32)) - 1) % 2
    b = jnp.arange(nb, dtype=jnp.int32)
    k = b - (pad_start // bm)[blk_e]
    n_e = jnp.maximum(n_blk[blk_e], 1)
    streams = (nxt_of_e[blk_e] < N_EXPERTS) & (b < n_used)
    c0 = jnp.where(streams, n_chunks * k // n_e, 0)
    c1 = jnp.where(streams, n_chunks * (k + 1) // n_e, 0)
    nxt_e = jnp.where(streams, nxt_of_e[blk_e], 0)
    return (blk_e, n_used.reshape(1), row_tok, jnp.concatenate([prime, row_dst]),
            set_of_e[blk_e], nxt_e, c0, c1)


def _moe_kernel(layer, blk_e_ref, n_used_ref, row_tok_ref, row_dst_ref, set_ref, nxt_ref, c0_ref, c1_ref,
                h2_hbm, wg_hbm, wu_hbm, wd_hbm, out_hbm,
                xbuf, ybuf, wg_s, wu_s, wd_s, st_gu, st_d, gsem, ssem, wsem_gu, wsem_d):
    i = pl.program_id(0)
    bm = xbuf.shape[1]
    n_used = n_used_ref[0]
    n_mat = MOE_MAT_CHUNKS
    n_stage = st_gu.shape[0]
    rg, rd = st_gu.shape[1], st_d.shape[1]

    def chunk_ops(c, j, e, s):
        cg, cu, cd = c, c - n_mat, c - 2 * n_mat
        rows_g = pl.ds(pl.multiple_of(cg * rg, rg), rg)
        rows_u = pl.ds(pl.multiple_of(cu * rg, rg), rg)
        rows_d = pl.ds(pl.multiple_of(cd * rd, 2 * SUBLANES), rd)

        def to_set(dst, rows, stage):
            def convert():
                dst[s, rows, :] = stage[j].astype(BF16)
            return convert

        return (
            (c < n_mat, lambda: pltpu.make_async_copy(wg_hbm.at[layer, e, rows_g, :], st_gu.at[j], wsem_gu.at[j]),
             to_set(wg_s, rows_g, st_gu)),
            ((c >= n_mat) & (c < 2 * n_mat),
             lambda: pltpu.make_async_copy(wu_hbm.at[layer, e, rows_u, :], st_gu.at[j], wsem_gu.at[j]),
             to_set(wu_s, rows_u, st_gu)),
            (c >= 2 * n_mat, lambda: pltpu.make_async_copy(wd_hbm.at[layer, e, rows_d, :], st_d.at[j], wsem_d.at[j]),
             to_set(wd_s, rows_d, st_d)),
        )

    def issue_round(r, c_lo, c_hi, e, s):
        for j in range(n_stage):
            c = c_lo + r * n_stage + j
            for pred, copy, _ in chunk_ops(c, j, e, s):
                pl.when(pred & (c < c_hi))(lambda copy=copy: copy().start())

    def finish_round(r, c_lo, c_hi, e, s):
        for j in range(n_stage):
            c = c_lo + r * n_stage + j
            for pred, copy, convert in chunk_ops(c, j, e, s):
                @pl.when(pred & (c < c_hi))
                def _(copy=copy, convert=convert):
                    copy().wait()
                    convert()

    def sync_rounds(r_lo, c_lo, c_hi, e, s):
        def body(r, carry):
            issue_round(r, c_lo, c_hi, e, s)
            finish_round(r, c_lo, c_hi, e, s)
            return carry
        n_rounds = (jnp.maximum(c_hi - c_lo, 0) + n_stage - 1) // n_stage
        lax.fori_loop(r_lo, jnp.maximum(n_rounds, r_lo), body, 0)

    def gather_row(blk, slot, r):
        tok = row_tok_ref[blk * bm + r]
        pltpu.make_async_copy(h2_hbm.at[pl.ds(tok, 1)], xbuf.at[slot, pl.ds(r, 1)], gsem.at[slot]).start()

    def scatter_row(tab_blk, slot, r):
        dst = row_dst_ref[tab_blk * bm + r]
        pltpu.make_async_copy(ybuf.at[slot, pl.ds(r, 1)], out_hbm.at[pl.ds(dst, 1)], ssem.at[slot]).start()

    def looped(fn, *args):
        def body(r, carry):
            fn(*args, r)
            return carry
        lax.fori_loop(0, bm, body, 0, unroll=8)

    def wait_gather(slot):
        pltpu.make_async_copy(h2_hbm.at[pl.ds(0, bm)], xbuf.at[slot], gsem.at[slot]).wait()

    def wait_scatter(slot):
        pltpu.make_async_copy(ybuf.at[slot], out_hbm.at[pl.ds(0, bm)], ssem.at[slot]).wait()

    @pl.when(i == 0)
    def _():
        ybuf[...] = jnp.zeros_like(ybuf)
        looped(gather_row, 0, 0)
        looped(scatter_row, 0, 0)
        sync_rounds(0, 0, 3 * n_mat, blk_e_ref[0], set_ref[0])

    def compute_block(slot):
        other = 1 - slot
        nxt = jnp.minimum(i + 1, n_used - 1)
        s, e_next, c_lo, c_hi = set_ref[i], nxt_ref[i], c0_ref[i], c1_ref[i]
        stream = (c_lo, c_hi, e_next, 1 - s)
        wait_gather(slot)
        for r in range(bm):
            gather_row(nxt, other, r)
            scatter_row(i + 1, other, r)
        xb = xbuf[slot].astype(BF16)
        issue_round(0, *stream)
        gate = _dot(xb, wg_s[s])
        finish_round(0, *stream)
        issue_round(1, *stream)
        act = _silu(gate) * _dot(xb, wu_s[s])
        finish_round(1, *stream)
        issue_round(2, *stream)
        y = _dot(act.astype(BF16), wd_s[s])
        finish_round(2, *stream)
        sync_rounds(3, *stream)
        wait_scatter(slot)
        ybuf[slot] = y

    for parity in range(2):
        pl.when((i < n_used) & (i % 2 == parity))(functools.partial(compute_block, parity))

    @pl.when(i == n_used)
    def _():
        looped(scatter_row, i + 1, (i - 1) % 2)
        wait_gather(i % 2)
        wait_scatter(0)
        wait_scatter(1)


def _moe(h2, meta, wg, wu, wd, layer):
    m, d = h2.shape
    bm = MOE_ROWS
    nb = meta[0].shape[0]
    f = wg.shape[3]
    rg, rd = d // MOE_MAT_CHUNKS, f // MOE_MAT_CHUNKS
    assert rg % (2 * SUBLANES) == 0 and rd % (2 * SUBLANES) == 0
    any_spec = pl.BlockSpec(memory_space=pl.ANY)
    dma = pltpu.SemaphoreType.DMA
    return pl.pallas_call(
        functools.partial(_moe_kernel, layer),
        grid_spec=pltpu.PrefetchScalarGridSpec(
            num_scalar_prefetch=len(meta),
            grid=(nb + 1,),
            in_specs=[any_spec] * 4,
            out_specs=any_spec,
            scratch_shapes=[
                pltpu.VMEM((2, bm, d), F32), pltpu.VMEM((2, bm, d), F32),
                pltpu.VMEM((2, d, f), BF16), pltpu.VMEM((2, d, f), BF16), pltpu.VMEM((2, f, d), BF16),
                pltpu.VMEM((MOE_STAGE_SLOTS, rg, f), F32), pltpu.VMEM((MOE_STAGE_SLOTS, rd, d), F32),
                dma((2,)), dma((2,)), dma((MOE_STAGE_SLOTS,)), dma((MOE_STAGE_SLOTS,)),
            ],
        ),
        out_shape=jax.ShapeDtypeStruct((TOP_K * m + 2 * bm, d), F32),
        compiler_params=_cparams(1, MOE_VMEM_LIMIT),
        name="moe_experts",
    )(*meta, h2, wg, wu, wd)


def _combine_kernel(n_ctx, final, y0_ref, y1_ref, x1_ref, mod_ref, w_ref, g_ref, modn_ref, o_ref, o2_ref):
    d = x1_ref.shape[1]
    f = w_ref[:, 0:1] * y0_ref[...] + w_ref[:, 1:2] * y1_ref[...]
    x2 = x1_ref[...] + mod_ref[:, 5 * d:6 * d] * f
    if final:
        y = _rms(x2) * g_ref[...]
        is_ctx = pl.program_id(0) * x1_ref.shape[0] < n_ctx

        @pl.when(is_ctx)
        def _():
            o_ref[...] = y

        @pl.when(jnp.logical_not(is_ctx))
        def _():
            o2_ref[...] = y
    else:
        o_ref[...] = x2
        o2_ref[...] = _modulated_norm(x2, g_ref[...], modn_ref[...], d).astype(BF16)


def _combine(y2, x1, mod_l, w_col, gain, mod_next, final, cond_of_row):
    m, d = x1.shape
    r = ROW_TILE
    k1 = m // r
    n_ctx = cond_of_row.n_ctx
    nct = n_ctx // r
    cond_row = pl.BlockSpec((None, 1, 6 * d), lambda i: (cond_of_row(i * r), 0, 0))
    tile = pl.BlockSpec((r, d), lambda i: (i, 0))
    if final:
        out_specs = [pl.BlockSpec((r, d), lambda i: (jnp.minimum(i, nct - 1), 0)),
                     pl.BlockSpec((r, d), lambda i: (jnp.maximum(i - nct, 0), 0))]
        out_shape = [jax.ShapeDtypeStruct((n_ctx, d), F32), jax.ShapeDtypeStruct((m - n_ctx, d), F32)]
    else:
        out_specs = [tile, tile]
        out_shape = [jax.ShapeDtypeStruct((m, d), F32), jax.ShapeDtypeStruct((m, d), BF16)]
    return pl.pallas_call(
        functools.partial(_combine_kernel, n_ctx, final),
        grid=(m // r,),
        in_specs=[
            tile,
            pl.BlockSpec((r, d), lambda i: (k1 + i, 0)),
            tile, cond_row,
            pl.BlockSpec((r, TOP_K), lambda i: (i, 0)),
            pl.BlockSpec((1, d), lambda i: (0, 0)),
            cond_row,
        ],
        out_specs=out_specs,
        out_shape=out_shape,
        compiler_params=_cparams(1),
        name="moe_combine",
    )(y2, y2, x1, mod_l, w_col, gain.reshape(1, d), mod_next)


class _CondOfRow:
    def __init__(self, n_ctx, t_lat):
        self.n_ctx, self.t_lat = n_ctx, t_lat
        self.tile_in = _pick_tile((1024, 512, 256), n_ctx, t_lat)

    def __call__(self, row0):
        return jnp.where(row0 < self.n_ctx, 0, 1 + (row0 - self.n_ctx) // self.t_lat)


def kernel(x_prompt, x_sample, cache_k, cache_v, state_delta, c, c_ctx, w_mod, b_mod, norm1, norm2, w_in,
           lambda_qk, subln, conv_w, a_log, dt_bias, delta_norm, pool_w, pool_scale, w_pa, w_pb, w_pc,
           w_out, w_router, b_router, w_gate, w_up, w_down, norm_final):
    b_ctx, t_ctx, d = x_prompt.shape
    b_lat, t_lat, _ = x_sample.shape
    depth = w_in.shape[0]
    past = cache_k.shape[2]
    n_ctx, n_lat = b_ctx * t_ctx, b_lat * t_lat
    m = n_ctx + n_lat
    assert d == D_MODEL and w_in.shape[2] == D_IN and 1 + b_lat <= SUBLANES
    assert t_ctx % ROW_TILE == 0 and t_lat % ROW_TILE == 0 and n_ctx % t_lat == 0
    cond_of_row = _CondOfRow(n_ctx, t_lat)
    dims = (b_ctx, t_ctx, b_lat, t_lat)

    def seq(row0):
        is_ctx = row0 < n_ctx
        return (jnp.where(is_ctx, row0 % t_ctx, (row0 - n_ctx) % t_lat), jnp.where(is_ctx, t_ctx, t_lat))

    x = jnp.concatenate([x_prompt.reshape(n_ctx, d), x_sample.reshape(n_lat, d)], axis=0)
    cond8 = jnp.zeros((SUBLANES, d), F32).at[0].set(c_ctx).at[1:1 + b_lat].set(c)
    mod = _modulation(cond8, w_mod, b_mod).reshape(depth, SUBLANES, 1, 6 * d)
    rope_tabs = _rope_tables(t_lat)
    wr_t = w_router.T
    br = b_router.reshape(N_EXPERTS, 1)
    ck = cache_k.reshape(b_lat, depth, past, QK_A)
    cv = cache_v.reshape(b_lat, depth, past, W_A)
    s_lat = state_delta.reshape(b_lat, depth, 2 * N_HEADS_B, DK_B, DV_B)

    new_k, new_v, new_s = [], [], []
    h = _norm_mod(x, mod[0], norm1[0], cond_of_row)
    w_in_t = jnp.swapaxes(w_in, 1, 2)
    w_proj = [w.astype(BF16) for w in (w_pa, w_pb, w_pc, w_out)]
    for l in range(depth):
        lam_init = 0.8 - 0.6 * math.exp(-0.3 * l)
        last = l == depth - 1
        z = _in_proj(h, w_in_t, l, cond_of_row.tile_in)
        oa, k_l, v_l = _attention(z, lambda_qk[l], subln[l], ck[:, l], cv[:, l], rope_tabs, lam_init, dims)
        new_k.append(k_l.reshape(b_ctx, t_ctx, N_HEADS_A, 2 * DK_A))
        new_v.append(v_l.reshape(b_ctx, t_ctx, N_HEADS_A, DV_A))
        bat = z[:, OFF_BA:OFF_BA + 4 * N_HEADS_B].T
        oc, u, l1, l2, e = _local(z, bat, conv_w[l], pool_w[l], pool_scale[l], a_log[l], dt_bias[l], seq)
        *delta_o, s_ctx = _delta_scan(u, l1, l2, e, s_lat[:, l], dims)
        new_s.append(s_ctx.reshape(b_ctx, 2, N_HEADS_B, DK_B, DV_B))

        x1, h2, idx_t, wt_t = _mix(x, mod[l], oa, delta_o, z, oc, delta_norm[l], norm2[l], *w_proj, l,
                                   wr_t, br, cond_of_row)
        y2 = _moe(h2, _dispatch_meta(idx_t, MOE_ROWS), w_gate, w_up, w_down, l)
        if last:
            y_ctx, y_lat = _combine(y2, x1, mod[l], wt_t.T, norm_final, mod[l], True, cond_of_row)
        else:
            x, h = _combine(y2, x1, mod[l], wt_t.T, norm1[l + 1], mod[l + 1], False, cond_of_row)

    y_prompt = y_ctx.reshape(b_ctx, t_ctx, d)
    y_sample = y_lat.reshape(b_lat, t_lat, d)
    return (y_prompt, y_sample, jnp.stack(new_k, axis=1), jnp.stack(new_v, axis=1), jnp.stack(new_s, axis=1))
```

```python
import functools
import math

import jax
import jax.numpy as jnp
from jax import lax
from jax.experimental import pallas as pl
from jax.experimental.pallas import tpu as pltpu

F32 = jnp.float32
BF16 = jnp.bfloat16
HIGHEST = lax.Precision.HIGHEST

D_MODEL = 2048
GRID_W = 64
EPS = 1e-6
N_HEADS_A = 8
DK_A = 64
DV_A = 128
QK_A = N_HEADS_A * 2 * DK_A
W_A = N_HEADS_A * DV_A
ROPE_BASE = 10000.0
ROPE_AXIS = DK_A // 2
N_HEADS_B = 4
DK_B = 128
DV_B = 128
W_B = N_HEADS_B * DV_B
DELTA_CHUNK = 64
POOL_WINDOWS = (2, 4, 8, 16)
POOL_GROUP_W = 128
W_C = len(POOL_WINDOWS) * POOL_GROUP_W
N_BRANCH = 3
N_EXPERTS = 16
N_EXPERT_GROUPS = 4
EXPERTS_PER_GROUP = N_EXPERTS // N_EXPERT_GROUPS
TOP_K = 2
D_FF = 1408

SRC_QKV_B = 2 * QK_A + W_A
SRC_GATE_B = SRC_QKV_B + 3 * W_B
SRC_BA = SRC_GATE_B + W_B
SRC_POOL = SRC_BA + 4 * N_HEADS_B
SRC_BRANCH = SRC_POOL + W_C
D_IN = SRC_BRANCH + N_BRANCH * D_MODEL
LANES = 128
SUBLANES = 8
IN_TILE_N = 1024
IN_HALF_N = IN_TILE_N // 2
OFF_RAW = N_BRANCH * D_MODEL
RAW_W = -(-SRC_BRANCH // IN_TILE_N) * IN_TILE_N
NP_IN = OFF_RAW + RAW_W
OFF_Q = OFF_RAW
OFF_K = OFF_Q + QK_A
OFF_V = OFF_K + QK_A
OFF_QKV_B = OFF_RAW + SRC_QKV_B
OFF_GATE_B = OFF_RAW + SRC_GATE_B
OFF_BA = OFF_RAW + SRC_BA
BRANCH_SHIFT = SRC_BRANCH % LANES
assert SRC_BA % IN_TILE_N == 0 and (SRC_BRANCH - BRANCH_SHIFT) % IN_HALF_N == 0
assert SRC_POOL - SRC_BA == BRANCH_SHIFT and OFF_RAW % IN_TILE_N == 0

ROW_TILE = 256
MOE_ROWS = 256
INV_BASE_BLOCK = 16
MOE_MAT_CHUNKS = 8
MOE_STAGE_SLOTS = 2
MOE_VMEM_LIMIT = 60 * 1024 * 1024
VMEM_LIMIT = 56 * 1024 * 1024


def _cparams(n_axes, vmem=VMEM_LIMIT):
    return pltpu.CompilerParams(dimension_semantics=("arbitrary",) * n_axes, vmem_limit_bytes=vmem)


def _dot(a, b, precision=None):
    return jnp.dot(a, b, preferred_element_type=F32, precision=precision)


def _dot_nt(a, b, precision=None):
    return lax.dot_general(a, b, (((1,), (1,)), ((), ())), preferred_element_type=F32, precision=precision)


def _bdot(a, b):
    return _dot(a.astype(BF16), b.astype(BF16))


def _bdot_nt(a, b):
    return _dot_nt(a.astype(BF16), b.astype(BF16))


def _sigmoid(x):
    return 1.0 / (1.0 + jnp.exp(-x))


def _silu(x):
    return x * _sigmoid(x)


def _rms(x, eps=EPS):
    return x * lax.rsqrt(jnp.mean(x * x, axis=-1, keepdims=True) + eps)


def _pick_tile(cands, *dims):
    for t in cands:
        if all(d % t == 0 for d in dims):
            return t
    raise ValueError(f"no tile in {cands} divides {dims}")


def _mod_kernel(c_ref, w_ref, b_ref, o_ref):
    a = _silu(c_ref[...]).astype(BF16)
    o_ref[...] = _dot(a, w_ref[...].astype(BF16)) + b_ref[...]


def _modulation(cond8, w_mod, b_mod):
    depth, d, n6 = w_mod.shape
    tn = 1024
    return pl.pallas_call(
        _mod_kernel,
        grid=(depth, n6 // tn),
        in_specs=[
            pl.BlockSpec((SUBLANES, d), lambda l, n: (0, 0)),
            pl.BlockSpec((None, d, tn), lambda l, n: (l, 0, n)),
            pl.BlockSpec((None, 1, tn), lambda l, n: (l, 0, n)),
        ],
        out_specs=pl.BlockSpec((None, SUBLANES, tn), lambda l, n: (l, 0, n)),
        out_shape=jax.ShapeDtypeStruct((depth, SUBLANES, n6), F32),
        compiler_params=_cparams(2),
        name="adaln_mod",
    )(cond8, w_mod, b_mod.reshape(depth, 1, n6))


def _modulated_norm(x, g, mod, d):
    return _rms(x) * g * (1.0 + mod[:, d:2 * d]) + mod[:, 0:d]


def _norm_kernel(x_ref, mod_ref, g_ref, h_ref):
    h_ref[...] = _modulated_norm(x_ref[...], g_ref[...], mod_ref[...], x_ref.shape[1]).astype(BF16)


def _norm_mod(x, mod_l, norm_l, cond_of_row):
    m, d = x.shape
    r = ROW_TILE
    return pl.pallas_call(
        _norm_kernel,
        grid=(m // r,),
        in_specs=[
            pl.BlockSpec((r, d), lambda i: (i, 0)),
            pl.BlockSpec((None, 1, 6 * d), lambda i: (cond_of_row(i * r), 0, 0)),
            pl.BlockSpec((1, d), lambda i: (0, 0)),
        ],
        out_specs=pl.BlockSpec((r, d), lambda i: (i, 0)),
        out_shape=jax.ShapeDtypeStruct((m, d), BF16),
        compiler_params=_cparams(1),
        name="norm_mod",
    )(x, mod_l, norm_l.reshape(1, d))


def _in_kernel(h_ref, wt_ref, z_ref, w_scr):
    @pl.when(pl.program_id(1) == 0)
    def _():
        w_scr[...] = wt_ref[0].astype(BF16)

    z_ref[...] = _dot_nt(h_ref[...], w_scr[...])


def _in_proj(h, w_in_t, layer, tm):
    m, d = h.shape
    n_branch_tiles = OFF_RAW // IN_TILE_N
    assert SRC_BRANCH % SUBLANES == 0
    tile8 = IN_TILE_N // SUBLANES
    row0 = lambda n: SUBLANES * jnp.where(n < n_branch_tiles, SRC_BRANCH // SUBLANES + tile8 * n,
                                          tile8 * (n - n_branch_tiles))
    return pl.pallas_call(
        _in_kernel,
        grid=(NP_IN // IN_TILE_N, m // tm),
        in_specs=[
            pl.BlockSpec((tm, d), lambda n, i: (i, 0)),
            pl.BlockSpec((pl.Element(1), pl.Element(IN_TILE_N), pl.Element(d)),
                         lambda n, i: (layer, row0(n), 0)),
        ],
        out_specs=pl.BlockSpec((tm, IN_TILE_N), lambda n, i: (i, n)),
        out_shape=jax.ShapeDtypeStruct((m, NP_IN), F32),
        scratch_shapes=[pltpu.VMEM((IN_TILE_N, d), BF16)],
        compiler_params=_cparams(2),
        name="in_proj",
    )(h, w_in_t)


def _lambda(lq_ref, lam_init):
    lq = lq_ref[...]
    a = jnp.sum(lq[0:1] * lq[1:2], axis=-1, keepdims=True)
    b = jnp.sum(lq[2:3] * lq[3:4], axis=-1, keepdims=True)
    return jnp.exp(a) - jnp.exp(b) + lam_init


def _diff_attn(qb, kb, vb, lam):
    probs = []
    for mp in range(2):
        s = _dot_nt(qb[:, mp * DK_A:(mp + 1) * DK_A], kb[:, mp * DK_A:(mp + 1) * DK_A])
        e = jnp.exp(s - jnp.max(s, axis=-1, keepdims=True))
        probs.append(e / jnp.sum(e, axis=-1, keepdims=True))
    pd = probs[0] - lam * probs[1]
    return _dot(pd.astype(BF16), vb)


def _attn_ctx_kernel(lam_init, lq_ref, sub_ref, q_ref, k_ref, v_ref, o_ref, knew_ref, vnew_ref):
    lam = _lambda(lq_ref, lam_init)
    qb = (q_ref[...] * DK_A ** -0.5).astype(BF16)
    k, v = k_ref[...], v_ref[...]
    knew_ref[...] = k
    vnew_ref[...] = v
    o = _diff_attn(qb, k.astype(BF16), v.astype(BF16), lam)
    o_ref[...] = (_rms(o) * sub_ref[...] * (1.0 - lam_init)).astype(BF16)


def _rope(x, cos, sin_signed):
    lane = lax.broadcasted_iota(jnp.int32, x.shape, 1)
    first_half = (lane % (ROPE_AXIS)) < (ROPE_AXIS // 2)
    partner = jnp.where(first_half, pltpu.roll(x, LANES - ROPE_AXIS // 2, 1), pltpu.roll(x, ROPE_AXIS // 2, 1))
    return x * cos + partner * sin_signed


def _attn_lat_kernel(lam_init, t_lat, lq_ref, sub_ref, q_ref, k_ref, v_ref, ck_ref, cv_ref,
                     cq_ref, sq_ref, ckk_ref, skk_ref, o_ref, k_scr, v_scr):
    @pl.when(pl.program_id(2) == 0)
    def _():
        k_scr[0:t_lat, :] = _rope(k_ref[...], ckk_ref[...], skk_ref[...]).astype(BF16)
        k_scr[t_lat:, :] = ck_ref[...].astype(BF16)
        v_scr[0:t_lat, :] = v_ref[...].astype(BF16)
        v_scr[t_lat:, :] = cv_ref[...].astype(BF16)

    lam = _lambda(lq_ref, lam_init)
    q = _rope(q_ref[...], cq_ref[...], sq_ref[...]) * DK_A ** -0.5
    o = _diff_attn(q.astype(BF16), k_scr[...], v_scr[...], lam)
    o_ref[...] = (_rms(o) * sub_ref[...] * (1.0 - lam_init)).astype(BF16)


def _attention(z, lq_l, sub_l, cache_k_l, cache_v_l, rope_tabs, lam_init, dims):
    b_ctx, t_ctx, b_lat, t_lat = dims
    n_ctx = b_ctx * t_ctx
    past = cache_k_l.shape[1]
    cq, ck, cv = OFF_Q // LANES, OFF_K // LANES, OFF_V // LANES
    small = [pl.BlockSpec((4, DK_A), lambda *_: (0, 0)), pl.BlockSpec((1, DV_A), lambda *_: (0, 0))]
    sub2 = sub_l.reshape(1, DV_A)

    head_blk = pl.BlockSpec((t_ctx, LANES), lambda b, h: (b, h))
    oa_ctx, k_new, v_new = pl.pallas_call(
        functools.partial(_attn_ctx_kernel, lam_init),
        grid=(b_ctx, N_HEADS_A),
        in_specs=small + [
            pl.BlockSpec((t_ctx, LANES), lambda b, h: (b, cq + h)),
            pl.BlockSpec((t_ctx, LANES), lambda b, h: (b, ck + h)),
            pl.BlockSpec((t_ctx, LANES), lambda b, h: (b, cv + h)),
        ],
        out_specs=[head_blk] * 3,
        out_shape=[jax.ShapeDtypeStruct((n_ctx, W_A), BF16), jax.ShapeDtypeStruct((n_ctx, QK_A), F32),
                   jax.ShapeDtypeStruct((n_ctx, W_A), F32)],
        compiler_params=_cparams(2),
        name="attn_ctx",
    )(lq_l, sub2, z, z, z)

    tq = _pick_tile((256, 128), t_lat)
    nq = t_lat // tq
    rb_q = n_ctx // tq
    rb_k = n_ctx // t_lat
    cos_t, sin_t = rope_tabs
    oa_lat = pl.pallas_call(
        functools.partial(_attn_lat_kernel, lam_init, t_lat),
        grid=(b_lat, N_HEADS_A, nq),
        in_specs=small + [
            pl.BlockSpec((tq, LANES), lambda b, h, i: (rb_q + b * nq + i, cq + h)),
            pl.BlockSpec((t_lat, LANES), lambda b, h, i: (rb_k + b, ck + h)),
            pl.BlockSpec((t_lat, LANES), lambda b, h, i: (rb_k + b, cv + h)),
            pl.BlockSpec((None, past, LANES), lambda b, h, i: (b, 0, h)),
            pl.BlockSpec((None, past, LANES), lambda b, h, i: (b, 0, h)),
            pl.BlockSpec((tq, LANES), lambda b, h, i: (i, 0)),
            pl.BlockSpec((tq, LANES), lambda b, h, i: (i, 0)),
            pl.BlockSpec((t_lat, LANES), lambda b, h, i: (0, 0)),
            pl.BlockSpec((t_lat, LANES), lambda b, h, i: (0, 0)),
        ],
        out_specs=pl.BlockSpec((tq, LANES), lambda b, h, i: (b * nq + i, h)),
        out_shape=jax.ShapeDtypeStruct((b_lat * t_lat, W_A), BF16),
        scratch_shapes=[pltpu.VMEM((t_lat + past, LANES), BF16), pltpu.VMEM((t_lat + past, LANES), BF16)],
        compiler_params=_cparams(3),
        name="attn_lat",
    )(lq_l, sub2, z, z, z, cache_k_l, cache_v_l, cos_t, sin_t, cos_t, sin_t)
    return (oa_ctx, oa_lat), k_new, v_new


def _rope_tables(t_lat):
    rows = t_lat // GRID_W
    r = jnp.repeat(jnp.arange(rows), GRID_W).astype(F32)
    col = jnp.tile(jnp.arange(GRID_W), rows).astype(F32)
    inv = ROPE_BASE ** (-jnp.arange(0, ROPE_AXIS, 2, dtype=F32) / ROPE_AXIS)
    ang_r, ang_c = r[:, None] * inv, col[:, None] * inv
    cos64 = jnp.concatenate([jnp.cos(ang_r)] * 2 + [jnp.cos(ang_c)] * 2, axis=-1)
    sin64 = jnp.concatenate([-jnp.sin(ang_r), jnp.sin(ang_r), -jnp.sin(ang_c), jnp.sin(ang_c)], axis=-1)
    return jnp.tile(cos64, (1, 2)), jnp.tile(sin64, (1, 2))


def _log_decay(alpha, a_log, dt):
    x = alpha + dt
    return -jnp.exp(a_log) * (jnp.maximum(x, 0.0) + jnp.log(1.0 + jnp.exp(-jnp.abs(x))))


def _chunk_cumsum(g, axis):
    n = g.shape[axis]
    pos = lax.broadcasted_iota(jnp.int32, g.shape, axis) % DELTA_CHUNK
    f, b = g, g
    s = 1
    while s < DELTA_CHUNK:
        f = f + jnp.where(pos >= s, pltpu.roll(f, s, axis), 0.0)
        b = b + jnp.where(pos < DELTA_CHUNK - s, pltpu.roll(b, n - s, axis), 0.0)
        s *= 2
    return f, b


def _delta_prepare(q_scr, k_scr, v_scr, ba, bat, al_row, al_col, dt_row, dt_col,
                   u_ref, l1_ref, l2_ref, e_ref):
    c, nh = DELTA_CHUNK, N_HEADS_B
    rows_n = q_scr.shape[0]
    beta_all = _sigmoid(ba[:, 0:2 * nh])
    gf, gb = _chunk_cumsum(_log_decay(ba[:, 2 * nh:4 * nh], al_row[...], dt_row[...]), 0)
    gc_all = jnp.where(lax.broadcasted_iota(jnp.int32, gf.shape, 1) < nh, gf, gb)
    gtf, gtb = _chunk_cumsum(_log_decay(bat[2 * nh:4 * nh, :], al_col[...], dt_col[...]), 1)
    gct_all = jnp.where(lax.broadcasted_iota(jnp.int32, gtf.shape, 0) < nh, gtf, gtb)

    ri = lax.broadcasted_iota(jnp.int32, (c, c), 0)
    ci = lax.broadcasted_iota(jnp.int32, (c, c), 1)
    masks = ((ri >= ci, ri > ci), (ri <= ci, ri < ci))
    eye = (ri == ci).astype(F32)
    blk_sizes = [INV_BASE_BLOCK << i for i in range(int(math.log2(c // INV_BASE_BLOCK)) + 1)]
    same_blk = [(ri // s) == (ci // s) for s in blk_sizes]
    n_sq = int(math.log2(INV_BASE_BLOCK)) - 1

    for j in range(rows_n // c):
        rs = slice(j * c, (j + 1) * c)
        chains = []
        for d in range(2):
            incl, strict = masks[d]
            last = 0 if d else c - 1
            for h in range(nh):
                idx = d * nh + h
                hs = slice(h * DK_B, (h + 1) * DK_B)
                gc = gc_all[rs, idx:idx + 1]
                gr = gct_all[idx:idx + 1, rs]
                ch = dict(d=d, h=h, idx=idx, hs=hs, strict=strict, gc=gc, g_last=gc[last:last + 1, :],
                          q=q_scr[rs, hs] * DK_B ** -0.5, k=k_scr[rs, hs], v=v_scr[rs, hs],
                          beta=beta_all[rs, idx:idx + 1],
                          decay=jnp.where(incl, jnp.exp(jnp.where(incl, gc - gr, 0.0)), 0.0))
                ch["kb"] = ch["k"] * ch["beta"]
                chains.append(ch)
        for ch in chains:
            kbf = ch["k"].astype(BF16)
            a = jnp.where(ch["strict"], -(_dot_nt(ch["kb"].astype(BF16), kbf) * ch["decay"]), 0.0)
            ch["attn"] = _dot_nt(ch["q"].astype(BF16), kbf) * ch["decay"]
            ch["a"] = a
            ch["apow"] = jnp.where(same_blk[0], a, 0.0)
            ch["p"] = eye + ch["apow"]
        for _ in range(n_sq):
            for ch in chains:
                ch["apow"] = _bdot(ch["apow"], ch["apow"])
            for ch in chains:
                ch["p"] = ch["p"] + _bdot(ch["p"], ch["apow"])
        for lvl in range(1, len(same_blk)):
            for ch in chains:
                a_off = jnp.where(same_blk[lvl] & jnp.logical_not(same_blk[lvl - 1]), ch["a"], 0.0)
                ch["t"] = _bdot(ch["p"], a_off)
            for ch in chains:
                ch["p"] = ch["p"] + _bdot(ch["t"], ch["p"])
        for ch in chains:
            rhs = jnp.concatenate([ch["v"] * ch["beta"], ch["kb"] * jnp.exp(ch["gc"])], axis=-1)
            ch["sol"] = _bdot(ch["p"], rhs)
        for ch in chains:
            d, h, hs = ch["d"], ch["h"], ch["hs"]
            u_ref[d, rs, hs] = ch["sol"][:, :DV_B]
            l1_ref[d, j, h, 0:c, :] = ch["sol"][:, DV_B:].astype(BF16)
            l1_ref[d, j, h, c:2 * c, :] = (ch["q"] * jnp.exp(ch["gc"])).astype(BF16)
            l2_ref[d, j, h, 0:c, :] = ch["attn"].astype(BF16)
            l2_ref[d, j, h, c:, :] = (ch["k"] * jnp.exp(ch["g_last"] - ch["gc"])).T.astype(BF16)
            e_ref[j, ch["idx"]:ch["idx"] + 1, :] = jnp.broadcast_to(jnp.exp(ch["g_last"]), (1, LANES))


def _local_kernel(seq, cur_ref, prev_ref, next_ref, pcur_ref, pprev_ref, pnext_ref, bat_ref,
                  cw_ref, pw_ref, ps_ref, al_row, al_col, dt_row, dt_col,
                  oc_ref, u_ref, l1_ref, l2_ref, e_ref, q_scr, k_scr, v_scr):
    rows_n = cur_ref.shape[0]
    pos0, t_seq = seq(pl.program_id(0) * rows_n)
    has_prev = pos0 > 0
    has_next = pos0 + rows_n < t_seq
    rows = lax.broadcasted_iota(jnp.int32, (rows_n, 1), 0)

    x = cur_ref[...]
    prev_row = jnp.where(has_prev, prev_ref[SUBLANES - 1:SUBLANES, :], 0.0)
    next_row = jnp.where(has_next, next_ref[0:1, :], 0.0)
    xm1 = jnp.where(rows == 0, prev_row, pltpu.roll(x, 1, 0))
    xp1 = jnp.where(rows == rows_n - 1, next_row, pltpu.roll(x, rows_n - 1, 0))
    y = _silu(cw_ref[0:1, :] * xm1 + cw_ref[1:2, :] * x + cw_ref[2:3, :] * xp1)
    for h in range(N_HEADS_B):
        for j, ref in enumerate((q_scr, k_scr)):
            a = y[:, j * W_B + h * DK_B:j * W_B + (h + 1) * DK_B]
            ref[:, h * DK_B:(h + 1) * DK_B] = a * lax.rsqrt(jnp.sum(a * a, axis=-1, keepdims=True) + EPS)
    v_scr[...] = y[:, 2 * W_B:]
    _delta_prepare(q_scr, k_scr, v_scr, pcur_ref[:, 0:LANES], bat_ref[...], al_row, al_col, dt_row, dt_col,
                   u_ref, l1_ref, l2_ref, e_ref)

    def pool_cols(ref):
        t = ref[...]
        return pltpu.roll(t, t.shape[1] - BRANCH_SHIFT, 1)[:, :W_C]

    pz = pool_cols(pcur_ref)
    ext = jnp.concatenate([jnp.where(has_prev, pool_cols(pprev_ref), 0.0), pz,
                           jnp.where(has_next, pool_cols(pnext_ref), 0.0)], axis=0)
    n_ext = rows_n + 2 * SUBLANES
    tpos = pos0 + rows
    for gi, win in enumerate(POOL_WINDOWS):
        sl = slice(gi * POOL_GROUP_W, (gi + 1) * POOL_GROUP_W)
        xg = ext[:, sl]
        acc = xg + pltpu.roll(xg, 1, 0)
        s = 1
        while 2 * s < win:
            acc = pltpu.roll(acc, s, 0) + pltpu.roll(acc, n_ext - s, 0)
            s *= 2
        acc = acc[SUBLANES:SUBLANES + rows_n]
        cnt = (jnp.minimum(tpos + win // 2, t_seq) - jnp.maximum(tpos - win // 2, 0)).astype(F32)
        d = acc / cnt - pz[:, sl]
        yg = _dot(d.astype(BF16), pw_ref[gi].astype(BF16)) * ps_ref[:, sl]
        oc_ref[:, sl] = yg.astype(BF16)


def _local(z, bat, conv_w_l, pool_w_l, pool_scale_l, a_log_l, dt_l, seq):
    m = z.shape[0]
    r, c = ROW_TILE, DELTA_CHUNK
    hb = r // SUBLANES
    last_hb = m // SUBLANES - 1
    c_qkv, c_pool = OFF_QKV_B // (3 * W_B), OFF_BA // IN_TILE_N
    nh2 = 2 * N_HEADS_B
    mc, cpt = m // c, r // c
    prev_map = lambda cb: (lambda i: (jnp.maximum(i * hb - 1, 0), cb))
    next_map = lambda cb: (lambda i: (jnp.minimum((i + 1) * hb, last_hb), cb))
    row = pl.BlockSpec((1, nh2), lambda i: (0, 0))
    col = pl.BlockSpec((nh2, 1), lambda i: (0, 0))
    return pl.pallas_call(
        functools.partial(_local_kernel, seq),
        grid=(m // r,),
        in_specs=[
            pl.BlockSpec((r, 3 * W_B), lambda i: (i, c_qkv)),
            pl.BlockSpec((SUBLANES, 3 * W_B), prev_map(c_qkv)),
            pl.BlockSpec((SUBLANES, 3 * W_B), next_map(c_qkv)),
            pl.BlockSpec((r, IN_TILE_N), lambda i: (i, c_pool)),
            pl.BlockSpec((SUBLANES, IN_TILE_N), prev_map(c_pool)),
            pl.BlockSpec((SUBLANES, IN_TILE_N), next_map(c_pool)),
            pl.BlockSpec((2 * nh2, r), lambda i: (0, i)),
            pl.BlockSpec((3, 3 * W_B), lambda i: (0, 0)),
            pl.BlockSpec((len(POOL_WINDOWS), POOL_GROUP_W, POOL_GROUP_W), lambda i: (0, 0, 0)),
            pl.BlockSpec((1, W_C), lambda i: (0, 0)),
            row, col, row, col,
        ],
        out_specs=[
            pl.BlockSpec((r, W_C), lambda i: (i, 0)),
            pl.BlockSpec((2, r, W_B), lambda i: (0, i, 0)),
            pl.BlockSpec((2, cpt, N_HEADS_B, 2 * c, DV_B), lambda i: (0, i, 0, 0, 0)),
            pl.BlockSpec((2, cpt, N_HEADS_B, c + DK_B, c), lambda i: (0, i, 0, 0, 0)),
            pl.BlockSpec((cpt, nh2, LANES), lambda i: (i, 0, 0)),
        ],
        out_shape=[
            jax.ShapeDtypeStruct((m, W_C), BF16),
            jax.ShapeDtypeStruct((2, m, W_B), F32),
            jax.ShapeDtypeStruct((2, mc, N_HEADS_B, 2 * c, DV_B), BF16),
            jax.ShapeDtypeStruct((2, mc, N_HEADS_B, c + DK_B, c), BF16),
            jax.ShapeDtypeStruct((mc, nh2, LANES), F32),
        ],
        scratch_shapes=[pltpu.VMEM((r, W_B), F32)] * 3,
        compiler_params=_cparams(1),
        name="local_conv_pool",
    )(z, z, z, z, z, z, bat, conv_w_l, pool_w_l, pool_scale_l.reshape(1, W_C),
      a_log_l.reshape(1, nh2), a_log_l.reshape(nh2, 1), dt_l.reshape(1, nh2), dt_l.reshape(nh2, 1))


def _scan_kernel(par, b_lat, n_ctx_ch, *refs):
    c, nh = DELTA_CHUNK, N_HEADS_B
    n_slot = par + b_lat
    ins = refs[:n_slot * 8]
    s0_ref = refs[n_slot * 8]
    of_ctx, ob_ctx, of_lat, ob_lat, sfin_ref, s_scr = refs[n_slot * 8 + 1:]
    step = pl.program_id(0)
    cc = step % n_ctx_ch

    @pl.when(cc == 0)
    def _():
        s_scr[0:par] = jnp.zeros((par,) + s_scr.shape[1:], F32)

    @pl.when(step == 0)
    def _():
        s_scr[par:] = s0_ref[...]

    for slot in range(n_slot):
        outs = (of_ctx, ob_ctx, slot) if slot < par else (of_lat, ob_lat, slot - par)
        chains = []
        for d in range(2):
            l1_ref, l2_ref, u_ref, e_ref = ins[(slot * 2 + d) * 4:(slot * 2 + d) * 4 + 4]
            for h in range(nh):
                chains.append(dict(d=d, h=h, idx=d * nh + h, l1=l1_ref, l2=l2_ref, u=u_ref, e=e_ref))
        for ch in chains:
            ch["s"] = s_scr[slot, ch["idx"]]
            ch["r1"] = _dot(ch["l1"][ch["h"]], ch["s"].astype(BF16))
        for ch in chains:
            hs = slice(ch["h"] * DV_B, (ch["h"] + 1) * DV_B)
            v_new = ch["u"][:, hs] - ch["r1"][:c]
            ch["r2"] = _dot(ch["l2"][ch["h"]], v_new.astype(BF16))
        for ch in chains:
            hs = slice(ch["h"] * DV_B, (ch["h"] + 1) * DV_B)
            outs[ch["d"]][outs[2], :, hs] = ch["r1"][c:] + ch["r2"][:c]
            e = ch["e"][ch["idx"]:ch["idx"] + 1, :]
            s_scr[slot, ch["idx"]] = ch["s"] * e + ch["r2"][c:]

    @pl.when(cc == n_ctx_ch - 1)
    def _():
        sfin_ref[...] = s_scr[0:par]


def _delta_scan(u, l1, l2, e, s0_lat, dims):
    b_ctx, t_ctx, b_lat, t_lat = dims
    c, nh = DELTA_CHUNK, N_HEADS_B
    n_ctx_ch, n_lat_ch = t_ctx // c, t_lat // c
    par = b_ctx * n_ctx_ch // n_lat_ch
    assert par >= 1 and par * n_lat_ch == b_ctx * n_ctx_ch and b_ctx % par == 0
    mc = l1.shape[1]
    ctx_chunks = b_ctx * n_ctx_ch
    u4 = u.reshape(2, mc, c, W_B)

    def chunk_of(slot, d):
        if slot < par:
            return lambda s: ((s // n_ctx_ch) * par + slot) * n_ctx_ch + (
                (n_ctx_ch - 1 - s % n_ctx_ch) if d else s % n_ctx_ch)
        q = slot - par
        return lambda s: ctx_chunks + q * n_lat_ch + ((n_lat_ch - 1 - s) if d else s)

    in_specs, args = [], []
    for slot in range(par + b_lat):
        for d in range(2):
            cg = chunk_of(slot, d)
            in_specs += [
                pl.BlockSpec((None, None, nh, 2 * c, DV_B), lambda s, cg=cg, d=d: (d, cg(s), 0, 0, 0)),
                pl.BlockSpec((None, None, nh, c + DK_B, c), lambda s, cg=cg, d=d: (d, cg(s), 0, 0, 0)),
                pl.BlockSpec((None, None, c, W_B), lambda s, cg=cg, d=d: (d, cg(s), 0, 0)),
                pl.BlockSpec((None, 2 * nh, LANES), lambda s, cg=cg: (cg(s), 0, 0)),
            ]
            args += [l1, l2, u4, e]
    in_specs.append(pl.BlockSpec((b_lat, 2 * nh, DK_B, DV_B), lambda s: (0, 0, 0, 0)))
    args.append(s0_lat)

    ctx_o = jax.ShapeDtypeStruct((b_ctx // par, par, n_ctx_ch, c, W_B), F32)
    lat_o = jax.ShapeDtypeStruct((b_lat, n_lat_ch, c, W_B), F32)
    cf = lambda s: s % n_ctx_ch
    of_ctx, ob_ctx, of_lat, ob_lat, s_fin = pl.pallas_call(
        functools.partial(_scan_kernel, par, b_lat, n_ctx_ch),
        grid=(n_lat_ch,),
        in_specs=in_specs,
        out_specs=[
            pl.BlockSpec((None, par, None, c, W_B), lambda s: (s // n_ctx_ch, 0, cf(s), 0, 0)),
            pl.BlockSpec((None, par, None, c, W_B), lambda s: (s // n_ctx_ch, 0, n_ctx_ch - 1 - cf(s), 0, 0)),
            pl.BlockSpec((b_lat, None, c, W_B), lambda s: (0, s, 0, 0)),
            pl.BlockSpec((b_lat, None, c, W_B), lambda s: (0, n_lat_ch - 1 - s, 0, 0)),
            pl.BlockSpec((par, 2 * nh, DK_B, DV_B), lambda s: (s // n_ctx_ch, 0, 0, 0)),
        ],
        out_shape=[ctx_o, ctx_o, lat_o, lat_o, jax.ShapeDtypeStruct((b_ctx, 2 * nh, DK_B, DV_B), F32)],
        scratch_shapes=[pltpu.VMEM((par + b_lat, 2 * nh, DK_B, DV_B), F32)],
        compiler_params=_cparams(1),
        name="delta_scan",
    )(*args)
    n_ctx, n_lat = b_ctx * t_ctx, b_lat * t_lat
    return (of_ctx.reshape(n_ctx, W_B), ob_ctx.reshape(n_ctx, W_B), of_lat.reshape(n_lat, W_B),
            ob_lat.reshape(n_lat, W_B), s_fin)


def _route_rows(sel, scores):
    g = EXPERTS_PER_GROUP
    gscore = []
    for gi in range(N_EXPERT_GROUPS):
        a, b, c, d = sel[gi * g:(gi + 1) * g]
        hi1, lo1 = jnp.maximum(a, b), jnp.minimum(a, b)
        hi2, lo2 = jnp.maximum(c, d), jnp.minimum(c, d)
        gscore.append(jnp.maximum(hi1, hi2) + jnp.maximum(jnp.minimum(hi1, hi2), jnp.maximum(lo1, lo2)))
    best = jnp.zeros_like(gscore[0], dtype=jnp.int32)
    bestv = gscore[0]
    for gi in range(1, N_EXPERT_GROUPS):
        upd = gscore[gi] > bestv
        best = jnp.where(upd, gi, best)
        bestv = jnp.where(upd, gscore[gi], bestv)

    def in_best(rows, j):
        out = rows[j]
        for gi in range(1, N_EXPERT_GROUPS):
            out = jnp.where(best == gi, rows[gi * g + j], out)
        return out

    e_sel = [in_best(sel, j) for j in range(g)]
    e_sc = [in_best(scores, j) for j in range(g)]

    def first_argmax(vals):
        bi, bv, bs = jnp.zeros_like(best), vals[0], e_sc[0]
        for j in range(1, g):
            upd = vals[j] > bv
            bi = jnp.where(upd, j, bi)
            bv = jnp.where(upd, vals[j], bv)
            bs = jnp.where(upd, e_sc[j], bs)
        return bi, bs

    i0, s0 = first_argmax(e_sel)
    i1, s1 = first_argmax([jnp.where(i0 == j, -jnp.inf, e_sel[j]) for j in range(g)])
    tot = s0 + s1
    return best * g + i0, best * g + i1, s0 / tot, s1 / tot


def _mix_kernel(n_ctx, x_ref, mod_ref, oac_ref, oal_ref, ofc_ref, obc_ref, ofl_ref, obl_ref,
                gb_ref, oc_ref, g0_ref, g1_ref, g2_ref,
                dn_ref, n2_ref, wpa_ref, wpb_ref, wpc_ref, wout_ref, wr_ref, br_ref,
                x1_ref, h2_ref, idx_ref, wt_ref):
    d = x_ref.shape[1]
    is_ctx = pl.program_id(0) * x_ref.shape[0] < n_ctx
    dn = dn_ref[...]
    o = jnp.where(is_ctx, ofc_ref[...] + obc_ref[...], ofl_ref[...] + obl_ref[...])
    o_a = jnp.where(is_ctx, oac_ref[...], oal_ref[...])
    gate = _silu(gb_ref[...])
    parts = []
    for h in range(N_HEADS_B):
        sl = slice(h * DV_B, (h + 1) * DV_B)
        parts.append((_rms(o[:, sl]) * dn * gate[:, sl]).astype(BF16))
    o_b = jnp.concatenate(parts, axis=-1)
    mixed = (_sigmoid(g0_ref[...]) * _dot(o_a, wpa_ref[...])
             + _sigmoid(g1_ref[...]) * _dot(o_b, wpb_ref[...])
             + _sigmoid(g2_ref[...]) * _dot(oc_ref[...], wpc_ref[...]))
    gate1 = mod_ref[:, 2 * d:3 * d]
    x1 = x_ref[...] + gate1 * _dot(mixed.astype(BF16), wout_ref[...])
    x1_ref[...] = x1
    shift2 = mod_ref[:, 3 * d:4 * d]
    scale2 = mod_ref[:, 4 * d:5 * d]
    h2 = _rms(x1) * n2_ref[...] * (1.0 + scale2) + shift2
    h2_ref[...] = h2
    sc = _sigmoid(_dot_nt(wr_ref[...], h2, HIGHEST))
    sel = sc + br_ref[...]
    i0, i1, w0, w1 = _route_rows([sel[e:e + 1] for e in range(N_EXPERTS)],
                                 [sc[e:e + 1] for e in range(N_EXPERTS)])
    idx_ref[0:1, :] = i0
    idx_ref[1:2, :] = i1
    wt_ref[0:1, :] = w0
    wt_ref[1:2, :] = w1


def _mix(x, mod_l, oa, delta_o, z, oc, delta_norm_l, norm2_l, wpa, wpb, wpc, wout, layer, wr_t, br,
         cond_of_row):
    m, d = x.shape
    r = ROW_TILE
    n_ctx = cond_of_row.n_ctx
    nct = n_ctx // r
    c_gb = OFF_GATE_B // W_B
    once = dict(pipeline_mode=pl.Buffered(1))
    full = lambda shape: pl.BlockSpec(shape, lambda i: (0,) * len(shape), **once)
    of_layer = lambda shape: pl.BlockSpec((None,) + shape, lambda i: (layer,) + (0,) * len(shape), **once)
    ctx_rows = lambda w: pl.BlockSpec((r, w), lambda i: (jnp.minimum(i, nct - 1), 0))
    lat_rows = lambda w: pl.BlockSpec((r, w), lambda i: (jnp.maximum(i - nct, 0), 0))
    of_ctx, ob_ctx, of_lat, ob_lat = delta_o
    return pl.pallas_call(
        functools.partial(_mix_kernel, n_ctx),
        grid=(m // r,),
        in_specs=[
            pl.BlockSpec((r, d), lambda i: (i, 0)),
            pl.BlockSpec((None, 1, 6 * d), lambda i: (cond_of_row(i * r), 0, 0)),
            ctx_rows(W_A), lat_rows(W_A),
            ctx_rows(W_B), ctx_rows(W_B), lat_rows(W_B), lat_rows(W_B),
            pl.BlockSpec((r, W_B), lambda i: (i, c_gb)),
            pl.BlockSpec((r, W_C), lambda i: (i, 0)),
            pl.BlockSpec((r, d), lambda i: (i, 0)),
            pl.BlockSpec((r, d), lambda i: (i, 1)),
            pl.BlockSpec((r, d), lambda i: (i, 2)),
            full((1, DV_B)), full((1, d)),
            of_layer((W_A, d)), of_layer((W_B, d)), of_layer((W_C, d)), of_layer((d, d)),
            full((N_EXPERTS, d)), full((N_EXPERTS, 1)),
        ],
        out_specs=[
            pl.BlockSpec((r, d), lambda i: (i, 0)),
            pl.BlockSpec((r, d), lambda i: (i, 0)),
            pl.BlockSpec((TOP_K, r), lambda i: (0, i)),
            pl.BlockSpec((TOP_K, r), lambda i: (0, i)),
        ],
        out_shape=[
            jax.ShapeDtypeStruct((m, d), F32),
            jax.ShapeDtypeStruct((m, d), F32),
            jax.ShapeDtypeStruct((TOP_K, m), jnp.int32),
            jax.ShapeDtypeStruct((TOP_K, m), F32),
        ],
        compiler_params=_cparams(1),
        name="mix_route",
    )(x, mod_l, oa[0], oa[1], of_ctx, ob_ctx, of_lat, ob_lat, z, oc, z, z, z,
      delta_norm_l.reshape(1, DV_B), norm2_l.reshape(1, d), wpa, wpb, wpc, wout, wr_t, br)


def _dispatch_meta(idx_t, bm):
    m = idx_t.shape[1]
    nk = m * TOP_K
    flat_e = idx_t.T.reshape(nk)
    experts = jnp.arange(N_EXPERTS, dtype=jnp.int32)[None, :]
    counts = jnp.sum((flat_e[:, None] == experts).astype(jnp.int32), axis=0)
    order = jnp.argsort(flat_e, stable=True).astype(jnp.int32)
    start = jnp.cumsum(counts) - counts
    padded = (counts + bm - 1) // bm * bm
    pad_end = jnp.cumsum(padded)
    pad_start = pad_end - padded
    nb = -(-(nk + N_EXPERTS * (bm - 1)) // bm)
    p = jnp.arange(nb * bm, dtype=jnp.int32)
    e_of_p = jnp.minimum(jnp.sum((pad_end[None, :] <= p[:, None]).astype(jnp.int32), axis=1), N_EXPERTS - 1)
    pick = (e_of_p[:, None] == experts).astype(jnp.int32)
    rank = p - jnp.sum(pick * pad_start[None, :], axis=1)
    real = rank < jnp.sum(pick * counts[None, :], axis=1)
    assign = order[jnp.clip(jnp.sum(pick * start[None, :], axis=1) + rank, 0, nk - 1)]
    row_tok = jnp.where(real, assign // TOP_K, 0)
    spare = nk + ((p // bm) % 2) * bm + p % bm
    row_dst = jnp.where(real, (assign % TOP_K) * m + assign // TOP_K, spare)
    prime = nk + jnp.arange(2 * bm, dtype=jnp.int32)
    n_used = (pad_end[-1] // bm).astype(jnp.int32)
    blk_e = e_of_p[::bm]
    n_chunks = 3 * MOE_MAT_CHUNKS
    e_ids = experts[0]
    n_blk = padded // bm
    has = n_blk > 0
    later = lax.cummin(jnp.where(has, e_ids, N_EXPERTS)[::-1])[::-1]
    nxt_of_e = jnp.concatenate([later[1:], jnp.full((1,), N_EXPERTS, jnp.int32)])
    set_of_e = (jnp.cumsum(has.astype(jnp.int32)) - 1) % 2
    b = jnp.arange(nb, dtype=jnp.int32)
    k = b - (pad_start // bm)[blk_e]
    n_e = jnp.maximum(n_blk[blk_e], 1)
    streams = (nxt_of_e[blk_e] < N_EXPERTS) & (b < n_used)
    c0 = jnp.where(streams, n_chunks * k // n_e, 0)
    c1 = jnp.where(streams, n_chunks * (k + 1) // n_e, 0)
    nxt_e = jnp.where(streams, nxt_of_e[blk_e], 0)
    return (blk_e, n_used.reshape(1), row_tok, jnp.concatenate([prime, row_dst]),
            set_of_e[blk_e], nxt_e, c0, c1)


def _moe_kernel(layer, blk_e_ref, n_used_ref, row_tok_ref, row_dst_ref, set_ref, nxt_ref, c0_ref, c1_ref,
                h2_hbm, wg_hbm, wu_hbm, wd_hbm, out_hbm,
                xbuf, ybuf, wg_s, wu_s, wd_s, st_gu, st_d, gsem, ssem, wsem_gu, wsem_d):
    i = pl.program_id(0)
    bm = xbuf.shape[1]
    n_used = n_used_ref[0]
    n_mat = MOE_MAT_CHUNKS
    n_stage = st_gu.shape[0]
    rg, rd = st_gu.shape[1], st_d.shape[1]

    def chunk_ops(c, j, e, s):
        cg, cu, cd = c, c - n_mat, c - 2 * n_mat
        rows_g = pl.ds(pl.multiple_of(cg * rg, rg), rg)
        rows_u = pl.ds(pl.multiple_of(cu * rg, rg), rg)
        rows_d = pl.ds(pl.multiple_of(cd * rd, 2 * SUBLANES), rd)

        def to_set(dst, rows, stage):
            def convert():
                dst[s, rows, :] = stage[j].astype(BF16)
            return convert

        return (
            (c < n_mat, lambda: pltpu.make_async_copy(wg_hbm.at[layer, e, rows_g, :], st_gu.at[j], wsem_gu.at[j]),
             to_set(wg_s, rows_g, st_gu)),
            ((c >= n_mat) & (c < 2 * n_mat),
             lambda: pltpu.make_async_copy(wu_hbm.at[layer, e, rows_u, :], st_gu.at[j], wsem_gu.at[j]),
             to_set(wu_s, rows_u, st_gu)),
            (c >= 2 * n_mat, lambda: pltpu.make_async_copy(wd_hbm.at[layer, e, rows_d, :], st_d.at[j], wsem_d.at[j]),
             to_set(wd_s, rows_d, st_d)),
        )

    def issue_round(r, c_lo, c_hi, e, s):
        for j in range(n_stage):
            c = c_lo + r * n_stage + j
            for pred, copy, _ in chunk_ops(c, j, e, s):
                pl.when(pred & (c < c_hi))(lambda copy=copy: copy().start(priority=1))

    def finish_round(r, c_lo, c_hi, e, s):
        for j in range(n_stage):
            c = c_lo + r * n_stage + j
            for pred, copy, convert in chunk_ops(c, j, e, s):
                @pl.when(pred & (c < c_hi))
                def _(copy=copy, convert=convert):
                    copy().wait()
                    convert()

    def sync_rounds(r_lo, c_lo, c_hi, e, s):
        def body(r, carry):
            issue_round(r, c_lo, c_hi, e, s)
            finish_round(r, c_lo, c_hi, e, s)
            return carry
        n_rounds = (jnp.maximum(c_hi - c_lo, 0) + n_stage - 1) // n_stage
        lax.fori_loop(r_lo, jnp.maximum(n_rounds, r_lo), body, 0)

    def gather_row(blk, slot, r):
        tok = row_tok_ref[blk * bm + r]
        pltpu.make_async_copy(h2_hbm.at[pl.ds(tok, 1)], xbuf.at[slot, pl.ds(r, 1)], gsem.at[slot]).start()

    def scatter_row(tab_blk, slot, r):
        dst = row_dst_ref[tab_blk * bm + r]
        pltpu.make_async_copy(ybuf.at[slot, pl.ds(r, 1)], out_hbm.at[pl.ds(dst, 1)], ssem.at[slot]).start()

    def looped(fn, *args):
        def body(r, carry):
            fn(*args, r)
            return carry
        lax.fori_loop(0, bm, body, 0, unroll=8)

    def wait_gather(slot):
        pltpu.make_async_copy(h2_hbm.at[pl.ds(0, bm)], xbuf.at[slot], gsem.at[slot]).wait()

    def wait_scatter(slot):
        pltpu.make_async_copy(ybuf.at[slot], out_hbm.at[pl.ds(0, bm)], ssem.at[slot]).wait()

    @pl.when(i == 0)
    def _():
        ybuf[...] = jnp.zeros_like(ybuf)
        looped(gather_row, 0, 0)
        looped(scatter_row, 0, 0)
        sync_rounds(0, 0, 3 * n_mat, blk_e_ref[0], set_ref[0])

    def compute_block(slot):
        other = 1 - slot
        nxt = jnp.minimum(i + 1, n_used - 1)
        s, e_next, c_lo, c_hi = set_ref[i], nxt_ref[i], c0_ref[i], c1_ref[i]
        stream = (c_lo, c_hi, e_next, 1 - s)
        wait_gather(slot)
        issue_round(0, *stream)
        for r in range(bm):
            gather_row(nxt, other, r)
            scatter_row(i + 1, other, r)
        xb = xbuf[slot].astype(BF16)
        gate = _dot(xb, wg_s[s])
        finish_round(0, *stream)
        issue_round(1, *stream)
        act = _silu(gate) * _dot(xb, wu_s[s])
        finish_round(1, *stream)
        issue_round(2, *stream)
        y = _dot(act.astype(BF16), wd_s[s])
        finish_round(2, *stream)
        sync_rounds(3, *stream)
        wait_scatter(slot)
        ybuf[slot] = y

    for parity in range(2):
        pl.when((i < n_used) & (i % 2 == parity))(functools.partial(compute_block, parity))

    @pl.when(i == n_used)
    def _():
        looped(scatter_row, i + 1, (i - 1) % 2)
        wait_gather(i % 2)
        wait_scatter(0)
        wait_scatter(1)


def _moe(h2, meta, wg, wu, wd, layer):
    m, d = h2.shape
    bm = MOE_ROWS
    nb = meta[0].shape[0]
    f = wg.shape[3]
    rg, rd = d // MOE_MAT_CHUNKS, f // MOE_MAT_CHUNKS
    assert rg % (2 * SUBLANES) == 0 and rd % (2 * SUBLANES) == 0
    any_spec = pl.BlockSpec(memory_space=pl.ANY)
    dma = pltpu.SemaphoreType.DMA
    return pl.pallas_call(
        functools.partial(_moe_kernel, layer),
        grid_spec=pltpu.PrefetchScalarGridSpec(
            num_scalar_prefetch=len(meta),
            grid=(nb + 1,),
            in_specs=[any_spec] * 4,
            out_specs=any_spec,
            scratch_shapes=[
                pltpu.VMEM((2, bm, d), F32), pltpu.VMEM((2, bm, d), F32),
                pltpu.VMEM((2, d, f), BF16), pltpu.VMEM((2, d, f), BF16), pltpu.VMEM((2, f, d), BF16),
                pltpu.VMEM((MOE_STAGE_SLOTS, rg, f), F32), pltpu.VMEM((MOE_STAGE_SLOTS, rd, d), F32),
                dma((2,)), dma((2,)), dma((MOE_STAGE_SLOTS,)), dma((MOE_STAGE_SLOTS,)),
            ],
        ),
        out_shape=jax.ShapeDtypeStruct((TOP_K * m + 2 * bm, d), F32),
        compiler_params=_cparams(1, MOE_VMEM_LIMIT),
        name="moe_experts",
    )(*meta, h2, wg, wu, wd)


def _combine_kernel(n_ctx, final, y0_ref, y1_ref, x1_ref, mod_ref, w_ref, g_ref, modn_ref, o_ref, o2_ref):
    d = x1_ref.shape[1]
    f = w_ref[:, 0:1] * y0_ref[...] + w_ref[:, 1:2] * y1_ref[...]
    x2 = x1_ref[...] + mod_ref[:, 5 * d:6 * d] * f
    if final:
        y = _rms(x2) * g_ref[...]
        is_ctx = pl.program_id(0) * x1_ref.shape[0] < n_ctx

        @pl.when(is_ctx)
        def _():
            o_ref[...] = y

        @pl.when(jnp.logical_not(is_ctx))
        def _():
            o2_ref[...] = y
    else:
        o_ref[...] = x2
        o2_ref[...] = _modulated_norm(x2, g_ref[...], modn_ref[...], d).astype(BF16)


def _combine(y2, x1, mod_l, w_col, gain, mod_next, final, cond_of_row):
    m, d = x1.shape
    r = ROW_TILE
    k1 = m // r
    n_ctx = cond_of_row.n_ctx
    nct = n_ctx // r
    cond_row = pl.BlockSpec((None, 1, 6 * d), lambda i: (cond_of_row(i * r), 0, 0))
    tile = pl.BlockSpec((r, d), lambda i: (i, 0))
    if final:
        out_specs = [pl.BlockSpec((r, d), lambda i: (jnp.minimum(i, nct - 1), 0)),
                     pl.BlockSpec((r, d), lambda i: (jnp.maximum(i - nct, 0), 0))]
        out_shape = [jax.ShapeDtypeStruct((n_ctx, d), F32), jax.ShapeDtypeStruct((m - n_ctx, d), F32)]
    else:
        out_specs = [tile, tile]
        out_shape = [jax.ShapeDtypeStruct((m, d), F32), jax.ShapeDtypeStruct((m, d), BF16)]
    return pl.pallas_call(
        functools.partial(_combine_kernel, n_ctx, final),
        grid=(m // r,),
        in_specs=[
            tile,
            pl.BlockSpec((r, d), lambda i: (k1 + i, 0)),
            tile, cond_row,
            pl.BlockSpec((r, TOP_K), lambda i: (i, 0)),
            pl.BlockSpec((1, d), lambda i: (0, 0)),
            cond_row,
        ],
        out_specs=out_specs,
        out_shape=out_shape,
        compiler_params=_cparams(1),
        name="moe_combine",
    )(y2, y2, x1, mod_l, w_col, gain.reshape(1, d), mod_next)


class _CondOfRow:
    def __init__(self, n_ctx, t_lat):
        self.n_ctx, self.t_lat = n_ctx, t_lat
        self.tile_in = _pick_tile((1024, 512, 256), n_ctx, t_lat)

    def __call__(self, row0):
        return jnp.where(row0 < self.n_ctx, 0, 1 + (row0 - self.n_ctx) // self.t_lat)


def kernel(x_prompt, x_sample, cache_k, cache_v, state_delta, c, c_ctx, w_mod, b_mod, norm1, norm2, w_in,
           lambda_qk, subln, conv_w, a_log, dt_bias, delta_norm, pool_w, pool_scale, w_pa, w_pb, w_pc,
           w_out, w_router, b_router, w_gate, w_up, w_down, norm_final):
    b_ctx, t_ctx, d = x_prompt.shape
    b_lat, t_lat, _ = x_sample.shape
    depth = w_in.shape[0]
    past = cache_k.shape[2]
    n_ctx, n_lat = b_ctx * t_ctx, b_lat * t_lat
    m = n_ctx + n_lat
    assert d == D_MODEL and w_in.shape[2] == D_IN and 1 + b_lat <= SUBLANES
    assert t_ctx % ROW_TILE == 0 and t_lat % ROW_TILE == 0 and n_ctx % t_lat == 0
    cond_of_row = _CondOfRow(n_ctx, t_lat)
    dims = (b_ctx, t_ctx, b_lat, t_lat)

    def seq(row0):
        is_ctx = row0 < n_ctx
        return (jnp.where(is_ctx, row0 % t_ctx, (row0 - n_ctx) % t_lat), jnp.where(is_ctx, t_ctx, t_lat))

    x = jnp.concatenate([x_prompt.reshape(n_ctx, d), x_sample.reshape(n_lat, d)], axis=0)
    cond8 = jnp.zeros((SUBLANES, d), F32).at[0].set(c_ctx).at[1:1 + b_lat].set(c)
    mod = _modulation(cond8, w_mod, b_mod).reshape(depth, SUBLANES, 1, 6 * d)
    rope_tabs = _rope_tables(t_lat)
    wr_t = w_router.T
    br = b_router.reshape(N_EXPERTS, 1)
    ck = cache_k.reshape(b_lat, depth, past, QK_A)
    cv = cache_v.reshape(b_lat, depth, past, W_A)
    s_lat = state_delta.reshape(b_lat, depth, 2 * N_HEADS_B, DK_B, DV_B)

    new_k, new_v, new_s = [], [], []
    h = _norm_mod(x, mod[0], norm1[0], cond_of_row)
    w_in_t = jnp.swapaxes(w_in, 1, 2)
    w_proj = [w.astype(BF16) for w in (w_pa, w_pb, w_pc, w_out)]
    for l in range(depth):
        lam_init = 0.8 - 0.6 * math.exp(-0.3 * l)
        last = l == depth - 1
        z = _in_proj(h, w_in_t, l, cond_of_row.tile_in)
        oa, k_l, v_l = _attention(z, lambda_qk[l], subln[l], ck[:, l], cv[:, l], rope_tabs, lam_init, dims)
        new_k.append(k_l.reshape(b_ctx, t_ctx, N_HEADS_A, 2 * DK_A))
        new_v.append(v_l.reshape(b_ctx, t_ctx, N_HEADS_A, DV_A))
        bat = z[:, OFF_BA:OFF_BA + 4 * N_HEADS_B].T
        oc, u, l1, l2, e = _local(z, bat, conv_w[l], pool_w[l], pool_scale[l], a_log[l], dt_bias[l], seq)
        *delta_o, s_ctx = _delta_scan(u, l1, l2, e, s_lat[:, l], dims)
        new_s.append(s_ctx.reshape(b_ctx, 2, N_HEADS_B, DK_B, DV_B))

        x1, h2, idx_t, wt_t = _mix(x, mod[l], oa, delta_o, z, oc, delta_norm[l], norm2[l], *w_proj, l,
                                   wr_t, br, cond_of_row)
        y2 = _moe(h2, _dispatch_meta(idx_t, MOE_ROWS), w_gate, w_up, w_down, l)
        if last:
            y_ctx, y_lat = _combine(y2, x1, mod[l], wt_t.T, norm_final, mod[l], True, cond_of_row)
        else:
            x, h = _combine(y2, x1, mod[l], wt_t.T, norm1[l + 1], mod[l + 1], False, cond_of_row)

    y_prompt = y_ctx.reshape(b_ctx, t_ctx, d)
    y_sample = y_lat.reshape(b_lat, t_lat, d)
    return (y_prompt, y_sample, jnp.stack(new_k, axis=1), jnp.stack(new_v, axis=1), jnp.stack(new_s, axis=1))
```

```python
import functools
import math

import jax
import jax.numpy as jnp
from jax import lax
from jax.experimental import pallas as pl
from jax.experimental.pallas import tpu as pltpu

F32 = jnp.float32
BF16 = jnp.bfloat16
HIGHEST = lax.Precision.HIGHEST

D_MODEL = 2048
GRID_W = 64
EPS = 1e-6
N_HEADS_A = 8
DK_A = 64
DV_A = 128
QK_A = N_HEADS_A * 2 * DK_A
W_A = N_HEADS_A * DV_A
ROPE_BASE = 10000.0
ROPE_AXIS = DK_A // 2
N_HEADS_B = 4
DK_B = 128
DV_B = 128
W_B = N_HEADS_B * DV_B
DELTA_CHUNK = 64
POOL_WINDOWS = (2, 4, 8, 16)
POOL_GROUP_W = 128
W_C = len(POOL_WINDOWS) * POOL_GROUP_W
N_BRANCH = 3
N_EXPERTS = 16
N_EXPERT_GROUPS = 4
EXPERTS_PER_GROUP = N_EXPERTS // N_EXPERT_GROUPS
TOP_K = 2
D_FF = 1408

SRC_QKV_B = 2 * QK_A + W_A
SRC_GATE_B = SRC_QKV_B + 3 * W_B
SRC_BA = SRC_GATE_B + W_B
SRC_POOL = SRC_BA + 4 * N_HEADS_B
SRC_BRANCH = SRC_POOL + W_C
D_IN = SRC_BRANCH + N_BRANCH * D_MODEL
LANES = 128
SUBLANES = 8
IN_TILE_N = 1024
IN_HALF_N = IN_TILE_N // 2
OFF_RAW = N_BRANCH * D_MODEL
RAW_W = -(-SRC_BRANCH // IN_TILE_N) * IN_TILE_N
NP_IN = OFF_RAW + RAW_W
OFF_Q = OFF_RAW
OFF_K = OFF_Q + QK_A
OFF_V = OFF_K + QK_A
OFF_QKV_B = OFF_RAW + SRC_QKV_B
OFF_GATE_B = OFF_RAW + SRC_GATE_B
OFF_BA = OFF_RAW + SRC_BA
BRANCH_SHIFT = SRC_BRANCH % LANES
assert SRC_BA % IN_TILE_N == 0 and (SRC_BRANCH - BRANCH_SHIFT) % IN_HALF_N == 0
assert SRC_POOL - SRC_BA == BRANCH_SHIFT and OFF_RAW % IN_TILE_N == 0

ROW_TILE = 256
MOE_ROWS = 256
INV_BASE_BLOCK = 16
MOE_MAT_CHUNKS = 8
MOE_STAGE_SLOTS = 2
MOE_VMEM_LIMIT = 60 * 1024 * 1024
VMEM_LIMIT = 56 * 1024 * 1024


def _cparams(n_axes, vmem=VMEM_LIMIT):
    return pltpu.CompilerParams(dimension_semantics=("arbitrary",) * n_axes, vmem_limit_bytes=vmem)


def _dot(a, b, precision=None):
    return jnp.dot(a, b, preferred_element_type=F32, precision=precision)


def _dot_nt(a, b, precision=None):
    return lax.dot_general(a, b, (((1,), (1,)), ((), ())), preferred_element_type=F32, precision=precision)


def _bdot(a, b):
    return _dot(a.astype(BF16), b.astype(BF16))


def _bdot_nt(a, b):
    return _dot_nt(a.astype(BF16), b.astype(BF16))


def _sigmoid(x):
    return 1.0 / (1.0 + jnp.exp(-x))


def _silu(x):
    return x * _sigmoid(x)


def _rms(x, eps=EPS):
    return x * lax.rsqrt(jnp.mean(x * x, axis=-1, keepdims=True) + eps)


def _pick_tile(cands, *dims):
    for t in cands:
        if all(d % t == 0 for d in dims):
            return t
    raise ValueError(f"no tile in {cands} divides {dims}")


def _mod_kernel(c_ref, w_ref, b_ref, o_ref):
    a = _silu(c_ref[...]).astype(BF16)
    o_ref[...] = _dot(a, w_ref[...].astype(BF16)) + b_ref[...]


def _modulation(cond8, w_mod, b_mod):
    depth, d, n6 = w_mod.shape
    tn = 1024
    return pl.pallas_call(
        _mod_kernel,
        grid=(depth, n6 // tn),
        in_specs=[
            pl.BlockSpec((SUBLANES, d), lambda l, n: (0, 0)),
            pl.BlockSpec((None, d, tn), lambda l, n: (l, 0, n)),
            pl.BlockSpec((None, 1, tn), lambda l, n: (l, 0, n)),
        ],
        out_specs=pl.BlockSpec((None, SUBLANES, tn), lambda l, n: (l, 0, n)),
        out_shape=jax.ShapeDtypeStruct((depth, SUBLANES, n6), F32),
        compiler_params=_cparams(2),
        name="adaln_mod",
    )(cond8, w_mod, b_mod.reshape(depth, 1, n6))


def _modulated_norm(x, g, mod, d):
    return _rms(x) * g * (1.0 + mod[:, d:2 * d]) + mod[:, 0:d]


def _norm_kernel(x_ref, mod_ref, g_ref, h_ref):
    h_ref[...] = _modulated_norm(x_ref[...], g_ref[...], mod_ref[...], x_ref.shape[1]).astype(BF16)


def _norm_mod(x, mod_l, norm_l, cond_of_row):
    m, d = x.shape
    r = ROW_TILE
    return pl.pallas_call(
        _norm_kernel,
        grid=(m // r,),
        in_specs=[
            pl.BlockSpec((r, d), lambda i: (i, 0)),
            pl.BlockSpec((None, 1, 6 * d), lambda i: (cond_of_row(i * r), 0, 0)),
            pl.BlockSpec((1, d), lambda i: (0, 0)),
        ],
        out_specs=pl.BlockSpec((r, d), lambda i: (i, 0)),
        out_shape=jax.ShapeDtypeStruct((m, d), BF16),
        compiler_params=_cparams(1),
        name="norm_mod",
    )(x, mod_l, norm_l.reshape(1, d))


def _in_kernel(h_ref, wt_ref, z_ref, w_scr):
    @pl.when(pl.program_id(1) == 0)
    def _():
        w_scr[...] = wt_ref[0].astype(BF16)

    z_ref[...] = _dot_nt(h_ref[...], w_scr[...])


def _in_proj(h, w_in_t, layer, tm):
    m, d = h.shape
    n_branch_tiles = OFF_RAW // IN_TILE_N
    assert SRC_BRANCH % SUBLANES == 0
    tile8 = IN_TILE_N // SUBLANES
    row0 = lambda n: SUBLANES * jnp.where(n < n_branch_tiles, SRC_BRANCH // SUBLANES + tile8 * n,
                                          tile8 * (n - n_branch_tiles))
    return pl.pallas_call(
        _in_kernel,
        grid=(NP_IN // IN_TILE_N, m // tm),
        in_specs=[
            pl.BlockSpec((tm, d), lambda n, i: (i, 0)),
            pl.BlockSpec((pl.Element(1), pl.Element(IN_TILE_N), pl.Element(d)),
                         lambda n, i: (layer, row0(n), 0)),
        ],
        out_specs=pl.BlockSpec((tm, IN_TILE_N), lambda n, i: (i, n)),
        out_shape=jax.ShapeDtypeStruct((m, NP_IN), F32),
        scratch_shapes=[pltpu.VMEM((IN_TILE_N, d), BF16)],
        compiler_params=_cparams(2),
        name="in_proj",
    )(h, w_in_t)


def _lambda(lq_ref, lam_init):
    lq = lq_ref[...]
    a = jnp.sum(lq[0:1] * lq[1:2], axis=-1, keepdims=True)
    b = jnp.sum(lq[2:3] * lq[3:4], axis=-1, keepdims=True)
    return jnp.exp(a) - jnp.exp(b) + lam_init


def _diff_attn(qb, kb, vb, lam):
    probs = []
    for mp in range(2):
        s = _dot_nt(qb[:, mp * DK_A:(mp + 1) * DK_A], kb[:, mp * DK_A:(mp + 1) * DK_A])
        e = jnp.exp(s - jnp.max(s, axis=-1, keepdims=True))
        probs.append(e / jnp.sum(e, axis=-1, keepdims=True))
    pd = probs[0] - lam * probs[1]
    return _dot(pd.astype(BF16), vb)


def _attn_ctx_kernel(lam_init, lq_ref, sub_ref, q_ref, k_ref, v_ref, o_ref, knew_ref, vnew_ref):
    lam = _lambda(lq_ref, lam_init)
    qb = (q_ref[...] * DK_A ** -0.5).astype(BF16)
    k, v = k_ref[...], v_ref[...]
    knew_ref[...] = k
    vnew_ref[...] = v
    o = _diff_attn(qb, k.astype(BF16), v.astype(BF16), lam)
    o_ref[...] = (_rms(o) * sub_ref[...] * (1.0 - lam_init)).astype(BF16)


def _rope(x, cos, sin_signed):
    lane = lax.broadcasted_iota(jnp.int32, x.shape, 1)
    first_half = (lane % (ROPE_AXIS)) < (ROPE_AXIS // 2)
    partner = jnp.where(first_half, pltpu.roll(x, LANES - ROPE_AXIS // 2, 1), pltpu.roll(x, ROPE_AXIS // 2, 1))
    return x * cos + partner * sin_signed


def _attn_lat_kernel(lam_init, t_lat, lq_ref, sub_ref, q_ref, k_ref, v_ref, ck_ref, cv_ref,
                     cq_ref, sq_ref, ckk_ref, skk_ref, o_ref, k_scr, v_scr):
    @pl.when(pl.program_id(2) == 0)
    def _():
        k_scr[0:t_lat, :] = _rope(k_ref[...], ckk_ref[...], skk_ref[...]).astype(BF16)
        k_scr[t_lat:, :] = ck_ref[...].astype(BF16)
        v_scr[0:t_lat, :] = v_ref[...].astype(BF16)
        v_scr[t_lat:, :] = cv_ref[...].astype(BF16)

    lam = _lambda(lq_ref, lam_init)
    q = _rope(q_ref[...], cq_ref[...], sq_ref[...]) * DK_A ** -0.5
    o = _diff_attn(q.astype(BF16), k_scr[...], v_scr[...], lam)
    o_ref[...] = (_rms(o) * sub_ref[...] * (1.0 - lam_init)).astype(BF16)


def _attention(z, lq_l, sub_l, cache_k_l, cache_v_l, rope_tabs, lam_init, dims):
    b_ctx, t_ctx, b_lat, t_lat = dims
    n_ctx = b_ctx * t_ctx
    past = cache_k_l.shape[1]
    cq, ck, cv = OFF_Q // LANES, OFF_K // LANES, OFF_V // LANES
    small = [pl.BlockSpec((4, DK_A), lambda *_: (0, 0)), pl.BlockSpec((1, DV_A), lambda *_: (0, 0))]
    sub2 = sub_l.reshape(1, DV_A)

    head_blk = pl.BlockSpec((t_ctx, LANES), lambda b, h: (b, h))
    oa_ctx, k_new, v_new = pl.pallas_call(
        functools.partial(_attn_ctx_kernel, lam_init),
        grid=(b_ctx, N_HEADS_A),
        in_specs=small + [
            pl.BlockSpec((t_ctx, LANES), lambda b, h: (b, cq + h)),
            pl.BlockSpec((t_ctx, LANES), lambda b, h: (b, ck + h)),
            pl.BlockSpec((t_ctx, LANES), lambda b, h: (b, cv + h)),
        ],
        out_specs=[head_blk] * 3,
        out_shape=[jax.ShapeDtypeStruct((n_ctx, W_A), BF16), jax.ShapeDtypeStruct((n_ctx, QK_A), F32),
                   jax.ShapeDtypeStruct((n_ctx, W_A), F32)],
        compiler_params=_cparams(2),
        name="attn_ctx",
    )(lq_l, sub2, z, z, z)

    tq = _pick_tile((256, 128), t_lat)
    nq = t_lat // tq
    rb_q = n_ctx // tq
    rb_k = n_ctx // t_lat
    cos_t, sin_t = rope_tabs
    oa_lat = pl.pallas_call(
        functools.partial(_attn_lat_kernel, lam_init, t_lat),
        grid=(b_lat, N_HEADS_A, nq),
        in_specs=small + [
            pl.BlockSpec((tq, LANES), lambda b, h, i: (rb_q + b * nq + i, cq + h)),
            pl.BlockSpec((t_lat, LANES), lambda b, h, i: (rb_k + b, ck + h)),
            pl.BlockSpec((t_lat, LANES), lambda b, h, i: (rb_k + b, cv + h)),
            pl.BlockSpec((None, past, LANES), lambda b, h, i: (b, 0, h)),
            pl.BlockSpec((None, past, LANES), lambda b, h, i: (b, 0, h)),
            pl.BlockSpec((tq, LANES), lambda b, h, i: (i, 0)),
            pl.BlockSpec((tq, LANES), lambda b, h, i: (i, 0)),
            pl.BlockSpec((t_lat, LANES), lambda b, h, i: (0, 0)),
            pl.BlockSpec((t_lat, LANES), lambda b, h, i: (0, 0)),
        ],
        out_specs=pl.BlockSpec((tq, LANES), lambda b, h, i: (b * nq + i, h)),
        out_shape=jax.ShapeDtypeStruct((b_lat * t_lat, W_A), BF16),
        scratch_shapes=[pltpu.VMEM((t_lat + past, LANES), BF16), pltpu.VMEM((t_lat + past, LANES), BF16)],
        compiler_params=_cparams(3),
        name="attn_lat",
    )(lq_l, sub2, z, z, z, cache_k_l, cache_v_l, cos_t, sin_t, cos_t, sin_t)
    return (oa_ctx, oa_lat), k_new, v_new


def _rope_tables(t_lat):
    rows = t_lat // GRID_W
    r = jnp.repeat(jnp.arange(rows), GRID_W).astype(F32)
    col = jnp.tile(jnp.arange(GRID_W), rows).astype(F32)
    inv = ROPE_BASE ** (-jnp.arange(0, ROPE_AXIS, 2, dtype=F32) / ROPE_AXIS)
    ang_r, ang_c = r[:, None] * inv, col[:, None] * inv
    cos64 = jnp.concatenate([jnp.cos(ang_r)] * 2 + [jnp.cos(ang_c)] * 2, axis=-1)
    sin64 = jnp.concatenate([-jnp.sin(ang_r), jnp.sin(ang_r), -jnp.sin(ang_c), jnp.sin(ang_c)], axis=-1)
    return jnp.tile(cos64, (1, 2)), jnp.tile(sin64, (1, 2))


def _log_decay(alpha, a_log, dt):
    x = alpha + dt
    return -jnp.exp(a_log) * (jnp.maximum(x, 0.0) + jnp.log(1.0 + jnp.exp(-jnp.abs(x))))


def _chunk_cumsum(g, axis):
    n = g.shape[axis]
    pos = lax.broadcasted_iota(jnp.int32, g.shape, axis) % DELTA_CHUNK
    f, b = g, g
    s = 1
    while s < DELTA_CHUNK:
        f = f + jnp.where(pos >= s, pltpu.roll(f, s, axis), 0.0)
        b = b + jnp.where(pos < DELTA_CHUNK - s, pltpu.roll(b, n - s, axis), 0.0)
        s *= 2
    return f, b


def _delta_prepare(q_scr, k_scr, v_scr, ba, bat, al_row, al_col, dt_row, dt_col,
                   u_ref, l1_ref, l2_ref, e_ref):
    c, nh = DELTA_CHUNK, N_HEADS_B
    rows_n = q_scr.shape[0]
    beta_all = _sigmoid(ba[:, 0:2 * nh])
    gf, gb = _chunk_cumsum(_log_decay(ba[:, 2 * nh:4 * nh], al_row[...], dt_row[...]), 0)
    gc_all = jnp.where(lax.broadcasted_iota(jnp.int32, gf.shape, 1) < nh, gf, gb)
    gtf, gtb = _chunk_cumsum(_log_decay(bat[2 * nh:4 * nh, :], al_col[...], dt_col[...]), 1)
    gct_all = jnp.where(lax.broadcasted_iota(jnp.int32, gtf.shape, 0) < nh, gtf, gtb)

    ri = lax.broadcasted_iota(jnp.int32, (c, c), 0)
    ci = lax.broadcasted_iota(jnp.int32, (c, c), 1)
    masks = ((ri >= ci, ri > ci), (ri <= ci, ri < ci))
    eye = (ri == ci).astype(F32)
    blk_sizes = [INV_BASE_BLOCK << i for i in range(int(math.log2(c // INV_BASE_BLOCK)) + 1)]
    same_blk = [(ri // s) == (ci // s) for s in blk_sizes]
    n_sq = int(math.log2(INV_BASE_BLOCK)) - 1

    for j in range(rows_n // c):
        rs = slice(j * c, (j + 1) * c)
        chains = []
        for d in range(2):
            incl, strict = masks[d]
            last = 0 if d else c - 1
            for h in range(nh):
                idx = d * nh + h
                hs = slice(h * DK_B, (h + 1) * DK_B)
                gc = gc_all[rs, idx:idx + 1]
                gr = gct_all[idx:idx + 1, rs]
                ch = dict(d=d, h=h, idx=idx, hs=hs, strict=strict, gc=gc, g_last=gc[last:last + 1, :],
                          q=q_scr[rs, hs] * DK_B ** -0.5, k=k_scr[rs, hs], v=v_scr[rs, hs],
                          beta=beta_all[rs, idx:idx + 1],
                          decay=jnp.where(incl, jnp.exp(jnp.where(incl, gc - gr, 0.0)), 0.0))
                ch["kb"] = ch["k"] * ch["beta"]
                chains.append(ch)
        for ch in chains:
            kbf = ch["k"].astype(BF16)
            a = jnp.where(ch["strict"], -(_dot_nt(ch["kb"].astype(BF16), kbf) * ch["decay"]), 0.0)
            ch["attn"] = _dot_nt(ch["q"].astype(BF16), kbf) * ch["decay"]
            ch["a"] = a
            ch["apow"] = jnp.where(same_blk[0], a, 0.0)
            ch["p"] = eye + ch["apow"]
        for _ in range(n_sq):
            for ch in chains:
                ch["apow"] = _bdot(ch["apow"], ch["apow"])
            for ch in chains:
                ch["p"] = ch["p"] + _bdot(ch["p"], ch["apow"])
        for lvl in range(1, len(same_blk)):
            for ch in chains:
                a_off = jnp.where(same_blk[lvl] & jnp.logical_not(same_blk[lvl - 1]), ch["a"], 0.0)
                ch["t"] = _bdot(ch["p"], a_off)
            for ch in chains:
                ch["p"] = ch["p"] + _bdot(ch["t"], ch["p"])
        for ch in chains:
            rhs = jnp.concatenate([ch["v"] * ch["beta"], ch["kb"] * jnp.exp(ch["gc"])], axis=-1)
            ch["sol"] = _bdot(ch["p"], rhs)
        for ch in chains:
            d, h, hs = ch["d"], ch["h"], ch["hs"]
            u_ref[d, rs, hs] = ch["sol"][:, :DV_B]
            l1_ref[d, j, h, 0:c, :] = ch["sol"][:, DV_B:].astype(BF16)
            l1_ref[d, j, h, c:2 * c, :] = (ch["q"] * jnp.exp(ch["gc"])).astype(BF16)
            l2_ref[d, j, h, 0:c, :] = ch["attn"].astype(BF16)
            l2_ref[d, j, h, c:, :] = (ch["k"] * jnp.exp(ch["g_last"] - ch["gc"])).T.astype(BF16)
            e_ref[j, ch["idx"]:ch["idx"] + 1, :] = jnp.broadcast_to(jnp.exp(ch["g_last"]), (1, LANES))


def _local_kernel(seq, cur_ref, prev_ref, next_ref, pcur_ref, pprev_ref, pnext_ref, bat_ref,
                  cw_ref, pw_ref, ps_ref, al_row, al_col, dt_row, dt_col,
                  oc_ref, u_ref, l1_ref, l2_ref, e_ref, q_scr, k_scr, v_scr):
    rows_n = cur_ref.shape[0]
    pos0, t_seq = seq(pl.program_id(0) * rows_n)
    has_prev = pos0 > 0
    has_next = pos0 + rows_n < t_seq
    rows = lax.broadcasted_iota(jnp.int32, (rows_n, 1), 0)

    x = cur_ref[...]
    prev_row = jnp.where(has_prev, prev_ref[SUBLANES - 1:SUBLANES, :], 0.0)
    next_row = jnp.where(has_next, next_ref[0:1, :], 0.0)
    xm1 = jnp.where(rows == 0, prev_row, pltpu.roll(x, 1, 0))
    xp1 = jnp.where(rows == rows_n - 1, next_row, pltpu.roll(x, rows_n - 1, 0))
    y = _silu(cw_ref[0:1, :] * xm1 + cw_ref[1:2, :] * x + cw_ref[2:3, :] * xp1)
    for h in range(N_HEADS_B):
        for j, ref in enumerate((q_scr, k_scr)):
            a = y[:, j * W_B + h * DK_B:j * W_B + (h + 1) * DK_B]
            ref[:, h * DK_B:(h + 1) * DK_B] = a * lax.rsqrt(jnp.sum(a * a, axis=-1, keepdims=True) + EPS)
    v_scr[...] = y[:, 2 * W_B:]
    _delta_prepare(q_scr, k_scr, v_scr, pcur_ref[:, 0:LANES], bat_ref[...], al_row, al_col, dt_row, dt_col,
                   u_ref, l1_ref, l2_ref, e_ref)

    def pool_cols(ref):
        t = ref[...]
        return pltpu.roll(t, t.shape[1] - BRANCH_SHIFT, 1)[:, :W_C]

    pz = pool_cols(pcur_ref)
    ext = jnp.concatenate([jnp.where(has_prev, pool_cols(pprev_ref), 0.0), pz,
                           jnp.where(has_next, pool_cols(pnext_ref), 0.0)], axis=0)
    n_ext = rows_n + 2 * SUBLANES
    tpos = pos0 + rows
    for gi, win in enumerate(POOL_WINDOWS):
        sl = slice(gi * POOL_GROUP_W, (gi + 1) * POOL_GROUP_W)
        xg = ext[:, sl]
        acc = xg + pltpu.roll(xg, 1, 0)
        s = 1
        while 2 * s < win:
            acc = pltpu.roll(acc, s, 0) + pltpu.roll(acc, n_ext - s, 0)
            s *= 2
        acc = acc[SUBLANES:SUBLANES + rows_n]
        cnt = (jnp.minimum(tpos + win // 2, t_seq) - jnp.maximum(tpos - win // 2, 0)).astype(F32)
        d = acc / cnt - pz[:, sl]
        yg = _dot(d.astype(BF16), pw_ref[gi].astype(BF16)) * ps_ref[:, sl]
        oc_ref[:, sl] = yg.astype(BF16)


def _local(z, bat, conv_w_l, pool_w_l, pool_scale_l, a_log_l, dt_l, seq):
    m = z.shape[0]
    r, c = ROW_TILE, DELTA_CHUNK
    hb = r // SUBLANES
    last_hb = m // SUBLANES - 1
    c_qkv, c_pool = OFF_QKV_B // (3 * W_B), OFF_BA // IN_TILE_N
    nh2 = 2 * N_HEADS_B
    mc, cpt = m // c, r // c
    prev_map = lambda cb: (lambda i: (jnp.maximum(i * hb - 1, 0), cb))
    next_map = lambda cb: (lambda i: (jnp.minimum((i + 1) * hb, last_hb), cb))
    row = pl.BlockSpec((1, nh2), lambda i: (0, 0))
    col = pl.BlockSpec((nh2, 1), lambda i: (0, 0))
    return pl.pallas_call(
        functools.partial(_local_kernel, seq),
        grid=(m // r,),
        in_specs=[
            pl.BlockSpec((r, 3 * W_B), lambda i: (i, c_qkv)),
            pl.BlockSpec((SUBLANES, 3 * W_B), prev_map(c_qkv)),
            pl.BlockSpec((SUBLANES, 3 * W_B), next_map(c_qkv)),
            pl.BlockSpec((r, IN_TILE_N), lambda i: (i, c_pool)),
            pl.BlockSpec((SUBLANES, IN_TILE_N), prev_map(c_pool)),
            pl.BlockSpec((SUBLANES, IN_TILE_N), next_map(c_pool)),
            pl.BlockSpec((2 * nh2, r), lambda i: (0, i)),
            pl.BlockSpec((3, 3 * W_B), lambda i: (0, 0)),
            pl.BlockSpec((len(POOL_WINDOWS), POOL_GROUP_W, POOL_GROUP_W), lambda i: (0, 0, 0)),
            pl.BlockSpec((1, W_C), lambda i: (0, 0)),
            row, col, row, col,
        ],
        out_specs=[
            pl.BlockSpec((r, W_C), lambda i: (i, 0)),
            pl.BlockSpec((2, r, W_B), lambda i: (0, i, 0)),
            pl.BlockSpec((2, cpt, N_HEADS_B, 2 * c, DV_B), lambda i: (0, i, 0, 0, 0)),
            pl.BlockSpec((2, cpt, N_HEADS_B, c + DK_B, c), lambda i: (0, i, 0, 0, 0)),
            pl.BlockSpec((cpt, nh2, LANES), lambda i: (i, 0, 0)),
        ],
        out_shape=[
            jax.ShapeDtypeStruct((m, W_C), BF16),
            jax.ShapeDtypeStruct((2, m, W_B), F32),
            jax.ShapeDtypeStruct((2, mc, N_HEADS_B, 2 * c, DV_B), BF16),
            jax.ShapeDtypeStruct((2, mc, N_HEADS_B, c + DK_B, c), BF16),
            jax.ShapeDtypeStruct((mc, nh2, LANES), F32),
        ],
        scratch_shapes=[pltpu.VMEM((r, W_B), F32)] * 3,
        compiler_params=_cparams(1),
        name="local_conv_pool",
    )(z, z, z, z, z, z, bat, conv_w_l, pool_w_l, pool_scale_l.reshape(1, W_C),
      a_log_l.reshape(1, nh2), a_log_l.reshape(nh2, 1), dt_l.reshape(1, nh2), dt_l.reshape(nh2, 1))


def _scan_kernel(par, b_lat, n_ctx_ch, *refs):
    c, nh = DELTA_CHUNK, N_HEADS_B
    n_slot = par + b_lat
    ins = refs[:n_slot * 8]
    s0_ref = refs[n_slot * 8]
    of_ctx, ob_ctx, of_lat, ob_lat, sfin_ref, s_scr = refs[n_slot * 8 + 1:]
    step = pl.program_id(0)
    cc = step % n_ctx_ch

    @pl.when(cc == 0)
    def _():
        s_scr[0:par] = jnp.zeros((par,) + s_scr.shape[1:], F32)

    @pl.when(step == 0)
    def _():
        s_scr[par:] = s0_ref[...]

    for slot in range(n_slot):
        outs = (of_ctx, ob_ctx, slot) if slot < par else (of_lat, ob_lat, slot - par)
        chains = []
        for d in range(2):
            l1_ref, l2_ref, u_ref, e_ref = ins[(slot * 2 + d) * 4:(slot * 2 + d) * 4 + 4]
            for h in range(nh):
                chains.append(dict(d=d, h=h, idx=d * nh + h, l1=l1_ref, l2=l2_ref, u=u_ref, e=e_ref))
        for ch in chains:
            ch["s"] = s_scr[slot, ch["idx"]]
            ch["r1"] = _dot(ch["l1"][ch["h"]], ch["s"].astype(BF16))
        for ch in chains:
            hs = slice(ch["h"] * DV_B, (ch["h"] + 1) * DV_B)
            v_new = ch["u"][:, hs] - ch["r1"][:c]
            ch["r2"] = _dot(ch["l2"][ch["h"]], v_new.astype(BF16))
        for ch in chains:
            hs = slice(ch["h"] * DV_B, (ch["h"] + 1) * DV_B)
            outs[ch["d"]][outs[2], :, hs] = ch["r1"][c:] + ch["r2"][:c]
            e = ch["e"][ch["idx"]:ch["idx"] + 1, :]
            s_scr[slot, ch["idx"]] = ch["s"] * e + ch["r2"][c:]

    @pl.when(cc == n_ctx_ch - 1)
    def _():
        sfin_ref[...] = s_scr[0:par]


def _delta_scan(u, l1, l2, e, s0_lat, dims):
    b_ctx, t_ctx, b_lat, t_lat = dims
    c, nh = DELTA_CHUNK, N_HEADS_B
    n_ctx_ch, n_lat_ch = t_ctx // c, t_lat // c
    par = b_ctx * n_ctx_ch // n_lat_ch
    assert par >= 1 and par * n_lat_ch == b_ctx * n_ctx_ch and b_ctx % par == 0
    mc = l1.shape[1]
    ctx_chunks = b_ctx * n_ctx_ch
    u4 = u.reshape(2, mc, c, W_B)

    def chunk_of(slot, d):
        if slot < par:
            return lambda s: ((s // n_ctx_ch) * par + slot) * n_ctx_ch + (
                (n_ctx_ch - 1 - s % n_ctx_ch) if d else s % n_ctx_ch)
        q = slot - par
        return lambda s: ctx_chunks + q * n_lat_ch + ((n_lat_ch - 1 - s) if d else s)

    in_specs, args = [], []
    for slot in range(par + b_lat):
        for d in range(2):
            cg = chunk_of(slot, d)
            in_specs += [
                pl.BlockSpec((None, None, nh, 2 * c, DV_B), lambda s, cg=cg, d=d: (d, cg(s), 0, 0, 0)),
                pl.BlockSpec((None, None, nh, c + DK_B, c), lambda s, cg=cg, d=d: (d, cg(s), 0, 0, 0)),
                pl.BlockSpec((None, None, c, W_B), lambda s, cg=cg, d=d: (d, cg(s), 0, 0)),
                pl.BlockSpec((None, 2 * nh, LANES), lambda s, cg=cg: (cg(s), 0, 0)),
            ]
            args += [l1, l2, u4, e]
    in_specs.append(pl.BlockSpec((b_lat, 2 * nh, DK_B, DV_B), lambda s: (0, 0, 0, 0)))
    args.append(s0_lat)

    ctx_o = jax.ShapeDtypeStruct((b_ctx // par, par, n_ctx_ch, c, W_B), F32)
    lat_o = jax.ShapeDtypeStruct((b_lat, n_lat_ch, c, W_B), F32)
    cf = lambda s: s % n_ctx_ch
    of_ctx, ob_ctx, of_lat, ob_lat, s_fin = pl.pallas_call(
        functools.partial(_scan_kernel, par, b_lat, n_ctx_ch),
        grid=(n_lat_ch,),
        in_specs=in_specs,
        out_specs=[
            pl.BlockSpec((None, par, None, c, W_B), lambda s: (s // n_ctx_ch, 0, cf(s), 0, 0)),
            pl.BlockSpec((None, par, None, c, W_B), lambda s: (s // n_ctx_ch, 0, n_ctx_ch - 1 - cf(s), 0, 0)),
            pl.BlockSpec((b_lat, None, c, W_B), lambda s: (0, s, 0, 0)),
            pl.BlockSpec((b_lat, None, c, W_B), lambda s: (0, n_lat_ch - 1 - s, 0, 0)),
            pl.BlockSpec((par, 2 * nh, DK_B, DV_B), lambda s: (s // n_ctx_ch, 0, 0, 0)),
        ],
        out_shape=[ctx_o, ctx_o, lat_o, lat_o, jax.ShapeDtypeStruct((b_ctx, 2 * nh, DK_B, DV_B), F32)],
        scratch_shapes=[pltpu.VMEM((par + b_lat, 2 * nh, DK_B, DV_B), F32)],
        compiler_params=_cparams(1),
        name="delta_scan",
    )(*args)
    n_ctx, n_lat = b_ctx * t_ctx, b_lat * t_lat
    return (of_ctx.reshape(n_ctx, W_B), ob_ctx.reshape(n_ctx, W_B), of_lat.reshape(n_lat, W_B),
            ob_lat.reshape(n_lat, W_B), s_fin)


def _route_rows(sel, scores):
    g = EXPERTS_PER_GROUP
    gscore = []
    for gi in range(N_EXPERT_GROUPS):
        a, b, c, d = sel[gi * g:(gi + 1) * g]
        hi1, lo1 = jnp.maximum(a, b), jnp.minimum(a, b)
        hi2, lo2 = jnp.maximum(c, d), jnp.minimum(c, d)
        gscore.append(jnp.maximum(hi1, hi2) + jnp.maximum(jnp.minimum(hi1, hi2), jnp.maximum(lo1, lo2)))
    best = jnp.zeros_like(gscore[0], dtype=jnp.int32)
    bestv = gscore[0]
    for gi in range(1, N_EXPERT_GROUPS):
        upd = gscore[gi] > bestv
        best = jnp.where(upd, gi, best)
        bestv = jnp.where(upd, gscore[gi], bestv)

    def in_best(rows, j):
        out = rows[j]
        for gi in range(1, N_EXPERT_GROUPS):
            out = jnp.where(best == gi, rows[gi * g + j], out)
        return out

    e_sel = [in_best(sel, j) for j in range(g)]
    e_sc = [in_best(scores, j) for j in range(g)]

    def first_argmax(vals):
        bi, bv, bs = jnp.zeros_like(best), vals[0], e_sc[0]
        for j in range(1, g):
            upd = vals[j] > bv
            bi = jnp.where(upd, j, bi)
            bv = jnp.where(upd, vals[j], bv)
            bs = jnp.where(upd, e_sc[j], bs)
        return bi, bs

    i0, s0 = first_argmax(e_sel)
    i1, s1 = first_argmax([jnp.where(i0 == j, -jnp.inf, e_sel[j]) for j in range(g)])
    tot = s0 + s1
    return best * g + i0, best * g + i1, s0 / tot, s1 / tot


def _mix_kernel(n_ctx, x_ref, mod_ref, oac_ref, oal_ref, ofc_ref, obc_ref, ofl_ref, obl_ref,
                gb_ref, oc_ref, g0_ref, g1_ref, g2_ref,
                dn_ref, n2_ref, wpa_ref, wpb_ref, wpc_ref, wout_ref, wr_ref, br_ref,
                x1_ref, h2_ref, idx_ref, wt_ref):
    d = x_ref.shape[1]
    is_ctx = pl.program_id(0) * x_ref.shape[0] < n_ctx
    dn = dn_ref[...]
    o = jnp.where(is_ctx, ofc_ref[...] + obc_ref[...], ofl_ref[...] + obl_ref[...])
    o_a = jnp.where(is_ctx, oac_ref[...], oal_ref[...])
    gate = _silu(gb_ref[...])
    parts = []
    for h in range(N_HEADS_B):
        sl = slice(h * DV_B, (h + 1) * DV_B)
        parts.append((_rms(o[:, sl]) * dn * gate[:, sl]).astype(BF16))
    o_b = jnp.concatenate(parts, axis=-1)
    mixed = (_sigmoid(g0_ref[...]) * _dot(o_a, wpa_ref[...])
             + _sigmoid(g1_ref[...]) * _dot(o_b, wpb_ref[...])
             + _sigmoid(g2_ref[...]) * _dot(oc_ref[...], wpc_ref[...]))
    gate1 = mod_ref[:, 2 * d:3 * d]
    x1 = x_ref[...] + gate1 * _dot(mixed.astype(BF16), wout_ref[...])
    x1_ref[...] = x1
    shift2 = mod_ref[:, 3 * d:4 * d]
    scale2 = mod_ref[:, 4 * d:5 * d]
    h2 = _rms(x1) * n2_ref[...] * (1.0 + scale2) + shift2
    h2_ref[...] = h2
    sc = _sigmoid(_dot_nt(wr_ref[...], h2, HIGHEST))
    sel = sc + br_ref[...]
    i0, i1, w0, w1 = _route_rows([sel[e:e + 1] for e in range(N_EXPERTS)],
                                 [sc[e:e + 1] for e in range(N_EXPERTS)])
    idx_ref[0:1, :] = i0
    idx_ref[1:2, :] = i1
    wt_ref[0:1, :] = w0
    wt_ref[1:2, :] = w1


def _mix(x, mod_l, oa, delta_o, z, oc, delta_norm_l, norm2_l, wpa, wpb, wpc, wout, layer, wr_t, br,
         cond_of_row):
    m, d = x.shape
    r = ROW_TILE
    n_ctx = cond_of_row.n_ctx
    nct = n_ctx // r
    c_gb = OFF_GATE_B // W_B
    once = dict(pipeline_mode=pl.Buffered(1))
    full = lambda shape: pl.BlockSpec(shape, lambda i: (0,) * len(shape), **once)
    of_layer = lambda shape: pl.BlockSpec((None,) + shape, lambda i: (layer,) + (0,) * len(shape), **once)
    ctx_rows = lambda w: pl.BlockSpec((r, w), lambda i: (jnp.minimum(i, nct - 1), 0))
    lat_rows = lambda w: pl.BlockSpec((r, w), lambda i: (jnp.maximum(i - nct, 0), 0))
    of_ctx, ob_ctx, of_lat, ob_lat = delta_o
    return pl.pallas_call(
        functools.partial(_mix_kernel, n_ctx),
        grid=(m // r,),
        in_specs=[
            pl.BlockSpec((r, d), lambda i: (i, 0)),
            pl.BlockSpec((None, 1, 6 * d), lambda i: (cond_of_row(i * r), 0, 0)),
            ctx_rows(W_A), lat_rows(W_A),
            ctx_rows(W_B), ctx_rows(W_B), lat_rows(W_B), lat_rows(W_B),
            pl.BlockSpec((r, W_B), lambda i: (i, c_gb)),
            pl.BlockSpec((r, W_C), lambda i: (i, 0)),
            pl.BlockSpec((r, d), lambda i: (i, 0)),
            pl.BlockSpec((r, d), lambda i: (i, 1)),
            pl.BlockSpec((r, d), lambda i: (i, 2)),
            full((1, DV_B)), full((1, d)),
            of_layer((W_A, d)), of_layer((W_B, d)), of_layer((W_C, d)), of_layer((d, d)),
            full((N_EXPERTS, d)), full((N_EXPERTS, 1)),
        ],
        out_specs=[
            pl.BlockSpec((r, d), lambda i: (i, 0)),
            pl.BlockSpec((r, d), lambda i: (i, 0)),
            pl.BlockSpec((TOP_K, r), lambda i: (0, i)),
            pl.BlockSpec((TOP_K, r), lambda i: (0, i)),
        ],
        out_shape=[
            jax.ShapeDtypeStruct((m, d), F32),
            jax.ShapeDtypeStruct((m, d), F32),
            jax.ShapeDtypeStruct((TOP_K, m), jnp.int32),
            jax.ShapeDtypeStruct((TOP_K, m), F32),
        ],
        compiler_params=_cparams(1),
        name="mix_route",
    )(x, mod_l, oa[0], oa[1], of_ctx, ob_ctx, of_lat, ob_lat, z, oc, z, z, z,
      delta_norm_l.reshape(1, DV_B), norm2_l.reshape(1, d), wpa, wpb, wpc, wout, wr_t, br)


def _dispatch_meta(idx_t, bm):
    m = idx_t.shape[1]
    nk = m * TOP_K
    flat_e = idx_t.T.reshape(nk)
    experts = jnp.arange(N_EXPERTS, dtype=jnp.int32)[None, :]
    counts = jnp.sum((flat_e[:, None] == experts).astype(jnp.int32), axis=0)
    order = jnp.argsort(flat_e, stable=True).astype(jnp.int32)
    start = jnp.cumsum(counts) - counts
    padded = (counts + bm - 1) // bm * bm
    pad_end = jnp.cumsum(padded)
    pad_start = pad_end - padded
    nb = -(-(nk + N_EXPERTS * (bm - 1)) // bm)
    p = jnp.arange(nb * bm, dtype=jnp.int32)
    e_of_p = jnp.minimum(jnp.sum((pad_end[None, :] <= p[:, None]).astype(jnp.int32), axis=1), N_EXPERTS - 1)
    pick = (e_of_p[:, None] == experts).astype(jnp.int32)
    rank = p - jnp.sum(pick * pad_start[None, :], axis=1)
    real = rank < jnp.sum(pick * counts[None, :], axis=1)
    assign = order[jnp.clip(jnp.sum(pick * start[None, :], axis=1) + rank, 0, nk - 1)]
    row_tok = jnp.where(real, assign // TOP_K, 0)
    spare = nk + ((p // bm) % 2) * bm + p % bm
    row_dst = jnp.where(real, (assign % TOP_K) * m + assign // TOP_K, spare)
    prime = nk + jnp.arange(2 * bm, dtype=jnp.int32)
    n_used = (pad_end[-1] // bm).astype(jnp.int32)
    blk_e = e_of_p[::bm]
    n_chunks = 3 * MOE_MAT_CHUNKS
    e_ids = experts[0]
    n_blk = padded // bm
    has = n_blk > 0
    later = lax.cummin(jnp.where(has, e_ids, N_EXPERTS)[::-1])[::-1]
    nxt_of_e = jnp.concatenate([later[1:], jnp.full((1,), N_EXPERTS, jnp.int32)])
    set_of_e = (jnp.cumsum(has.astype(jnp.int32)) - 1) % 2
    b = jnp.arange(nb, dtype=jnp.int32)
    k = b - (pad_start // bm)[blk_e]
    n_e = jnp.maximum(n_blk[blk_e], 1)
    streams = (nxt_of_e[blk_e] < N_EXPERTS) & (b < n_used)
    c0 = jnp.where(streams, n_chunks * k // n_e, 0)
    c1 = jnp.where(streams, n_chunks * (k + 1) // n_e, 0)
    nxt_e = jnp.where(streams, nxt_of_e[blk_e], 0)
    return (blk_e, n_used.reshape(1), row_tok, jnp.concatenate([prime, row_dst]),
            set_of_e[blk_e], nxt_e, c0, c1)


def _moe_kernel(layer, blk_e_ref, n_used_ref, row_tok_ref, row_dst_ref, set_ref, nxt_ref, c0_ref, c1_ref,
                h2_hbm, wg_hbm, wu_hbm, wd_hbm, out_hbm,
                xbuf, ybuf, wg_s, wu_s, wd_s, st_gu, st_d, gate_scr, act_scr, gsem, ssem, wsem_gu, wsem_d):
    i = pl.program_id(0)
    bm = xbuf.shape[1]
    n_used = n_used_ref[0]
    n_mat = MOE_MAT_CHUNKS
    n_stage = st_gu.shape[0]
    rg, rd = st_gu.shape[1], st_d.shape[1]

    def chunk_ops(c, j, e, s):
        cg, cu, cd = c, c - n_mat, c - 2 * n_mat
        rows_g = pl.ds(pl.multiple_of(cg * rg, rg), rg)
        rows_u = pl.ds(pl.multiple_of(cu * rg, rg), rg)
        rows_d = pl.ds(pl.multiple_of(cd * rd, 2 * SUBLANES), rd)

        def to_set(dst, rows, stage):
            def convert():
                dst[s, rows, :] = stage[j].astype(BF16)
            return convert

        return (
            (c < n_mat, lambda: pltpu.make_async_copy(wg_hbm.at[layer, e, rows_g, :], st_gu.at[j], wsem_gu.at[j]),
             to_set(wg_s, rows_g, st_gu)),
            ((c >= n_mat) & (c < 2 * n_mat),
             lambda: pltpu.make_async_copy(wu_hbm.at[layer, e, rows_u, :], st_gu.at[j], wsem_gu.at[j]),
             to_set(wu_s, rows_u, st_gu)),
            (c >= 2 * n_mat, lambda: pltpu.make_async_copy(wd_hbm.at[layer, e, rows_d, :], st_d.at[j], wsem_d.at[j]),
             to_set(wd_s, rows_d, st_d)),
        )

    def issue_round(r, c_lo, c_hi, e, s):
        for j in range(n_stage):
            c = c_lo + r * n_stage + j
            for pred, copy, _ in chunk_ops(c, j, e, s):
                pl.when(pred & (c < c_hi))(lambda copy=copy: copy().start(priority=1))

    def finish_round(r, c_lo, c_hi, e, s):
        for j in range(n_stage):
            c = c_lo + r * n_stage + j
            for pred, copy, convert in chunk_ops(c, j, e, s):
                @pl.when(pred & (c < c_hi))
                def _(copy=copy, convert=convert):
                    copy().wait()
                    convert()

    def sync_rounds(r_lo, c_lo, c_hi, e, s):
        def body(r, carry):
            issue_round(r, c_lo, c_hi, e, s)
            finish_round(r, c_lo, c_hi, e, s)
            return carry
        n_rounds = (jnp.maximum(c_hi - c_lo, 0) + n_stage - 1) // n_stage
        lax.fori_loop(r_lo, jnp.maximum(n_rounds, r_lo), body, 0)

    def gather_row(blk, slot, r):
        tok = row_tok_ref[blk * bm + r]
        pltpu.make_async_copy(h2_hbm.at[pl.ds(tok, 1)], xbuf.at[slot, pl.ds(r, 1)], gsem.at[slot]).start()

    def scatter_row(tab_blk, slot, r):
        dst = row_dst_ref[tab_blk * bm + r]
        pltpu.make_async_copy(ybuf.at[slot, pl.ds(r, 1)], out_hbm.at[pl.ds(dst, 1)], ssem.at[slot]).start()

    def looped(fn, *args):
        def body(r, carry):
            fn(*args, r)
            return carry
        lax.fori_loop(0, bm, body, 0, unroll=8)

    def wait_gather(slot):
        pltpu.make_async_copy(h2_hbm.at[pl.ds(0, bm)], xbuf.at[slot], gsem.at[slot]).wait()

    def wait_scatter(slot):
        pltpu.make_async_copy(ybuf.at[slot], out_hbm.at[pl.ds(0, bm)], ssem.at[slot]).wait()

    @pl.when(i == 0)
    def _():
        ybuf[...] = jnp.zeros_like(ybuf)
        looped(gather_row, 0, 0)
        looped(scatter_row, 0, 0)
        sync_rounds(0, 0, 3 * n_mat, blk_e_ref[0], set_ref[0])

    def compute_block(slot):
        other = 1 - slot
        nxt = jnp.minimum(i + 1, n_used - 1)
        s, e_next, c_lo, c_hi = set_ref[i], nxt_ref[i], c0_ref[i], c1_ref[i]
        stream = (c_lo, c_hi, e_next, 1 - s)
        wait_gather(slot)
        issue_round(0, *stream)
        for r in range(bm):
            gather_row(nxt, other, r)
            scatter_row(i + 1, other, r)
        xb = xbuf[slot].astype(BF16)
        gate_scr[...] = _dot(xb, wg_s[s])
        finish_round(0, *stream)
        issue_round(1, *stream)
        act_scr[...] = (_silu(gate_scr[...]) * _dot(xb, wu_s[s])).astype(BF16)
        finish_round(1, *stream)
        wait_scatter(slot)
        issue_round(2, *stream)
        ybuf[slot] = _dot(act_scr[...], wd_s[s])
        finish_round(2, *stream)
        sync_rounds(3, *stream)

    for parity in range(2):
        pl.when((i < n_used) & (i % 2 == parity))(functools.partial(compute_block, parity))

    @pl.when(i == n_used)
    def _():
        looped(scatter_row, i + 1, (i - 1) % 2)
        wait_gather(i % 2)
        wait_scatter(0)
        wait_scatter(1)


def _moe(h2, meta, wg, wu, wd, layer):
    m, d = h2.shape
    bm = MOE_ROWS
    nb = meta[0].shape[0]
    f = wg.shape[3]
    rg, rd = d // MOE_MAT_CHUNKS, f // MOE_MAT_CHUNKS
    assert rg % (2 * SUBLANES) == 0 and rd % (2 * SUBLANES) == 0
    any_spec = pl.BlockSpec(memory_space=pl.ANY)
    dma = pltpu.SemaphoreType.DMA
    return pl.pallas_call(
        functools.partial(_moe_kernel, layer),
        grid_spec=pltpu.PrefetchScalarGridSpec(
            num_scalar_prefetch=len(meta),
            grid=(nb + 1,),
            in_specs=[any_spec] * 4,
            out_specs=any_spec,
            scratch_shapes=[
                pltpu.VMEM((2, bm, d), F32), pltpu.VMEM((2, bm, d), F32),
                pltpu.VMEM((2, d, f), BF16), pltpu.VMEM((2, d, f), BF16), pltpu.VMEM((2, f, d), BF16),
                pltpu.VMEM((MOE_STAGE_SLOTS, rg, f), F32), pltpu.VMEM((MOE_STAGE_SLOTS, rd, d), F32),
                pltpu.VMEM((bm, f), F32), pltpu.VMEM((bm, f), BF16),
                dma((2,)), dma((2,)), dma((MOE_STAGE_SLOTS,)), dma((MOE_STAGE_SLOTS,)),
            ],
        ),
        out_shape=jax.ShapeDtypeStruct((TOP_K * m + 2 * bm, d), F32),
        compiler_params=_cparams(1, MOE_VMEM_LIMIT),
        name="moe_experts",
    )(*meta, h2, wg, wu, wd)


def _combine_kernel(n_ctx, final, y0_ref, y1_ref, x1_ref, mod_ref, w_ref, g_ref, modn_ref, o_ref, o2_ref):
    d = x1_ref.shape[1]
    f = w_ref[:, 0:1] * y0_ref[...] + w_ref[:, 1:2] * y1_ref[...]
    x2 = x1_ref[...] + mod_ref[:, 5 * d:6 * d] * f
    if final:
        y = _rms(x2) * g_ref[...]
        is_ctx = pl.program_id(0) * x1_ref.shape[0] < n_ctx

        @pl.when(is_ctx)
        def _():
            o_ref[...] = y

        @pl.when(jnp.logical_not(is_ctx))
        def _():
            o2_ref[...] = y
    else:
        o_ref[...] = x2
        o2_ref[...] = _modulated_norm(x2, g_ref[...], modn_ref[...], d).astype(BF16)


def _combine(y2, x1, mod_l, w_col, gain, mod_next, final, cond_of_row):
    m, d = x1.shape
    r = ROW_TILE
    k1 = m // r
    n_ctx = cond_of_row.n_ctx
    nct = n_ctx // r
    cond_row = pl.BlockSpec((None, 1, 6 * d), lambda i: (cond_of_row(i * r), 0, 0))
    tile = pl.BlockSpec((r, d), lambda i: (i, 0))
    if final:
        out_specs = [pl.BlockSpec((r, d), lambda i: (jnp.minimum(i, nct - 1), 0)),
                     pl.BlockSpec((r, d), lambda i: (jnp.maximum(i - nct, 0), 0))]
        out_shape = [jax.ShapeDtypeStruct((n_ctx, d), F32), jax.ShapeDtypeStruct((m - n_ctx, d), F32)]
    else:
        out_specs = [tile, tile]
        out_shape = [jax.ShapeDtypeStruct((m, d), F32), jax.ShapeDtypeStruct((m, d), BF16)]
    return pl.pallas_call(
        functools.partial(_combine_kernel, n_ctx, final),
        grid=(m // r,),
        in_specs=[
            tile,
            pl.BlockSpec((r, d), lambda i: (k1 + i, 0)),
            tile, cond_row,
            pl.BlockSpec((r, TOP_K), lambda i: (i, 0)),
            pl.BlockSpec((1, d), lambda i: (0, 0)),
            cond_row,
        ],
        out_specs=out_specs,
        out_shape=out_shape,
        compiler_params=_cparams(1),
        name="moe_combine",
    )(y2, y2, x1, mod_l, w_col, gain.reshape(1, d), mod_next)


class _CondOfRow:
    def __init__(self, n_ctx, t_lat):
        self.n_ctx, self.t_lat = n_ctx, t_lat
        self.tile_in = _pick_tile((1024, 512, 256), n_ctx, t_lat)

    def __call__(self, row0):
        return jnp.where(row0 < self.n_ctx, 0, 1 + (row0 - self.n_ctx) // self.t_lat)


def kernel(x_prompt, x_sample, cache_k, cache_v, state_delta, c, c_ctx, w_mod, b_mod, norm1, norm2, w_in,
           lambda_qk, subln, conv_w, a_log, dt_bias, delta_norm, pool_w, pool_scale, w_pa, w_pb, w_pc,
           w_out, w_router, b_router, w_gate, w_up, w_down, norm_final):
    b_ctx, t_ctx, d = x_prompt.shape
    b_lat, t_lat, _ = x_sample.shape
    depth = w_in.shape[0]
    past = cache_k.shape[2]
    n_ctx, n_lat = b_ctx * t_ctx, b_lat * t_lat
    m = n_ctx + n_lat
    assert d == D_MODEL and w_in.shape[2] == D_IN and 1 + b_lat <= SUBLANES
    assert t_ctx % ROW_TILE == 0 and t_lat % ROW_TILE == 0 and n_ctx % t_lat == 0
    cond_of_row = _CondOfRow(n_ctx, t_lat)
    dims = (b_ctx, t_ctx, b_lat, t_lat)

    def seq(row0):
        is_ctx = row0 < n_ctx
        return (jnp.where(is_ctx, row0 % t_ctx, (row0 - n_ctx) % t_lat), jnp.where(is_ctx, t_ctx, t_lat))

    x = jnp.concatenate([x_prompt.reshape(n_ctx, d), x_sample.reshape(n_lat, d)], axis=0)
    cond8 = jnp.zeros((SUBLANES, d), F32).at[0].set(c_ctx).at[1:1 + b_lat].set(c)
    mod = _modulation(cond8, w_mod, b_mod).reshape(depth, SUBLANES, 1, 6 * d)
    rope_tabs = _rope_tables(t_lat)
    wr_t = w_router.T
    br = b_router.reshape(N_EXPERTS, 1)
    ck = cache_k.reshape(b_lat, depth, past, QK_A)
    cv = cache_v.reshape(b_lat, depth, past, W_A)
    s_lat = state_delta.reshape(b_lat, depth, 2 * N_HEADS_B, DK_B, DV_B)

    new_k, new_v, new_s = [], [], []
    h = _norm_mod(x, mod[0], norm1[0], cond_of_row)
    w_in_t = jnp.swapaxes(w_in, 1, 2)
    w_proj = [w.astype(BF16) for w in (w_pa, w_pb, w_pc, w_out)]
    for l in range(depth):
        lam_init = 0.8 - 0.6 * math.exp(-0.3 * l)
        last = l == depth - 1
        z = _in_proj(h, w_in_t, l, cond_of_row.tile_in)
        oa, k_l, v_l = _attention(z, lambda_qk[l], subln[l], ck[:, l], cv[:, l], rope_tabs, lam_init, dims)
        new_k.append(k_l.reshape(b_ctx, t_ctx, N_HEADS_A, 2 * DK_A))
        new_v.append(v_l.reshape(b_ctx, t_ctx, N_HEADS_A, DV_A))
        bat = z[:, OFF_BA:OFF_BA + 4 * N_HEADS_B].T
        oc, u, l1, l2, e = _local(z, bat, conv_w[l], pool_w[l], pool_scale[l], a_log[l], dt_bias[l], seq)
        *delta_o, s_ctx = _delta_scan(u, l1, l2, e, s_lat[:, l], dims)
        new_s.append(s_ctx.reshape(b_ctx, 2, N_HEADS_B, DK_B, DV_B))

        x1, h2, idx_t, wt_t = _mix(x, mod[l], oa, delta_o, z, oc, delta_norm[l], norm2[l], *w_proj, l,
                                   wr_t, br, cond_of_row)
        y2 = _moe(h2, _dispatch_meta(idx_t, MOE_ROWS), w_gate, w_up, w_down, l)
        if last:
            y_ctx, y_lat = _combine(y2, x1, mod[l], wt_t.T, norm_final, mod[l], True, cond_of_row)
        else:
            x, h = _combine(y2, x1, mod[l], wt_t.T, norm1[l + 1], mod[l + 1], False, cond_of_row)

    y_prompt = y_ctx.reshape(b_ctx, t_ctx, d)
    y_sample = y_lat.reshape(b_lat, t_lat, d)
    return (y_prompt, y_sample, jnp.stack(new_k, axis=1), jnp.stack(new_v, axis=1), jnp.stack(new_s, axis=1))
```

```python
import functools
import math

import jax
import jax.numpy as jnp
from jax import lax
from jax.experimental import pallas as pl
from jax.experimental.pallas import tpu as pltpu

F32 = jnp.float32
BF16 = jnp.bfloat16
HIGHEST = lax.Precision.HIGHEST

D_MODEL = 2048
GRID_W = 64
EPS = 1e-6
N_HEADS_A = 8
DK_A = 64
DV_A = 128
QK_A = N_HEADS_A * 2 * DK_A
W_A = N_HEADS_A * DV_A
ROPE_BASE = 10000.0
ROPE_AXIS = DK_A // 2
N_HEADS_B = 4
DK_B = 128
DV_B = 128
W_B = N_HEADS_B * DV_B
DELTA_CHUNK = 64
POOL_WINDOWS = (2, 4, 8, 16)
POOL_GROUP_W = 128
W_C = len(POOL_WINDOWS) * POOL_GROUP_W
N_BRANCH = 3
N_EXPERTS = 16
N_EXPERT_GROUPS = 4
EXPERTS_PER_GROUP = N_EXPERTS // N_EXPERT_GROUPS
TOP_K = 2
D_FF = 1408

SRC_QKV_B = 2 * QK_A + W_A
SRC_GATE_B = SRC_QKV_B + 3 * W_B
SRC_BA = SRC_GATE_B + W_B
SRC_POOL = SRC_BA + 4 * N_HEADS_B
SRC_BRANCH = SRC_POOL + W_C
D_IN = SRC_BRANCH + N_BRANCH * D_MODEL
LANES = 128
SUBLANES = 8
IN_TILE_N = 1024
IN_HALF_N = IN_TILE_N // 2
OFF_RAW = N_BRANCH * D_MODEL
RAW_W = -(-SRC_BRANCH // IN_TILE_N) * IN_TILE_N
NP_IN = OFF_RAW + RAW_W
OFF_Q = OFF_RAW
OFF_K = OFF_Q + QK_A
OFF_V = OFF_K + QK_A
OFF_QKV_B = OFF_RAW + SRC_QKV_B
OFF_GATE_B = OFF_RAW + SRC_GATE_B
OFF_BA = OFF_RAW + SRC_BA
BRANCH_SHIFT = SRC_BRANCH % LANES
assert SRC_BA % IN_TILE_N == 0 and (SRC_BRANCH - BRANCH_SHIFT) % IN_HALF_N == 0
assert SRC_POOL - SRC_BA == BRANCH_SHIFT and OFF_RAW % IN_TILE_N == 0

ROW_TILE = 256
MOE_ROWS = 256
INV_BASE_BLOCK = 16
MOE_MAT_CHUNKS = 8
MOE_STAGE_SLOTS = 2
MOE_VMEM_LIMIT = 60 * 1024 * 1024
VMEM_LIMIT = 56 * 1024 * 1024


def _cparams(n_axes, vmem=VMEM_LIMIT):
    return pltpu.CompilerParams(dimension_semantics=("arbitrary",) * n_axes, vmem_limit_bytes=vmem)


def _dot(a, b, precision=None):
    return jnp.dot(a, b, preferred_element_type=F32, precision=precision)


def _dot_nt(a, b, precision=None):
    return lax.dot_general(a, b, (((1,), (1,)), ((), ())), preferred_element_type=F32, precision=precision)


def _bdot(a, b):
    return _dot(a.astype(BF16), b.astype(BF16))


def _bdot_nt(a, b):
    return _dot_nt(a.astype(BF16), b.astype(BF16))


def _sigmoid(x):
    return 1.0 / (1.0 + jnp.exp(-x))


def _silu(x):
    return x * _sigmoid(x)


def _rms(x, eps=EPS):
    return x * lax.rsqrt(jnp.mean(x * x, axis=-1, keepdims=True) + eps)


def _pick_tile(cands, *dims):
    for t in cands:
        if all(d % t == 0 for d in dims):
            return t
    raise ValueError(f"no tile in {cands} divides {dims}")


def _mod_kernel(c_ref, w_ref, b_ref, o_ref):
    a = _silu(c_ref[...]).astype(BF16)
    o_ref[...] = _dot(a, w_ref[...].astype(BF16)) + b_ref[...]


def _modulation(cond8, w_mod, b_mod):
    depth, d, n6 = w_mod.shape
    tn = 1024
    return pl.pallas_call(
        _mod_kernel,
        grid=(depth, n6 // tn),
        in_specs=[
            pl.BlockSpec((SUBLANES, d), lambda l, n: (0, 0)),
            pl.BlockSpec((None, d, tn), lambda l, n: (l, 0, n)),
            pl.BlockSpec((None, 1, tn), lambda l, n: (l, 0, n)),
        ],
        out_specs=pl.BlockSpec((None, SUBLANES, tn), lambda l, n: (l, 0, n)),
        out_shape=jax.ShapeDtypeStruct((depth, SUBLANES, n6), F32),
        compiler_params=_cparams(2),
        name="adaln_mod",
    )(cond8, w_mod, b_mod.reshape(depth, 1, n6))


def _modulated_norm(x, g, mod, d):
    return _rms(x) * g * (1.0 + mod[:, d:2 * d]) + mod[:, 0:d]


def _norm_kernel(x_ref, mod_ref, g_ref, h_ref):
    h_ref[...] = _modulated_norm(x_ref[...], g_ref[...], mod_ref[...], x_ref.shape[1]).astype(BF16)


def _norm_mod(x, mod_l, norm_l, cond_of_row):
    m, d = x.shape
    r = ROW_TILE
    return pl.pallas_call(
        _norm_kernel,
        grid=(m // r,),
        in_specs=[
            pl.BlockSpec((r, d), lambda i: (i, 0)),
            pl.BlockSpec((None, 1, 6 * d), lambda i: (cond_of_row(i * r), 0, 0)),
            pl.BlockSpec((1, d), lambda i: (0, 0)),
        ],
        out_specs=pl.BlockSpec((r, d), lambda i: (i, 0)),
        out_shape=jax.ShapeDtypeStruct((m, d), BF16),
        compiler_params=_cparams(1),
        name="norm_mod",
    )(x, mod_l, norm_l.reshape(1, d))


def _in_kernel(h_ref, wt_ref, z_ref, w_scr):
    @pl.when(pl.program_id(1) == 0)
    def _():
        w_scr[...] = wt_ref[0].astype(BF16)

    z_ref[...] = _dot_nt(h_ref[...], w_scr[...])


def _in_proj(h, w_in_t, layer, tm):
    m, d = h.shape
    n_branch_tiles = OFF_RAW // IN_TILE_N
    assert SRC_BRANCH % SUBLANES == 0
    tile8 = IN_TILE_N // SUBLANES
    row0 = lambda n: SUBLANES * jnp.where(n < n_branch_tiles, SRC_BRANCH // SUBLANES + tile8 * n,
                                          tile8 * (n - n_branch_tiles))
    return pl.pallas_call(
        _in_kernel,
        grid=(NP_IN // IN_TILE_N, m // tm),
        in_specs=[
            pl.BlockSpec((tm, d), lambda n, i: (i, 0)),
            pl.BlockSpec((pl.Element(1), pl.Element(IN_TILE_N), pl.Element(d)),
                         lambda n, i: (layer, row0(n), 0)),
        ],
        out_specs=pl.BlockSpec((tm, IN_TILE_N), lambda n, i: (i, n)),
        out_shape=jax.ShapeDtypeStruct((m, NP_IN), F32),
        scratch_shapes=[pltpu.VMEM((IN_TILE_N, d), BF16)],
        compiler_params=_cparams(2),
        name="in_proj",
    )(h, w_in_t)


def _lambda(lq_ref, lam_init):
    lq = lq_ref[...]
    a = jnp.sum(lq[0:1] * lq[1:2], axis=-1, keepdims=True)
    b = jnp.sum(lq[2:3] * lq[3:4], axis=-1, keepdims=True)
    return jnp.exp(a) - jnp.exp(b) + lam_init


def _diff_attn(qb, kb, vb, lam):
    probs = []
    for mp in range(2):
        s = _dot_nt(qb[:, mp * DK_A:(mp + 1) * DK_A], kb[:, mp * DK_A:(mp + 1) * DK_A])
        e = jnp.exp(s - jnp.max(s, axis=-1, keepdims=True))
        probs.append(e / jnp.sum(e, axis=-1, keepdims=True))
    pd = probs[0] - lam * probs[1]
    return _dot(pd.astype(BF16), vb)


def _attn_ctx_kernel(lam_init, lq_ref, sub_ref, q_ref, k_ref, v_ref, o_ref, knew_ref, vnew_ref):
    lam = _lambda(lq_ref, lam_init)
    qb = (q_ref[...] * DK_A ** -0.5).astype(BF16)
    k, v = k_ref[...], v_ref[...]
    knew_ref[...] = k
    vnew_ref[...] = v
    o = _diff_attn(qb, k.astype(BF16), v.astype(BF16), lam)
    o_ref[...] = (_rms(o) * sub_ref[...] * (1.0 - lam_init)).astype(BF16)


def _rope(x, cos, sin_signed):
    lane = lax.broadcasted_iota(jnp.int32, x.shape, 1)
    first_half = (lane % (ROPE_AXIS)) < (ROPE_AXIS // 2)
    partner = jnp.where(first_half, pltpu.roll(x, LANES - ROPE_AXIS // 2, 1), pltpu.roll(x, ROPE_AXIS // 2, 1))
    return x * cos + partner * sin_signed


def _attn_lat_kernel(lam_init, t_lat, lq_ref, sub_ref, q_ref, k_ref, v_ref, ck_ref, cv_ref,
                     cq_ref, sq_ref, ckk_ref, skk_ref, o_ref, k_scr, v_scr):
    @pl.when(pl.program_id(2) == 0)
    def _():
        k_scr[0:t_lat, :] = _rope(k_ref[...], ckk_ref[...], skk_ref[...]).astype(BF16)
        k_scr[t_lat:, :] = ck_ref[...].astype(BF16)
        v_scr[0:t_lat, :] = v_ref[...].astype(BF16)
        v_scr[t_lat:, :] = cv_ref[...].astype(BF16)

    lam = _lambda(lq_ref, lam_init)
    q = _rope(q_ref[...], cq_ref[...], sq_ref[...]) * DK_A ** -0.5
    o = _diff_attn(q.astype(BF16), k_scr[...], v_scr[...], lam)
    o_ref[...] = (_rms(o) * sub_ref[...] * (1.0 - lam_init)).astype(BF16)


def _attention(z, lq_l, sub_l, cache_k_l, cache_v_l, rope_tabs, lam_init, dims):
    b_ctx, t_ctx, b_lat, t_lat = dims
    n_ctx = b_ctx * t_ctx
    past = cache_k_l.shape[1]
    cq, ck, cv = OFF_Q // LANES, OFF_K // LANES, OFF_V // LANES
    small = [pl.BlockSpec((4, DK_A), lambda *_: (0, 0)), pl.BlockSpec((1, DV_A), lambda *_: (0, 0))]
    sub2 = sub_l.reshape(1, DV_A)

    head_blk = pl.BlockSpec((t_ctx, LANES), lambda b, h: (b, h))
    oa_ctx, k_new, v_new = pl.pallas_call(
        functools.partial(_attn_ctx_kernel, lam_init),
        grid=(b_ctx, N_HEADS_A),
        in_specs=small + [
            pl.BlockSpec((t_ctx, LANES), lambda b, h: (b, cq + h)),
            pl.BlockSpec((t_ctx, LANES), lambda b, h: (b, ck + h)),
            pl.BlockSpec((t_ctx, LANES), lambda b, h: (b, cv + h)),
        ],
        out_specs=[head_blk] * 3,
        out_shape=[jax.ShapeDtypeStruct((n_ctx, W_A), BF16), jax.ShapeDtypeStruct((n_ctx, QK_A), F32),
                   jax.ShapeDtypeStruct((n_ctx, W_A), F32)],
        compiler_params=_cparams(2),
        name="attn_ctx",
    )(lq_l, sub2, z, z, z)

    tq = _pick_tile((256, 128), t_lat)
    nq = t_lat // tq
    rb_q = n_ctx // tq
    rb_k = n_ctx // t_lat
    cos_t, sin_t = rope_tabs
    oa_lat = pl.pallas_call(
        functools.partial(_attn_lat_kernel, lam_init, t_lat),
        grid=(b_lat, N_HEADS_A, nq),
        in_specs=small + [
            pl.BlockSpec((tq, LANES), lambda b, h, i: (rb_q + b * nq + i, cq + h)),
            pl.BlockSpec((t_lat, LANES), lambda b, h, i: (rb_k + b, ck + h)),
            pl.BlockSpec((t_lat, LANES), lambda b, h, i: (rb_k + b, cv + h)),
            pl.BlockSpec((None, past, LANES), lambda b, h, i: (b, 0, h)),
            pl.BlockSpec((None, past, LANES), lambda b, h, i: (b, 0, h)),
            pl.BlockSpec((tq, LANES), lambda b, h, i: (i, 0)),
            pl.BlockSpec((tq, LANES), lambda b, h, i: (i, 0)),
            pl.BlockSpec((t_lat, LANES), lambda b, h, i: (0, 0)),
            pl.BlockSpec((t_lat, LANES), lambda b, h, i: (0, 0)),
        ],
        out_specs=pl.BlockSpec((tq, LANES), lambda b, h, i: (b * nq + i, h)),
        out_shape=jax.ShapeDtypeStruct((b_lat * t_lat, W_A), BF16),
        scratch_shapes=[pltpu.VMEM((t_lat + past, LANES), BF16), pltpu.VMEM((t_lat + past, LANES), BF16)],
        compiler_params=_cparams(3),
        name="attn_lat",
    )(lq_l, sub2, z, z, z, cache_k_l, cache_v_l, cos_t, sin_t, cos_t, sin_t)
    return (oa_ctx, oa_lat), k_new, v_new


def _rope_tables(t_lat):
    rows = t_lat // GRID_W
    r = jnp.repeat(jnp.arange(rows), GRID_W).astype(F32)
    col = jnp.tile(jnp.arange(GRID_W), rows).astype(F32)
    inv = ROPE_BASE ** (-jnp.arange(0, ROPE_AXIS, 2, dtype=F32) / ROPE_AXIS)
    ang_r, ang_c = r[:, None] * inv, col[:, None] * inv
    cos64 = jnp.concatenate([jnp.cos(ang_r)] * 2 + [jnp.cos(ang_c)] * 2, axis=-1)
    sin64 = jnp.concatenate([-jnp.sin(ang_r), jnp.sin(ang_r), -jnp.sin(ang_c), jnp.sin(ang_c)], axis=-1)
    return jnp.tile(cos64, (1, 2)), jnp.tile(sin64, (1, 2))


def _log_decay(alpha, a_log, dt):
    x = alpha + dt
    return -jnp.exp(a_log) * (jnp.maximum(x, 0.0) + jnp.log(1.0 + jnp.exp(-jnp.abs(x))))


def _chunk_cumsum(g, axis):
    n = g.shape[axis]
    pos = lax.broadcasted_iota(jnp.int32, g.shape, axis) % DELTA_CHUNK
    f, b = g, g
    s = 1
    while s < DELTA_CHUNK:
        f = f + jnp.where(pos >= s, pltpu.roll(f, s, axis), 0.0)
        b = b + jnp.where(pos < DELTA_CHUNK - s, pltpu.roll(b, n - s, axis), 0.0)
        s *= 2
    return f, b


def _delta_prepare(q_scr, k_scr, v_scr, ba, bat, al_row, al_col, dt_row, dt_col,
                   u_ref, l1_ref, l2_ref, e_ref):
    c, nh = DELTA_CHUNK, N_HEADS_B
    rows_n = q_scr.shape[0]
    beta_all = _sigmoid(ba[:, 0:2 * nh])
    gf, gb = _chunk_cumsum(_log_decay(ba[:, 2 * nh:4 * nh], al_row[...], dt_row[...]), 0)
    gc_all = jnp.where(lax.broadcasted_iota(jnp.int32, gf.shape, 1) < nh, gf, gb)
    gtf, gtb = _chunk_cumsum(_log_decay(bat[2 * nh:4 * nh, :], al_col[...], dt_col[...]), 1)
    gct_all = jnp.where(lax.broadcasted_iota(jnp.int32, gtf.shape, 0) < nh, gtf, gtb)

    ri = lax.broadcasted_iota(jnp.int32, (c, c), 0)
    ci = lax.broadcasted_iota(jnp.int32, (c, c), 1)
    masks = ((ri >= ci, ri > ci), (ri <= ci, ri < ci))
    eye = (ri == ci).astype(F32)
    blk_sizes = [INV_BASE_BLOCK << i for i in range(int(math.log2(c // INV_BASE_BLOCK)) + 1)]
    same_blk = [(ri // s) == (ci // s) for s in blk_sizes]
    n_sq = int(math.log2(INV_BASE_BLOCK)) - 1

    for j in range(rows_n // c):
        rs = slice(j * c, (j + 1) * c)
        chains = []
        for d in range(2):
            incl, strict = masks[d]
            last = 0 if d else c - 1
            for h in range(nh):
                idx = d * nh + h
                hs = slice(h * DK_B, (h + 1) * DK_B)
                gc = gc_all[rs, idx:idx + 1]
                gr = gct_all[idx:idx + 1, rs]
                ch = dict(d=d, h=h, idx=idx, hs=hs, strict=strict, gc=gc, g_last=gc[last:last + 1, :],
                          q=q_scr[rs, hs] * DK_B ** -0.5, k=k_scr[rs, hs], v=v_scr[rs, hs],
                          beta=beta_all[rs, idx:idx + 1],
                          decay=jnp.where(incl, jnp.exp(jnp.where(incl, gc - gr, 0.0)), 0.0))
                ch["kb"] = ch["k"] * ch["beta"]
                chains.append(ch)
        for ch in chains:
            kbf = ch["k"].astype(BF16)
            a = jnp.where(ch["strict"], -(_dot_nt(ch["kb"].astype(BF16), kbf) * ch["decay"]), 0.0)
            ch["attn"] = _dot_nt(ch["q"].astype(BF16), kbf) * ch["decay"]
            ch["a"] = a
            ch["apow"] = jnp.where(same_blk[0], a, 0.0)
            ch["p"] = eye + ch["apow"]
        for _ in range(n_sq):
            for ch in chains:
                ch["apow"] = _bdot(ch["apow"], ch["apow"])
            for ch in chains:
                ch["p"] = ch["p"] + _bdot(ch["p"], ch["apow"])
        for lvl in range(1, len(same_blk)):
            for ch in chains:
                a_off = jnp.where(same_blk[lvl] & jnp.logical_not(same_blk[lvl - 1]), ch["a"], 0.0)
                ch["t"] = _bdot(ch["p"], a_off)
            for ch in chains:
                ch["p"] = ch["p"] + _bdot(ch["t"], ch["p"])
        for ch in chains:
            rhs = jnp.concatenate([ch["v"] * ch["beta"], ch["kb"] * jnp.exp(ch["gc"])], axis=-1)
            ch["sol"] = _bdot(ch["p"], rhs)
        for ch in chains:
            d, h, hs = ch["d"], ch["h"], ch["hs"]
            u_ref[d, rs, hs] = ch["sol"][:, :DV_B]
            l1_ref[d, j, h, 0:c, :] = ch["sol"][:, DV_B:].astype(BF16)
            l1_ref[d, j, h, c:2 * c, :] = (ch["q"] * jnp.exp(ch["gc"])).astype(BF16)
            l2_ref[d, j, h, 0:c, :] = ch["attn"].astype(BF16)
            l2_ref[d, j, h, c:, :] = (ch["k"] * jnp.exp(ch["g_last"] - ch["gc"])).T.astype(BF16)
            e_ref[j, ch["idx"]:ch["idx"] + 1, :] = jnp.broadcast_to(jnp.exp(ch["g_last"]), (1, LANES))


def _local_kernel(seq, cur_ref, prev_ref, next_ref, pcur_ref, pprev_ref, pnext_ref, bat_ref,
                  cw_ref, pw_ref, ps_ref, al_row, al_col, dt_row, dt_col,
                  oc_ref, u_ref, l1_ref, l2_ref, e_ref, q_scr, k_scr, v_scr):
    rows_n = cur_ref.shape[0]
    pos0, t_seq = seq(pl.program_id(0) * rows_n)
    has_prev = pos0 > 0
    has_next = pos0 + rows_n < t_seq
    rows = lax.broadcasted_iota(jnp.int32, (rows_n, 1), 0)

    x = cur_ref[...]
    prev_row = jnp.where(has_prev, prev_ref[SUBLANES - 1:SUBLANES, :], 0.0)
    next_row = jnp.where(has_next, next_ref[0:1, :], 0.0)
    xm1 = jnp.where(rows == 0, prev_row, pltpu.roll(x, 1, 0))
    xp1 = jnp.where(rows == rows_n - 1, next_row, pltpu.roll(x, rows_n - 1, 0))
    y = _silu(cw_ref[0:1, :] * xm1 + cw_ref[1:2, :] * x + cw_ref[2:3, :] * xp1)
    for h in range(N_HEADS_B):
        for j, ref in enumerate((q_scr, k_scr)):
            a = y[:, j * W_B + h * DK_B:j * W_B + (h + 1) * DK_B]
            ref[:, h * DK_B:(h + 1) * DK_B] = a * lax.rsqrt(jnp.sum(a * a, axis=-1, keepdims=True) + EPS)
    v_scr[...] = y[:, 2 * W_B:]
    _delta_prepare(q_scr, k_scr, v_scr, pcur_ref[:, 0:LANES], bat_ref[...], al_row, al_col, dt_row, dt_col,
                   u_ref, l1_ref, l2_ref, e_ref)

    def pool_cols(ref):
        t = ref[...]
        return pltpu.roll(t, t.shape[1] - BRANCH_SHIFT, 1)[:, :W_C]

    pz = pool_cols(pcur_ref)
    ext = jnp.concatenate([jnp.where(has_prev, pool_cols(pprev_ref), 0.0), pz,
                           jnp.where(has_next, pool_cols(pnext_ref), 0.0)], axis=0)
    n_ext = rows_n + 2 * SUBLANES
    tpos = pos0 + rows
    for gi, win in enumerate(POOL_WINDOWS):
        sl = slice(gi * POOL_GROUP_W, (gi + 1) * POOL_GROUP_W)
        xg = ext[:, sl]
        acc = xg + pltpu.roll(xg, 1, 0)
        s = 1
        while 2 * s < win:
            acc = pltpu.roll(acc, s, 0) + pltpu.roll(acc, n_ext - s, 0)
            s *= 2
        acc = acc[SUBLANES:SUBLANES + rows_n]
        cnt = (jnp.minimum(tpos + win // 2, t_seq) - jnp.maximum(tpos - win // 2, 0)).astype(F32)
        d = acc / cnt - pz[:, sl]
        yg = _dot(d.astype(BF16), pw_ref[gi].astype(BF16)) * ps_ref[:, sl]
        oc_ref[:, sl] = yg.astype(BF16)


def _local(z, bat, conv_w_l, pool_w_l, pool_scale_l, a_log_l, dt_l, seq):
    m = z.shape[0]
    r, c = ROW_TILE, DELTA_CHUNK
    hb = r // SUBLANES
    last_hb = m // SUBLANES - 1
    c_qkv, c_pool = OFF_QKV_B // (3 * W_B), OFF_BA // IN_TILE_N
    nh2 = 2 * N_HEADS_B
    mc, cpt = m // c, r // c
    prev_map = lambda cb: (lambda i: (jnp.maximum(i * hb - 1, 0), cb))
    next_map = lambda cb: (lambda i: (jnp.minimum((i + 1) * hb, last_hb), cb))
    row = pl.BlockSpec((1, nh2), lambda i: (0, 0))
    col = pl.BlockSpec((nh2, 1), lambda i: (0, 0))
    return pl.pallas_call(
        functools.partial(_local_kernel, seq),
        grid=(m // r,),
        in_specs=[
            pl.BlockSpec((r, 3 * W_B), lambda i: (i, c_qkv)),
            pl.BlockSpec((SUBLANES, 3 * W_B), prev_map(c_qkv)),
            pl.BlockSpec((SUBLANES, 3 * W_B), next_map(c_qkv)),
            pl.BlockSpec((r, IN_TILE_N), lambda i: (i, c_pool)),
            pl.BlockSpec((SUBLANES, IN_TILE_N), prev_map(c_pool)),
            pl.BlockSpec((SUBLANES, IN_TILE_N), next_map(c_pool)),
            pl.BlockSpec((2 * nh2, r), lambda i: (0, i)),
            pl.BlockSpec((3, 3 * W_B), lambda i: (0, 0)),
            pl.BlockSpec((len(POOL_WINDOWS), POOL_GROUP_W, POOL_GROUP_W), lambda i: (0, 0, 0)),
            pl.BlockSpec((1, W_C), lambda i: (0, 0)),
            row, col, row, col,
        ],
        out_specs=[
            pl.BlockSpec((r, W_C), lambda i: (i, 0)),
            pl.BlockSpec((2, r, W_B), lambda i: (0, i, 0)),
            pl.BlockSpec((2, cpt, N_HEADS_B, 2 * c, DV_B), lambda i: (0, i, 0, 0, 0)),
            pl.BlockSpec((2, cpt, N_HEADS_B, c + DK_B, c), lambda i: (0, i, 0, 0, 0)),
            pl.BlockSpec((cpt, nh2, LANES), lambda i: (i, 0, 0)),
        ],
        out_shape=[
            jax.ShapeDtypeStruct((m, W_C), BF16),
            jax.ShapeDtypeStruct((2, m, W_B), F32),
            jax.ShapeDtypeStruct((2, mc, N_HEADS_B, 2 * c, DV_B), BF16),
            jax.ShapeDtypeStruct((2, mc, N_HEADS_B, c + DK_B, c), BF16),
            jax.ShapeDtypeStruct((mc, nh2, LANES), F32),
        ],
        scratch_shapes=[pltpu.VMEM((r, W_B), F32)] * 3,
        compiler_params=_cparams(1),
        name="local_conv_pool",
    )(z, z, z, z, z, z, bat, conv_w_l, pool_w_l, pool_scale_l.reshape(1, W_C),
      a_log_l.reshape(1, nh2), a_log_l.reshape(nh2, 1), dt_l.reshape(1, nh2), dt_l.reshape(nh2, 1))


def _scan_kernel(par, b_lat, n_ctx_ch, *refs):
    c, nh = DELTA_CHUNK, N_HEADS_B
    n_slot = par + b_lat
    ins = refs[:n_slot * 8]
    s0_ref = refs[n_slot * 8]
    of_ctx, ob_ctx, of_lat, ob_lat, sfin_ref, s_scr = refs[n_slot * 8 + 1:]
    step = pl.program_id(0)
    cc = step % n_ctx_ch

    @pl.when(cc == 0)
    def _():
        s_scr[0:par] = jnp.zeros((par,) + s_scr.shape[1:], F32)

    @pl.when(step == 0)
    def _():
        s_scr[par:] = s0_ref[...]

    for slot in range(n_slot):
        outs = (of_ctx, ob_ctx, slot) if slot < par else (of_lat, ob_lat, slot - par)
        chains = []
        for d in range(2):
            l1_ref, l2_ref, u_ref, e_ref = ins[(slot * 2 + d) * 4:(slot * 2 + d) * 4 + 4]
            for h in range(nh):
                chains.append(dict(d=d, h=h, idx=d * nh + h, l1=l1_ref, l2=l2_ref, u=u_ref, e=e_ref))
        for ch in chains:
            ch["s"] = s_scr[slot, ch["idx"]]
            ch["r1"] = _dot(ch["l1"][ch["h"]], ch["s"].astype(BF16))
        for ch in chains:
            hs = slice(ch["h"] * DV_B, (ch["h"] + 1) * DV_B)
            v_new = ch["u"][:, hs] - ch["r1"][:c]
            ch["r2"] = _dot(ch["l2"][ch["h"]], v_new.astype(BF16))
        for ch in chains:
            hs = slice(ch["h"] * DV_B, (ch["h"] + 1) * DV_B)
            outs[ch["d"]][outs[2], :, hs] = ch["r1"][c:] + ch["r2"][:c]
            e = ch["e"][ch["idx"]:ch["idx"] + 1, :]
            s_scr[slot, ch["idx"]] = ch["s"] * e + ch["r2"][c:]

    @pl.when(cc == n_ctx_ch - 1)
    def _():
        sfin_ref[...] = s_scr[0:par]


def _delta_scan(u, l1, l2, e, s0_lat, dims):
    b_ctx, t_ctx, b_lat, t_lat = dims
    c, nh = DELTA_CHUNK, N_HEADS_B
    n_ctx_ch, n_lat_ch = t_ctx // c, t_lat // c
    par = b_ctx * n_ctx_ch // n_lat_ch
    assert par >= 1 and par * n_lat_ch == b_ctx * n_ctx_ch and b_ctx % par == 0
    mc = l1.shape[1]
    ctx_chunks = b_ctx * n_ctx_ch
    u4 = u.reshape(2, mc, c, W_B)

    def chunk_of(slot, d):
        if slot < par:
            return lambda s: ((s // n_ctx_ch) * par + slot) * n_ctx_ch + (
                (n_ctx_ch - 1 - s % n_ctx_ch) if d else s % n_ctx_ch)
        q = slot - par
        return lambda s: ctx_chunks + q * n_lat_ch + ((n_lat_ch - 1 - s) if d else s)

    in_specs, args = [], []
    for slot in range(par + b_lat):
        for d in range(2):
            cg = chunk_of(slot, d)
            in_specs += [
                pl.BlockSpec((None, None, nh, 2 * c, DV_B), lambda s, cg=cg, d=d: (d, cg(s), 0, 0, 0)),
                pl.BlockSpec((None, None, nh, c + DK_B, c), lambda s, cg=cg, d=d: (d, cg(s), 0, 0, 0)),
                pl.BlockSpec((None, None, c, W_B), lambda s, cg=cg, d=d: (d, cg(s), 0, 0)),
                pl.BlockSpec((None, 2 * nh, LANES), lambda s, cg=cg: (cg(s), 0, 0)),
            ]
            args += [l1, l2, u4, e]
    in_specs.append(pl.BlockSpec((b_lat, 2 * nh, DK_B, DV_B), lambda s: (0, 0, 0, 0)))
    args.append(s0_lat)

    ctx_o = jax.ShapeDtypeStruct((b_ctx // par, par, n_ctx_ch, c, W_B), F32)
    lat_o = jax.ShapeDtypeStruct((b_lat, n_lat_ch, c, W_B), F32)
    cf = lambda s: s % n_ctx_ch
    of_ctx, ob_ctx, of_lat, ob_lat, s_fin = pl.pallas_call(
        functools.partial(_scan_kernel, par, b_lat, n_ctx_ch),
        grid=(n_lat_ch,),
        in_specs=in_specs,
        out_specs=[
            pl.BlockSpec((None, par, None, c, W_B), lambda s: (s // n_ctx_ch, 0, cf(s), 0, 0)),
            pl.BlockSpec((None, par, None, c, W_B), lambda s: (s // n_ctx_ch, 0, n_ctx_ch - 1 - cf(s), 0, 0)),
            pl.BlockSpec((b_lat, None, c, W_B), lambda s: (0, s, 0, 0)),
            pl.BlockSpec((b_lat, None, c, W_B), lambda s: (0, n_lat_ch - 1 - s, 0, 0)),
            pl.BlockSpec((par, 2 * nh, DK_B, DV_B), lambda s: (s // n_ctx_ch, 0, 0, 0)),
        ],
        out_shape=[ctx_o, ctx_o, lat_o, lat_o, jax.ShapeDtypeStruct((b_ctx, 2 * nh, DK_B, DV_B), F32)],
        scratch_shapes=[pltpu.VMEM((par + b_lat, 2 * nh, DK_B, DV_B), F32)],
        compiler_params=_cparams(1),
        name="delta_scan",
    )(*args)
    n_ctx, n_lat = b_ctx * t_ctx, b_lat * t_lat
    return (of_ctx.reshape(n_ctx, W_B), ob_ctx.reshape(n_ctx, W_B), of_lat.reshape(n_lat, W_B),
            ob_lat.reshape(n_lat, W_B), s_fin)


def _route_rows(sel, scores):
    g = EXPERTS_PER_GROUP
    gscore = []
    for gi in range(N_EXPERT_GROUPS):
        a, b, c, d = sel[gi * g:(gi + 1) * g]
        hi1, lo1 = jnp.maximum(a, b), jnp.minimum(a, b)
        hi2, lo2 = jnp.maximum(c, d), jnp.minimum(c, d)
        gscore.append(jnp.maximum(hi1, hi2) + jnp.maximum(jnp.minimum(hi1, hi2), jnp.maximum(lo1, lo2)))
    best = jnp.zeros_like(gscore[0], dtype=jnp.int32)
    bestv = gscore[0]
    for gi in range(1, N_EXPERT_GROUPS):
        upd = gscore[gi] > bestv
        best = jnp.where(upd, gi, best)
        bestv = jnp.where(upd, gscore[gi], bestv)

    def in_best(rows, j):
        out = rows[j]
        for gi in range(1, N_EXPERT_GROUPS):
            out = jnp.where(best == gi, rows[gi * g + j], out)
        return out

    e_sel = [in_best(sel, j) for j in range(g)]
    e_sc = [in_best(scores, j) for j in range(g)]

    def first_argmax(vals):
        bi, bv, bs = jnp.zeros_like(best), vals[0], e_sc[0]
        for j in range(1, g):
            upd = vals[j] > bv
            bi = jnp.where(upd, j, bi)
            bv = jnp.where(upd, vals[j], bv)
            bs = jnp.where(upd, e_sc[j], bs)
        return bi, bs

    i0, s0 = first_argmax(e_sel)
    i1, s1 = first_argmax([jnp.where(i0 == j, -jnp.inf, e_sel[j]) for j in range(g)])
    tot = s0 + s1
    return best * g + i0, best * g + i1, s0 / tot, s1 / tot


def _mix_kernel(n_ctx, x_ref, mod_ref, oac_ref, oal_ref, ofc_ref, obc_ref, ofl_ref, obl_ref,
                gb_ref, oc_ref, g0_ref, g1_ref, g2_ref,
                dn_ref, n2_ref, wpa_ref, wpb_ref, wpc_ref, wout_ref, wr_ref, br_ref,
                x1_ref, h2_ref, idx_ref, wt_ref):
    d = x_ref.shape[1]
    is_ctx = pl.program_id(0) * x_ref.shape[0] < n_ctx
    dn = dn_ref[...]
    o = jnp.where(is_ctx, ofc_ref[...] + obc_ref[...], ofl_ref[...] + obl_ref[...])
    o_a = jnp.where(is_ctx, oac_ref[...], oal_ref[...])
    gate = _silu(gb_ref[...])
    parts = []
    for h in range(N_HEADS_B):
        sl = slice(h * DV_B, (h + 1) * DV_B)
        parts.append((_rms(o[:, sl]) * dn * gate[:, sl]).astype(BF16))
    o_b = jnp.concatenate(parts, axis=-1)
    mixed = (_sigmoid(g0_ref[...]) * _dot(o_a, wpa_ref[...])
             + _sigmoid(g1_ref[...]) * _dot(o_b, wpb_ref[...])
             + _sigmoid(g2_ref[...]) * _dot(oc_ref[...], wpc_ref[...]))
    gate1 = mod_ref[:, 2 * d:3 * d]
    x1 = x_ref[...] + gate1 * _dot(mixed.astype(BF16), wout_ref[...])
    x1_ref[...] = x1
    shift2 = mod_ref[:, 3 * d:4 * d]
    scale2 = mod_ref[:, 4 * d:5 * d]
    h2 = _rms(x1) * n2_ref[...] * (1.0 + scale2) + shift2
    h2_ref[...] = h2
    sc = _sigmoid(_dot_nt(wr_ref[...], h2, HIGHEST))
    sel = sc + br_ref[...]
    i0, i1, w0, w1 = _route_rows([sel[e:e + 1] for e in range(N_EXPERTS)],
                                 [sc[e:e + 1] for e in range(N_EXPERTS)])
    idx_ref[0:1, :] = i0
    idx_ref[1:2, :] = i1
    wt_ref[0:1, :] = w0
    wt_ref[1:2, :] = w1


def _mix(x, mod_l, oa, delta_o, z, oc, delta_norm_l, norm2_l, wpa, wpb, wpc, wout, layer, wr_t, br,
         cond_of_row):
    m, d = x.shape
    r = ROW_TILE
    n_ctx = cond_of_row.n_ctx
    nct = n_ctx // r
    c_gb = OFF_GATE_B // W_B
    once = dict(pipeline_mode=pl.Buffered(1))
    full = lambda shape: pl.BlockSpec(shape, lambda i: (0,) * len(shape), **once)
    of_layer = lambda shape: pl.BlockSpec((None,) + shape, lambda i: (layer,) + (0,) * len(shape), **once)
    ctx_rows = lambda w: pl.BlockSpec((r, w), lambda i: (jnp.minimum(i, nct - 1), 0))
    lat_rows = lambda w: pl.BlockSpec((r, w), lambda i: (jnp.maximum(i - nct, 0), 0))
    of_ctx, ob_ctx, of_lat, ob_lat = delta_o
    return pl.pallas_call(
        functools.partial(_mix_kernel, n_ctx),
        grid=(m // r,),
        in_specs=[
            pl.BlockSpec((r, d), lambda i: (i, 0)),
            pl.BlockSpec((None, 1, 6 * d), lambda i: (cond_of_row(i * r), 0, 0)),
            ctx_rows(W_A), lat_rows(W_A),
            ctx_rows(W_B), ctx_rows(W_B), lat_rows(W_B), lat_rows(W_B),
            pl.BlockSpec((r, W_B), lambda i: (i, c_gb)),
            pl.BlockSpec((r, W_C), lambda i: (i, 0)),
            pl.BlockSpec((r, d), lambda i: (i, 0)),
            pl.BlockSpec((r, d), lambda i: (i, 1)),
            pl.BlockSpec((r, d), lambda i: (i, 2)),
            full((1, DV_B)), full((1, d)),
            of_layer((W_A, d)), of_layer((W_B, d)), of_layer((W_C, d)), of_layer((d, d)),
            full((N_EXPERTS, d)), full((N_EXPERTS, 1)),
        ],
        out_specs=[
            pl.BlockSpec((r, d), lambda i: (i, 0)),
            pl.BlockSpec((r, d), lambda i: (i, 0)),
            pl.BlockSpec((TOP_K, r), lambda i: (0, i)),
            pl.BlockSpec((TOP_K, r), lambda i: (0, i)),
        ],
        out_shape=[
            jax.ShapeDtypeStruct((m, d), F32),
            jax.ShapeDtypeStruct((m, d), F32),
            jax.ShapeDtypeStruct((TOP_K, m), jnp.int32),
            jax.ShapeDtypeStruct((TOP_K, m), F32),
        ],
        compiler_params=_cparams(1),
        name="mix_route",
    )(x, mod_l, oa[0], oa[1], of_ctx, ob_ctx, of_lat, ob_lat, z, oc, z, z, z,
      delta_norm_l.reshape(1, DV_B), norm2_l.reshape(1, d), wpa, wpb, wpc, wout, wr_t, br)


def _dispatch_meta(idx_t, bm):
    m = idx_t.shape[1]
    nk = m * TOP_K
    flat_e = idx_t.T.reshape(nk)
    experts = jnp.arange(N_EXPERTS, dtype=jnp.int32)[None, :]
    counts = jnp.sum((flat_e[:, None] == experts).astype(jnp.int32), axis=0)
    order = jnp.argsort(flat_e, stable=True).astype(jnp.int32)
    start = jnp.cumsum(counts) - counts
    padded = (counts + bm - 1) // bm * bm
    pad_end = jnp.cumsum(padded)
    pad_start = pad_end - padded
    nb = -(-(nk + N_EXPERTS * (bm - 1)) // bm)
    p = jnp.arange(nb * bm, dtype=jnp.int32)
    e_of_p = jnp.minimum(jnp.sum((pad_end[None, :] <= p[:, None]).astype(jnp.int32), axis=1), N_EXPERTS - 1)
    pick = (e_of_p[:, None] == experts).astype(jnp.int32)
    rank = p - jnp.sum(pick * pad_start[None, :], axis=1)
    real = rank < jnp.sum(pick * counts[None, :], axis=1)
    assign = order[jnp.clip(jnp.sum(pick * start[None, :], axis=1) + rank, 0, nk - 1)]
    row_tok = jnp.where(real, assign // TOP_K, 0)
    spare = nk + ((p // bm) % 2) * bm + p % bm
    row_dst = jnp.where(real, (assign % TOP_K) * m + assign // TOP_K, spare)
    prime = nk + jnp.arange(2 * bm, dtype=jnp.int32)
    n_used = (pad_end[-1] // bm).astype(jnp.int32)
    blk_e = e_of_p[::bm]
    n_chunks = 3 * MOE_MAT_CHUNKS
    e_ids = experts[0]
    n_blk = padded // bm
    has = n_blk > 0
    later = lax.cummin(jnp.where(has, e_ids, N_EXPERTS)[::-1])[::-1]
    nxt_of_e = jnp.concatenate([later[1:], jnp.full((1,), N_EXPERTS, jnp.int32)])
    set_of_e = (jnp.cumsum(has.astype(jnp.int32)) - 1) % 2
    b = jnp.arange(nb, dtype=jnp.int32)
    k = b - (pad_start // bm)[blk_e]
    n_e = jnp.maximum(n_blk[blk_e], 1)
    streams = (nxt_of_e[blk_e] < N_EXPERTS) & (b < n_used)
    c0 = jnp.where(streams, n_chunks * k // n_e, 0)
    c1 = jnp.where(streams, n_chunks * (k + 1) // n_e, 0)
    nxt_e = jnp.where(streams, nxt_of_e[blk_e], 0)
    return (blk_e, n_used.reshape(1), row_tok, jnp.concatenate([prime, row_dst]),
            set_of_e[blk_e], nxt_e, c0, c1)


def _moe_kernel(layer, blk_e_ref, n_used_ref, row_tok_ref, row_dst_ref, set_ref, nxt_ref, c0_ref, c1_ref,
                h2_hbm, wg_hbm, wu_hbm, wd_hbm, out_hbm,
                xbuf, ybuf, wg_s, wu_s, wd_s, st_gu, st_d, gate_scr, act_scr, gsem, ssem, wsem_gu, wsem_d):
    i = pl.program_id(0)
    bm = xbuf.shape[1]
    n_used = n_used_ref[0]
    n_mat = MOE_MAT_CHUNKS
    n_stage = st_gu.shape[0]
    rg, rd = st_gu.shape[1], st_d.shape[1]

    def chunk_ops(c, j, e, s):
        cg, cu, cd = c, c - n_mat, c - 2 * n_mat
        rows_g = pl.ds(pl.multiple_of(cg * rg, rg), rg)
        rows_u = pl.ds(pl.multiple_of(cu * rg, rg), rg)
        rows_d = pl.ds(pl.multiple_of(cd * rd, 2 * SUBLANES), rd)

        def to_set(dst, rows, stage):
            def convert():
                dst[s, rows, :] = stage[j].astype(BF16)
            return convert

        return (
            (c < n_mat, lambda: pltpu.make_async_copy(wg_hbm.at[layer, e, rows_g, :], st_gu.at[j], wsem_gu.at[j]),
             to_set(wg_s, rows_g, st_gu)),
            ((c >= n_mat) & (c < 2 * n_mat),
             lambda: pltpu.make_async_copy(wu_hbm.at[layer, e, rows_u, :], st_gu.at[j], wsem_gu.at[j]),
             to_set(wu_s, rows_u, st_gu)),
            (c >= 2 * n_mat, lambda: pltpu.make_async_copy(wd_hbm.at[layer, e, rows_d, :], st_d.at[j], wsem_d.at[j]),
             to_set(wd_s, rows_d, st_d)),
        )

    def round_chunks(r, c_lo, c_hi):
        first = c_lo + r * n_stage
        return first, jnp.clip(c_hi - first, 0, n_stage)

    def issue_round(r, c_lo, c_hi, e, s):
        first, n = round_chunks(r, c_lo, c_hi)

        def body(j, carry):
            for pred, copy, _ in chunk_ops(first + j, j, e, s):
                pl.when(pred)(lambda copy=copy: copy().start(priority=1))
            return carry
        lax.fori_loop(0, n, body, 0)

    def finish_round(r, c_lo, c_hi, e, s):
        first, n = round_chunks(r, c_lo, c_hi)

        def body(j, carry):
            for pred, copy, convert in chunk_ops(first + j, j, e, s):
                @pl.when(pred)
                def _(copy=copy, convert=convert):
                    copy().wait()
                    convert()
            return carry
        lax.fori_loop(0, n, body, 0)

    def sync_rounds(r_lo, c_lo, c_hi, e, s):
        def body(r, carry):
            issue_round(r, c_lo, c_hi, e, s)
            finish_round(r, c_lo, c_hi, e, s)
            return carry
        n_rounds = (jnp.maximum(c_hi - c_lo, 0) + n_stage - 1) // n_stage
        lax.fori_loop(r_lo, jnp.maximum(n_rounds, r_lo), body, 0)

    def gather_row(blk, slot, r):
        tok = row_tok_ref[blk * bm + r]
        pltpu.make_async_copy(h2_hbm.at[pl.ds(tok, 1)], xbuf.at[slot, pl.ds(r, 1)], gsem.at[slot]).start()

    def scatter_row(tab_blk, slot, r):
        dst = row_dst_ref[tab_blk * bm + r]
        pltpu.make_async_copy(ybuf.at[slot, pl.ds(r, 1)], out_hbm.at[pl.ds(dst, 1)], ssem.at[slot]).start()

    def looped(fn, *args):
        def body(r, carry):
            fn(*args, r)
            return carry
        lax.fori_loop(0, bm, body, 0, unroll=8)

    def wait_gather(slot):
        pltpu.make_async_copy(h2_hbm.at[pl.ds(0, bm)], xbuf.at[slot], gsem.at[slot]).wait()

    def wait_scatter(slot):
        pltpu.make_async_copy(ybuf.at[slot], out_hbm.at[pl.ds(0, bm)], ssem.at[slot]).wait()

    @pl.when(i == 0)
    def _():
        ybuf[...] = jnp.zeros_like(ybuf)
        looped(gather_row, 0, 0)
        looped(scatter_row, 0, 0)
        sync_rounds(0, 0, 3 * n_mat, blk_e_ref[0], set_ref[0])

    def compute_block(slot):
        other = 1 - slot
        nxt = jnp.minimum(i + 1, n_used - 1)
        s, e_next, c_lo, c_hi = set_ref[i], nxt_ref[i], c0_ref[i], c1_ref[i]
        stream = (c_lo, c_hi, e_next, 1 - s)
        wait_gather(slot)
        issue_round(0, *stream)
        for r in range(bm):
            gather_row(nxt, other, r)
            scatter_row(i + 1, other, r)
        xb = xbuf[slot].astype(BF16)
        gate_scr[...] = _dot(xb, wg_s[s])
        finish_round(0, *stream)
        issue_round(1, *stream)
        act_scr[...] = (_silu(gate_scr[...]) * _dot(xb, wu_s[s])).astype(BF16)
        finish_round(1, *stream)
        wait_scatter(slot)
        issue_round(2, *stream)
        ybuf[slot] = _dot(act_scr[...], wd_s[s])
        finish_round(2, *stream)
        sync_rounds(3, *stream)

    for parity in range(2):
        pl.when((i < n_used) & (i % 2 == parity))(functools.partial(compute_block, parity))

    @pl.when(i == n_used)
    def _():
        looped(scatter_row, i + 1, (i - 1) % 2)
        wait_gather(i % 2)
        wait_scatter(0)
        wait_scatter(1)


def _moe(h2, meta, wg, wu, wd, layer):
    m, d = h2.shape
    bm = MOE_ROWS
    nb = meta[0].shape[0]
    f = wg.shape[3]
    rg, rd = d // MOE_MAT_CHUNKS, f // MOE_MAT_CHUNKS
    assert rg % (2 * SUBLANES) == 0 and rd % (2 * SUBLANES) == 0
    any_spec = pl.BlockSpec(memory_space=pl.ANY)
    dma = pltpu.SemaphoreType.DMA
    return pl.pallas_call(
        functools.partial(_moe_kernel, layer),
        grid_spec=pltpu.PrefetchScalarGridSpec(
            num_scalar_prefetch=len(meta),
            grid=(nb + 1,),
            in_specs=[any_spec] * 4,
            out_specs=any_spec,
            scratch_shapes=[
                pltpu.VMEM((2, bm, d), F32), pltpu.VMEM((2, bm, d), F32),
                pltpu.VMEM((2, d, f), BF16), pltpu.VMEM((2, d, f), BF16), pltpu.VMEM((2, f, d), BF16),
                pltpu.VMEM((MOE_STAGE_SLOTS, rg, f), F32), pltpu.VMEM((MOE_STAGE_SLOTS, rd, d), F32),
                pltpu.VMEM((bm, f), F32), pltpu.VMEM((bm, f), BF16),
                dma((2,)), dma((2,)), dma((MOE_STAGE_SLOTS,)), dma((MOE_STAGE_SLOTS,)),
            ],
        ),
        out_shape=jax.ShapeDtypeStruct((TOP_K * m + 2 * bm, d), F32),
        compiler_params=_cparams(1, MOE_VMEM_LIMIT),
        name="moe_experts",
    )(*meta, h2, wg, wu, wd)


def _combine_kernel(n_ctx, final, y0_ref, y1_ref, x1_ref, mod_ref, w_ref, g_ref, modn_ref, o_ref, o2_ref):
    d = x1_ref.shape[1]
    f = w_ref[:, 0:1] * y0_ref[...] + w_ref[:, 1:2] * y1_ref[...]
    x2 = x1_ref[...] + mod_ref[:, 5 * d:6 * d] * f
    if final:
        y = _rms(x2) * g_ref[...]
        is_ctx = pl.program_id(0) * x1_ref.shape[0] < n_ctx

        @pl.when(is_ctx)
        def _():
            o_ref[...] = y

        @pl.when(jnp.logical_not(is_ctx))
        def _():
            o2_ref[...] = y
    else:
        o_ref[...] = x2
        o2_ref[...] = _modulated_norm(x2, g_ref[...], modn_ref[...], d).astype(BF16)


def _combine(y2, x1, mod_l, w_col, gain, mod_next, final, cond_of_row):
    m, d = x1.shape
    r = ROW_TILE
    k1 = m // r
    n_ctx = cond_of_row.n_ctx
    nct = n_ctx // r
    cond_row = pl.BlockSpec((None, 1, 6 * d), lambda i: (cond_of_row(i * r), 0, 0))
    tile = pl.BlockSpec((r, d), lambda i: (i, 0))
    if final:
        out_specs = [pl.BlockSpec((r, d), lambda i: (jnp.minimum(i, nct - 1), 0)),
                     pl.BlockSpec((r, d), lambda i: (jnp.maximum(i - nct, 0), 0))]
        out_shape = [jax.ShapeDtypeStruct((n_ctx, d), F32), jax.ShapeDtypeStruct((m - n_ctx, d), F32)]
    else:
        out_specs = [tile, tile]
        out_shape = [jax.ShapeDtypeStruct((m, d), F32), jax.ShapeDtypeStruct((m, d), BF16)]
    return pl.pallas_call(
        functools.partial(_combine_kernel, n_ctx, final),
        grid=(m // r,),
        in_specs=[
            tile,
            pl.BlockSpec((r, d), lambda i: (k1 + i, 0)),
            tile, cond_row,
            pl.BlockSpec((r, TOP_K), lambda i: (i, 0)),
            pl.BlockSpec((1, d), lambda i: (0, 0)),
            cond_row,
        ],
        out_specs=out_specs,
        out_shape=out_shape,
        compiler_params=_cparams(1),
        name="moe_combine",
    )(y2, y2, x1, mod_l, w_col, gain.reshape(1, d), mod_next)


class _CondOfRow:
    def __init__(self, n_ctx, t_lat):
        self.n_ctx, self.t_lat = n_ctx, t_lat
        self.tile_in = _pick_tile((1024, 512, 256), n_ctx, t_lat)

    def __call__(self, row0):
        return jnp.where(row0 < self.n_ctx, 0, 1 + (row0 - self.n_ctx) // self.t_lat)


def kernel(x_prompt, x_sample, cache_k, cache_v, state_delta, c, c_ctx, w_mod, b_mod, norm1, norm2, w_in,
           lambda_qk, subln, conv_w, a_log, dt_bias, delta_norm, pool_w, pool_scale, w_pa, w_pb, w_pc,
           w_out, w_router, b_router, w_gate, w_up, w_down, norm_final):
    b_ctx, t_ctx, d = x_prompt.shape
    b_lat, t_lat, _ = x_sample.shape
    depth = w_in.shape[0]
    past = cache_k.shape[2]
    n_ctx, n_lat = b_ctx * t_ctx, b_lat * t_lat
    m = n_ctx + n_lat
    assert d == D_MODEL and w_in.shape[2] == D_IN and 1 + b_lat <= SUBLANES
    assert t_ctx % ROW_TILE == 0 and t_lat % ROW_TILE == 0 and n_ctx % t_lat == 0
    cond_of_row = _CondOfRow(n_ctx, t_lat)
    dims = (b_ctx, t_ctx, b_lat, t_lat)

    def seq(row0):
        is_ctx = row0 < n_ctx
        return (jnp.where(is_ctx, row0 % t_ctx, (row0 - n_ctx) % t_lat), jnp.where(is_ctx, t_ctx, t_lat))

    x = jnp.concatenate([x_prompt.reshape(n_ctx, d), x_sample.reshape(n_lat, d)], axis=0)
    cond8 = jnp.zeros((SUBLANES, d), F32).at[0].set(c_ctx).at[1:1 + b_lat].set(c)
    mod = _modulation(cond8, w_mod, b_mod).reshape(depth, SUBLANES, 1, 6 * d)
    rope_tabs = _rope_tables(t_lat)
    wr_t = w_router.T
    br = b_router.reshape(N_EXPERTS, 1)
    ck = cache_k.reshape(b_lat, depth, past, QK_A)
    cv = cache_v.reshape(b_lat, depth, past, W_A)
    s_lat = state_delta.reshape(b_lat, depth, 2 * N_HEADS_B, DK_B, DV_B)

    new_k, new_v, new_s = [], [], []
    h = _norm_mod(x, mod[0], norm1[0], cond_of_row)
    w_in_t = jnp.swapaxes(w_in, 1, 2)
    w_proj = [w.astype(BF16) for w in (w_pa, w_pb, w_pc, w_out)]
    for l in range(depth):
        lam_init = 0.8 - 0.6 * math.exp(-0.3 * l)
        last = l == depth - 1
        z = _in_proj(h, w_in_t, l, cond_of_row.tile_in)
        oa, k_l, v_l = _attention(z, lambda_qk[l], subln[l], ck[:, l], cv[:, l], rope_tabs, lam_init, dims)
        new_k.append(k_l.reshape(b_ctx, t_ctx, N_HEADS_A, 2 * DK_A))
        new_v.append(v_l.reshape(b_ctx, t_ctx, N_HEADS_A, DV_A))
        bat = z[:, OFF_BA:OFF_BA + 4 * N_HEADS_B].T
        oc, u, l1, l2, e = _local(z, bat, conv_w[l], pool_w[l], pool_scale[l], a_log[l], dt_bias[l], seq)
        *delta_o, s_ctx = _delta_scan(u, l1, l2, e, s_lat[:, l], dims)
        new_s.append(s_ctx.reshape(b_ctx, 2, N_HEADS_B, DK_B, DV_B))

        x1, h2, idx_t, wt_t = _mix(x, mod[l], oa, delta_o, z, oc, delta_norm[l], norm2[l], *w_proj, l,
                                   wr_t, br, cond_of_row)
        y2 = _moe(h2, _dispatch_meta(idx_t, MOE_ROWS), w_gate, w_up, w_down, l)
        if last:
            y_ctx, y_lat = _combine(y2, x1, mod[l], wt_t.T, norm_final, mod[l], True, cond_of_row)
        else:
            x, h = _combine(y2, x1, mod[l], wt_t.T, norm1[l + 1], mod[l + 1], False, cond_of_row)

    y_prompt = y_ctx.reshape(b_ctx, t_ctx, d)
    y_sample = y_lat.reshape(b_lat, t_lat, d)
    return (y_prompt, y_sample, jnp.stack(new_k, axis=1), jnp.stack(new_v, axis=1), jnp.stack(new_s, axis=1))
```

```python
import functools
import math

import jax
import jax.numpy as jnp
from jax import lax
from jax.experimental import pallas as pl
from jax.experimental.pallas import tpu as pltpu

F32 = jnp.float32
BF16 = jnp.bfloat16
HIGHEST = lax.Precision.HIGHEST

D_MODEL = 2048
GRID_W = 64
EPS = 1e-6
N_HEADS_A = 8
DK_A = 64
DV_A = 128
QK_A = N_HEADS_A * 2 * DK_A
W_A = N_HEADS_A * DV_A
ROPE_BASE = 10000.0
ROPE_AXIS = DK_A // 2
N_HEADS_B = 4
DK_B = 128
DV_B = 128
W_B = N_HEADS_B * DV_B
DELTA_CHUNK = 64
POOL_WINDOWS = (2, 4, 8, 16)
POOL_GROUP_W = 128
W_C = len(POOL_WINDOWS) * POOL_GROUP_W
N_BRANCH = 3
N_EXPERTS = 16
N_EXPERT_GROUPS = 4
EXPERTS_PER_GROUP = N_EXPERTS // N_EXPERT_GROUPS
TOP_K = 2
D_FF = 1408

SRC_QKV_B = 2 * QK_A + W_A
SRC_GATE_B = SRC_QKV_B + 3 * W_B
SRC_BA = SRC_GATE_B + W_B
SRC_POOL = SRC_BA + 4 * N_HEADS_B
SRC_BRANCH = SRC_POOL + W_C
D_IN = SRC_BRANCH + N_BRANCH * D_MODEL
LANES = 128
SUBLANES = 8
IN_TILE_N = 1024
IN_HALF_N = IN_TILE_N // 2
OFF_RAW = N_BRANCH * D_MODEL
RAW_W = -(-SRC_BRANCH // IN_TILE_N) * IN_TILE_N
NP_IN = OFF_RAW + RAW_W
OFF_Q = OFF_RAW
OFF_K = OFF_Q + QK_A
OFF_V = OFF_K + QK_A
OFF_QKV_B = OFF_RAW + SRC_QKV_B
OFF_GATE_B = OFF_RAW + SRC_GATE_B
OFF_BA = OFF_RAW + SRC_BA
BRANCH_SHIFT = SRC_BRANCH % LANES
assert SRC_BA % IN_TILE_N == 0 and (SRC_BRANCH - BRANCH_SHIFT) % IN_HALF_N == 0
assert SRC_POOL - SRC_BA == BRANCH_SHIFT and OFF_RAW % IN_TILE_N == 0

ROW_TILE = 256
MOE_ROWS = 256
INV_BASE_BLOCK = 16
MOE_MAT_CHUNKS = 8
MOE_STAGE_SLOTS = 2
MOE_VMEM_LIMIT = 60 * 1024 * 1024
VMEM_LIMIT = 56 * 1024 * 1024


def _cparams(n_axes, vmem=VMEM_LIMIT):
    return pltpu.CompilerParams(dimension_semantics=("arbitrary",) * n_axes, vmem_limit_bytes=vmem)


def _dot(a, b, precision=None):
    return jnp.dot(a, b, preferred_element_type=F32, precision=precision)


def _dot_nt(a, b, precision=None):
    return lax.dot_general(a, b, (((1,), (1,)), ((), ())), preferred_element_type=F32, precision=precision)


def _bdot(a, b):
    return _dot(a.astype(BF16), b.astype(BF16))


def _bdot_nt(a, b):
    return _dot_nt(a.astype(BF16), b.astype(BF16))


def _sigmoid(x):
    return 1.0 / (1.0 + jnp.exp(-x))


def _silu(x):
    return x * _sigmoid(x)


def _rms(x, eps=EPS):
    return x * lax.rsqrt(jnp.mean(x * x, axis=-1, keepdims=True) + eps)


def _pick_tile(cands, *dims):
    for t in cands:
        if all(d % t == 0 for d in dims):
            return t
    raise ValueError(f"no tile in {cands} divides {dims}")


def _mod_kernel(c_ref, w_ref, b_ref, o_ref):
    a = _silu(c_ref[...]).astype(BF16)
    o_ref[...] = _dot(a, w_ref[...].astype(BF16)) + b_ref[...]


def _modulation(cond8, w_mod, b_mod):
    depth, d, n6 = w_mod.shape
    tn = 1024
    return pl.pallas_call(
        _mod_kernel,
        grid=(depth, n6 // tn),
        in_specs=[
            pl.BlockSpec((SUBLANES, d), lambda l, n: (0, 0)),
            pl.BlockSpec((None, d, tn), lambda l, n: (l, 0, n)),
            pl.BlockSpec((None, 1, tn), lambda l, n: (l, 0, n)),
        ],
        out_specs=pl.BlockSpec((None, SUBLANES, tn), lambda l, n: (l, 0, n)),
        out_shape=jax.ShapeDtypeStruct((depth, SUBLANES, n6), F32),
        compiler_params=_cparams(2),
        name="adaln_mod",
    )(cond8, w_mod, b_mod.reshape(depth, 1, n6))


def _modulated_norm(x, g, mod, d):
    return _rms(x) * g * (1.0 + mod[:, d:2 * d]) + mod[:, 0:d]


def _norm_kernel(x_ref, mod_ref, g_ref, h_ref):
    h_ref[...] = _modulated_norm(x_ref[...], g_ref[...], mod_ref[...], x_ref.shape[1]).astype(BF16)


def _norm_mod(x, mod_l, norm_l, cond_of_row):
    m, d = x.shape
    r = ROW_TILE
    return pl.pallas_call(
        _norm_kernel,
        grid=(m // r,),
        in_specs=[
            pl.BlockSpec((r, d), lambda i: (i, 0)),
            pl.BlockSpec((None, 1, 6 * d), lambda i: (cond_of_row(i * r), 0, 0)),
            pl.BlockSpec((1, d), lambda i: (0, 0)),
        ],
        out_specs=pl.BlockSpec((r, d), lambda i: (i, 0)),
        out_shape=jax.ShapeDtypeStruct((m, d), BF16),
        compiler_params=_cparams(1),
        name="norm_mod",
    )(x, mod_l, norm_l.reshape(1, d))


def _in_kernel(h_ref, wt_ref, z_ref, w_scr):
    @pl.when(pl.program_id(1) == 0)
    def _():
        w_scr[...] = wt_ref[0].astype(BF16)

    z_ref[...] = _dot_nt(h_ref[...], w_scr[...])


def _in_proj(h, w_in_t, layer, tm):
    m, d = h.shape
    n_branch_tiles = OFF_RAW // IN_TILE_N
    assert SRC_BRANCH % SUBLANES == 0
    tile8 = IN_TILE_N // SUBLANES
    row0 = lambda n: SUBLANES * jnp.where(n < n_branch_tiles, SRC_BRANCH // SUBLANES + tile8 * n,
                                          tile8 * (n - n_branch_tiles))
    return pl.pallas_call(
        _in_kernel,
        grid=(NP_IN // IN_TILE_N, m // tm),
        in_specs=[
            pl.BlockSpec((tm, d), lambda n, i: (i, 0)),
            pl.BlockSpec((pl.Element(1), pl.Element(IN_TILE_N), pl.Element(d)),
                         lambda n, i: (layer, row0(n), 0)),
        ],
        out_specs=pl.BlockSpec((tm, IN_TILE_N), lambda n, i: (i, n)),
        out_shape=jax.ShapeDtypeStruct((m, NP_IN), F32),
        scratch_shapes=[pltpu.VMEM((IN_TILE_N, d), BF16)],
        compiler_params=_cparams(2),
        name="in_proj",
    )(h, w_in_t)


def _lambda(lq_ref, lam_init):
    lq = lq_ref[...]
    a = jnp.sum(lq[0:1] * lq[1:2], axis=-1, keepdims=True)
    b = jnp.sum(lq[2:3] * lq[3:4], axis=-1, keepdims=True)
    return jnp.exp(a) - jnp.exp(b) + lam_init


def _diff_attn(qb, kb, vb, lam):
    probs = []
    for mp in range(2):
        s = _dot_nt(qb[:, mp * DK_A:(mp + 1) * DK_A], kb[:, mp * DK_A:(mp + 1) * DK_A])
        e = jnp.exp(s - jnp.max(s, axis=-1, keepdims=True))
        probs.append(e / jnp.sum(e, axis=-1, keepdims=True))
    pd = probs[0] - lam * probs[1]
    return _dot(pd.astype(BF16), vb)


def _attn_ctx_kernel(lam_init, lq_ref, sub_ref, q_ref, k_ref, v_ref, o_ref, knew_ref, vnew_ref):
    lam = _lambda(lq_ref, lam_init)
    for h in range(N_HEADS_A):
        hs = slice(h * LANES, (h + 1) * LANES)
        qb = (q_ref[:, hs] * DK_A ** -0.5).astype(BF16)
        k, v = k_ref[:, hs], v_ref[:, hs]
        knew_ref[:, hs] = k
        vnew_ref[:, hs] = v
        o = _diff_attn(qb, k.astype(BF16), v.astype(BF16), lam)
        o_ref[:, hs] = (_rms(o) * sub_ref[...] * (1.0 - lam_init)).astype(BF16)


def _rope(x, cos, sin_signed):
    lane = lax.broadcasted_iota(jnp.int32, x.shape, 1)
    first_half = (lane % (ROPE_AXIS)) < (ROPE_AXIS // 2)
    partner = jnp.where(first_half, pltpu.roll(x, LANES - ROPE_AXIS // 2, 1), pltpu.roll(x, ROPE_AXIS // 2, 1))
    return x * cos + partner * sin_signed


def _attn_lat_kernel(lam_init, t_lat, lq_ref, sub_ref, q_ref, k_ref, v_ref, ck_ref, cv_ref,
                     cq_ref, sq_ref, ckk_ref, skk_ref, o_ref, k_scr, v_scr):
    @pl.when(pl.program_id(2) == 0)
    def _():
        k_scr[0:t_lat, :] = _rope(k_ref[...], ckk_ref[...], skk_ref[...]).astype(BF16)
        k_scr[t_lat:, :] = ck_ref[...].astype(BF16)
        v_scr[0:t_lat, :] = v_ref[...].astype(BF16)
        v_scr[t_lat:, :] = cv_ref[...].astype(BF16)

    lam = _lambda(lq_ref, lam_init)
    q = _rope(q_ref[...], cq_ref[...], sq_ref[...]) * DK_A ** -0.5
    o = _diff_attn(q.astype(BF16), k_scr[...], v_scr[...], lam)
    o_ref[...] = (_rms(o) * sub_ref[...] * (1.0 - lam_init)).astype(BF16)


def _attention(z, lq_l, sub_l, cache_k_l, cache_v_l, rope_tabs, lam_init, dims):
    b_ctx, t_ctx, b_lat, t_lat = dims
    n_ctx = b_ctx * t_ctx
    past = cache_k_l.shape[1]
    cq, ck, cv = OFF_Q // LANES, OFF_K // LANES, OFF_V // LANES
    small = [pl.BlockSpec((4, DK_A), lambda *_: (0, 0)), pl.BlockSpec((1, DV_A), lambda *_: (0, 0))]
    sub2 = sub_l.reshape(1, DV_A)

    seq_blk = pl.BlockSpec((t_ctx, W_A), lambda b: (b, 0))
    oa_ctx, k_new, v_new = pl.pallas_call(
        functools.partial(_attn_ctx_kernel, lam_init),
        grid=(b_ctx,),
        in_specs=small + [
            pl.BlockSpec((t_ctx, QK_A), lambda b: (b, OFF_Q // QK_A)),
            pl.BlockSpec((t_ctx, QK_A), lambda b: (b, OFF_K // QK_A)),
            pl.BlockSpec((t_ctx, W_A), lambda b: (b, OFF_V // W_A)),
        ],
        out_specs=[seq_blk] * 3,
        out_shape=[jax.ShapeDtypeStruct((n_ctx, W_A), BF16), jax.ShapeDtypeStruct((n_ctx, QK_A), F32),
                   jax.ShapeDtypeStruct((n_ctx, W_A), F32)],
        compiler_params=_cparams(1),
        name="attn_ctx",
    )(lq_l, sub2, z, z, z)

    tq = _pick_tile((256, 128), t_lat)
    nq = t_lat // tq
    rb_q = n_ctx // tq
    rb_k = n_ctx // t_lat
    cos_t, sin_t = rope_tabs
    oa_lat = pl.pallas_call(
        functools.partial(_attn_lat_kernel, lam_init, t_lat),
        grid=(b_lat, N_HEADS_A, nq),
        in_specs=small + [
            pl.BlockSpec((tq, LANES), lambda b, h, i: (rb_q + b * nq + i, cq + h)),
            pl.BlockSpec((t_lat, LANES), lambda b, h, i: (rb_k + b, ck + h)),
            pl.BlockSpec((t_lat, LANES), lambda b, h, i: (rb_k + b, cv + h)),
            pl.BlockSpec((None, past, LANES), lambda b, h, i: (b, 0, h)),
            pl.BlockSpec((None, past, LANES), lambda b, h, i: (b, 0, h)),
            pl.BlockSpec((tq, LANES), lambda b, h, i: (i, 0)),
            pl.BlockSpec((tq, LANES), lambda b, h, i: (i, 0)),
            pl.BlockSpec((t_lat, LANES), lambda b, h, i: (0, 0)),
            pl.BlockSpec((t_lat, LANES), lambda b, h, i: (0, 0)),
        ],
        out_specs=pl.BlockSpec((tq, LANES), lambda b, h, i: (b * nq + i, h)),
        out_shape=jax.ShapeDtypeStruct((b_lat * t_lat, W_A), BF16),
        scratch_shapes=[pltpu.VMEM((t_lat + past, LANES), BF16), pltpu.VMEM((t_lat + past, LANES), BF16)],
        compiler_params=_cparams(3),
        name="attn_lat",
    )(lq_l, sub2, z, z, z, cache_k_l, cache_v_l, cos_t, sin_t, cos_t, sin_t)
    return (oa_ctx, oa_lat), k_new, v_new


def _rope_tables(t_lat):
    rows = t_lat // GRID_W
    r = jnp.repeat(jnp.arange(rows), GRID_W).astype(F32)
    col = jnp.tile(jnp.arange(GRID_W), rows).astype(F32)
    inv = ROPE_BASE ** (-jnp.arange(0, ROPE_AXIS, 2, dtype=F32) / ROPE_AXIS)
    ang_r, ang_c = r[:, None] * inv, col[:, None] * inv
    cos64 = jnp.concatenate([jnp.cos(ang_r)] * 2 + [jnp.cos(ang_c)] * 2, axis=-1)
    sin64 = jnp.concatenate([-jnp.sin(ang_r), jnp.sin(ang_r), -jnp.sin(ang_c), jnp.sin(ang_c)], axis=-1)
    return jnp.tile(cos64, (1, 2)), jnp.tile(sin64, (1, 2))


def _log_decay(alpha, a_log, dt):
    x = alpha + dt
    return -jnp.exp(a_log) * (jnp.maximum(x, 0.0) + jnp.log(1.0 + jnp.exp(-jnp.abs(x))))


def _chunk_cumsum(g, axis):
    n = g.shape[axis]
    pos = lax.broadcasted_iota(jnp.int32, g.shape, axis) % DELTA_CHUNK
    f, b = g, g
    s = 1
    while s < DELTA_CHUNK:
        f = f + jnp.where(pos >= s, pltpu.roll(f, s, axis), 0.0)
        b = b + jnp.where(pos < DELTA_CHUNK - s, pltpu.roll(b, n - s, axis), 0.0)
        s *= 2
    return f, b


def _delta_prepare(q_scr, k_scr, v_scr, ba, bat, al_row, al_col, dt_row, dt_col,
                   u_ref, l1_ref, l2_ref, e_ref):
    c, nh = DELTA_CHUNK, N_HEADS_B
    rows_n = q_scr.shape[0]
    beta_all = _sigmoid(ba[:, 0:2 * nh])
    gf, gb = _chunk_cumsum(_log_decay(ba[:, 2 * nh:4 * nh], al_row[...], dt_row[...]), 0)
    gc_all = jnp.where(lax.broadcasted_iota(jnp.int32, gf.shape, 1) < nh, gf, gb)
    gtf, gtb = _chunk_cumsum(_log_decay(bat[2 * nh:4 * nh, :], al_col[...], dt_col[...]), 1)
    gct_all = jnp.where(lax.broadcasted_iota(jnp.int32, gtf.shape, 0) < nh, gtf, gtb)

    ri = lax.broadcasted_iota(jnp.int32, (c, c), 0)
    ci = lax.broadcasted_iota(jnp.int32, (c, c), 1)
    masks = ((ri >= ci, ri > ci), (ri <= ci, ri < ci))
    eye = (ri == ci).astype(F32)
    blk_sizes = [INV_BASE_BLOCK << i for i in range(int(math.log2(c // INV_BASE_BLOCK)) + 1)]
    same_blk = [(ri // s) == (ci // s) for s in blk_sizes]
    n_sq = int(math.log2(INV_BASE_BLOCK)) - 1

    for j in range(rows_n // c):
        rs = slice(j * c, (j + 1) * c)
        chains = []
        for d in range(2):
            incl, strict = masks[d]
            last = 0 if d else c - 1
            for h in range(nh):
                idx = d * nh + h
                hs = slice(h * DK_B, (h + 1) * DK_B)
                gc = gc_all[rs, idx:idx + 1]
                gr = gct_all[idx:idx + 1, rs]
                ch = dict(d=d, h=h, idx=idx, hs=hs, strict=strict, gc=gc, g_last=gc[last:last + 1, :],
                          q=q_scr[rs, hs] * DK_B ** -0.5, k=k_scr[rs, hs], v=v_scr[rs, hs],
                          beta=beta_all[rs, idx:idx + 1],
                          decay=jnp.where(incl, jnp.exp(jnp.where(incl, gc - gr, 0.0)), 0.0))
                ch["kb"] = ch["k"] * ch["beta"]
                chains.append(ch)
        for ch in chains:
            kbf = ch["k"].astype(BF16)
            a = jnp.where(ch["strict"], -(_dot_nt(ch["kb"].astype(BF16), kbf) * ch["decay"]), 0.0)
            ch["attn"] = _dot_nt(ch["q"].astype(BF16), kbf) * ch["decay"]
            ch["a"] = a
            ch["apow"] = jnp.where(same_blk[0], a, 0.0)
            ch["p"] = eye + ch["apow"]
        for _ in range(n_sq):
            for ch in chains:
                ch["apow"] = _bdot(ch["apow"], ch["apow"])
            for ch in chains:
                ch["p"] = ch["p"] + _bdot(ch["p"], ch["apow"])
        for lvl in range(1, len(same_blk)):
            for ch in chains:
                a_off = jnp.where(same_blk[lvl] & jnp.logical_not(same_blk[lvl - 1]), ch["a"], 0.0)
                ch["t"] = _bdot(ch["p"], a_off)
            for ch in chains:
                ch["p"] = ch["p"] + _bdot(ch["t"], ch["p"])
        for ch in chains:
            rhs = jnp.concatenate([ch["v"] * ch["beta"], ch["kb"] * jnp.exp(ch["gc"])], axis=-1)
            ch["sol"] = _bdot(ch["p"], rhs)
        for ch in chains:
            d, h, hs = ch["d"], ch["h"], ch["hs"]
            u_ref[d, rs, hs] = ch["sol"][:, :DV_B]
            l1_ref[d, j, h, 0:c, :] = ch["sol"][:, DV_B:].astype(BF16)
            l1_ref[d, j, h, c:2 * c, :] = (ch["q"] * jnp.exp(ch["gc"])).astype(BF16)
            l2_ref[d, j, h, 0:c, :] = ch["attn"].astype(BF16)
            l2_ref[d, j, h, c:, :] = (ch["k"] * jnp.exp(ch["g_last"] - ch["gc"])).T.astype(BF16)
            e_ref[j, ch["idx"]:ch["idx"] + 1, :] = jnp.broadcast_to(jnp.exp(ch["g_last"]), (1, LANES))


def _local_kernel(seq, cur_ref, prev_ref, next_ref, pcur_ref, pprev_ref, pnext_ref, bat_ref,
                  cw_ref, pw_ref, ps_ref, al_row, al_col, dt_row, dt_col,
                  oc_ref, u_ref, l1_ref, l2_ref, e_ref, q_scr, k_scr, v_scr):
    rows_n = cur_ref.shape[0]
    pos0, t_seq = seq(pl.program_id(0) * rows_n)
    has_prev = pos0 > 0
    has_next = pos0 + rows_n < t_seq
    rows = lax.broadcasted_iota(jnp.int32, (rows_n, 1), 0)

    x = cur_ref[...]
    prev_row = jnp.where(has_prev, prev_ref[SUBLANES - 1:SUBLANES, :], 0.0)
    next_row = jnp.where(has_next, next_ref[0:1, :], 0.0)
    xm1 = jnp.where(rows == 0, prev_row, pltpu.roll(x, 1, 0))
    xp1 = jnp.where(rows == rows_n - 1, next_row, pltpu.roll(x, rows_n - 1, 0))
    y = _silu(cw_ref[0:1, :] * xm1 + cw_ref[1:2, :] * x + cw_ref[2:3, :] * xp1)
    for h in range(N_HEADS_B):
        for j, ref in enumerate((q_scr, k_scr)):
            a = y[:, j * W_B + h * DK_B:j * W_B + (h + 1) * DK_B]
            ref[:, h * DK_B:(h + 1) * DK_B] = a * lax.rsqrt(jnp.sum(a * a, axis=-1, keepdims=True) + EPS)
    v_scr[...] = y[:, 2 * W_B:]
    _delta_prepare(q_scr, k_scr, v_scr, pcur_ref[:, 0:LANES], bat_ref[...], al_row, al_col, dt_row, dt_col,
                   u_ref, l1_ref, l2_ref, e_ref)

    def pool_cols(ref):
        t = ref[...]
        return pltpu.roll(t, t.shape[1] - BRANCH_SHIFT, 1)[:, :W_C]

    pz = pool_cols(pcur_ref)
    ext = jnp.concatenate([jnp.where(has_prev, pool_cols(pprev_ref), 0.0), pz,
                           jnp.where(has_next, pool_cols(pnext_ref), 0.0)], axis=0)
    n_ext = rows_n + 2 * SUBLANES
    tpos = pos0 + rows
    for gi, win in enumerate(POOL_WINDOWS):
        sl = slice(gi * POOL_GROUP_W, (gi + 1) * POOL_GROUP_W)
        xg = ext[:, sl]
        acc = xg + pltpu.roll(xg, 1, 0)
        s = 1
        while 2 * s < win:
            acc = pltpu.roll(acc, s, 0) + pltpu.roll(acc, n_ext - s, 0)
            s *= 2
        acc = acc[SUBLANES:SUBLANES + rows_n]
        cnt = (jnp.minimum(tpos + win // 2, t_seq) - jnp.maximum(tpos - win // 2, 0)).astype(F32)
        d = acc / cnt - pz[:, sl]
        yg = _dot(d.astype(BF16), pw_ref[gi].astype(BF16)) * ps_ref[:, sl]
        oc_ref[:, sl] = yg.astype(BF16)


def _local(z, bat, conv_w_l, pool_w_l, pool_scale_l, a_log_l, dt_l, seq):
    m = z.shape[0]
    r, c = ROW_TILE, DELTA_CHUNK
    hb = r // SUBLANES
    last_hb = m // SUBLANES - 1
    c_qkv, c_pool = OFF_QKV_B // (3 * W_B), OFF_BA // IN_TILE_N
    nh2 = 2 * N_HEADS_B
    mc, cpt = m // c, r // c
    prev_map = lambda cb: (lambda i: (jnp.maximum(i * hb - 1, 0), cb))
    next_map = lambda cb: (lambda i: (jnp.minimum((i + 1) * hb, last_hb), cb))
    row = pl.BlockSpec((1, nh2), lambda i: (0, 0))
    col = pl.BlockSpec((nh2, 1), lambda i: (0, 0))
    return pl.pallas_call(
        functools.partial(_local_kernel, seq),
        grid=(m // r,),
        in_specs=[
            pl.BlockSpec((r, 3 * W_B), lambda i: (i, c_qkv)),
            pl.BlockSpec((SUBLANES, 3 * W_B), prev_map(c_qkv)),
            pl.BlockSpec((SUBLANES, 3 * W_B), next_map(c_qkv)),
            pl.BlockSpec((r, IN_TILE_N), lambda i: (i, c_pool)),
            pl.BlockSpec((SUBLANES, IN_TILE_N), prev_map(c_pool)),
            pl.BlockSpec((SUBLANES, IN_TILE_N), next_map(c_pool)),
            pl.BlockSpec((2 * nh2, r), lambda i: (0, i)),
            pl.BlockSpec((3, 3 * W_B), lambda i: (0, 0)),
            pl.BlockSpec((len(POOL_WINDOWS), POOL_GROUP_W, POOL_GROUP_W), lambda i: (0, 0, 0)),
            pl.BlockSpec((1, W_C), lambda i: (0, 0)),
            row, col, row, col,
        ],
        out_specs=[
            pl.BlockSpec((r, W_C), lambda i: (i, 0)),
            pl.BlockSpec((2, r, W_B), lambda i: (0, i, 0)),
            pl.BlockSpec((2, cpt, N_HEADS_B, 2 * c, DV_B), lambda i: (0, i, 0, 0, 0)),
            pl.BlockSpec((2, cpt, N_HEADS_B, c + DK_B, c), lambda i: (0, i, 0, 0, 0)),
            pl.BlockSpec((cpt, nh2, LANES), lambda i: (i, 0, 0)),
        ],
        out_shape=[
            jax.ShapeDtypeStruct((m, W_C), BF16),
            jax.ShapeDtypeStruct((2, m, W_B), F32),
            jax.ShapeDtypeStruct((2, mc, N_HEADS_B, 2 * c, DV_B), BF16),
            jax.ShapeDtypeStruct((2, mc, N_HEADS_B, c + DK_B, c), BF16),
            jax.ShapeDtypeStruct((mc, nh2, LANES), F32),
        ],
        scratch_shapes=[pltpu.VMEM((r, W_B), F32)] * 3,
        compiler_params=_cparams(1),
        name="local_conv_pool",
    )(z, z, z, z, z, z, bat, conv_w_l, pool_w_l, pool_scale_l.reshape(1, W_C),
      a_log_l.reshape(1, nh2), a_log_l.reshape(nh2, 1), dt_l.reshape(1, nh2), dt_l.reshape(nh2, 1))


def _scan_kernel(par, b_lat, n_ctx_ch, *refs):
    c, nh = DELTA_CHUNK, N_HEADS_B
    n_slot = par + b_lat
    ins = refs[:n_slot * 8]
    s0_ref = refs[n_slot * 8]
    of_ctx, ob_ctx, of_lat, ob_lat, sfin_ref, s_scr = refs[n_slot * 8 + 1:]
    step = pl.program_id(0)
    cc = step % n_ctx_ch

    @pl.when(cc == 0)
    def _():
        s_scr[0:par] = jnp.zeros((par,) + s_scr.shape[1:], F32)

    @pl.when(step == 0)
    def _():
        s_scr[par:] = s0_ref[...]

    for slot in range(n_slot):
        outs = (of_ctx, ob_ctx, slot) if slot < par else (of_lat, ob_lat, slot - par)
        chains = []
        for d in range(2):
            l1_ref, l2_ref, u_ref, e_ref = ins[(slot * 2 + d) * 4:(slot * 2 + d) * 4 + 4]
            for h in range(nh):
                chains.append(dict(d=d, h=h, idx=d * nh + h, l1=l1_ref, l2=l2_ref, u=u_ref, e=e_ref))
        for ch in chains:
            ch["s"] = s_scr[slot, ch["idx"]]
            ch["r1"] = _dot(ch["l1"][ch["h"]], ch["s"].astype(BF16))
        for ch in chains:
            hs = slice(ch["h"] * DV_B, (ch["h"] + 1) * DV_B)
            v_new = ch["u"][:, hs] - ch["r1"][:c]
            ch["r2"] = _dot(ch["l2"][ch["h"]], v_new.astype(BF16))
        for ch in chains:
            hs = slice(ch["h"] * DV_B, (ch["h"] + 1) * DV_B)
            outs[ch["d"]][outs[2], :, hs] = ch["r1"][c:] + ch["r2"][:c]
            e = ch["e"][ch["idx"]:ch["idx"] + 1, :]
            s_scr[slot, ch["idx"]] = ch["s"] * e + ch["r2"][c:]

    @pl.when(cc == n_ctx_ch - 1)
    def _():
        sfin_ref[...] = s_scr[0:par]


def _delta_scan(u, l1, l2, e, s0_lat, dims):
    b_ctx, t_ctx, b_lat, t_lat = dims
    c, nh = DELTA_CHUNK, N_HEADS_B
    n_ctx_ch, n_lat_ch = t_ctx // c, t_lat // c
    par = b_ctx * n_ctx_ch // n_lat_ch
    assert par >= 1 and par * n_lat_ch == b_ctx * n_ctx_ch and b_ctx % par == 0
    mc = l1.shape[1]
    ctx_chunks = b_ctx * n_ctx_ch
    u4 = u.reshape(2, mc, c, W_B)

    def chunk_of(slot, d):
        if slot < par:
            return lambda s: ((s // n_ctx_ch) * par + slot) * n_ctx_ch + (
                (n_ctx_ch - 1 - s % n_ctx_ch) if d else s % n_ctx_ch)
        q = slot - par
        return lambda s: ctx_chunks + q * n_lat_ch + ((n_lat_ch - 1 - s) if d else s)

    in_specs, args = [], []
    for slot in range(par + b_lat):
        for d in range(2):
            cg = chunk_of(slot, d)
            in_specs += [
                pl.BlockSpec((None, None, nh, 2 * c, DV_B), lambda s, cg=cg, d=d: (d, cg(s), 0, 0, 0)),
                pl.BlockSpec((None, None, nh, c + DK_B, c), lambda s, cg=cg, d=d: (d, cg(s), 0, 0, 0)),
                pl.BlockSpec((None, None, c, W_B), lambda s, cg=cg, d=d: (d, cg(s), 0, 0)),
                pl.BlockSpec((None, 2 * nh, LANES), lambda s, cg=cg: (cg(s), 0, 0)),
            ]
            args += [l1, l2, u4, e]
    in_specs.append(pl.BlockSpec((b_lat, 2 * nh, DK_B, DV_B), lambda s: (0, 0, 0, 0)))
    args.append(s0_lat)

    ctx_o = jax.ShapeDtypeStruct((b_ctx // par, par, n_ctx_ch, c, W_B), F32)
    lat_o = jax.ShapeDtypeStruct((b_lat, n_lat_ch, c, W_B), F32)
    cf = lambda s: s % n_ctx_ch
    of_ctx, ob_ctx, of_lat, ob_lat, s_fin = pl.pallas_call(
        functools.partial(_scan_kernel, par, b_lat, n_ctx_ch),
        grid=(n_lat_ch,),
        in_specs=in_specs,
        out_specs=[
            pl.BlockSpec((None, par, None, c, W_B), lambda s: (s // n_ctx_ch, 0, cf(s), 0, 0)),
            pl.BlockSpec((None, par, None, c, W_B), lambda s: (s // n_ctx_ch, 0, n_ctx_ch - 1 - cf(s), 0, 0)),
            pl.BlockSpec((b_lat, None, c, W_B), lambda s: (0, s, 0, 0)),
            pl.BlockSpec((b_lat, None, c, W_B), lambda s: (0, n_lat_ch - 1 - s, 0, 0)),
            pl.BlockSpec((par, 2 * nh, DK_B, DV_B), lambda s: (s // n_ctx_ch, 0, 0, 0)),
        ],
        out_shape=[ctx_o, ctx_o, lat_o, lat_o, jax.ShapeDtypeStruct((b_ctx, 2 * nh, DK_B, DV_B), F32)],
        scratch_shapes=[pltpu.VMEM((par + b_lat, 2 * nh, DK_B, DV_B), F32)],
        compiler_params=_cparams(1),
        name="delta_scan",
    )(*args)
    n_ctx, n_lat = b_ctx * t_ctx, b_lat * t_lat
    return (of_ctx.reshape(n_ctx, W_B), ob_ctx.reshape(n_ctx, W_B), of_lat.reshape(n_lat, W_B),
            ob_lat.reshape(n_lat, W_B), s_fin)


def _route_rows(sel, scores):
    g = EXPERTS_PER_GROUP
    gscore = []
    for gi in range(N_EXPERT_GROUPS):
        a, b, c, d = sel[gi * g:(gi + 1) * g]
        hi1, lo1 = jnp.maximum(a, b), jnp.minimum(a, b)
        hi2, lo2 = jnp.maximum(c, d), jnp.minimum(c, d)
        gscore.append(jnp.maximum(hi1, hi2) + jnp.maximum(jnp.minimum(hi1, hi2), jnp.maximum(lo1, lo2)))
    best = jnp.zeros_like(gscore[0], dtype=jnp.int32)
    bestv = gscore[0]
    for gi in range(1, N_EXPERT_GROUPS):
        upd = gscore[gi] > bestv
        best = jnp.where(upd, gi, best)
        bestv = jnp.where(upd, gscore[gi], bestv)

    def in_best(rows, j):
        out = rows[j]
        for gi in range(1, N_EXPERT_GROUPS):
            out = jnp.where(best == gi, rows[gi * g + j], out)
        return out

    e_sel = [in_best(sel, j) for j in range(g)]
    e_sc = [in_best(scores, j) for j in range(g)]

    def first_argmax(vals):
        bi, bv, bs = jnp.zeros_like(best), vals[0], e_sc[0]
        for j in range(1, g):
            upd = vals[j] > bv
            bi = jnp.where(upd, j, bi)
            bv = jnp.where(upd, vals[j], bv)
            bs = jnp.where(upd, e_sc[j], bs)
        return bi, bs

    i0, s0 = first_argmax(e_sel)
    i1, s1 = first_argmax([jnp.where(i0 == j, -jnp.inf, e_sel[j]) for j in range(g)])
    tot = s0 + s1
    return best * g + i0, best * g + i1, s0 / tot, s1 / tot


def _mix_kernel(n_ctx, x_ref, mod_ref, oac_ref, oal_ref, ofc_ref, obc_ref, ofl_ref, obl_ref,
                gb_ref, oc_ref, g0_ref, g1_ref, g2_ref,
                dn_ref, n2_ref, wpa_ref, wpb_ref, wpc_ref, wout_ref, wr_ref, br_ref,
                x1_ref, h2_ref, idx_ref, wt_ref):
    d = x_ref.shape[1]
    is_ctx = pl.program_id(0) * x_ref.shape[0] < n_ctx
    dn = dn_ref[...]
    o = jnp.where(is_ctx, ofc_ref[...] + obc_ref[...], ofl_ref[...] + obl_ref[...])
    o_a = jnp.where(is_ctx, oac_ref[...], oal_ref[...])
    gate = _silu(gb_ref[...])
    parts = []
    for h in range(N_HEADS_B):
        sl = slice(h * DV_B, (h + 1) * DV_B)
        parts.append((_rms(o[:, sl]) * dn * gate[:, sl]).astype(BF16))
    o_b = jnp.concatenate(parts, axis=-1)
    mixed = (_sigmoid(g0_ref[...]) * _dot(o_a, wpa_ref[...])
             + _sigmoid(g1_ref[...]) * _dot(o_b, wpb_ref[...])
             + _sigmoid(g2_ref[...]) * _dot(oc_ref[...], wpc_ref[...]))
    gate1 = mod_ref[:, 2 * d:3 * d]
    x1 = x_ref[...] + gate1 * _dot(mixed.astype(BF16), wout_ref[...])
    x1_ref[...] = x1
    shift2 = mod_ref[:, 3 * d:4 * d]
    scale2 = mod_ref[:, 4 * d:5 * d]
    h2 = _rms(x1) * n2_ref[...] * (1.0 + scale2) + shift2
    h2_ref[...] = h2
    h_hi = h2.astype(BF16)
    h_lo = (h2 - h_hi.astype(F32)).astype(BF16)
    w_r = wr_ref[...]
    w_hi = w_r.astype(BF16)
    w_lo = (w_r - w_hi.astype(F32)).astype(BF16)
    both = _dot_nt(jnp.concatenate([w_hi, w_lo], axis=0), h_hi)
    sc = _sigmoid(both[:N_EXPERTS] + both[N_EXPERTS:] + _dot_nt(w_hi, h_lo))
    sel = sc + br_ref[...]
    i0, i1, w0, w1 = _route_rows([sel[e:e + 1] for e in range(N_EXPERTS)],
                                 [sc[e:e + 1] for e in range(N_EXPERTS)])
    idx_ref[0:1, :] = i0
    idx_ref[1:2, :] = i1
    wt_ref[0:1, :] = w0
    wt_ref[1:2, :] = w1


def _mix(x, mod_l, oa, delta_o, z, oc, delta_norm_l, norm2_l, wpa, wpb, wpc, wout, layer, wr_t, br,
         cond_of_row):
    m, d = x.shape
    r = ROW_TILE
    n_ctx = cond_of_row.n_ctx
    nct = n_ctx // r
    c_gb = OFF_GATE_B // W_B
    once = dict(pipeline_mode=pl.Buffered(1))
    full = lambda shape: pl.BlockSpec(shape, lambda i: (0,) * len(shape), **once)
    of_layer = lambda shape: pl.BlockSpec((None,) + shape, lambda i: (layer,) + (0,) * len(shape), **once)
    ctx_rows = lambda w: pl.BlockSpec((r, w), lambda i: (jnp.minimum(i, nct - 1), 0))
    lat_rows = lambda w: pl.BlockSpec((r, w), lambda i: (jnp.maximum(i - nct, 0), 0))
    of_ctx, ob_ctx, of_lat, ob_lat = delta_o
    return pl.pallas_call(
        functools.partial(_mix_kernel, n_ctx),
        grid=(m // r,),
        in_specs=[
            pl.BlockSpec((r, d), lambda i: (i, 0)),
            pl.BlockSpec((None, 1, 6 * d), lambda i: (cond_of_row(i * r), 0, 0)),
            ctx_rows(W_A), lat_rows(W_A),
            ctx_rows(W_B), ctx_rows(W_B), lat_rows(W_B), lat_rows(W_B),
            pl.BlockSpec((r, W_B), lambda i: (i, c_gb)),
            pl.BlockSpec((r, W_C), lambda i: (i, 0)),
            pl.BlockSpec((r, d), lambda i: (i, 0)),
            pl.BlockSpec((r, d), lambda i: (i, 1)),
            pl.BlockSpec((r, d), lambda i: (i, 2)),
            full((1, DV_B)), full((1, d)),
            of_layer((W_A, d)), of_layer((W_B, d)), of_layer((W_C, d)), of_layer((d, d)),
            full((N_EXPERTS, d)), full((N_EXPERTS, 1)),
        ],
        out_specs=[
            pl.BlockSpec((r, d), lambda i: (i, 0)),
            pl.BlockSpec((r, d), lambda i: (i, 0)),
            pl.BlockSpec((TOP_K, r), lambda i: (0, i)),
            pl.BlockSpec((TOP_K, r), lambda i: (0, i)),
        ],
        out_shape=[
            jax.ShapeDtypeStruct((m, d), F32),
            jax.ShapeDtypeStruct((m, d), F32),
            jax.ShapeDtypeStruct((TOP_K, m), jnp.int32),
            jax.ShapeDtypeStruct((TOP_K, m), F32),
        ],
        compiler_params=_cparams(1),
        name="mix_route",
    )(x, mod_l, oa[0], oa[1], of_ctx, ob_ctx, of_lat, ob_lat, z, oc, z, z, z,
      delta_norm_l.reshape(1, DV_B), norm2_l.reshape(1, d), wpa, wpb, wpc, wout, wr_t, br)


def _dispatch_meta(idx_t, bm):
    m = idx_t.shape[1]
    nk = m * TOP_K
    flat_e = idx_t.T.reshape(nk)
    experts = jnp.arange(N_EXPERTS, dtype=jnp.int32)[None, :]
    counts = jnp.sum((flat_e[:, None] == experts).astype(jnp.int32), axis=0)
    order = jnp.argsort(flat_e, stable=True).astype(jnp.int32)
    start = jnp.cumsum(counts) - counts
    padded = (counts + bm - 1) // bm * bm
    pad_end = jnp.cumsum(padded)
    pad_start = pad_end - padded
    nb = -(-(nk + N_EXPERTS * (bm - 1)) // bm)
    p = jnp.arange(nb * bm, dtype=jnp.int32)
    e_of_p = jnp.minimum(jnp.sum((pad_end[None, :] <= p[:, None]).astype(jnp.int32), axis=1), N_EXPERTS - 1)
    pick = (e_of_p[:, None] == experts).astype(jnp.int32)
    rank = p - jnp.sum(pick * pad_start[None, :], axis=1)
    real = rank < jnp.sum(pick * counts[None, :], axis=1)
    assign = order[jnp.clip(jnp.sum(pick * start[None, :], axis=1) + rank, 0, nk - 1)]
    row_tok = jnp.where(real, assign // TOP_K, 0)
    spare = nk + ((p // bm) % 2) * bm + p % bm
    row_dst = jnp.where(real, (assign % TOP_K) * m + assign // TOP_K, spare)
    prime = nk + jnp.arange(2 * bm, dtype=jnp.int32)
    n_used = (pad_end[-1] // bm).astype(jnp.int32)
    blk_e = e_of_p[::bm]
    n_chunks = 3 * MOE_MAT_CHUNKS
    e_ids = experts[0]
    n_blk = padded // bm
    has = n_blk > 0
    later = lax.cummin(jnp.where(has, e_ids, N_EXPERTS)[::-1])[::-1]
    nxt_of_e = jnp.concatenate([later[1:], jnp.full((1,), N_EXPERTS, jnp.int32)])
    set_of_e = (jnp.cumsum(has.astype(jnp.int32)) - 1) % 2
    b = jnp.arange(nb, dtype=jnp.int32)
    k = b - (pad_start // bm)[blk_e]
    n_e = jnp.maximum(n_blk[blk_e], 1)
    streams = (nxt_of_e[blk_e] < N_EXPERTS) & (b < n_used)
    c0 = jnp.where(streams, n_chunks * k // n_e, 0)
    c1 = jnp.where(streams, n_chunks * (k + 1) // n_e, 0)
    nxt_e = jnp.where(streams, nxt_of_e[blk_e], 0)
    return (blk_e, n_used.reshape(1), row_tok, jnp.concatenate([prime, row_dst]),
            set_of_e[blk_e], nxt_e, c0, c1)


def _moe_kernel(layer, blk_e_ref, n_used_ref, row_tok_ref, row_dst_ref, set_ref, nxt_ref, c0_ref, c1_ref,
                h2_hbm, wg_hbm, wu_hbm, wd_hbm, out_hbm,
                xbuf, ybuf, wg_s, wu_s, wd_s, st_gu, st_d, gate_scr, act_scr, gsem, ssem, wsem_gu, wsem_d):
    i = pl.program_id(0)
    bm = xbuf.shape[1]
    n_used = n_used_ref[0]
    n_mat = MOE_MAT_CHUNKS
    n_stage = st_gu.shape[0]
    rg, rd = st_gu.shape[1], st_d.shape[1]

    def chunk_ops(c, j, e, s):
        cg, cu, cd = c, c - n_mat, c - 2 * n_mat
        rows_g = pl.ds(pl.multiple_of(cg * rg, rg), rg)
        rows_u = pl.ds(pl.multiple_of(cu * rg, rg), rg)
        rows_d = pl.ds(pl.multiple_of(cd * rd, 2 * SUBLANES), rd)

        def to_set(dst, rows, stage):
            def convert():
                dst[s, rows, :] = stage[j].astype(BF16)
            return convert

        return (
            (c < n_mat, lambda: pltpu.make_async_copy(wg_hbm.at[layer, e, rows_g, :], st_gu.at[j], wsem_gu.at[j]),
             to_set(wg_s, rows_g, st_gu)),
            ((c >= n_mat) & (c < 2 * n_mat),
             lambda: pltpu.make_async_copy(wu_hbm.at[layer, e, rows_u, :], st_gu.at[j], wsem_gu.at[j]),
             to_set(wu_s, rows_u, st_gu)),
            (c >= 2 * n_mat, lambda: pltpu.make_async_copy(wd_hbm.at[layer, e, rows_d, :], st_d.at[j], wsem_d.at[j]),
             to_set(wd_s, rows_d, st_d)),
        )

    def round_chunks(r, c_lo, c_hi):
        first = c_lo + r * n_stage
        return first, jnp.clip(c_hi - first, 0, n_stage)

    def issue_round(r, c_lo, c_hi, e, s):
        first, n = round_chunks(r, c_lo, c_hi)

        def body(j, carry):
            for pred, copy, _ in chunk_ops(first + j, j, e, s):
                pl.when(pred)(lambda copy=copy: copy().start(priority=1))
            return carry
        lax.fori_loop(0, n, body, 0)

    def finish_round(r, c_lo, c_hi, e, s):
        first, n = round_chunks(r, c_lo, c_hi)

        def body(j, carry):
            for pred, copy, convert in chunk_ops(first + j, j, e, s):
                @pl.when(pred)
                def _(copy=copy, convert=convert):
                    copy().wait()
                    convert()
            return carry
        lax.fori_loop(0, n, body, 0)

    def sync_rounds(r_lo, c_lo, c_hi, e, s):
        def body(r, carry):
            issue_round(r, c_lo, c_hi, e, s)
            finish_round(r, c_lo, c_hi, e, s)
            return carry
        n_rounds = (jnp.maximum(c_hi - c_lo, 0) + n_stage - 1) // n_stage
        lax.fori_loop(r_lo, jnp.maximum(n_rounds, r_lo), body, 0)

    def gather_row(blk, slot, r):
        tok = row_tok_ref[blk * bm + r]
        pltpu.make_async_copy(h2_hbm.at[pl.ds(tok, 1)], xbuf.at[slot, pl.ds(r, 1)], gsem.at[slot]).start()

    def scatter_row(tab_blk, slot, r):
        dst = row_dst_ref[tab_blk * bm + r]
        pltpu.make_async_copy(ybuf.at[slot, pl.ds(r, 1)], out_hbm.at[pl.ds(dst, 1)], ssem.at[slot]).start()

    def looped(fn, *args):
        def body(r, carry):
            fn(*args, r)
            return carry
        lax.fori_loop(0, bm, body, 0, unroll=8)

    def wait_gather(slot):
        pltpu.make_async_copy(h2_hbm.at[pl.ds(0, bm)], xbuf.at[slot], gsem.at[slot]).wait()

    def wait_scatter(slot):
        pltpu.make_async_copy(ybuf.at[slot], out_hbm.at[pl.ds(0, bm)], ssem.at[slot]).wait()

    @pl.when(i == 0)
    def _():
        ybuf[...] = jnp.zeros_like(ybuf)
        looped(gather_row, 0, 0)
        looped(scatter_row, 0, 0)
        sync_rounds(0, 0, 3 * n_mat, blk_e_ref[0], set_ref[0])

    def compute_block(slot):
        other = 1 - slot
        nxt = jnp.minimum(i + 1, n_used - 1)
        s, e_next, c_lo, c_hi = set_ref[i], nxt_ref[i], c0_ref[i], c1_ref[i]
        stream = (c_lo, c_hi, e_next, 1 - s)
        wait_gather(slot)
        issue_round(0, *stream)
        for r in range(bm):
            gather_row(nxt, other, r)
            scatter_row(i + 1, other, r)
        xb = xbuf[slot].astype(BF16)
        gate_scr[...] = _dot(xb, wg_s[s])
        finish_round(0, *stream)
        issue_round(1, *stream)
        act_scr[...] = (_silu(gate_scr[...]) * _dot(xb, wu_s[s])).astype(BF16)
        finish_round(1, *stream)
        wait_scatter(slot)
        issue_round(2, *stream)
        ybuf[slot] = _dot(act_scr[...], wd_s[s])
        finish_round(2, *stream)
        sync_rounds(3, *stream)

    for parity in range(2):
        pl.when((i < n_used) & (i % 2 == parity))(functools.partial(compute_block, parity))

    @pl.when(i == n_used)
    def _():
        looped(scatter_row, i + 1, (i - 1) % 2)
        wait_gather(i % 2)
        wait_scatter(0)
        wait_scatter(1)


def _moe(h2, meta, wg, wu, wd, layer):
    m, d = h2.shape
    bm = MOE_ROWS
    nb = meta[0].shape[0]
    f = wg.shape[3]
    rg, rd = d // MOE_MAT_CHUNKS, f // MOE_MAT_CHUNKS
    assert rg % (2 * SUBLANES) == 0 and rd % (2 * SUBLANES) == 0
    any_spec = pl.BlockSpec(memory_space=pl.ANY)
    dma = pltpu.SemaphoreType.DMA
    return pl.pallas_call(
        functools.partial(_moe_kernel, layer),
        grid_spec=pltpu.PrefetchScalarGridSpec(
            num_scalar_prefetch=len(meta),
            grid=(nb + 1,),
            in_specs=[any_spec] * 4,
            out_specs=any_spec,
            scratch_shapes=[
                pltpu.VMEM((2, bm, d), F32), pltpu.VMEM((2, bm, d), F32),
                pltpu.VMEM((2, d, f), BF16), pltpu.VMEM((2, d, f), BF16), pltpu.VMEM((2, f, d), BF16),
                pltpu.VMEM((MOE_STAGE_SLOTS, rg, f), F32), pltpu.VMEM((MOE_STAGE_SLOTS, rd, d), F32),
                pltpu.VMEM((bm, f), F32), pltpu.VMEM((bm, f), BF16),
                dma((2,)), dma((2,)), dma((MOE_STAGE_SLOTS,)), dma((MOE_STAGE_SLOTS,)),
            ],
        ),
        out_shape=jax.ShapeDtypeStruct((TOP_K * m + 2 * bm, d), F32),
        compiler_params=_cparams(1, MOE_VMEM_LIMIT),
        name="moe_experts",
    )(*meta, h2, wg, wu, wd)


def _combine_kernel(n_ctx, final, y0_ref, y1_ref, x1_ref, mod_ref, w_ref, g_ref, modn_ref, o_ref, o2_ref):
    d = x1_ref.shape[1]
    f = w_ref[:, 0:1] * y0_ref[...] + w_ref[:, 1:2] * y1_ref[...]
    x2 = x1_ref[...] + mod_ref[:, 5 * d:6 * d] * f
    if final:
        y = _rms(x2) * g_ref[...]
        is_ctx = pl.program_id(0) * x1_ref.shape[0] < n_ctx

        @pl.when(is_ctx)
        def _():
            o_ref[...] = y

        @pl.when(jnp.logical_not(is_ctx))
        def _():
            o2_ref[...] = y
    else:
        o_ref[...] = x2
        o2_ref[...] = _modulated_norm(x2, g_ref[...], modn_ref[...], d).astype(BF16)


def _combine(y2, x1, mod_l, w_col, gain, mod_next, final, cond_of_row):
    m, d = x1.shape
    r = ROW_TILE
    k1 = m // r
    n_ctx = cond_of_row.n_ctx
    nct = n_ctx // r
    cond_row = pl.BlockSpec((None, 1, 6 * d), lambda i: (cond_of_row(i * r), 0, 0))
    tile = pl.BlockSpec((r, d), lambda i: (i, 0))
    if final:
        out_specs = [pl.BlockSpec((r, d), lambda i: (jnp.minimum(i, nct - 1), 0)),
                     pl.BlockSpec((r, d), lambda i: (jnp.maximum(i - nct, 0), 0))]
        out_shape = [jax.ShapeDtypeStruct((n_ctx, d), F32), jax.ShapeDtypeStruct((m - n_ctx, d), F32)]
    else:
        out_specs = [tile, tile]
        out_shape = [jax.ShapeDtypeStruct((m, d), F32), jax.ShapeDtypeStruct((m, d), BF16)]
    return pl.pallas_call(
        functools.partial(_combine_kernel, n_ctx, final),
        grid=(m // r,),
        in_specs=[
            tile,
            pl.BlockSpec((r, d), lambda i: (k1 + i, 0)),
            tile, cond_row,
            pl.BlockSpec((r, TOP_K), lambda i: (i, 0)),
            pl.BlockSpec((1, d), lambda i: (0, 0)),
            cond_row,
        ],
        out_specs=out_specs,
        out_shape=out_shape,
        compiler_params=_cparams(1),
        name="moe_combine",
    )(y2, y2, x1, mod_l, w_col, gain.reshape(1, d), mod_next)


class _CondOfRow:
    def __init__(self, n_ctx, t_lat):
        self.n_ctx, self.t_lat = n_ctx, t_lat
        self.tile_in = _pick_tile((1024, 512, 256), n_ctx, t_lat)

    def __call__(self, row0):
        return jnp.where(row0 < self.n_ctx, 0, 1 + (row0 - self.n_ctx) // self.t_lat)


def kernel(x_prompt, x_sample, cache_k, cache_v, state_delta, c, c_ctx, w_mod, b_mod, norm1, norm2, w_in,
           lambda_qk, subln, conv_w, a_log, dt_bias, delta_norm, pool_w, pool_scale, w_pa, w_pb, w_pc,
           w_out, w_router, b_router, w_gate, w_up, w_down, norm_final):
    b_ctx, t_ctx, d = x_prompt.shape
    b_lat, t_lat, _ = x_sample.shape
    depth = w_in.shape[0]
    past = cache_k.shape[2]
    n_ctx, n_lat = b_ctx * t_ctx, b_lat * t_lat
    m = n_ctx + n_lat
    assert d == D_MODEL and w_in.shape[2] == D_IN and 1 + b_lat <= SUBLANES
    assert t_ctx % ROW_TILE == 0 and t_lat % ROW_TILE == 0 and n_ctx % t_lat == 0
    cond_of_row = _CondOfRow(n_ctx, t_lat)
    dims = (b_ctx, t_ctx, b_lat, t_lat)

    def seq(row0):
        is_ctx = row0 < n_ctx
        return (jnp.where(is_ctx, row0 % t_ctx, (row0 - n_ctx) % t_lat), jnp.where(is_ctx, t_ctx, t_lat))

    x = jnp.concatenate([x_prompt.reshape(n_ctx, d), x_sample.reshape(n_lat, d)], axis=0)
    cond8 = jnp.zeros((SUBLANES, d), F32).at[0].set(c_ctx).at[1:1 + b_lat].set(c)
    mod = _modulation(cond8, w_mod, b_mod).reshape(depth, SUBLANES, 1, 6 * d)
    rope_tabs = _rope_tables(t_lat)
    wr_t = w_router.T
    br = b_router.reshape(N_EXPERTS, 1)
    ck = cache_k.reshape(b_lat, depth, past, QK_A)
    cv = cache_v.reshape(b_lat, depth, past, W_A)
    s_lat = state_delta.reshape(b_lat, depth, 2 * N_HEADS_B, DK_B, DV_B)

    new_k, new_v, new_s = [], [], []
    h = _norm_mod(x, mod[0], norm1[0], cond_of_row)
    w_in_t = jnp.swapaxes(w_in, 1, 2)
    w_proj = [w.astype(BF16) for w in (w_pa, w_pb, w_pc, w_out)]
    for l in range(depth):
        lam_init = 0.8 - 0.6 * math.exp(-0.3 * l)
        last = l == depth - 1
        z = _in_proj(h, w_in_t, l, cond_of_row.tile_in)
        oa, k_l, v_l = _attention(z, lambda_qk[l], subln[l], ck[:, l], cv[:, l], rope_tabs, lam_init, dims)
        new_k.append(k_l.reshape(b_ctx, t_ctx, N_HEADS_A, 2 * DK_A))
        new_v.append(v_l.reshape(b_ctx, t_ctx, N_HEADS_A, DV_A))
        bat = z[:, OFF_BA:OFF_BA + 4 * N_HEADS_B].T
        oc, u, l1, l2, e = _local(z, bat, conv_w[l], pool_w[l], pool_scale[l], a_log[l], dt_bias[l], seq)
        *delta_o, s_ctx = _delta_scan(u, l1, l2, e, s_lat[:, l], dims)
        new_s.append(s_ctx.reshape(b_ctx, 2, N_HEADS_B, DK_B, DV_B))

        x1, h2, idx_t, wt_t = _mix(x, mod[l], oa, delta_o, z, oc, delta_norm[l], norm2[l], *w_proj, l,
                                   wr_t, br, cond_of_row)
        y2 = _moe(h2, _dispatch_meta(idx_t, MOE_ROWS), w_gate, w_up, w_down, l)
        if last:
            y_ctx, y_lat = _combine(y2, x1, mod[l], wt_t.T, norm_final, mod[l], True, cond_of_row)
        else:
            x, h = _combine(y2, x1, mod[l], wt_t.T, norm1[l + 1], mod[l + 1], False, cond_of_row)

    y_prompt = y_ctx.reshape(b_ctx, t_ctx, d)
    y_sample = y_lat.reshape(b_lat, t_lat, d)
    return (y_prompt, y_sample, jnp.stack(new_k, axis=1), jnp.stack(new_v, axis=1), jnp.stack(new_s, axis=1))
```

```python
import functools
import math

import jax
import jax.numpy as jnp
from jax import lax
from jax.experimental import pallas as pl
from jax.experimental.pallas import tpu as pltpu

F32 = jnp.float32
BF16 = jnp.bfloat16
HIGHEST = lax.Precision.HIGHEST

D_MODEL = 2048
GRID_W = 64
EPS = 1e-6
N_HEADS_A = 8
DK_A = 64
DV_A = 128
QK_A = N_HEADS_A * 2 * DK_A
W_A = N_HEADS_A * DV_A
ROPE_BASE = 10000.0
ROPE_AXIS = DK_A // 2
N_HEADS_B = 4
DK_B = 128
DV_B = 128
W_B = N_HEADS_B * DV_B
DELTA_CHUNK = 64
POOL_WINDOWS = (2, 4, 8, 16)
POOL_GROUP_W = 128
W_C = len(POOL_WINDOWS) * POOL_GROUP_W
N_BRANCH = 3
N_EXPERTS = 16
N_EXPERT_GROUPS = 4
EXPERTS_PER_GROUP = N_EXPERTS // N_EXPERT_GROUPS
TOP_K = 2
D_FF = 1408

SRC_QKV_B = 2 * QK_A + W_A
SRC_GATE_B = SRC_QKV_B + 3 * W_B
SRC_BA = SRC_GATE_B + W_B
SRC_POOL = SRC_BA + 4 * N_HEADS_B
SRC_BRANCH = SRC_POOL + W_C
D_IN = SRC_BRANCH + N_BRANCH * D_MODEL
LANES = 128
SUBLANES = 8
IN_TILE_N = 1024
IN_HALF_N = IN_TILE_N // 2
OFF_RAW = N_BRANCH * D_MODEL
RAW_W = -(-SRC_BRANCH // IN_TILE_N) * IN_TILE_N
NP_IN = OFF_RAW + RAW_W
OFF_Q = OFF_RAW
OFF_K = OFF_Q + QK_A
OFF_V = OFF_K + QK_A
OFF_QKV_B = OFF_RAW + SRC_QKV_B
OFF_GATE_B = OFF_RAW + SRC_GATE_B
OFF_BA = OFF_RAW + SRC_BA
BRANCH_SHIFT = SRC_BRANCH % LANES
assert SRC_BA % IN_TILE_N == 0 and (SRC_BRANCH - BRANCH_SHIFT) % IN_HALF_N == 0
assert SRC_POOL - SRC_BA == BRANCH_SHIFT and OFF_RAW % IN_TILE_N == 0

ROW_TILE = 256
MOE_ROWS = 256
DELTA_CHUNKS_PER_GROUP = 2
INV_BASE_BLOCK = 16
MOE_MAT_CHUNKS = 8
MOE_STAGE_SLOTS = 2
MOE_VMEM_LIMIT = 60 * 1024 * 1024
VMEM_LIMIT = 56 * 1024 * 1024


def _cparams(n_axes, vmem=VMEM_LIMIT):
    return pltpu.CompilerParams(dimension_semantics=("arbitrary",) * n_axes, vmem_limit_bytes=vmem)


def _dot(a, b, precision=None):
    return jnp.dot(a, b, preferred_element_type=F32, precision=precision)


def _dot_nt(a, b, precision=None):
    return lax.dot_general(a, b, (((1,), (1,)), ((), ())), preferred_element_type=F32, precision=precision)


def _bdot(a, b):
    return _dot(a.astype(BF16), b.astype(BF16))


def _bdot_nt(a, b):
    return _dot_nt(a.astype(BF16), b.astype(BF16))


def _sigmoid(x):
    return 1.0 / (1.0 + jnp.exp(-x))


def _silu(x):
    return x * _sigmoid(x)


def _rms(x, eps=EPS):
    return x * lax.rsqrt(jnp.mean(x * x, axis=-1, keepdims=True) + eps)


def _pick_tile(cands, *dims):
    for t in cands:
        if all(d % t == 0 for d in dims):
            return t
    raise ValueError(f"no tile in {cands} divides {dims}")


def _mod_kernel(c_ref, w_ref, b_ref, o_ref):
    a = _silu(c_ref[...]).astype(BF16)
    o_ref[...] = _dot(a, w_ref[...].astype(BF16)) + b_ref[...]


def _modulation(cond8, w_mod, b_mod):
    depth, d, n6 = w_mod.shape
    tn = 1024
    return pl.pallas_call(
        _mod_kernel,
        grid=(depth, n6 // tn),
        in_specs=[
            pl.BlockSpec((SUBLANES, d), lambda l, n: (0, 0)),
            pl.BlockSpec((None, d, tn), lambda l, n: (l, 0, n)),
            pl.BlockSpec((None, 1, tn), lambda l, n: (l, 0, n)),
        ],
        out_specs=pl.BlockSpec((None, SUBLANES, tn), lambda l, n: (l, 0, n)),
        out_shape=jax.ShapeDtypeStruct((depth, SUBLANES, n6), F32),
        compiler_params=_cparams(2),
        name="adaln_mod",
    )(cond8, w_mod, b_mod.reshape(depth, 1, n6))


def _modulated_norm(x, g, mod, d):
    return _rms(x) * g * (1.0 + mod[:, d:2 * d]) + mod[:, 0:d]


def _norm_kernel(x_ref, mod_ref, g_ref, h_ref):
    h_ref[...] = _modulated_norm(x_ref[...], g_ref[...], mod_ref[...], x_ref.shape[1]).astype(BF16)


def _norm_mod(x, mod_l, norm_l, cond_of_row):
    m, d = x.shape
    r = ROW_TILE
    return pl.pallas_call(
        _norm_kernel,
        grid=(m // r,),
        in_specs=[
            pl.BlockSpec((r, d), lambda i: (i, 0)),
            pl.BlockSpec((None, 1, 6 * d), lambda i: (cond_of_row(i * r), 0, 0)),
            pl.BlockSpec((1, d), lambda i: (0, 0)),
        ],
        out_specs=pl.BlockSpec((r, d), lambda i: (i, 0)),
        out_shape=jax.ShapeDtypeStruct((m, d), BF16),
        compiler_params=_cparams(1),
        name="norm_mod",
    )(x, mod_l, norm_l.reshape(1, d))


def _in_kernel(h_ref, wt_ref, z_ref, w_scr):
    @pl.when(pl.program_id(1) == 0)
    def _():
        w_scr[...] = wt_ref[0].astype(BF16)

    z_ref[...] = _dot_nt(h_ref[...], w_scr[...])


def _in_proj(h, w_in_t, layer, tm):
    m, d = h.shape
    n_branch_tiles = OFF_RAW // IN_TILE_N
    assert SRC_BRANCH % SUBLANES == 0
    tile8 = IN_TILE_N // SUBLANES
    row0 = lambda n: SUBLANES * jnp.where(n < n_branch_tiles, SRC_BRANCH // SUBLANES + tile8 * n,
                                          tile8 * (n - n_branch_tiles))
    return pl.pallas_call(
        _in_kernel,
        grid=(NP_IN // IN_TILE_N, m // tm),
        in_specs=[
            pl.BlockSpec((tm, d), lambda n, i: (i, 0)),
            pl.BlockSpec((pl.Element(1), pl.Element(IN_TILE_N), pl.Element(d)),
                         lambda n, i: (layer, row0(n), 0)),
        ],
        out_specs=pl.BlockSpec((tm, IN_TILE_N), lambda n, i: (i, n)),
        out_shape=jax.ShapeDtypeStruct((m, NP_IN), F32),
        scratch_shapes=[pltpu.VMEM((IN_TILE_N, d), BF16)],
        compiler_params=_cparams(2),
        name="in_proj",
    )(h, w_in_t)


def _lambda(lq_ref, lam_init):
    lq = lq_ref[...]
    a = jnp.sum(lq[0:1] * lq[1:2], axis=-1, keepdims=True)
    b = jnp.sum(lq[2:3] * lq[3:4], axis=-1, keepdims=True)
    return jnp.exp(a) - jnp.exp(b) + lam_init


Q_SCALE = DK_A ** -0.5


def _diff_attn(qb, kb, vb, lam):
    probs = []
    for mp in range(2):
        s = _dot_nt(qb[:, mp * DK_A:(mp + 1) * DK_A], kb[:, mp * DK_A:(mp + 1) * DK_A])
        e = jnp.exp(s - jnp.max(s, axis=-1, keepdims=True))
        probs.append(e / jnp.sum(e, axis=-1, keepdims=True))
    pd = probs[0] - lam * probs[1]
    return _dot(pd.astype(BF16), vb)


def _attn_ctx_kernel(lam_init, lq_ref, sub_ref, q_ref, k_ref, v_ref, o_ref, knew_ref, vnew_ref):
    lam = _lambda(lq_ref, lam_init)
    for h in range(N_HEADS_A):
        hs = slice(h * LANES, (h + 1) * LANES)
        qb = (q_ref[:, hs] * Q_SCALE).astype(BF16)
        k, v = k_ref[:, hs], v_ref[:, hs]
        knew_ref[:, hs] = k
        vnew_ref[:, hs] = v
        o = _diff_attn(qb, k.astype(BF16), v.astype(BF16), lam)
        o_ref[:, hs] = (_rms(o) * sub_ref[...] * (1.0 - lam_init)).astype(BF16)


def _rope(x, cos, sin_signed):
    lane = lax.broadcasted_iota(jnp.int32, x.shape, 1)
    first_half = (lane % (ROPE_AXIS)) < (ROPE_AXIS // 2)
    partner = jnp.where(first_half, pltpu.roll(x, LANES - ROPE_AXIS // 2, 1), pltpu.roll(x, ROPE_AXIS // 2, 1))
    return x * cos + partner * sin_signed


def _attn_lat_kernel(lam_init, t_lat, lq_ref, sub_ref, q_ref, k_ref, v_ref, ck_ref, cv_ref,
                     cq_ref, sq_ref, ckk_ref, skk_ref, o_ref, k_scr, v_scr):
    @pl.when(pl.program_id(2) == 0)
    def _():
        k_scr[0:t_lat, :] = _rope(k_ref[...], ckk_ref[...], skk_ref[...]).astype(BF16)
        k_scr[t_lat:, :] = ck_ref[...].astype(BF16)
        v_scr[0:t_lat, :] = v_ref[...].astype(BF16)
        v_scr[t_lat:, :] = cv_ref[...].astype(BF16)

    lam = _lambda(lq_ref, lam_init)
    q = _rope(q_ref[...], cq_ref[...], sq_ref[...]) * Q_SCALE
    o = _diff_attn(q.astype(BF16), k_scr[...], v_scr[...], lam)
    o_ref[...] = (_rms(o) * sub_ref[...] * (1.0 - lam_init)).astype(BF16)


def _attention(z, lq_l, sub_l, cache_k_l, cache_v_l, rope_tabs, lam_init, dims):
    b_ctx, t_ctx, b_lat, t_lat = dims
    n_ctx = b_ctx * t_ctx
    past = cache_k_l.shape[1]
    cq, ck, cv = OFF_Q // LANES, OFF_K // LANES, OFF_V // LANES
    small = [pl.BlockSpec((4, DK_A), lambda *_: (0, 0)), pl.BlockSpec((1, DV_A), lambda *_: (0, 0))]
    sub2 = sub_l.reshape(1, DV_A)

    seq_blk = pl.BlockSpec((t_ctx, W_A), lambda b: (b, 0))
    oa_ctx, k_new, v_new = pl.pallas_call(
        functools.partial(_attn_ctx_kernel, lam_init),
        grid=(b_ctx,),
        in_specs=small + [
            pl.BlockSpec((t_ctx, QK_A), lambda b: (b, OFF_Q // QK_A)),
            pl.BlockSpec((t_ctx, QK_A), lambda b: (b, OFF_K // QK_A)),
            pl.BlockSpec((t_ctx, W_A), lambda b: (b, OFF_V // W_A)),
        ],
        out_specs=[seq_blk] * 3,
        out_shape=[jax.ShapeDtypeStruct((n_ctx, W_A), BF16), jax.ShapeDtypeStruct((n_ctx, QK_A), F32),
                   jax.ShapeDtypeStruct((n_ctx, W_A), F32)],
        compiler_params=_cparams(1),
        name="attn_ctx",
    )(lq_l, sub2, z, z, z)

    tq = _pick_tile((256, 128), t_lat)
    nq = t_lat // tq
    rb_q = n_ctx // tq
    rb_k = n_ctx // t_lat
    cos_t, sin_t = rope_tabs
    oa_lat = pl.pallas_call(
        functools.partial(_attn_lat_kernel, lam_init, t_lat),
        grid=(b_lat, N_HEADS_A, nq),
        in_specs=small + [
            pl.BlockSpec((tq, LANES), lambda b, h, i: (rb_q + b * nq + i, cq + h)),
            pl.BlockSpec((t_lat, LANES), lambda b, h, i: (rb_k + b, ck + h)),
            pl.BlockSpec((t_lat, LANES), lambda b, h, i: (rb_k + b, cv + h)),
            pl.BlockSpec((None, past, LANES), lambda b, h, i: (b, 0, h)),
            pl.BlockSpec((None, past, LANES), lambda b, h, i: (b, 0, h)),
            pl.BlockSpec((tq, LANES), lambda b, h, i: (i, 0)),
            pl.BlockSpec((tq, LANES), lambda b, h, i: (i, 0)),
            pl.BlockSpec((t_lat, LANES), lambda b, h, i: (0, 0)),
            pl.BlockSpec((t_lat, LANES), lambda b, h, i: (0, 0)),
        ],
        out_specs=pl.BlockSpec((tq, LANES), lambda b, h, i: (b * nq + i, h)),
        out_shape=jax.ShapeDtypeStruct((b_lat * t_lat, W_A), BF16),
        scratch_shapes=[pltpu.VMEM((t_lat + past, LANES), BF16), pltpu.VMEM((t_lat + past, LANES), BF16)],
        compiler_params=_cparams(3),
        name="attn_lat",
    )(lq_l, sub2, z, z, z, cache_k_l, cache_v_l, cos_t, sin_t, cos_t, sin_t)
    return (oa_ctx, oa_lat), k_new, v_new


def _rope_tables(t_lat):
    rows = t_lat // GRID_W
    r = jnp.repeat(jnp.arange(rows), GRID_W).astype(F32)
    col = jnp.tile(jnp.arange(GRID_W), rows).astype(F32)
    inv = ROPE_BASE ** (-jnp.arange(0, ROPE_AXIS, 2, dtype=F32) / ROPE_AXIS)
    ang_r, ang_c = r[:, None] * inv, col[:, None] * inv
    cos64 = jnp.concatenate([jnp.cos(ang_r)] * 2 + [jnp.cos(ang_c)] * 2, axis=-1)
    sin64 = jnp.concatenate([-jnp.sin(ang_r), jnp.sin(ang_r), -jnp.sin(ang_c), jnp.sin(ang_c)], axis=-1)
    return jnp.tile(cos64, (1, 2)), jnp.tile(sin64, (1, 2))


def _log_decay(alpha, a_log, dt):
    x = alpha + dt
    return -jnp.exp(a_log) * (jnp.maximum(x, 0.0) + jnp.log(1.0 + jnp.exp(-jnp.abs(x))))


def _chunk_cumsum(g, axis):
    n = g.shape[axis]
    pos = lax.broadcasted_iota(jnp.int32, g.shape, axis) % DELTA_CHUNK
    f, b = g, g
    s = 1
    while s < DELTA_CHUNK:
        f = f + jnp.where(pos >= s, pltpu.roll(f, s, axis), 0.0)
        b = b + jnp.where(pos < DELTA_CHUNK - s, pltpu.roll(b, n - s, axis), 0.0)
        s *= 2
    return f, b


def _delta_prepare(q_scr, k_scr, v_scr, ba, bat, al_row, al_col, dt_row, dt_col,
                   u_ref, l1_ref, l2_ref, e_ref):
    c, nh = DELTA_CHUNK, N_HEADS_B
    rows_n = q_scr.shape[0]
    beta_all = _sigmoid(ba[:, 0:2 * nh])
    gf, gb = _chunk_cumsum(_log_decay(ba[:, 2 * nh:4 * nh], al_row[...], dt_row[...]), 0)
    gc_all = jnp.where(lax.broadcasted_iota(jnp.int32, gf.shape, 1) < nh, gf, gb)
    gtf, gtb = _chunk_cumsum(_log_decay(bat[2 * nh:4 * nh, :], al_col[...], dt_col[...]), 1)
    gct_all = jnp.where(lax.broadcasted_iota(jnp.int32, gtf.shape, 0) < nh, gtf, gtb)

    ri = lax.broadcasted_iota(jnp.int32, (c, c), 0)
    ci = lax.broadcasted_iota(jnp.int32, (c, c), 1)
    masks = ((ri >= ci, ri > ci), (ri <= ci, ri < ci))
    eye = (ri == ci).astype(F32)
    blk_sizes = [INV_BASE_BLOCK << i for i in range(int(math.log2(c // INV_BASE_BLOCK)) + 1)]
    same_blk = [(ri // s) == (ci // s) for s in blk_sizes]
    n_sq = int(math.log2(INV_BASE_BLOCK)) - 1

    n_chunks = rows_n // c
    for j0 in range(0, n_chunks, DELTA_CHUNKS_PER_GROUP):
        chains = []
        for j, d, h in [(j, d, h) for j in range(j0, j0 + DELTA_CHUNKS_PER_GROUP) for d in range(2)
                        for h in range(nh)]:
            rs = slice(j * c, (j + 1) * c)
            incl, strict = masks[d]
            last = 0 if d else c - 1
            idx = d * nh + h
            hs = slice(h * DK_B, (h + 1) * DK_B)
            gc = gc_all[rs, idx:idx + 1]
            gr = gct_all[idx:idx + 1, rs]
            ch = dict(j=j, rs=rs, d=d, h=h, idx=idx, hs=hs, strict=strict, gc=gc, g_last=gc[last:last + 1, :],
                      q=q_scr[rs, hs] * DK_B ** -0.5, k=k_scr[rs, hs], v=v_scr[rs, hs],
                      beta=beta_all[rs, idx:idx + 1],
                      decay=jnp.where(incl, jnp.exp(jnp.where(incl, gc - gr, 0.0)), 0.0))
            ch["kb"] = ch["k"] * ch["beta"]
            chains.append(ch)
        for ch in chains:
            scores = _bdot_nt(jnp.concatenate([ch["kb"], ch["q"]], axis=0), ch["k"])
            a = jnp.where(ch["strict"], -(scores[:c] * ch["decay"]), 0.0)
            ch["attn"] = scores[c:] * ch["decay"]
            ch["a"] = a
            ch["apow"] = jnp.where(same_blk[0], a, 0.0)
            ch["p"] = eye + ch["apow"]
        for _ in range(n_sq):
            for ch in chains:
                ch["apow"] = _bdot(ch["apow"], ch["apow"])
            for ch in chains:
                ch["p"] = ch["p"] + _bdot(ch["p"], ch["apow"])
        for lvl in range(1, len(same_blk)):
            for ch in chains:
                a_off = jnp.where(same_blk[lvl] & jnp.logical_not(same_blk[lvl - 1]), ch["a"], 0.0)
                ch["t"] = _bdot(ch["p"], a_off)
            for ch in chains:
                ch["p"] = ch["p"] + _bdot(ch["t"], ch["p"])
        for ch in chains:
            rhs = jnp.concatenate([ch["v"] * ch["beta"], ch["kb"] * jnp.exp(ch["gc"])], axis=-1)
            ch["sol"] = _bdot(ch["p"], rhs)
        for ch in chains:
            j, rs, d, h, hs = ch["j"], ch["rs"], ch["d"], ch["h"], ch["hs"]
            u_ref[d, rs, hs] = ch["sol"][:, :DV_B]
            l1_ref[d, j, h, 0:c, :] = ch["sol"][:, DV_B:].astype(BF16)
            l1_ref[d, j, h, c:2 * c, :] = (ch["q"] * jnp.exp(ch["gc"])).astype(BF16)
            l2_ref[d, j, h, 0:c, :] = ch["attn"].astype(BF16)
            l2_ref[d, j, h, c:, :] = (ch["k"] * jnp.exp(ch["g_last"] - ch["gc"])).T.astype(BF16)
            e_ref[j, ch["idx"]:ch["idx"] + 1, :] = jnp.broadcast_to(jnp.exp(ch["g_last"]), (1, LANES))


def _local_kernel(seq, cur_ref, prev_ref, next_ref, pcur_ref, pprev_ref, pnext_ref, bat_ref,
                  cw_ref, pw_ref, ps_ref, al_row, al_col, dt_row, dt_col,
                  oc_ref, u_ref, l1_ref, l2_ref, e_ref, q_scr, k_scr, v_scr):
    rows_n = cur_ref.shape[0]
    pos0, t_seq = seq(pl.program_id(0) * rows_n)
    has_prev = pos0 > 0
    has_next = pos0 + rows_n < t_seq
    rows = lax.broadcasted_iota(jnp.int32, (rows_n, 1), 0)

    x = cur_ref[...]
    prev_row = jnp.where(has_prev, prev_ref[SUBLANES - 1:SUBLANES, :], 0.0)
    next_row = jnp.where(has_next, next_ref[0:1, :], 0.0)
    xm1 = jnp.where(rows == 0, prev_row, pltpu.roll(x, 1, 0))
    xp1 = jnp.where(rows == rows_n - 1, next_row, pltpu.roll(x, rows_n - 1, 0))
    y = _silu(cw_ref[0:1, :] * xm1 + cw_ref[1:2, :] * x + cw_ref[2:3, :] * xp1)
    for h in range(N_HEADS_B):
        for j, ref in enumerate((q_scr, k_scr)):
            a = y[:, j * W_B + h * DK_B:j * W_B + (h + 1) * DK_B]
            ref[:, h * DK_B:(h + 1) * DK_B] = a * lax.rsqrt(jnp.sum(a * a, axis=-1, keepdims=True) + EPS)
    v_scr[...] = y[:, 2 * W_B:]
    _delta_prepare(q_scr, k_scr, v_scr, pcur_ref[:, 0:LANES], bat_ref[...], al_row, al_col, dt_row, dt_col,
                   u_ref, l1_ref, l2_ref, e_ref)

    def pool_cols(ref):
        t = ref[...]
        return pltpu.roll(t, t.shape[1] - BRANCH_SHIFT, 1)[:, :W_C]

    pz = pool_cols(pcur_ref)
    ext = jnp.concatenate([jnp.where(has_prev, pool_cols(pprev_ref), 0.0), pz,
                           jnp.where(has_next, pool_cols(pnext_ref), 0.0)], axis=0)
    n_ext = rows_n + 2 * SUBLANES
    tpos = pos0 + rows
    for gi, win in enumerate(POOL_WINDOWS):
        sl = slice(gi * POOL_GROUP_W, (gi + 1) * POOL_GROUP_W)
        xg = ext[:, sl]
        acc = xg + pltpu.roll(xg, 1, 0)
        s = 1
        while 2 * s < win:
            acc = pltpu.roll(acc, s, 0) + pltpu.roll(acc, n_ext - s, 0)
            s *= 2
        acc = acc[SUBLANES:SUBLANES + rows_n]
        cnt = (jnp.minimum(tpos + win // 2, t_seq) - jnp.maximum(tpos - win // 2, 0)).astype(F32)
        d = acc / cnt - pz[:, sl]
        yg = _dot(d.astype(BF16), pw_ref[gi].astype(BF16)) * ps_ref[:, sl]
        oc_ref[:, sl] = yg.astype(BF16)


def _local(z, bat, conv_w_l, pool_w_l, pool_scale_l, a_log_l, dt_l, seq):
    m = z.shape[0]
    r, c = ROW_TILE, DELTA_CHUNK
    hb = r // SUBLANES
    last_hb = m // SUBLANES - 1
    c_qkv, c_pool = OFF_QKV_B // (3 * W_B), OFF_BA // IN_TILE_N
    nh2 = 2 * N_HEADS_B
    mc, cpt = m // c, r // c
    prev_map = lambda cb: (lambda i: (jnp.maximum(i * hb - 1, 0), cb))
    next_map = lambda cb: (lambda i: (jnp.minimum((i + 1) * hb, last_hb), cb))
    row = pl.BlockSpec((1, nh2), lambda i: (0, 0))
    col = pl.BlockSpec((nh2, 1), lambda i: (0, 0))
    return pl.pallas_call(
        functools.partial(_local_kernel, seq),
        grid=(m // r,),
        in_specs=[
            pl.BlockSpec((r, 3 * W_B), lambda i: (i, c_qkv)),
            pl.BlockSpec((SUBLANES, 3 * W_B), prev_map(c_qkv)),
            pl.BlockSpec((SUBLANES, 3 * W_B), next_map(c_qkv)),
            pl.BlockSpec((r, IN_TILE_N), lambda i: (i, c_pool)),
            pl.BlockSpec((SUBLANES, IN_TILE_N), prev_map(c_pool)),
            pl.BlockSpec((SUBLANES, IN_TILE_N), next_map(c_pool)),
            pl.BlockSpec((2 * nh2, r), lambda i: (0, i)),
            pl.BlockSpec((3, 3 * W_B), lambda i: (0, 0)),
            pl.BlockSpec((len(POOL_WINDOWS), POOL_GROUP_W, POOL_GROUP_W), lambda i: (0, 0, 0)),
            pl.BlockSpec((1, W_C), lambda i: (0, 0)),
            row, col, row, col,
        ],
        out_specs=[
            pl.BlockSpec((r, W_C), lambda i: (i, 0)),
            pl.BlockSpec((2, r, W_B), lambda i: (0, i, 0)),
            pl.BlockSpec((2, cpt, N_HEADS_B, 2 * c, DV_B), lambda i: (0, i, 0, 0, 0)),
            pl.BlockSpec((2, cpt, N_HEADS_B, c + DK_B, c), lambda i: (0, i, 0, 0, 0)),
            pl.BlockSpec((cpt, nh2, LANES), lambda i: (i, 0, 0)),
        ],
        out_shape=[
            jax.ShapeDtypeStruct((m, W_C), BF16),
            jax.ShapeDtypeStruct((2, m, W_B), F32),
            jax.ShapeDtypeStruct((2, mc, N_HEADS_B, 2 * c, DV_B), BF16),
            jax.ShapeDtypeStruct((2, mc, N_HEADS_B, c + DK_B, c), BF16),
            jax.ShapeDtypeStruct((mc, nh2, LANES), F32),
        ],
        scratch_shapes=[pltpu.VMEM((r, W_B), F32)] * 3,
        compiler_params=_cparams(1),
        name="local_conv_pool",
    )(z, z, z, z, z, z, bat, conv_w_l, pool_w_l, pool_scale_l.reshape(1, W_C),
      a_log_l.reshape(1, nh2), a_log_l.reshape(nh2, 1), dt_l.reshape(1, nh2), dt_l.reshape(nh2, 1))


def _scan_kernel(par, b_lat, n_ctx_ch, *refs):
    c, nh = DELTA_CHUNK, N_HEADS_B
    n_slot = par + b_lat
    ins = refs[:n_slot * 8]
    s0_ref = refs[n_slot * 8]
    of_ctx, ob_ctx, of_lat, ob_lat, sfin_ref, s_scr = refs[n_slot * 8 + 1:]
    step = pl.program_id(0)
    cc = step % n_ctx_ch

    @pl.when(cc == 0)
    def _():
        s_scr[0:par] = jnp.zeros((par,) + s_scr.shape[1:], F32)

    @pl.when(step == 0)
    def _():
        s_scr[par:] = s0_ref[...]

    for slot in range(n_slot):
        outs = (of_ctx, ob_ctx, slot) if slot < par else (of_lat, ob_lat, slot - par)
        chains = []
        for d in range(2):
            l1_ref, l2_ref, u_ref, e_ref = ins[(slot * 2 + d) * 4:(slot * 2 + d) * 4 + 4]
            for h in range(nh):
                chains.append(dict(d=d, h=h, idx=d * nh + h, l1=l1_ref, l2=l2_ref, u=u_ref, e=e_ref))
        for ch in chains:
            ch["s"] = s_scr[slot, ch["idx"]]
            ch["r1"] = _dot(ch["l1"][ch["h"]], ch["s"].astype(BF16))
        for ch in chains:
            hs = slice(ch["h"] * DV_B, (ch["h"] + 1) * DV_B)
            v_new = ch["u"][:, hs] - ch["r1"][:c]
            ch["r2"] = _dot(ch["l2"][ch["h"]], v_new.astype(BF16))
        for ch in chains:
            hs = slice(ch["h"] * DV_B, (ch["h"] + 1) * DV_B)
            outs[ch["d"]][outs[2], :, hs] = ch["r1"][c:] + ch["r2"][:c]
            e = ch["e"][ch["idx"]:ch["idx"] + 1, :]
            s_scr[slot, ch["idx"]] = ch["s"] * e + ch["r2"][c:]

    @pl.when(cc == n_ctx_ch - 1)
    def _():
        sfin_ref[...] = s_scr[0:par]


def _delta_scan(u, l1, l2, e, s0_lat, dims):
    b_ctx, t_ctx, b_lat, t_lat = dims
    c, nh = DELTA_CHUNK, N_HEADS_B
    n_ctx_ch, n_lat_ch = t_ctx // c, t_lat // c
    par = b_ctx * n_ctx_ch // n_lat_ch
    assert par >= 1 and par * n_lat_ch == b_ctx * n_ctx_ch and b_ctx % par == 0
    mc = l1.shape[1]
    ctx_chunks = b_ctx * n_ctx_ch
    u4 = u.reshape(2, mc, c, W_B)

    def chunk_of(slot, d):
        if slot < par:
            return lambda s: ((s // n_ctx_ch) * par + slot) * n_ctx_ch + (
                (n_ctx_ch - 1 - s % n_ctx_ch) if d else s % n_ctx_ch)
        q = slot - par
        return lambda s: ctx_chunks + q * n_lat_ch + ((n_lat_ch - 1 - s) if d else s)

    in_specs, args = [], []
    for slot in range(par + b_lat):
        for d in range(2):
            cg = chunk_of(slot, d)
            in_specs += [
                pl.BlockSpec((None, None, nh, 2 * c, DV_B), lambda s, cg=cg, d=d: (d, cg(s), 0, 0, 0)),
                pl.BlockSpec((None, None, nh, c + DK_B, c), lambda s, cg=cg, d=d: (d, cg(s), 0, 0, 0)),
                pl.BlockSpec((None, None, c, W_B), lambda s, cg=cg, d=d: (d, cg(s), 0, 0)),
                pl.BlockSpec((None, 2 * nh, LANES), lambda s, cg=cg: (cg(s), 0, 0)),
            ]
            args += [l1, l2, u4, e]
    in_specs.append(pl.BlockSpec((b_lat, 2 * nh, DK_B, DV_B), lambda s: (0, 0, 0, 0)))
    args.append(s0_lat)

    ctx_o = jax.ShapeDtypeStruct((b_ctx // par, par, n_ctx_ch, c, W_B), F32)
    lat_o = jax.ShapeDtypeStruct((b_lat, n_lat_ch, c, W_B), F32)
    cf = lambda s: s % n_ctx_ch
    of_ctx, ob_ctx, of_lat, ob_lat, s_fin = pl.pallas_call(
        functools.partial(_scan_kernel, par, b_lat, n_ctx_ch),
        grid=(n_lat_ch,),
        in_specs=in_specs,
        out_specs=[
            pl.BlockSpec((None, par, None, c, W_B), lambda s: (s // n_ctx_ch, 0, cf(s), 0, 0)),
            pl.BlockSpec((None, par, None, c, W_B), lambda s: (s // n_ctx_ch, 0, n_ctx_ch - 1 - cf(s), 0, 0)),
            pl.BlockSpec((b_lat, None, c, W_B), lambda s: (0, s, 0, 0)),
            pl.BlockSpec((b_lat, None, c, W_B), lambda s: (0, n_lat_ch - 1 - s, 0, 0)),
            pl.BlockSpec((par, 2 * nh, DK_B, DV_B), lambda s: (s // n_ctx_ch, 0, 0, 0)),
        ],
        out_shape=[ctx_o, ctx_o, lat_o, lat_o, jax.ShapeDtypeStruct((b_ctx, 2 * nh, DK_B, DV_B), F32)],
        scratch_shapes=[pltpu.VMEM((par + b_lat, 2 * nh, DK_B, DV_B), F32)],
        compiler_params=_cparams(1),
        name="delta_scan",
    )(*args)
    n_ctx, n_lat = b_ctx * t_ctx, b_lat * t_lat
    return (of_ctx.reshape(n_ctx, W_B), ob_ctx.reshape(n_ctx, W_B), of_lat.reshape(n_lat, W_B),
            ob_lat.reshape(n_lat, W_B), s_fin)


def _route_rows(sel, scores):
    g = EXPERTS_PER_GROUP
    gscore = []
    for gi in range(N_EXPERT_GROUPS):
        a, b, c, d = sel[gi * g:(gi + 1) * g]
        hi1, lo1 = jnp.maximum(a, b), jnp.minimum(a, b)
        hi2, lo2 = jnp.maximum(c, d), jnp.minimum(c, d)
        gscore.append(jnp.maximum(hi1, hi2) + jnp.maximum(jnp.minimum(hi1, hi2), jnp.maximum(lo1, lo2)))
    best = jnp.zeros_like(gscore[0], dtype=jnp.int32)
    bestv = gscore[0]
    for gi in range(1, N_EXPERT_GROUPS):
        upd = gscore[gi] > bestv
        best = jnp.where(upd, gi, best)
        bestv = jnp.where(upd, gscore[gi], bestv)

    def in_best(rows, j):
        out = rows[j]
        for gi in range(1, N_EXPERT_GROUPS):
            out = jnp.where(best == gi, rows[gi * g + j], out)
        return out

    e_sel = [in_best(sel, j) for j in range(g)]
    e_sc = [in_best(scores, j) for j in range(g)]

    def first_argmax(vals):
        bi, bv, bs = jnp.zeros_like(best), vals[0], e_sc[0]
        for j in range(1, g):
            upd = vals[j] > bv
            bi = jnp.where(upd, j, bi)
            bv = jnp.where(upd, vals[j], bv)
            bs = jnp.where(upd, e_sc[j], bs)
        return bi, bs

    i0, s0 = first_argmax(e_sel)
    i1, s1 = first_argmax([jnp.where(i0 == j, -jnp.inf, e_sel[j]) for j in range(g)])
    tot = s0 + s1
    return best * g + i0, best * g + i1, s0 / tot, s1 / tot


def _mix_kernel(n_ctx, x_ref, mod_ref, oac_ref, oal_ref, ofc_ref, obc_ref, ofl_ref, obl_ref,
                gb_ref, oc_ref, g0_ref, g1_ref, g2_ref,
                dn_ref, n2_ref, wpa_ref, wpb_ref, wpc_ref, wout_ref, wr_ref, br_ref,
                x1_ref, h2_ref, idx_ref, wt_ref):
    d = x_ref.shape[1]
    is_ctx = pl.program_id(0) * x_ref.shape[0] < n_ctx
    dn = dn_ref[...]
    o = jnp.where(is_ctx, ofc_ref[...] + obc_ref[...], ofl_ref[...] + obl_ref[...])
    o_a = jnp.where(is_ctx, oac_ref[...], oal_ref[...])
    gate = _silu(gb_ref[...])
    parts = []
    for h in range(N_HEADS_B):
        sl = slice(h * DV_B, (h + 1) * DV_B)
        parts.append((_rms(o[:, sl]) * dn * gate[:, sl]).astype(BF16))
    o_b = jnp.concatenate(parts, axis=-1)
    mixed = (_sigmoid(g0_ref[...]) * _dot(o_a, wpa_ref[...])
             + _sigmoid(g1_ref[...]) * _dot(o_b, wpb_ref[...])
             + _sigmoid(g2_ref[...]) * _dot(oc_ref[...], wpc_ref[...]))
    gate1 = mod_ref[:, 2 * d:3 * d]
    x1 = x_ref[...] + gate1 * _dot(mixed.astype(BF16), wout_ref[...])
    x1_ref[...] = x1
    shift2 = mod_ref[:, 3 * d:4 * d]
    scale2 = mod_ref[:, 4 * d:5 * d]
    h2 = _rms(x1) * n2_ref[...] * (1.0 + scale2) + shift2
    h2_ref[...] = h2
    h_hi = h2.astype(BF16)
    h_lo = (h2 - h_hi.astype(F32)).astype(BF16)
    w_r = wr_ref[...]
    w_hi = w_r.astype(BF16)
    w_lo = (w_r - w_hi.astype(F32)).astype(BF16)
    both = _dot_nt(jnp.concatenate([w_hi, w_lo], axis=0), h_hi)
    sc = _sigmoid(both[:N_EXPERTS] + both[N_EXPERTS:] + _dot_nt(w_hi, h_lo))
    sel = sc + br_ref[...]
    i0, i1, w0, w1 = _route_rows([sel[e:e + 1] for e in range(N_EXPERTS)],
                                 [sc[e:e + 1] for e in range(N_EXPERTS)])
    idx_ref[0:1, :] = i0
    idx_ref[1:2, :] = i1
    wt_ref[0:1, :] = w0
    wt_ref[1:2, :] = w1


def _mix(x, mod_l, oa, delta_o, z, oc, delta_norm_l, norm2_l, wpa, wpb, wpc, wout, layer, wr_t, br,
         cond_of_row):
    m, d = x.shape
    r = ROW_TILE
    n_ctx = cond_of_row.n_ctx
    nct = n_ctx // r
    c_gb = OFF_GATE_B // W_B
    once = dict(pipeline_mode=pl.Buffered(1))
    full = lambda shape: pl.BlockSpec(shape, lambda i: (0,) * len(shape), **once)
    of_layer = lambda shape: pl.BlockSpec((None,) + shape, lambda i: (layer,) + (0,) * len(shape), **once)
    ctx_rows = lambda w: pl.BlockSpec((r, w), lambda i: (jnp.minimum(i, nct - 1), 0))
    lat_rows = lambda w: pl.BlockSpec((r, w), lambda i: (jnp.maximum(i - nct, 0), 0))
    of_ctx, ob_ctx, of_lat, ob_lat = delta_o
    return pl.pallas_call(
        functools.partial(_mix_kernel, n_ctx),
        grid=(m // r,),
        in_specs=[
            pl.BlockSpec((r, d), lambda i: (i, 0)),
            pl.BlockSpec((None, 1, 6 * d), lambda i: (cond_of_row(i * r), 0, 0)),
            ctx_rows(W_A), lat_rows(W_A),
            ctx_rows(W_B), ctx_rows(W_B), lat_rows(W_B), lat_rows(W_B),
            pl.BlockSpec((r, W_B), lambda i: (i, c_gb)),
            pl.BlockSpec((r, W_C), lambda i: (i, 0)),
            pl.BlockSpec((r, d), lambda i: (i, 0)),
            pl.BlockSpec((r, d), lambda i: (i, 1)),
            pl.BlockSpec((r, d), lambda i: (i, 2)),
            full((1, DV_B)), full((1, d)),
            of_layer((W_A, d)), of_layer((W_B, d)), of_layer((W_C, d)), of_layer((d, d)),
            full((N_EXPERTS, d)), full((N_EXPERTS, 1)),
        ],
        out_specs=[
            pl.BlockSpec((r, d), lambda i: (i, 0)),
            pl.BlockSpec((r, d), lambda i: (i, 0)),
            pl.BlockSpec((TOP_K, r), lambda i: (0, i)),
            pl.BlockSpec((TOP_K, r), lambda i: (0, i)),
        ],
        out_shape=[
            jax.ShapeDtypeStruct((m, d), F32),
            jax.ShapeDtypeStruct((m, d), F32),
            jax.ShapeDtypeStruct((TOP_K, m), jnp.int32),
            jax.ShapeDtypeStruct((TOP_K, m), F32),
        ],
        compiler_params=_cparams(1),
        name="mix_route",
    )(x, mod_l, oa[0], oa[1], of_ctx, ob_ctx, of_lat, ob_lat, z, oc, z, z, z,
      delta_norm_l.reshape(1, DV_B), norm2_l.reshape(1, d), wpa, wpb, wpc, wout, wr_t, br)


def _dispatch_meta(idx_t, bm):
    m = idx_t.shape[1]
    nk = m * TOP_K
    flat_e = idx_t.T.reshape(nk)
    experts = jnp.arange(N_EXPERTS, dtype=jnp.int32)[None, :]
    counts = jnp.sum((flat_e[:, None] == experts).astype(jnp.int32), axis=0)
    order = jnp.argsort(flat_e, stable=True).astype(jnp.int32)
    start = jnp.cumsum(counts) - counts
    padded = (counts + bm - 1) // bm * bm
    pad_end = jnp.cumsum(padded)
    pad_start = pad_end - padded
    nb = -(-(nk + N_EXPERTS * (bm - 1)) // bm)
    p = jnp.arange(nb * bm, dtype=jnp.int32)
    e_of_p = jnp.minimum(jnp.sum((pad_end[None, :] <= p[:, None]).astype(jnp.int32), axis=1), N_EXPERTS - 1)
    pick = (e_of_p[:, None] == experts).astype(jnp.int32)
    rank = p - jnp.sum(pick * pad_start[None, :], axis=1)
    real = rank < jnp.sum(pick * counts[None, :], axis=1)
    assign = order[jnp.clip(jnp.sum(pick * start[None, :], axis=1) + rank, 0, nk - 1)]
    row_tok = jnp.where(real, assign // TOP_K, 0)
    spare = nk + ((p // bm) % 2) * bm + p % bm
    row_dst = jnp.where(real, (assign % TOP_K) * m + assign // TOP_K, spare)
    prime = nk + jnp.arange(2 * bm, dtype=jnp.int32)
    n_used = (pad_end[-1] // bm).astype(jnp.int32)
    blk_e = e_of_p[::bm]
    n_chunks = 3 * MOE_MAT_CHUNKS
    e_ids = experts[0]
    n_blk = padded // bm
    has = n_blk > 0
    later = lax.cummin(jnp.where(has, e_ids, N_EXPERTS)[::-1])[::-1]
    nxt_of_e = jnp.concatenate([later[1:], jnp.full((1,), N_EXPERTS, jnp.int32)])
    set_of_e = (jnp.cumsum(has.astype(jnp.int32)) - 1) % 2
    b = jnp.arange(nb, dtype=jnp.int32)
    of_blk = (blk_e[:, None] == experts).astype(jnp.int32)
    at_blk = lambda per_expert: jnp.sum(of_blk * per_expert[None, :], axis=1)
    k = b - at_blk(pad_start // bm)
    n_e = jnp.maximum(at_blk(n_blk), 1)
    nxt_b = at_blk(nxt_of_e)
    streams = (nxt_b < N_EXPERTS) & (b < n_used)
    c0 = jnp.where(streams, n_chunks * k // n_e, 0)
    c1 = jnp.where(streams, n_chunks * (k + 1) // n_e, 0)
    nxt_e = jnp.where(streams, nxt_b, 0)
    return (blk_e, n_used.reshape(1), row_tok, jnp.concatenate([prime, row_dst]),
            at_blk(set_of_e), nxt_e, c0, c1)


def _moe_kernel(layer, blk_e_ref, n_used_ref, row_tok_ref, row_dst_ref, set_ref, nxt_ref, c0_ref, c1_ref,
                h2_hbm, wg_hbm, wu_hbm, wd_hbm, out_hbm,
                xbuf, ybuf, wg_s, wu_s, wd_s, st_gu, st_d, gate_scr, act_scr, gsem, ssem, wsem_gu, wsem_d):
    i = pl.program_id(0)
    bm = xbuf.shape[1]
    n_used = n_used_ref[0]
    n_mat = MOE_MAT_CHUNKS
    n_stage = st_gu.shape[0]
    rg, rd = st_gu.shape[1], st_d.shape[1]

    def chunk_ops(c, j, e, s):
        cg, cu, cd = c, c - n_mat, c - 2 * n_mat
        rows_g = pl.ds(pl.multiple_of(cg * rg, rg), rg)
        rows_u = pl.ds(pl.multiple_of(cu * rg, rg), rg)
        rows_d = pl.ds(pl.multiple_of(cd * rd, 2 * SUBLANES), rd)

        def to_set(dst, rows, stage):
            def convert():
                dst[s, rows, :] = stage[j].astype(BF16)
            return convert

        return (
            (c < n_mat, lambda: pltpu.make_async_copy(wg_hbm.at[layer, e, rows_g, :], st_gu.at[j], wsem_gu.at[j]),
             to_set(wg_s, rows_g, st_gu)),
            ((c >= n_mat) & (c < 2 * n_mat),
             lambda: pltpu.make_async_copy(wu_hbm.at[layer, e, rows_u, :], st_gu.at[j], wsem_gu.at[j]),
             to_set(wu_s, rows_u, st_gu)),
            (c >= 2 * n_mat, lambda: pltpu.make_async_copy(wd_hbm.at[layer, e, rows_d, :], st_d.at[j], wsem_d.at[j]),
             to_set(wd_s, rows_d, st_d)),
        )

    def round_chunks(r, c_lo, c_hi):
        first = c_lo + r * n_stage
        return first, jnp.clip(c_hi - first, 0, n_stage)

    def issue_round(r, c_lo, c_hi, e, s):
        first, n = round_chunks(r, c_lo, c_hi)

        def body(j, carry):
            for pred, copy, _ in chunk_ops(first + j, j, e, s):
                pl.when(pred)(lambda copy=copy: copy().start(priority=1))
            return carry
        lax.fori_loop(0, n, body, 0)

    def finish_round(r, c_lo, c_hi, e, s):
        first, n = round_chunks(r, c_lo, c_hi)

        def body(j, carry):
            for pred, copy, convert in chunk_ops(first + j, j, e, s):
                @pl.when(pred)
                def _(copy=copy, convert=convert):
                    copy().wait()
                    convert()
            return carry
        lax.fori_loop(0, n, body, 0)

    def sync_rounds(r_lo, c_lo, c_hi, e, s):
        def body(r, carry):
            issue_round(r, c_lo, c_hi, e, s)
            finish_round(r, c_lo, c_hi, e, s)
            return carry
        n_rounds = (jnp.maximum(c_hi - c_lo, 0) + n_stage - 1) // n_stage
        lax.fori_loop(r_lo, jnp.maximum(n_rounds, r_lo), body, 0)

    def gather_row(blk, slot, r):
        tok = row_tok_ref[blk * bm + r]
        pltpu.make_async_copy(h2_hbm.at[pl.ds(tok, 1)], xbuf.at[slot, pl.ds(r, 1)], gsem.at[slot]).start()

    def scatter_row(tab_blk, slot, r):
        dst = row_dst_ref[tab_blk * bm + r]
        pltpu.make_async_copy(ybuf.at[slot, pl.ds(r, 1)], out_hbm.at[pl.ds(dst, 1)], ssem.at[slot]).start()

    def looped(fn, *args):
        def body(r, carry):
            fn(*args, r)
            return carry
        lax.fori_loop(0, bm, body, 0, unroll=8)

    def wait_gather(slot):
        pltpu.make_async_copy(h2_hbm.at[pl.ds(0, bm)], xbuf.at[slot], gsem.at[slot]).wait()

    def wait_scatter(slot):
        pltpu.make_async_copy(ybuf.at[slot], out_hbm.at[pl.ds(0, bm)], ssem.at[slot]).wait()

    @pl.when(i == 0)
    def _():
        ybuf[...] = jnp.zeros_like(ybuf)
        looped(gather_row, 0, 0)
        looped(scatter_row, 0, 0)
        sync_rounds(0, 0, 3 * n_mat, blk_e_ref[0], set_ref[0])

    def compute_block(slot):
        other = 1 - slot
        nxt = jnp.minimum(i + 1, n_used - 1)
        s, e_next, c_lo, c_hi = set_ref[i], nxt_ref[i], c0_ref[i], c1_ref[i]
        stream = (c_lo, c_hi, e_next, 1 - s)
        wait_gather(slot)
        issue_round(0, *stream)
        for r in range(bm):
            gather_row(nxt, other, r)
            scatter_row(i + 1, other, r)
        xb = xbuf[slot].astype(BF16)
        gate_scr[...] = _dot(xb, wg_s[s])
        finish_round(0, *stream)
        issue_round(1, *stream)
        act_scr[...] = (_silu(gate_scr[...]) * _dot(xb, wu_s[s])).astype(BF16)
        finish_round(1, *stream)
        wait_scatter(slot)
        issue_round(2, *stream)
        ybuf[slot] = _dot(act_scr[...], wd_s[s])
        finish_round(2, *stream)
        sync_rounds(3, *stream)

    for parity in range(2):
        pl.when((i < n_used) & (i % 2 == parity))(functools.partial(compute_block, parity))

    @pl.when(i == n_used)
    def _():
        looped(scatter_row, i + 1, (i - 1) % 2)
        wait_gather(i % 2)
        wait_scatter(0)
        wait_scatter(1)


def _moe(h2, meta, wg, wu, wd, layer):
    m, d = h2.shape
    bm = MOE_ROWS
    nb = meta[0].shape[0]
    f = wg.shape[3]
    rg, rd = d // MOE_MAT_CHUNKS, f // MOE_MAT_CHUNKS
    assert rg % (2 * SUBLANES) == 0 and rd % (2 * SUBLANES) == 0
    any_spec = pl.BlockSpec(memory_space=pl.ANY)
    dma = pltpu.SemaphoreType.DMA
    return pl.pallas_call(
        functools.partial(_moe_kernel, layer),
        grid_spec=pltpu.PrefetchScalarGridSpec(
            num_scalar_prefetch=len(meta),
            grid=(nb + 1,),
            in_specs=[any_spec] * 4,
            out_specs=any_spec,
            scratch_shapes=[
                pltpu.VMEM((2, bm, d), F32), pltpu.VMEM((2, bm, d), F32),
                pltpu.VMEM((2, d, f), BF16), pltpu.VMEM((2, d, f), BF16), pltpu.VMEM((2, f, d), BF16),
                pltpu.VMEM((MOE_STAGE_SLOTS, rg, f), F32), pltpu.VMEM((MOE_STAGE_SLOTS, rd, d), F32),
                pltpu.VMEM((bm, f), F32), pltpu.VMEM((bm, f), BF16),
                dma((2,)), dma((2,)), dma((MOE_STAGE_SLOTS,)), dma((MOE_STAGE_SLOTS,)),
            ],
        ),
        out_shape=jax.ShapeDtypeStruct((TOP_K * m + 2 * bm, d), F32),
        compiler_params=_cparams(1, MOE_VMEM_LIMIT),
        name="moe_experts",
    )(*meta, h2, wg, wu, wd)


def _combine_kernel(n_ctx, final, y0_ref, y1_ref, x1_ref, mod_ref, w_ref, g_ref, modn_ref, o_ref, o2_ref):
    d = x1_ref.shape[1]
    f = w_ref[:, 0:1] * y0_ref[...] + w_ref[:, 1:2] * y1_ref[...]
    x2 = x1_ref[...] + mod_ref[:, 5 * d:6 * d] * f
    if final:
        y = _rms(x2) * g_ref[...]
        is_ctx = pl.program_id(0) * x1_ref.shape[0] < n_ctx

        @pl.when(is_ctx)
        def _():
            o_ref[...] = y

        @pl.when(jnp.logical_not(is_ctx))
        def _():
            o2_ref[...] = y
    else:
        o_ref[...] = x2
        o2_ref[...] = _modulated_norm(x2, g_ref[...], modn_ref[...], d).astype(BF16)


def _combine(y2, x1, mod_l, w_col, gain, mod_next, final, cond_of_row):
    m, d = x1.shape
    r = ROW_TILE
    k1 = m // r
    n_ctx = cond_of_row.n_ctx
    nct = n_ctx // r
    cond_row = pl.BlockSpec((None, 1, 6 * d), lambda i: (cond_of_row(i * r), 0, 0))
    tile = pl.BlockSpec((r, d), lambda i: (i, 0))
    if final:
        out_specs = [pl.BlockSpec((r, d), lambda i: (jnp.minimum(i, nct - 1), 0)),
                     pl.BlockSpec((r, d), lambda i: (jnp.maximum(i - nct, 0), 0))]
        out_shape = [jax.ShapeDtypeStruct((n_ctx, d), F32), jax.ShapeDtypeStruct((m - n_ctx, d), F32)]
    else:
        out_specs = [tile, tile]
        out_shape = [jax.ShapeDtypeStruct((m, d), F32), jax.ShapeDtypeStruct((m, d), BF16)]
    return pl.pallas_call(
        functools.partial(_combine_kernel, n_ctx, final),
        grid=(m // r,),
        in_specs=[
            tile,
            pl.BlockSpec((r, d), lambda i: (k1 + i, 0)),
            tile, cond_row,
            pl.BlockSpec((r, TOP_K), lambda i: (i, 0)),
            pl.BlockSpec((1, d), lambda i: (0, 0)),
            cond_row,
        ],
        out_specs=out_specs,
        out_shape=out_shape,
        compiler_params=_cparams(1),
        name="moe_combine",
    )(y2, y2, x1, mod_l, w_col, gain.reshape(1, d), mod_next)


class _CondOfRow:
    def __init__(self, n_ctx, t_lat):
        self.n_ctx, self.t_lat = n_ctx, t_lat
        self.tile_in = _pick_tile((1024, 512, 256), n_ctx, t_lat)

    def __call__(self, row0):
        return jnp.where(row0 < self.n_ctx, 0, 1 + (row0 - self.n_ctx) // self.t_lat)


def kernel(x_prompt, x_sample, cache_k, cache_v, state_delta, c, c_ctx, w_mod, b_mod, norm1, norm2, w_in,
           lambda_qk, subln, conv_w, a_log, dt_bias, delta_norm, pool_w, pool_scale, w_pa, w_pb, w_pc,
           w_out, w_router, b_router, w_gate, w_up, w_down, norm_final):
    b_ctx, t_ctx, d = x_prompt.shape
    b_lat, t_lat, _ = x_sample.shape
    depth = w_in.shape[0]
    past = cache_k.shape[2]
    n_ctx, n_lat = b_ctx * t_ctx, b_lat * t_lat
    m = n_ctx + n_lat
    assert d == D_MODEL and w_in.shape[2] == D_IN and 1 + b_lat <= SUBLANES
    assert t_ctx % ROW_TILE == 0 and t_lat % ROW_TILE == 0 and n_ctx % t_lat == 0
    cond_of_row = _CondOfRow(n_ctx, t_lat)
    dims = (b_ctx, t_ctx, b_lat, t_lat)

    def seq(row0):
        is_ctx = row0 < n_ctx
        return (jnp.where(is_ctx, row0 % t_ctx, (row0 - n_ctx) % t_lat), jnp.where(is_ctx, t_ctx, t_lat))

    x = jnp.concatenate([x_prompt.reshape(n_ctx, d), x_sample.reshape(n_lat, d)], axis=0)
    cond8 = jnp.zeros((SUBLANES, d), F32).at[0].set(c_ctx).at[1:1 + b_lat].set(c)
    mod = _modulation(cond8, w_mod, b_mod).reshape(depth, SUBLANES, 1, 6 * d)
    rope_tabs = _rope_tables(t_lat)
    wr_t = w_router.T
    br = b_router.reshape(N_EXPERTS, 1)
    ck = cache_k.reshape(b_lat, depth, past, QK_A)
    cv = cache_v.reshape(b_lat, depth, past, W_A)
    s_lat = state_delta.reshape(b_lat, depth, 2 * N_HEADS_B, DK_B, DV_B)

    new_k, new_v, new_s = [], [], []
    h = _norm_mod(x, mod[0], norm1[0], cond_of_row)
    w_in_t = jnp.swapaxes(w_in, 1, 2)
    w_proj = [w.astype(BF16) for w in (w_pa, w_pb, w_pc, w_out)]
    for l in range(depth):
        lam_init = 0.8 - 0.6 * math.exp(-0.3 * l)
        last = l == depth - 1
        z = _in_proj(h, w_in_t, l, cond_of_row.tile_in)
        oa, k_l, v_l = _attention(z, lambda_qk[l], subln[l], ck[:, l], cv[:, l], rope_tabs, lam_init, dims)
        new_k.append(k_l.reshape(b_ctx, t_ctx, N_HEADS_A, 2 * DK_A))
        new_v.append(v_l.reshape(b_ctx, t_ctx, N_HEADS_A, DV_A))
        bat = z[:, OFF_BA:OFF_BA + 4 * N_HEADS_B].T
        oc, u, l1, l2, e = _local(z, bat, conv_w[l], pool_w[l], pool_scale[l], a_log[l], dt_bias[l], seq)
        *delta_o, s_ctx = _delta_scan(u, l1, l2, e, s_lat[:, l], dims)
        new_s.append(s_ctx.reshape(b_ctx, 2, N_HEADS_B, DK_B, DV_B))

        x1, h2, idx_t, wt_t = _mix(x, mod[l], oa, delta_o, z, oc, delta_norm[l], norm2[l], *w_proj, l,
                                   wr_t, br, cond_of_row)
        y2 = _moe(h2, _dispatch_meta(idx_t, MOE_ROWS), w_gate, w_up, w_down, l)
        if last:
            y_ctx, y_lat = _combine(y2, x1, mod[l], wt_t.T, norm_final, mod[l], True, cond_of_row)
        else:
            x, h = _combine(y2, x1, mod[l], wt_t.T, norm1[l + 1], mod[l + 1], False, cond_of_row)

    y_prompt = y_ctx.reshape(b_ctx, t_ctx, d)
    y_sample = y_lat.reshape(b_lat, t_lat, d)
    return (y_prompt, y_sample, jnp.stack(new_k, axis=1), jnp.stack(new_v, axis=1), jnp.stack(new_s, axis=1))
```

```python
import functools
import math

import jax
import jax.numpy as jnp
from jax import lax
from jax.experimental import pallas as pl
from jax.experimental.pallas import tpu as pltpu

F32 = jnp.float32
BF16 = jnp.bfloat16
HIGHEST = lax.Precision.HIGHEST

D_MODEL = 2048
GRID_W = 64
EPS = 1e-6
N_HEADS_A = 8
DK_A = 64
DV_A = 128
QK_A = N_HEADS_A * 2 * DK_A
W_A = N_HEADS_A * DV_A
ROPE_BASE = 10000.0
ROPE_AXIS = DK_A // 2
N_HEADS_B = 4
DK_B = 128
DV_B = 128
W_B = N_HEADS_B * DV_B
DELTA_CHUNK = 64
POOL_WINDOWS = (2, 4, 8, 16)
POOL_GROUP_W = 128
W_C = len(POOL_WINDOWS) * POOL_GROUP_W
N_BRANCH = 3
N_EXPERTS = 16
N_EXPERT_GROUPS = 4
EXPERTS_PER_GROUP = N_EXPERTS // N_EXPERT_GROUPS
TOP_K = 2
D_FF = 1408

SRC_QKV_B = 2 * QK_A + W_A
SRC_GATE_B = SRC_QKV_B + 3 * W_B
SRC_BA = SRC_GATE_B + W_B
SRC_POOL = SRC_BA + 4 * N_HEADS_B
SRC_BRANCH = SRC_POOL + W_C
D_IN = SRC_BRANCH + N_BRANCH * D_MODEL
LANES = 128
SUBLANES = 8
IN_TILE_N = 1024
IN_HALF_N = IN_TILE_N // 2
OFF_RAW = N_BRANCH * D_MODEL
RAW_W = -(-SRC_BRANCH // IN_TILE_N) * IN_TILE_N
NP_IN = OFF_RAW + RAW_W
OFF_Q = OFF_RAW
OFF_K = OFF_Q + QK_A
OFF_V = OFF_K + QK_A
OFF_QKV_B = OFF_RAW + SRC_QKV_B
OFF_GATE_B = OFF_RAW + SRC_GATE_B
OFF_BA = OFF_RAW + SRC_BA
BRANCH_SHIFT = SRC_BRANCH % LANES
assert SRC_BA % IN_TILE_N == 0 and (SRC_BRANCH - BRANCH_SHIFT) % IN_HALF_N == 0
assert SRC_POOL - SRC_BA == BRANCH_SHIFT and OFF_RAW % IN_TILE_N == 0

ROW_TILE = 256
MOE_ROWS = 256
DELTA_CHUNKS_PER_GROUP = 2
SCAN_SLOTS_PER_GROUP = 2
INV_BASE_BLOCK = 16
MOE_MAT_CHUNKS = 8
MOE_STAGE_SLOTS = 2
MOE_VMEM_LIMIT = 60 * 1024 * 1024
VMEM_LIMIT = 56 * 1024 * 1024


def _cparams(n_axes, vmem=VMEM_LIMIT):
    return pltpu.CompilerParams(dimension_semantics=("arbitrary",) * n_axes, vmem_limit_bytes=vmem)


def _dot(a, b, precision=None):
    return jnp.dot(a, b, preferred_element_type=F32, precision=precision)


def _dot_nt(a, b, precision=None):
    return lax.dot_general(a, b, (((1,), (1,)), ((), ())), preferred_element_type=F32, precision=precision)


def _bdot(a, b):
    return _dot(a.astype(BF16), b.astype(BF16))


def _bdot_nt(a, b):
    return _dot_nt(a.astype(BF16), b.astype(BF16))


def _sigmoid(x):
    return 1.0 / (1.0 + jnp.exp(-x))


def _silu(x):
    return x * _sigmoid(x)


def _rms(x, eps=EPS):
    return x * lax.rsqrt(jnp.mean(x * x, axis=-1, keepdims=True) + eps)


def _pick_tile(cands, *dims):
    for t in cands:
        if all(d % t == 0 for d in dims):
            return t
    raise ValueError(f"no tile in {cands} divides {dims}")


def _mod_kernel(c_ref, w_ref, b_ref, o_ref):
    a = _silu(c_ref[...]).astype(BF16)
    o_ref[...] = _dot(a, w_ref[...].astype(BF16)) + b_ref[...]


def _modulation(cond8, w_mod, b_mod):
    depth, d, n6 = w_mod.shape
    tn = 1024
    return pl.pallas_call(
        _mod_kernel,
        grid=(depth, n6 // tn),
        in_specs=[
            pl.BlockSpec((SUBLANES, d), lambda l, n: (0, 0)),
            pl.BlockSpec((None, d, tn), lambda l, n: (l, 0, n)),
            pl.BlockSpec((None, 1, tn), lambda l, n: (l, 0, n)),
        ],
        out_specs=pl.BlockSpec((None, SUBLANES, tn), lambda l, n: (l, 0, n)),
        out_shape=jax.ShapeDtypeStruct((depth, SUBLANES, n6), F32),
        compiler_params=_cparams(2),
        name="adaln_mod",
    )(cond8, w_mod, b_mod.reshape(depth, 1, n6))


def _modulated_norm(x, g, mod, d):
    return _rms(x) * g * (1.0 + mod[:, d:2 * d]) + mod[:, 0:d]


def _norm_kernel(x_ref, mod_ref, g_ref, h_ref):
    h_ref[...] = _modulated_norm(x_ref[...], g_ref[...], mod_ref[...], x_ref.shape[1]).astype(BF16)


def _norm_mod(x, mod_l, norm_l, cond_of_row):
    m, d = x.shape
    r = ROW_TILE
    return pl.pallas_call(
        _norm_kernel,
        grid=(m // r,),
        in_specs=[
            pl.BlockSpec((r, d), lambda i: (i, 0)),
            pl.BlockSpec((None, 1, 6 * d), lambda i: (cond_of_row(i * r), 0, 0)),
            pl.BlockSpec((1, d), lambda i: (0, 0)),
        ],
        out_specs=pl.BlockSpec((r, d), lambda i: (i, 0)),
        out_shape=jax.ShapeDtypeStruct((m, d), BF16),
        compiler_params=_cparams(1),
        name="norm_mod",
    )(x, mod_l, norm_l.reshape(1, d))


def _in_kernel(h_ref, wt_ref, z_ref, w_scr):
    @pl.when(pl.program_id(1) == 0)
    def _():
        w_scr[...] = wt_ref[0].astype(BF16)

    z_ref[...] = _dot_nt(h_ref[...], w_scr[...])


def _in_proj(h, w_in_t, layer, tm):
    m, d = h.shape
    n_branch_tiles = OFF_RAW // IN_TILE_N
    assert SRC_BRANCH % SUBLANES == 0
    tile8 = IN_TILE_N // SUBLANES
    row0 = lambda n: SUBLANES * jnp.where(n < n_branch_tiles, SRC_BRANCH // SUBLANES + tile8 * n,
                                          tile8 * (n - n_branch_tiles))
    return pl.pallas_call(
        _in_kernel,
        grid=(NP_IN // IN_TILE_N, m // tm),
        in_specs=[
            pl.BlockSpec((tm, d), lambda n, i: (i, 0)),
            pl.BlockSpec((pl.Element(1), pl.Element(IN_TILE_N), pl.Element(d)),
                         lambda n, i: (layer, row0(n), 0)),
        ],
        out_specs=pl.BlockSpec((tm, IN_TILE_N), lambda n, i: (i, n)),
        out_shape=jax.ShapeDtypeStruct((m, NP_IN), F32),
        scratch_shapes=[pltpu.VMEM((IN_TILE_N, d), BF16)],
        compiler_params=_cparams(2),
        name="in_proj",
    )(h, w_in_t)


def _lambda(lq_ref, lam_init):
    lq = lq_ref[...]
    a = jnp.sum(lq[0:1] * lq[1:2], axis=-1, keepdims=True)
    b = jnp.sum(lq[2:3] * lq[3:4], axis=-1, keepdims=True)
    return jnp.exp(a) - jnp.exp(b) + lam_init


Q_SCALE = DK_A ** -0.5


def _diff_attn(qb, kb, vb, lam):
    probs = []
    for mp in range(2):
        s = _dot_nt(qb[:, mp * DK_A:(mp + 1) * DK_A], kb[:, mp * DK_A:(mp + 1) * DK_A])
        e = jnp.exp(s - jnp.max(s, axis=-1, keepdims=True))
        probs.append(e / jnp.sum(e, axis=-1, keepdims=True))
    pd = probs[0] - lam * probs[1]
    return _dot(pd.astype(BF16), vb)


def _attn_ctx_kernel(lam_init, lq_ref, sub_ref, q_ref, k_ref, v_ref, o_ref, knew_ref, vnew_ref):
    lam = _lambda(lq_ref, lam_init)
    for h in range(N_HEADS_A):
        hs = slice(h * LANES, (h + 1) * LANES)
        qb = (q_ref[:, hs] * Q_SCALE).astype(BF16)
        k, v = k_ref[:, hs], v_ref[:, hs]
        knew_ref[:, hs] = k
        vnew_ref[:, hs] = v
        o = _diff_attn(qb, k.astype(BF16), v.astype(BF16), lam)
        o_ref[:, hs] = (_rms(o) * sub_ref[...] * (1.0 - lam_init)).astype(BF16)


def _rope(x, cos, sin_signed):
    lane = lax.broadcasted_iota(jnp.int32, x.shape, 1)
    first_half = (lane % (ROPE_AXIS)) < (ROPE_AXIS // 2)
    partner = jnp.where(first_half, pltpu.roll(x, LANES - ROPE_AXIS // 2, 1), pltpu.roll(x, ROPE_AXIS // 2, 1))
    return x * cos + partner * sin_signed


def _attn_lat_kernel(lam_init, t_lat, lq_ref, sub_ref, q_ref, k_ref, v_ref, ck_ref, cv_ref,
                     cq_ref, sq_ref, ckk_ref, skk_ref, o_ref, k_scr, v_scr):
    @pl.when(pl.program_id(2) == 0)
    def _():
        k_scr[0:t_lat, :] = _rope(k_ref[...], ckk_ref[...], skk_ref[...]).astype(BF16)
        k_scr[t_lat:, :] = ck_ref[...].astype(BF16)
        v_scr[0:t_lat, :] = v_ref[...].astype(BF16)
        v_scr[t_lat:, :] = cv_ref[...].astype(BF16)

    lam = _lambda(lq_ref, lam_init)
    q = _rope(q_ref[...], cq_ref[...], sq_ref[...]) * Q_SCALE
    o = _diff_attn(q.astype(BF16), k_scr[...], v_scr[...], lam)
    o_ref[...] = (_rms(o) * sub_ref[...] * (1.0 - lam_init)).astype(BF16)


def _attention(z, lq_l, sub_l, cache_k_l, cache_v_l, rope_tabs, lam_init, dims):
    b_ctx, t_ctx, b_lat, t_lat = dims
    n_ctx = b_ctx * t_ctx
    past = cache_k_l.shape[1]
    cq, ck, cv = OFF_Q // LANES, OFF_K // LANES, OFF_V // LANES
    small = [pl.BlockSpec((4, DK_A), lambda *_: (0, 0)), pl.BlockSpec((1, DV_A), lambda *_: (0, 0))]
    sub2 = sub_l.reshape(1, DV_A)

    seq_blk = pl.BlockSpec((t_ctx, W_A), lambda b: (b, 0))
    oa_ctx, k_new, v_new = pl.pallas_call(
        functools.partial(_attn_ctx_kernel, lam_init),
        grid=(b_ctx,),
        in_specs=small + [
            pl.BlockSpec((t_ctx, QK_A), lambda b: (b, OFF_Q // QK_A)),
            pl.BlockSpec((t_ctx, QK_A), lambda b: (b, OFF_K // QK_A)),
            pl.BlockSpec((t_ctx, W_A), lambda b: (b, OFF_V // W_A)),
        ],
        out_specs=[seq_blk] * 3,
        out_shape=[jax.ShapeDtypeStruct((n_ctx, W_A), BF16), jax.ShapeDtypeStruct((n_ctx, QK_A), F32),
                   jax.ShapeDtypeStruct((n_ctx, W_A), F32)],
        compiler_params=_cparams(1),
        name="attn_ctx",
    )(lq_l, sub2, z, z, z)

    tq = _pick_tile((2048, 1024, 512, 256, 128), t_lat)
    nq = t_lat // tq
    rb_q = n_ctx // tq
    rb_k = n_ctx // t_lat
    cos_t, sin_t = rope_tabs
    oa_lat = pl.pallas_call(
        functools.partial(_attn_lat_kernel, lam_init, t_lat),
        grid=(b_lat, N_HEADS_A, nq),
        in_specs=small + [
            pl.BlockSpec((tq, LANES), lambda b, h, i: (rb_q + b * nq + i, cq + h)),
            pl.BlockSpec((t_lat, LANES), lambda b, h, i: (rb_k + b, ck + h)),
            pl.BlockSpec((t_lat, LANES), lambda b, h, i: (rb_k + b, cv + h)),
            pl.BlockSpec((None, past, LANES), lambda b, h, i: (b, 0, h)),
            pl.BlockSpec((None, past, LANES), lambda b, h, i: (b, 0, h)),
            pl.BlockSpec((tq, LANES), lambda b, h, i: (i, 0)),
            pl.BlockSpec((tq, LANES), lambda b, h, i: (i, 0)),
            pl.BlockSpec((t_lat, LANES), lambda b, h, i: (0, 0)),
            pl.BlockSpec((t_lat, LANES), lambda b, h, i: (0, 0)),
        ],
        out_specs=pl.BlockSpec((tq, LANES), lambda b, h, i: (b * nq + i, h)),
        out_shape=jax.ShapeDtypeStruct((b_lat * t_lat, W_A), BF16),
        scratch_shapes=[pltpu.VMEM((t_lat + past, LANES), BF16), pltpu.VMEM((t_lat + past, LANES), BF16)],
        compiler_params=_cparams(3),
        name="attn_lat",
    )(lq_l, sub2, z, z, z, cache_k_l, cache_v_l, cos_t, sin_t, cos_t, sin_t)
    return (oa_ctx, oa_lat), k_new, v_new


def _rope_tables(t_lat):
    rows = t_lat // GRID_W
    r = jnp.repeat(jnp.arange(rows), GRID_W).astype(F32)
    col = jnp.tile(jnp.arange(GRID_W), rows).astype(F32)
    inv = ROPE_BASE ** (-jnp.arange(0, ROPE_AXIS, 2, dtype=F32) / ROPE_AXIS)
    ang_r, ang_c = r[:, None] * inv, col[:, None] * inv
    cos64 = jnp.concatenate([jnp.cos(ang_r)] * 2 + [jnp.cos(ang_c)] * 2, axis=-1)
    sin64 = jnp.concatenate([-jnp.sin(ang_r), jnp.sin(ang_r), -jnp.sin(ang_c), jnp.sin(ang_c)], axis=-1)
    return jnp.tile(cos64, (1, 2)), jnp.tile(sin64, (1, 2))


def _log_decay(alpha, a_log, dt):
    x = alpha + dt
    return -jnp.exp(a_log) * (jnp.maximum(x, 0.0) + jnp.log(1.0 + jnp.exp(-jnp.abs(x))))


def _chunk_cumsum(g, axis):
    n = g.shape[axis]
    pos = lax.broadcasted_iota(jnp.int32, g.shape, axis) % DELTA_CHUNK
    f, b = g, g
    s = 1
    while s < DELTA_CHUNK:
        f = f + jnp.where(pos >= s, pltpu.roll(f, s, axis), 0.0)
        b = b + jnp.where(pos < DELTA_CHUNK - s, pltpu.roll(b, n - s, axis), 0.0)
        s *= 2
    return f, b


def _delta_prepare(q_scr, k_scr, v_scr, ba, bat, al_row, al_col, dt_row, dt_col,
                   u_ref, l1_ref, l2_ref, e_ref):
    c, nh = DELTA_CHUNK, N_HEADS_B
    rows_n = q_scr.shape[0]
    beta_all = _sigmoid(ba[:, 0:2 * nh])
    gf, gb = _chunk_cumsum(_log_decay(ba[:, 2 * nh:4 * nh], al_row[...], dt_row[...]), 0)
    gc_all = jnp.where(lax.broadcasted_iota(jnp.int32, gf.shape, 1) < nh, gf, gb)
    gtf, gtb = _chunk_cumsum(_log_decay(bat[2 * nh:4 * nh, :], al_col[...], dt_col[...]), 1)
    gct_all = jnp.where(lax.broadcasted_iota(jnp.int32, gtf.shape, 0) < nh, gtf, gtb)

    ri = lax.broadcasted_iota(jnp.int32, (c, c), 0)
    ci = lax.broadcasted_iota(jnp.int32, (c, c), 1)
    masks = ((ri >= ci, ri > ci), (ri <= ci, ri < ci))
    eye = (ri == ci).astype(F32)
    blk_sizes = [INV_BASE_BLOCK << i for i in range(int(math.log2(c // INV_BASE_BLOCK)) + 1)]
    same_blk = [(ri // s) == (ci // s) for s in blk_sizes]
    n_sq = int(math.log2(INV_BASE_BLOCK)) - 1

    n_chunks = rows_n // c
    for j0 in range(0, n_chunks, DELTA_CHUNKS_PER_GROUP):
        chains = []
        for j, d, h in [(j, d, h) for j in range(j0, j0 + DELTA_CHUNKS_PER_GROUP) for d in range(2)
                        for h in range(nh)]:
            rs = slice(j * c, (j + 1) * c)
            incl, strict = masks[d]
            last = 0 if d else c - 1
            idx = d * nh + h
            hs = slice(h * DK_B, (h + 1) * DK_B)
            gc = gc_all[rs, idx:idx + 1]
            gr = gct_all[idx:idx + 1, rs]
            ch = dict(j=j, rs=rs, d=d, h=h, idx=idx, hs=hs, strict=strict, gc=gc, g_last=gc[last:last + 1, :],
                      q=q_scr[rs, hs] * DK_B ** -0.5, k=k_scr[rs, hs], v=v_scr[rs, hs],
                      beta=beta_all[rs, idx:idx + 1],
                      decay=jnp.where(incl, jnp.exp(jnp.where(incl, gc - gr, 0.0)), 0.0))
            ch["kb"] = ch["k"] * ch["beta"]
            chains.append(ch)
        for ch in chains:
            scores = _bdot_nt(jnp.concatenate([ch["kb"], ch["q"]], axis=0), ch["k"])
            a = jnp.where(ch["strict"], -(scores[:c] * ch["decay"]), 0.0)
            ch["attn"] = scores[c:] * ch["decay"]
            ch["a"] = a
            ch["apow"] = jnp.where(same_blk[0], a, 0.0)
            ch["p"] = eye + ch["apow"]
        for _ in range(n_sq):
            for ch in chains:
                ch["apow"] = _bdot(ch["apow"], ch["apow"])
            for ch in chains:
                ch["p"] = ch["p"] + _bdot(ch["p"], ch["apow"])
        for lvl in range(1, len(same_blk)):
            for ch in chains:
                a_off = jnp.where(same_blk[lvl] & jnp.logical_not(same_blk[lvl - 1]), ch["a"], 0.0)
                ch["t"] = _bdot(ch["p"], a_off)
            for ch in chains:
                ch["p"] = ch["p"] + _bdot(ch["t"], ch["p"])
        for ch in chains:
            rhs = jnp.concatenate([ch["v"] * ch["beta"], ch["kb"] * jnp.exp(ch["gc"])], axis=-1)
            ch["sol"] = _bdot(ch["p"], rhs)
        for ch in chains:
            j, rs, d, h, hs = ch["j"], ch["rs"], ch["d"], ch["h"], ch["hs"]
            u_ref[d, rs, hs] = ch["sol"][:, :DV_B]
            l1_ref[d, j, h, 0:c, :] = ch["sol"][:, DV_B:].astype(BF16)
            l1_ref[d, j, h, c:2 * c, :] = (ch["q"] * jnp.exp(ch["gc"])).astype(BF16)
            l2_ref[d, j, h, 0:c, :] = ch["attn"].astype(BF16)
            l2_ref[d, j, h, c:, :] = (ch["k"] * jnp.exp(ch["g_last"] - ch["gc"])).T.astype(BF16)
            e_ref[j, ch["idx"]:ch["idx"] + 1, :] = jnp.broadcast_to(jnp.exp(ch["g_last"]), (1, LANES))


def _local_kernel(seq, cur_ref, prev_ref, next_ref, pcur_ref, pprev_ref, pnext_ref, bat_ref,
                  cw_ref, pw_ref, ps_ref, al_row, al_col, dt_row, dt_col,
                  oc_ref, u_ref, l1_ref, l2_ref, e_ref, q_scr, k_scr, v_scr):
    rows_n = cur_ref.shape[0]
    pos0, t_seq = seq(pl.program_id(0) * rows_n)
    has_prev = pos0 > 0
    has_next = pos0 + rows_n < t_seq
    rows = lax.broadcasted_iota(jnp.int32, (rows_n, 1), 0)

    x = cur_ref[...]
    prev_row = jnp.where(has_prev, prev_ref[SUBLANES - 1:SUBLANES, :], 0.0)
    next_row = jnp.where(has_next, next_ref[0:1, :], 0.0)
    xm1 = jnp.where(rows == 0, prev_row, pltpu.roll(x, 1, 0))
    xp1 = jnp.where(rows == rows_n - 1, next_row, pltpu.roll(x, rows_n - 1, 0))
    y = _silu(cw_ref[0:1, :] * xm1 + cw_ref[1:2, :] * x + cw_ref[2:3, :] * xp1)
    for h in range(N_HEADS_B):
        for j, ref in enumerate((q_scr, k_scr)):
            a = y[:, j * W_B + h * DK_B:j * W_B + (h + 1) * DK_B]
            ref[:, h * DK_B:(h + 1) * DK_B] = a * lax.rsqrt(jnp.sum(a * a, axis=-1, keepdims=True) + EPS)
    v_scr[...] = y[:, 2 * W_B:]
    _delta_prepare(q_scr, k_scr, v_scr, pcur_ref[:, 0:LANES], bat_ref[...], al_row, al_col, dt_row, dt_col,
                   u_ref, l1_ref, l2_ref, e_ref)

    def pool_cols(ref):
        t = ref[...]
        return pltpu.roll(t, t.shape[1] - BRANCH_SHIFT, 1)[:, :W_C]

    pz = pool_cols(pcur_ref)
    ext = jnp.concatenate([jnp.where(has_prev, pool_cols(pprev_ref), 0.0), pz,
                           jnp.where(has_next, pool_cols(pnext_ref), 0.0)], axis=0)
    n_ext = rows_n + 2 * SUBLANES
    tpos = pos0 + rows
    for gi, win in enumerate(POOL_WINDOWS):
        sl = slice(gi * POOL_GROUP_W, (gi + 1) * POOL_GROUP_W)
        xg = ext[:, sl]
        acc = xg + pltpu.roll(xg, 1, 0)
        s = 1
        while 2 * s < win:
            acc = pltpu.roll(acc, s, 0) + pltpu.roll(acc, n_ext - s, 0)
            s *= 2
        acc = acc[SUBLANES:SUBLANES + rows_n]
        cnt = (jnp.minimum(tpos + win // 2, t_seq) - jnp.maximum(tpos - win // 2, 0)).astype(F32)
        d = acc / cnt - pz[:, sl]
        yg = _dot(d.astype(BF16), pw_ref[gi].astype(BF16)) * ps_ref[:, sl]
        oc_ref[:, sl] = yg.astype(BF16)


def _local(z, bat, conv_w_l, pool_w_l, pool_scale_l, a_log_l, dt_l, seq):
    m = z.shape[0]
    r, c = ROW_TILE, DELTA_CHUNK
    hb = r // SUBLANES
    last_hb = m // SUBLANES - 1
    c_qkv, c_pool = OFF_QKV_B // (3 * W_B), OFF_BA // IN_TILE_N
    nh2 = 2 * N_HEADS_B
    mc, cpt = m // c, r // c
    prev_map = lambda cb: (lambda i: (jnp.maximum(i * hb - 1, 0), cb))
    next_map = lambda cb: (lambda i: (jnp.minimum((i + 1) * hb, last_hb), cb))
    row = pl.BlockSpec((1, nh2), lambda i: (0, 0))
    col = pl.BlockSpec((nh2, 1), lambda i: (0, 0))
    return pl.pallas_call(
        functools.partial(_local_kernel, seq),
        grid=(m // r,),
        in_specs=[
            pl.BlockSpec((r, 3 * W_B), lambda i: (i, c_qkv)),
            pl.BlockSpec((SUBLANES, 3 * W_B), prev_map(c_qkv)),
            pl.BlockSpec((SUBLANES, 3 * W_B), next_map(c_qkv)),
            pl.BlockSpec((r, IN_TILE_N), lambda i: (i, c_pool)),
            pl.BlockSpec((SUBLANES, IN_TILE_N), prev_map(c_pool)),
            pl.BlockSpec((SUBLANES, IN_TILE_N), next_map(c_pool)),
            pl.BlockSpec((2 * nh2, r), lambda i: (0, i)),
            pl.BlockSpec((3, 3 * W_B), lambda i: (0, 0)),
            pl.BlockSpec((len(POOL_WINDOWS), POOL_GROUP_W, POOL_GROUP_W), lambda i: (0, 0, 0)),
            pl.BlockSpec((1, W_C), lambda i: (0, 0)),
            row, col, row, col,
        ],
        out_specs=[
            pl.BlockSpec((r, W_C), lambda i: (i, 0)),
            pl.BlockSpec((2, r, W_B), lambda i: (0, i, 0)),
            pl.BlockSpec((2, cpt, N_HEADS_B, 2 * c, DV_B), lambda i: (0, i, 0, 0, 0)),
            pl.BlockSpec((2, cpt, N_HEADS_B, c + DK_B, c), lambda i: (0, i, 0, 0, 0)),
            pl.BlockSpec((cpt, nh2, LANES), lambda i: (i, 0, 0)),
        ],
        out_shape=[
            jax.ShapeDtypeStruct((m, W_C), BF16),
            jax.ShapeDtypeStruct((2, m, W_B), F32),
            jax.ShapeDtypeStruct((2, mc, N_HEADS_B, 2 * c, DV_B), BF16),
            jax.ShapeDtypeStruct((2, mc, N_HEADS_B, c + DK_B, c), BF16),
            jax.ShapeDtypeStruct((mc, nh2, LANES), F32),
        ],
        scratch_shapes=[pltpu.VMEM((r, W_B), F32)] * 3,
        compiler_params=_cparams(1),
        name="local_conv_pool",
    )(z, z, z, z, z, z, bat, conv_w_l, pool_w_l, pool_scale_l.reshape(1, W_C),
      a_log_l.reshape(1, nh2), a_log_l.reshape(nh2, 1), dt_l.reshape(1, nh2), dt_l.reshape(nh2, 1))


def _scan_kernel(par, b_lat, n_ctx_ch, *refs):
    c, nh = DELTA_CHUNK, N_HEADS_B
    n_slot = par + b_lat
    ins = refs[:n_slot * 8]
    s0_ref = refs[n_slot * 8]
    of_ctx, ob_ctx, of_lat, ob_lat, sfin_ref, s_scr = refs[n_slot * 8 + 1:]
    step = pl.program_id(0)
    cc = step % n_ctx_ch

    @pl.when(cc == 0)
    def _():
        s_scr[0:par] = jnp.zeros((par,) + s_scr.shape[1:], F32)

    @pl.when(step == 0)
    def _():
        s_scr[par:] = s0_ref[...]

    for slot0 in range(0, n_slot, SCAN_SLOTS_PER_GROUP):
        chains = []
        for slot in range(slot0, min(slot0 + SCAN_SLOTS_PER_GROUP, n_slot)):
            outs = (of_ctx, ob_ctx, slot) if slot < par else (of_lat, ob_lat, slot - par)
            for d in range(2):
                l1_ref, l2_ref, u_ref, e_ref = ins[(slot * 2 + d) * 4:(slot * 2 + d) * 4 + 4]
                for h in range(nh):
                    chains.append(dict(slot=slot, out=outs[d], row=outs[2], h=h, idx=d * nh + h,
                                       l1=l1_ref, l2=l2_ref, u=u_ref, e=e_ref))
        for ch in chains:
            ch["s"] = s_scr[ch["slot"], ch["idx"]]
            ch["r1"] = _dot(ch["l1"][ch["h"]], ch["s"].astype(BF16))
        for ch in chains:
            hs = slice(ch["h"] * DV_B, (ch["h"] + 1) * DV_B)
            v_new = ch["u"][:, hs] - ch["r1"][:c]
            ch["r2"] = _dot(ch["l2"][ch["h"]], v_new.astype(BF16))
        for ch in chains:
            hs = slice(ch["h"] * DV_B, (ch["h"] + 1) * DV_B)
            ch["out"][ch["row"], :, hs] = ch["r1"][c:] + ch["r2"][:c]
            e = ch["e"][ch["idx"]:ch["idx"] + 1, :]
            s_scr[ch["slot"], ch["idx"]] = ch["s"] * e + ch["r2"][c:]

    @pl.when(cc == n_ctx_ch - 1)
    def _():
        sfin_ref[...] = s_scr[0:par]


def _delta_scan(u, l1, l2, e, s0_lat, dims):
    b_ctx, t_ctx, b_lat, t_lat = dims
    c, nh = DELTA_CHUNK, N_HEADS_B
    n_ctx_ch, n_lat_ch = t_ctx // c, t_lat // c
    par = b_ctx * n_ctx_ch // n_lat_ch
    assert par >= 1 and par * n_lat_ch == b_ctx * n_ctx_ch and b_ctx % par == 0
    mc = l1.shape[1]
    ctx_chunks = b_ctx * n_ctx_ch
    u4 = u.reshape(2, mc, c, W_B)

    def chunk_of(slot, d):
        if slot < par:
            return lambda s: ((s // n_ctx_ch) * par + slot) * n_ctx_ch + (
                (n_ctx_ch - 1 - s % n_ctx_ch) if d else s % n_ctx_ch)
        q = slot - par
        return lambda s: ctx_chunks + q * n_lat_ch + ((n_lat_ch - 1 - s) if d else s)

    in_specs, args = [], []
    for slot in range(par + b_lat):
        for d in range(2):
            cg = chunk_of(slot, d)
            in_specs += [
                pl.BlockSpec((None, None, nh, 2 * c, DV_B), lambda s, cg=cg, d=d: (d, cg(s), 0, 0, 0)),
                pl.BlockSpec((None, None, nh, c + DK_B, c), lambda s, cg=cg, d=d: (d, cg(s), 0, 0, 0)),
                pl.BlockSpec((None, None, c, W_B), lambda s, cg=cg, d=d: (d, cg(s), 0, 0)),
                pl.BlockSpec((None, 2 * nh, LANES), lambda s, cg=cg: (cg(s), 0, 0)),
            ]
            args += [l1, l2, u4, e]
    in_specs.append(pl.BlockSpec((b_lat, 2 * nh, DK_B, DV_B), lambda s: (0, 0, 0, 0)))
    args.append(s0_lat)

    ctx_o = jax.ShapeDtypeStruct((b_ctx // par, par, n_ctx_ch, c, W_B), F32)
    lat_o = jax.ShapeDtypeStruct((b_lat, n_lat_ch, c, W_B), F32)
    cf = lambda s: s % n_ctx_ch
    of_ctx, ob_ctx, of_lat, ob_lat, s_fin = pl.pallas_call(
        functools.partial(_scan_kernel, par, b_lat, n_ctx_ch),
        grid=(n_lat_ch,),
        in_specs=in_specs,
        out_specs=[
            pl.BlockSpec((None, par, None, c, W_B), lambda s: (s // n_ctx_ch, 0, cf(s), 0, 0)),
            pl.BlockSpec((None, par, None, c, W_B), lambda s: (s // n_ctx_ch, 0, n_ctx_ch - 1 - cf(s), 0, 0)),
            pl.BlockSpec((b_lat, None, c, W_B), lambda s: (0, s, 0, 0)),
            pl.BlockSpec((b_lat, None, c, W_B), lambda s: (0, n_lat_ch - 1 - s, 0, 0)),
            pl.BlockSpec((par, 2 * nh, DK_B, DV_B), lambda s: (s // n_ctx_ch, 0, 0, 0)),
        ],
        out_shape=[ctx_o, ctx_o, lat_o, lat_o, jax.ShapeDtypeStruct((b_ctx, 2 * nh, DK_B, DV_B), F32)],
        scratch_shapes=[pltpu.VMEM((par + b_lat, 2 * nh, DK_B, DV_B), F32)],
        compiler_params=_cparams(1),
        name="delta_scan",
    )(*args)
    n_ctx, n_lat = b_ctx * t_ctx, b_lat * t_lat
    return (of_ctx.reshape(n_ctx, W_B), ob_ctx.reshape(n_ctx, W_B), of_lat.reshape(n_lat, W_B),
            ob_lat.reshape(n_lat, W_B), s_fin)


def _route_rows(sel, scores):
    g = EXPERTS_PER_GROUP
    gscore = []
    for gi in range(N_EXPERT_GROUPS):
        a, b, c, d = sel[gi * g:(gi + 1) * g]
        hi1, lo1 = jnp.maximum(a, b), jnp.minimum(a, b)
        hi2, lo2 = jnp.maximum(c, d), jnp.minimum(c, d)
        gscore.append(jnp.maximum(hi1, hi2) + jnp.maximum(jnp.minimum(hi1, hi2), jnp.maximum(lo1, lo2)))
    best = jnp.zeros_like(gscore[0], dtype=jnp.int32)
    bestv = gscore[0]
    for gi in range(1, N_EXPERT_GROUPS):
        upd = gscore[gi] > bestv
        best = jnp.where(upd, gi, best)
        bestv = jnp.where(upd, gscore[gi], bestv)

    def in_best(rows, j):
        out = rows[j]
        for gi in range(1, N_EXPERT_GROUPS):
            out = jnp.where(best == gi, rows[gi * g + j], out)
        return out

    e_sel = [in_best(sel, j) for j in range(g)]
    e_sc = [in_best(scores, j) for j in range(g)]

    def first_argmax(vals):
        bi, bv, bs = jnp.zeros_like(best), vals[0], e_sc[0]
        for j in range(1, g):
            upd = vals[j] > bv
            bi = jnp.where(upd, j, bi)
            bv = jnp.where(upd, vals[j], bv)
            bs = jnp.where(upd, e_sc[j], bs)
        return bi, bs

    i0, s0 = first_argmax(e_sel)
    i1, s1 = first_argmax([jnp.where(i0 == j, -jnp.inf, e_sel[j]) for j in range(g)])
    tot = s0 + s1
    return best * g + i0, best * g + i1, s0 / tot, s1 / tot


def _mix_kernel(n_ctx, x_ref, mod_ref, oac_ref, oal_ref, ofc_ref, obc_ref, ofl_ref, obl_ref,
                gb_ref, oc_ref, g0_ref, g1_ref, g2_ref,
                dn_ref, n2_ref, wpa_ref, wpb_ref, wpc_ref, wout_ref, wr_ref, br_ref,
                x1_ref, h2_ref, idx_ref, wt_ref):
    d = x_ref.shape[1]
    is_ctx = pl.program_id(0) * x_ref.shape[0] < n_ctx
    dn = dn_ref[...]
    o = jnp.where(is_ctx, ofc_ref[...] + obc_ref[...], ofl_ref[...] + obl_ref[...])
    o_a = jnp.where(is_ctx, oac_ref[...], oal_ref[...])
    gate = _silu(gb_ref[...])
    parts = []
    for h in range(N_HEADS_B):
        sl = slice(h * DV_B, (h + 1) * DV_B)
        parts.append((_rms(o[:, sl]) * dn * gate[:, sl]).astype(BF16))
    o_b = jnp.concatenate(parts, axis=-1)
    mixed = (_sigmoid(g0_ref[...]) * _dot(o_a, wpa_ref[...])
             + _sigmoid(g1_ref[...]) * _dot(o_b, wpb_ref[...])
             + _sigmoid(g2_ref[...]) * _dot(oc_ref[...], wpc_ref[...]))
    gate1 = mod_ref[:, 2 * d:3 * d]
    x1 = x_ref[...] + gate1 * _dot(mixed.astype(BF16), wout_ref[...])
    x1_ref[...] = x1
    shift2 = mod_ref[:, 3 * d:4 * d]
    scale2 = mod_ref[:, 4 * d:5 * d]
    h2 = _rms(x1) * n2_ref[...] * (1.0 + scale2) + shift2
    h2_ref[...] = h2
    h_hi = h2.astype(BF16)
    h_lo = (h2 - h_hi.astype(F32)).astype(BF16)
    w_r = wr_ref[...]
    w_hi = w_r.astype(BF16)
    w_lo = (w_r - w_hi.astype(F32)).astype(BF16)
    both = _dot_nt(jnp.concatenate([w_hi, w_lo], axis=0), h_hi)
    sc = _sigmoid(both[:N_EXPERTS] + both[N_EXPERTS:] + _dot_nt(w_hi, h_lo))
    sel = sc + br_ref[...]
    i0, i1, w0, w1 = _route_rows([sel[e:e + 1] for e in range(N_EXPERTS)],
                                 [sc[e:e + 1] for e in range(N_EXPERTS)])
    idx_ref[0:1, :] = i0
    idx_ref[1:2, :] = i1
    wt_ref[0:1, :] = w0
    wt_ref[1:2, :] = w1


def _mix(x, mod_l, oa, delta_o, z, oc, delta_norm_l, norm2_l, wpa, wpb, wpc, wout, layer, wr_t, br,
         cond_of_row):
    m, d = x.shape
    r = ROW_TILE
    n_ctx = cond_of_row.n_ctx
    nct = n_ctx // r
    c_gb = OFF_GATE_B // W_B
    once = dict(pipeline_mode=pl.Buffered(1))
    full = lambda shape: pl.BlockSpec(shape, lambda i: (0,) * len(shape), **once)
    of_layer = lambda shape: pl.BlockSpec((None,) + shape, lambda i: (layer,) + (0,) * len(shape), **once)
    ctx_rows = lambda w: pl.BlockSpec((r, w), lambda i: (jnp.minimum(i, nct - 1), 0))
    lat_rows = lambda w: pl.BlockSpec((r, w), lambda i: (jnp.maximum(i - nct, 0), 0))
    of_ctx, ob_ctx, of_lat, ob_lat = delta_o
    return pl.pallas_call(
        functools.partial(_mix_kernel, n_ctx),
        grid=(m // r,),
        in_specs=[
            pl.BlockSpec((r, d), lambda i: (i, 0)),
            pl.BlockSpec((None, 1, 6 * d), lambda i: (cond_of_row(i * r), 0, 0)),
            ctx_rows(W_A), lat_rows(W_A),
            ctx_rows(W_B), ctx_rows(W_B), lat_rows(W_B), lat_rows(W_B),
            pl.BlockSpec((r, W_B), lambda i: (i, c_gb)),
            pl.BlockSpec((r, W_C), lambda i: (i, 0)),
            pl.BlockSpec((r, d), lambda i: (i, 0)),
            pl.BlockSpec((r, d), lambda i: (i, 1)),
            pl.BlockSpec((r, d), lambda i: (i, 2)),
            full((1, DV_B)), full((1, d)),
            of_layer((W_A, d)), of_layer((W_B, d)), of_layer((W_C, d)), of_layer((d, d)),
            full((N_EXPERTS, d)), full((N_EXPERTS, 1)),
        ],
        out_specs=[
            pl.BlockSpec((r, d), lambda i: (i, 0)),
            pl.BlockSpec((r, d), lambda i: (i, 0)),
            pl.BlockSpec((TOP_K, r), lambda i: (0, i)),
            pl.BlockSpec((TOP_K, r), lambda i: (0, i)),
        ],
        out_shape=[
            jax.ShapeDtypeStruct((m, d), F32),
            jax.ShapeDtypeStruct((m, d), F32),
            jax.ShapeDtypeStruct((TOP_K, m), jnp.int32),
            jax.ShapeDtypeStruct((TOP_K, m), F32),
        ],
        compiler_params=_cparams(1),
        name="mix_route",
    )(x, mod_l, oa[0], oa[1], of_ctx, ob_ctx, of_lat, ob_lat, z, oc, z, z, z,
      delta_norm_l.reshape(1, DV_B), norm2_l.reshape(1, d), wpa, wpb, wpc, wout, wr_t, br)


def _dispatch_meta(idx_t, bm):
    m = idx_t.shape[1]
    nk = m * TOP_K
    flat_e = idx_t.T.reshape(nk)
    experts = jnp.arange(N_EXPERTS, dtype=jnp.int32)[None, :]
    counts = jnp.sum((flat_e[:, None] == experts).astype(jnp.int32), axis=0)
    order = jnp.argsort(flat_e, stable=True).astype(jnp.int32)
    start = jnp.cumsum(counts) - counts
    padded = (counts + bm - 1) // bm * bm
    pad_end = jnp.cumsum(padded)
    pad_start = pad_end - padded
    nb = -(-(nk + N_EXPERTS * (bm - 1)) // bm)
    p = jnp.arange(nb * bm, dtype=jnp.int32)
    e_of_p = jnp.minimum(jnp.sum((pad_end[None, :] <= p[:, None]).astype(jnp.int32), axis=1), N_EXPERTS - 1)
    pick = (e_of_p[:, None] == experts).astype(jnp.int32)
    rank = p - jnp.sum(pick * pad_start[None, :], axis=1)
    real = rank < jnp.sum(pick * counts[None, :], axis=1)
    assign = order[jnp.clip(jnp.sum(pick * start[None, :], axis=1) + rank, 0, nk - 1)]
    row_tok = jnp.where(real, assign // TOP_K, 0)
    spare = nk + ((p // bm) % 2) * bm + p % bm
    row_dst = jnp.where(real, (assign % TOP_K) * m + assign // TOP_K, spare)
    prime = nk + jnp.arange(2 * bm, dtype=jnp.int32)
    n_used = (pad_end[-1] // bm).astype(jnp.int32)
    blk_e = e_of_p[::bm]
    n_chunks = 3 * MOE_MAT_CHUNKS
    e_ids = experts[0]
    n_blk = padded // bm
    has = n_blk > 0
    later = lax.cummin(jnp.where(has, e_ids, N_EXPERTS)[::-1])[::-1]
    nxt_of_e = jnp.concatenate([later[1:], jnp.full((1,), N_EXPERTS, jnp.int32)])
    set_of_e = (jnp.cumsum(has.astype(jnp.int32)) - 1) % 2
    b = jnp.arange(nb, dtype=jnp.int32)
    of_blk = (blk_e[:, None] == experts).astype(jnp.int32)
    at_blk = lambda per_expert: jnp.sum(of_blk * per_expert[None, :], axis=1)
    k = b - at_blk(pad_start // bm)
    n_e = jnp.maximum(at_blk(n_blk), 1)
    nxt_b = at_blk(nxt_of_e)
    streams = (nxt_b < N_EXPERTS) & (b < n_used)
    c0 = jnp.where(streams, n_chunks * k // n_e, 0)
    c1 = jnp.where(streams, n_chunks * (k + 1) // n_e, 0)
    nxt_e = jnp.where(streams, nxt_b, 0)
    return (blk_e, n_used.reshape(1), row_tok, jnp.concatenate([prime, row_dst]),
            at_blk(set_of_e), nxt_e, c0, c1)


def _moe_kernel(layer, blk_e_ref, n_used_ref, row_tok_ref, row_dst_ref, set_ref, nxt_ref, c0_ref, c1_ref,
                h2_hbm, wg_hbm, wu_hbm, wd_hbm, out_hbm,
                xbuf, ybuf, wg_s, wu_s, wd_s, st_gu, st_d, gate_scr, act_scr, gsem, ssem, wsem_gu, wsem_d):
    i = pl.program_id(0)
    bm = xbuf.shape[1]
    n_used = n_used_ref[0]
    n_mat = MOE_MAT_CHUNKS
    n_stage = st_gu.shape[0]
    rg, rd = st_gu.shape[1], st_d.shape[1]

    def chunk_ops(c, j, e, s):
        cg, cu, cd = c, c - n_mat, c - 2 * n_mat
        rows_g = pl.ds(pl.multiple_of(cg * rg, rg), rg)
        rows_u = pl.ds(pl.multiple_of(cu * rg, rg), rg)
        rows_d = pl.ds(pl.multiple_of(cd * rd, 2 * SUBLANES), rd)

        def to_set(dst, rows, stage):
            def convert():
                dst[s, rows, :] = stage[j].astype(BF16)
            return convert

        return (
            (c < n_mat, lambda: pltpu.make_async_copy(wg_hbm.at[layer, e, rows_g, :], st_gu.at[j], wsem_gu.at[j]),
             to_set(wg_s, rows_g, st_gu)),
            ((c >= n_mat) & (c < 2 * n_mat),
             lambda: pltpu.make_async_copy(wu_hbm.at[layer, e, rows_u, :], st_gu.at[j], wsem_gu.at[j]),
             to_set(wu_s, rows_u, st_gu)),
            (c >= 2 * n_mat, lambda: pltpu.make_async_copy(wd_hbm.at[layer, e, rows_d, :], st_d.at[j], wsem_d.at[j]),
             to_set(wd_s, rows_d, st_d)),
        )

    def round_chunks(r, c_lo, c_hi):
        first = c_lo + r * n_stage
        return first, jnp.clip(c_hi - first, 0, n_stage)

    def issue_round(r, c_lo, c_hi, e, s):
        first, n = round_chunks(r, c_lo, c_hi)

        def body(j, carry):
            for pred, copy, _ in chunk_ops(first + j, j, e, s):
                pl.when(pred)(lambda copy=copy: copy().start(priority=1))
            return carry
        lax.fori_loop(0, n, body, 0)

    def finish_round(r, c_lo, c_hi, e, s):
        first, n = round_chunks(r, c_lo, c_hi)

        def body(j, carry):
            for pred, copy, convert in chunk_ops(first + j, j, e, s):
                @pl.when(pred)
                def _(copy=copy, convert=convert):
                    copy().wait()
                    convert()
            return carry
        lax.fori_loop(0, n, body, 0)

    def sync_rounds(r_lo, c_lo, c_hi, e, s):
        def body(r, carry):
            issue_round(r, c_lo, c_hi, e, s)
            finish_round(r, c_lo, c_hi, e, s)
            return carry
        n_rounds = (jnp.maximum(c_hi - c_lo, 0) + n_stage - 1) // n_stage
        lax.fori_loop(r_lo, jnp.maximum(n_rounds, r_lo), body, 0)

    def gather_row(blk, slot, r):
        tok = row_tok_ref[blk * bm + r]
        pltpu.make_async_copy(h2_hbm.at[pl.ds(tok, 1)], xbuf.at[slot, pl.ds(r, 1)], gsem.at[slot]).start()

    def scatter_row(tab_blk, slot, r):
        dst = row_dst_ref[tab_blk * bm + r]
        pltpu.make_async_copy(ybuf.at[slot, pl.ds(r, 1)], out_hbm.at[pl.ds(dst, 1)], ssem.at[slot]).start()

    def looped(fn, *args):
        def body(r, carry):
            fn(*args, r)
            return carry
        lax.fori_loop(0, bm, body, 0, unroll=8)

    def wait_gather(slot):
        pltpu.make_async_copy(h2_hbm.at[pl.ds(0, bm)], xbuf.at[slot], gsem.at[slot]).wait()

    def wait_scatter(slot):
        pltpu.make_async_copy(ybuf.at[slot], out_hbm.at[pl.ds(0, bm)], ssem.at[slot]).wait()

    @pl.when(i == 0)
    def _():
        ybuf[...] = jnp.zeros_like(ybuf)
        looped(gather_row, 0, 0)
        looped(scatter_row, 0, 0)
        sync_rounds(0, 0, 3 * n_mat, blk_e_ref[0], set_ref[0])

    def compute_block(slot):
        other = 1 - slot
        nxt = jnp.minimum(i + 1, n_used - 1)
        s, e_next, c_lo, c_hi = set_ref[i], nxt_ref[i], c0_ref[i], c1_ref[i]
        stream = (c_lo, c_hi, e_next, 1 - s)
        wait_gather(slot)
        issue_round(0, *stream)
        for r in range(bm):
            gather_row(nxt, other, r)
            scatter_row(i + 1, other, r)
        xb = xbuf[slot].astype(BF16)
        gate_scr[...] = _dot(xb, wg_s[s])
        finish_round(0, *stream)
        issue_round(1, *stream)
        act_scr[...] = (_silu(gate_scr[...]) * _dot(xb, wu_s[s])).astype(BF16)
        finish_round(1, *stream)
        wait_scatter(slot)
        issue_round(2, *stream)
        ybuf[slot] = _dot(act_scr[...], wd_s[s])
        finish_round(2, *stream)
        sync_rounds(3, *stream)

    for parity in range(2):
        pl.when((i < n_used) & (i % 2 == parity))(functools.partial(compute_block, parity))

    @pl.when(i == n_used)
    def _():
        looped(scatter_row, i + 1, (i - 1) % 2)
        wait_gather(i % 2)
        wait_scatter(0)
        wait_scatter(1)


def _moe(h2, meta, wg, wu, wd, layer):
    m, d = h2.shape
    bm = MOE_ROWS
    nb = meta[0].shape[0]
    f = wg.shape[3]
    rg, rd = d // MOE_MAT_CHUNKS, f // MOE_MAT_CHUNKS
    assert rg % (2 * SUBLANES) == 0 and rd % (2 * SUBLANES) == 0
    any_spec = pl.BlockSpec(memory_space=pl.ANY)
    dma = pltpu.SemaphoreType.DMA
    return pl.pallas_call(
        functools.partial(_moe_kernel, layer),
        grid_spec=pltpu.PrefetchScalarGridSpec(
            num_scalar_prefetch=len(meta),
            grid=(nb + 1,),
            in_specs=[any_spec] * 4,
            out_specs=any_spec,
            scratch_shapes=[
                pltpu.VMEM((2, bm, d), F32), pltpu.VMEM((2, bm, d), F32),
                pltpu.VMEM((2, d, f), BF16), pltpu.VMEM((2, d, f), BF16), pltpu.VMEM((2, f, d), BF16),
                pltpu.VMEM((MOE_STAGE_SLOTS, rg, f), F32), pltpu.VMEM((MOE_STAGE_SLOTS, rd, d), F32),
                pltpu.VMEM((bm, f), F32), pltpu.VMEM((bm, f), BF16),
                dma((2,)), dma((2,)), dma((MOE_STAGE_SLOTS,)), dma((MOE_STAGE_SLOTS,)),
            ],
        ),
        out_shape=jax.ShapeDtypeStruct((TOP_K * m + 2 * bm, d), F32),
        compiler_params=_cparams(1, MOE_VMEM_LIMIT),
        name="moe_experts",
    )(*meta, h2, wg, wu, wd)


def _combine_kernel(n_ctx, final, y0_ref, y1_ref, x1_ref, mod_ref, w_ref, g_ref, modn_ref, o_ref, o2_ref):
    d = x1_ref.shape[1]
    f = w_ref[:, 0:1] * y0_ref[...] + w_ref[:, 1:2] * y1_ref[...]
    x2 = x1_ref[...] + mod_ref[:, 5 * d:6 * d] * f
    if final:
        y = _rms(x2) * g_ref[...]
        is_ctx = pl.program_id(0) * x1_ref.shape[0] < n_ctx

        @pl.when(is_ctx)
        def _():
            o_ref[...] = y

        @pl.when(jnp.logical_not(is_ctx))
        def _():
            o2_ref[...] = y
    else:
        o_ref[...] = x2
        o2_ref[...] = _modulated_norm(x2, g_ref[...], modn_ref[...], d).astype(BF16)


def _combine(y2, x1, mod_l, w_col, gain, mod_next, final, cond_of_row):
    m, d = x1.shape
    r = ROW_TILE
    k1 = m // r
    n_ctx = cond_of_row.n_ctx
    nct = n_ctx // r
    cond_row = pl.BlockSpec((None, 1, 6 * d), lambda i: (cond_of_row(i * r), 0, 0))
    tile = pl.BlockSpec((r, d), lambda i: (i, 0))
    if final:
        out_specs = [pl.BlockSpec((r, d), lambda i: (jnp.minimum(i, nct - 1), 0)),
                     pl.BlockSpec((r, d), lambda i: (jnp.maximum(i - nct, 0), 0))]
        out_shape = [jax.ShapeDtypeStruct((n_ctx, d), F32), jax.ShapeDtypeStruct((m - n_ctx, d), F32)]
    else:
        out_specs = [tile, tile]
        out_shape = [jax.ShapeDtypeStruct((m, d), F32), jax.ShapeDtypeStruct((m, d), BF16)]
    return pl.pallas_call(
        functools.partial(_combine_kernel, n_ctx, final),
        grid=(m // r,),
        in_specs=[
            tile,
            pl.BlockSpec((r, d), lambda i: (k1 + i, 0)),
            tile, cond_row,
            pl.BlockSpec((r, TOP_K), lambda i: (i, 0)),
            pl.BlockSpec((1, d), lambda i: (0, 0)),
            cond_row,
        ],
        out_specs=out_specs,
        out_shape=out_shape,
        compiler_params=_cparams(1),
        name="moe_combine",
    )(y2, y2, x1, mod_l, w_col, gain.reshape(1, d), mod_next)


class _CondOfRow:
    def __init__(self, n_ctx, t_lat):
        self.n_ctx, self.t_lat = n_ctx, t_lat
        self.tile_in = _pick_tile((1024, 512, 256), n_ctx, t_lat)

    def __call__(self, row0):
        return jnp.where(row0 < self.n_ctx, 0, 1 + (row0 - self.n_ctx) // self.t_lat)


def kernel(x_prompt, x_sample, cache_k, cache_v, state_delta, c, c_ctx, w_mod, b_mod, norm1, norm2, w_in,
           lambda_qk, subln, conv_w, a_log, dt_bias, delta_norm, pool_w, pool_scale, w_pa, w_pb, w_pc,
           w_out, w_router, b_router, w_gate, w_up, w_down, norm_final):
    b_ctx, t_ctx, d = x_prompt.shape
    b_lat, t_lat, _ = x_sample.shape
    depth = w_in.shape[0]
    past = cache_k.shape[2]
    n_ctx, n_lat = b_ctx * t_ctx, b_lat * t_lat
    m = n_ctx + n_lat
    assert d == D_MODEL and w_in.shape[2] == D_IN and 1 + b_lat <= SUBLANES
    assert t_ctx % ROW_TILE == 0 and t_lat % ROW_TILE == 0 and n_ctx % t_lat == 0
    cond_of_row = _CondOfRow(n_ctx, t_lat)
    dims = (b_ctx, t_ctx, b_lat, t_lat)

    def seq(row0):
        is_ctx = row0 < n_ctx
        return (jnp.where(is_ctx, row0 % t_ctx, (row0 - n_ctx) % t_lat), jnp.where(is_ctx, t_ctx, t_lat))

    x = jnp.concatenate([x_prompt.reshape(n_ctx, d), x_sample.reshape(n_lat, d)], axis=0)
    cond8 = jnp.zeros((SUBLANES, d), F32).at[0].set(c_ctx).at[1:1 + b_lat].set(c)
    mod = _modulation(cond8, w_mod, b_mod).reshape(depth, SUBLANES, 1, 6 * d)
    rope_tabs = _rope_tables(t_lat)
    wr_t = w_router.T
    br = b_router.reshape(N_EXPERTS, 1)
    ck = cache_k.reshape(b_lat, depth, past, QK_A)
    cv = cache_v.reshape(b_lat, depth, past, W_A)
    s_lat = state_delta.reshape(b_lat, depth, 2 * N_HEADS_B, DK_B, DV_B)

    new_k, new_v, new_s = [], [], []
    h = _norm_mod(x, mod[0], norm1[0], cond_of_row)
    w_in_t = jnp.swapaxes(w_in, 1, 2)
    w_proj = [w.astype(BF16) for w in (w_pa, w_pb, w_pc, w_out)]
    for l in range(depth):
        lam_init = 0.8 - 0.6 * math.exp(-0.3 * l)
        last = l == depth - 1
        z = _in_proj(h, w_in_t, l, cond_of_row.tile_in)
        oa, k_l, v_l = _attention(z, lambda_qk[l], subln[l], ck[:, l], cv[:, l], rope_tabs, lam_init, dims)
        new_k.append(k_l.reshape(b_ctx, t_ctx, N_HEADS_A, 2 * DK_A))
        new_v.append(v_l.reshape(b_ctx, t_ctx, N_HEADS_A, DV_A))
        bat = z[:, OFF_BA:OFF_BA + 4 * N_HEADS_B].T
        oc, u, l1, l2, e = _local(z, bat, conv_w[l], pool_w[l], pool_scale[l], a_log[l], dt_bias[l], seq)
        *delta_o, s_ctx = _delta_scan(u, l1, l2, e, s_lat[:, l], dims)
        new_s.append(s_ctx.reshape(b_ctx, 2, N_HEADS_B, DK_B, DV_B))

        x1, h2, idx_t, wt_t = _mix(x, mod[l], oa, delta_o, z, oc, delta_norm[l], norm2[l], *w_proj, l,
                                   wr_t, br, cond_of_row)
        y2 = _moe(h2, _dispatch_meta(idx_t, MOE_ROWS), w_gate, w_up, w_down, l)
        if last:
            y_ctx, y_lat = _combine(y2, x1, mod[l], wt_t.T, norm_final, mod[l], True, cond_of_row)
        else:
            x, h = _combine(y2, x1, mod[l], wt_t.T, norm1[l + 1], mod[l + 1], False, cond_of_row)

    y_prompt = y_ctx.reshape(b_ctx, t_ctx, d)
    y_sample = y_lat.reshape(b_lat, t_lat, d)
    return (y_prompt, y_sample, jnp.stack(new_k, axis=1), jnp.stack(new_v, axis=1), jnp.stack(new_s, axis=1))
```

```python
import functools
import math

import jax
import jax.numpy as jnp
from jax import lax
from jax.experimental import pallas as pl
from jax.experimental.pallas import tpu as pltpu

F32 = jnp.float32
BF16 = jnp.bfloat16

D_MODEL = 2048
GRID_W = 64
EPS = 1e-6
N_HEADS_A = 8
DK_A = 64
DV_A = 128
QK_A = N_HEADS_A * 2 * DK_A
W_A = N_HEADS_A * DV_A
ROPE_BASE = 10000.0
ROPE_AXIS = DK_A // 2
N_HEADS_B = 4
DK_B = 128
DV_B = 128
W_B = N_HEADS_B * DV_B
DELTA_CHUNK = 64
POOL_WINDOWS = (2, 4, 8, 16)
POOL_GROUP_W = 128
W_C = len(POOL_WINDOWS) * POOL_GROUP_W
N_BRANCH = 3
N_EXPERTS = 16
N_EXPERT_GROUPS = 4
EXPERTS_PER_GROUP = N_EXPERTS // N_EXPERT_GROUPS
TOP_K = 2

SRC_QKV_B = 2 * QK_A + W_A
SRC_GATE_B = SRC_QKV_B + 3 * W_B
SRC_BA = SRC_GATE_B + W_B
SRC_POOL = SRC_BA + 4 * N_HEADS_B
SRC_BRANCH = SRC_POOL + W_C
D_IN = SRC_BRANCH + N_BRANCH * D_MODEL
LANES = 128
SUBLANES = 8
IN_TILE_N = 1024
OFF_RAW = N_BRANCH * D_MODEL
RAW_W = -(-SRC_BRANCH // IN_TILE_N) * IN_TILE_N
NP_IN = OFF_RAW + RAW_W
OFF_Q = OFF_RAW
OFF_K = OFF_Q + QK_A
OFF_V = OFF_K + QK_A
OFF_QKV_B = OFF_RAW + SRC_QKV_B
OFF_GATE_B = OFF_RAW + SRC_GATE_B
OFF_BA = OFF_RAW + SRC_BA
BRANCH_SHIFT = SRC_BRANCH % LANES
assert SRC_BA % IN_TILE_N == 0 and SRC_POOL - SRC_BA == BRANCH_SHIFT and OFF_RAW % IN_TILE_N == 0

ROW_TILE = 256
MOE_ROWS = 256
DELTA_CHUNKS_PER_GROUP = 2
SCAN_SLOTS_PER_GROUP = 2
INV_BASE_BLOCK = 16
MOE_MAT_CHUNKS = 8
MOE_STAGE_SLOTS = 2
MOE_VMEM_LIMIT = 60 * 1024 * 1024
VMEM_LIMIT = 56 * 1024 * 1024


def _cparams(n_axes, vmem=VMEM_LIMIT):
    return pltpu.CompilerParams(dimension_semantics=("arbitrary",) * n_axes, vmem_limit_bytes=vmem)


def _dot(a, b, precision=None):
    return jnp.dot(a, b, preferred_element_type=F32, precision=precision)


def _dot_nt(a, b, precision=None):
    return lax.dot_general(a, b, (((1,), (1,)), ((), ())), preferred_element_type=F32, precision=precision)


def _bdot(a, b):
    return _dot(a.astype(BF16), b.astype(BF16))


def _bdot_nt(a, b):
    return _dot_nt(a.astype(BF16), b.astype(BF16))


def _sigmoid(x):
    return 1.0 / (1.0 + jnp.exp(-x))


def _silu(x):
    return x * _sigmoid(x)


def _rms(x, eps=EPS):
    return x * lax.rsqrt(jnp.mean(x * x, axis=-1, keepdims=True) + eps)


def _pick_tile(cands, *dims):
    for t in cands:
        if all(d % t == 0 for d in dims):
            return t
    raise ValueError(f"no tile in {cands} divides {dims}")


def _mod_kernel(c_ref, w_ref, b_ref, o_ref):
    a = _silu(c_ref[...]).astype(BF16)
    o_ref[...] = _dot(a, w_ref[...].astype(BF16)) + b_ref[...]


def _modulation(cond8, w_mod, b_mod):
    depth, d, n6 = w_mod.shape
    tn = 1024
    return pl.pallas_call(
        _mod_kernel,
        grid=(depth, n6 // tn),
        in_specs=[
            pl.BlockSpec((SUBLANES, d), lambda l, n: (0, 0)),
            pl.BlockSpec((None, d, tn), lambda l, n: (l, 0, n)),
            pl.BlockSpec((None, 1, tn), lambda l, n: (l, 0, n)),
        ],
        out_specs=pl.BlockSpec((None, SUBLANES, tn), lambda l, n: (l, 0, n)),
        out_shape=jax.ShapeDtypeStruct((depth, SUBLANES, n6), F32),
        compiler_params=_cparams(2),
        name="adaln_mod",
    )(cond8, w_mod, b_mod.reshape(depth, 1, n6))


def _modulated_norm(x, g, mod, d):
    return _rms(x) * g * (1.0 + mod[:, d:2 * d]) + mod[:, 0:d]


def _norm_kernel(x_ref, mod_ref, g_ref, h_ref):
    h_ref[...] = _modulated_norm(x_ref[...], g_ref[...], mod_ref[...], x_ref.shape[1]).astype(BF16)


def _norm_mod(x, mod_l, norm_l, cond_of_row):
    m, d = x.shape
    r = ROW_TILE
    return pl.pallas_call(
        _norm_kernel,
        grid=(m // r,),
        in_specs=[
            pl.BlockSpec((r, d), lambda i: (i, 0)),
            pl.BlockSpec((None, 1, 6 * d), lambda i: (cond_of_row(i * r), 0, 0)),
            pl.BlockSpec((1, d), lambda i: (0, 0)),
        ],
        out_specs=pl.BlockSpec((r, d), lambda i: (i, 0)),
        out_shape=jax.ShapeDtypeStruct((m, d), BF16),
        compiler_params=_cparams(1),
        name="norm_mod",
    )(x, mod_l, norm_l.reshape(1, d))


def _in_kernel(h_ref, wt_ref, z_ref, w_scr):
    @pl.when(pl.program_id(1) == 0)
    def _():
        w_scr[...] = wt_ref[0].astype(BF16)

    z_ref[...] = _dot_nt(h_ref[...], w_scr[...])


def _in_proj(h, w_in_t, layer, tm):
    m, d = h.shape
    n_branch_tiles = OFF_RAW // IN_TILE_N
    assert SRC_BRANCH % SUBLANES == 0
    tile8 = IN_TILE_N // SUBLANES
    row0 = lambda n: SUBLANES * jnp.where(n < n_branch_tiles, SRC_BRANCH // SUBLANES + tile8 * n,
                                          tile8 * (n - n_branch_tiles))
    return pl.pallas_call(
        _in_kernel,
        grid=(NP_IN // IN_TILE_N, m // tm),
        in_specs=[
            pl.BlockSpec((tm, d), lambda n, i: (i, 0)),
            pl.BlockSpec((pl.Element(1), pl.Element(IN_TILE_N), pl.Element(d)),
                         lambda n, i: (layer, row0(n), 0)),
        ],
        out_specs=pl.BlockSpec((tm, IN_TILE_N), lambda n, i: (i, n)),
        out_shape=jax.ShapeDtypeStruct((m, NP_IN), F32),
        scratch_shapes=[pltpu.VMEM((IN_TILE_N, d), BF16)],
        compiler_params=_cparams(2),
        name="in_proj",
    )(h, w_in_t)


def _lambda(lq_ref, lam_init):
    lq = lq_ref[...]
    a = jnp.sum(lq[0:1] * lq[1:2], axis=-1, keepdims=True)
    b = jnp.sum(lq[2:3] * lq[3:4], axis=-1, keepdims=True)
    return jnp.exp(a) - jnp.exp(b) + lam_init


Q_SCALE = DK_A ** -0.5


def _diff_attn(qb, kb, vb, lam):
    probs = []
    for mp in range(2):
        s = _dot_nt(qb[:, mp * DK_A:(mp + 1) * DK_A], kb[:, mp * DK_A:(mp + 1) * DK_A])
        e = jnp.exp(s - jnp.max(s, axis=-1, keepdims=True))
        probs.append(e / jnp.sum(e, axis=-1, keepdims=True))
    pd = probs[0] - lam * probs[1]
    return _dot(pd.astype(BF16), vb)


def _attn_ctx_kernel(lam_init, lq_ref, sub_ref, q_ref, k_ref, v_ref, o_ref, knew_ref, vnew_ref):
    lam = _lambda(lq_ref, lam_init)
    for h in range(N_HEADS_A):
        hs = slice(h * LANES, (h + 1) * LANES)
        qb = (q_ref[:, hs] * Q_SCALE).astype(BF16)
        k, v = k_ref[:, hs], v_ref[:, hs]
        knew_ref[:, hs] = k
        vnew_ref[:, hs] = v
        o = _diff_attn(qb, k.astype(BF16), v.astype(BF16), lam)
        o_ref[:, hs] = (_rms(o) * sub_ref[...] * (1.0 - lam_init)).astype(BF16)


def _rope(x, cos, sin_signed):
    lane = lax.broadcasted_iota(jnp.int32, x.shape, 1)
    first_half = (lane % (ROPE_AXIS)) < (ROPE_AXIS // 2)
    partner = jnp.where(first_half, pltpu.roll(x, LANES - ROPE_AXIS // 2, 1), pltpu.roll(x, ROPE_AXIS // 2, 1))
    return x * cos + partner * sin_signed


def _attn_lat_kernel(lam_init, t_lat, lq_ref, sub_ref, q_ref, k_ref, v_ref, ck_ref, cv_ref,
                     cq_ref, sq_ref, ckk_ref, skk_ref, o_ref, k_scr, v_scr):
    @pl.when(pl.program_id(2) == 0)
    def _():
        k_scr[0:t_lat, :] = _rope(k_ref[...], ckk_ref[...], skk_ref[...]).astype(BF16)
        k_scr[t_lat:, :] = ck_ref[...].astype(BF16)
        v_scr[0:t_lat, :] = v_ref[...].astype(BF16)
        v_scr[t_lat:, :] = cv_ref[...].astype(BF16)

    lam = _lambda(lq_ref, lam_init)
    q = _rope(q_ref[...], cq_ref[...], sq_ref[...]) * Q_SCALE
    o = _diff_attn(q.astype(BF16), k_scr[...], v_scr[...], lam)
    o_ref[...] = (_rms(o) * sub_ref[...] * (1.0 - lam_init)).astype(BF16)


def _attention(z, lq_l, sub_l, cache_k_l, cache_v_l, rope_tabs, lam_init, dims):
    b_ctx, t_ctx, b_lat, t_lat = dims
    n_ctx = b_ctx * t_ctx
    past = cache_k_l.shape[1]
    cq, ck, cv = OFF_Q // LANES, OFF_K // LANES, OFF_V // LANES
    small = [pl.BlockSpec((4, DK_A), lambda *_: (0, 0)), pl.BlockSpec((1, DV_A), lambda *_: (0, 0))]
    sub2 = sub_l.reshape(1, DV_A)

    seq_blk = pl.BlockSpec((t_ctx, W_A), lambda b: (b, 0))
    oa_ctx, k_new, v_new = pl.pallas_call(
        functools.partial(_attn_ctx_kernel, lam_init),
        grid=(b_ctx,),
        in_specs=small + [
            pl.BlockSpec((t_ctx, QK_A), lambda b: (b, OFF_Q // QK_A)),
            pl.BlockSpec((t_ctx, QK_A), lambda b: (b, OFF_K // QK_A)),
            pl.BlockSpec((t_ctx, W_A), lambda b: (b, OFF_V // W_A)),
        ],
        out_specs=[seq_blk] * 3,
        out_shape=[jax.ShapeDtypeStruct((n_ctx, W_A), BF16), jax.ShapeDtypeStruct((n_ctx, QK_A), F32),
                   jax.ShapeDtypeStruct((n_ctx, W_A), F32)],
        compiler_params=_cparams(1),
        name="attn_ctx",
    )(lq_l, sub2, z, z, z)

    tq = _pick_tile((2048, 1024, 512, 256, 128), t_lat)
    nq = t_lat // tq
    rb_q = n_ctx // tq
    rb_k = n_ctx // t_lat
    cos_t, sin_t = rope_tabs
    oa_lat = pl.pallas_call(
        functools.partial(_attn_lat_kernel, lam_init, t_lat),
        grid=(b_lat, N_HEADS_A, nq),
        in_specs=small + [
            pl.BlockSpec((tq, LANES), lambda b, h, i: (rb_q + b * nq + i, cq + h)),
            pl.BlockSpec((t_lat, LANES), lambda b, h, i: (rb_k + b, ck + h)),
            pl.BlockSpec((t_lat, LANES), lambda b, h, i: (rb_k + b, cv + h)),
            pl.BlockSpec((None, past, LANES), lambda b, h, i: (b, 0, h)),
            pl.BlockSpec((None, past, LANES), lambda b, h, i: (b, 0, h)),
            pl.BlockSpec((tq, LANES), lambda b, h, i: (i, 0)),
            pl.BlockSpec((tq, LANES), lambda b, h, i: (i, 0)),
            pl.BlockSpec((t_lat, LANES), lambda b, h, i: (0, 0)),
            pl.BlockSpec((t_lat, LANES), lambda b, h, i: (0, 0)),
        ],
        out_specs=pl.BlockSpec((tq, LANES), lambda b, h, i: (b * nq + i, h)),
        out_shape=jax.ShapeDtypeStruct((b_lat * t_lat, W_A), BF16),
        scratch_shapes=[pltpu.VMEM((t_lat + past, LANES), BF16), pltpu.VMEM((t_lat + past, LANES), BF16)],
        compiler_params=_cparams(3),
        name="attn_lat",
    )(lq_l, sub2, z, z, z, cache_k_l, cache_v_l, cos_t, sin_t, cos_t, sin_t)
    return (oa_ctx, oa_lat), k_new, v_new


def _rope_tables(t_lat):
    rows = t_lat // GRID_W
    r = jnp.repeat(jnp.arange(rows), GRID_W).astype(F32)
    col = jnp.tile(jnp.arange(GRID_W), rows).astype(F32)
    inv = ROPE_BASE ** (-jnp.arange(0, ROPE_AXIS, 2, dtype=F32) / ROPE_AXIS)
    ang_r, ang_c = r[:, None] * inv, col[:, None] * inv
    cos64 = jnp.concatenate([jnp.cos(ang_r)] * 2 + [jnp.cos(ang_c)] * 2, axis=-1)
    sin64 = jnp.concatenate([-jnp.sin(ang_r), jnp.sin(ang_r), -jnp.sin(ang_c), jnp.sin(ang_c)], axis=-1)
    return jnp.tile(cos64, (1, 2)), jnp.tile(sin64, (1, 2))


def _log_decay(alpha, a_log, dt):
    x = alpha + dt
    return -jnp.exp(a_log) * (jnp.maximum(x, 0.0) + jnp.log(1.0 + jnp.exp(-jnp.abs(x))))


def _chunk_cumsum(g, axis):
    n = g.shape[axis]
    pos = lax.broadcasted_iota(jnp.int32, g.shape, axis) % DELTA_CHUNK
    f, b = g, g
    s = 1
    while s < DELTA_CHUNK:
        f = f + jnp.where(pos >= s, pltpu.roll(f, s, axis), 0.0)
        b = b + jnp.where(pos < DELTA_CHUNK - s, pltpu.roll(b, n - s, axis), 0.0)
        s *= 2
    return f, b


def _delta_prepare(q_scr, k_scr, v_scr, ba, bat, al_row, al_col, dt_row, dt_col,
                   u_ref, l1_ref, l2_ref, e_ref):
    c, nh = DELTA_CHUNK, N_HEADS_B
    rows_n = q_scr.shape[0]
    beta_all = _sigmoid(ba[:, 0:2 * nh])
    gf, gb = _chunk_cumsum(_log_decay(ba[:, 2 * nh:4 * nh], al_row[...], dt_row[...]), 0)
    gc_all = jnp.where(lax.broadcasted_iota(jnp.int32, gf.shape, 1) < nh, gf, gb)
    gtf, gtb = _chunk_cumsum(_log_decay(bat[2 * nh:4 * nh, :], al_col[...], dt_col[...]), 1)
    gct_all = jnp.where(lax.broadcasted_iota(jnp.int32, gtf.shape, 0) < nh, gtf, gtb)

    ri = lax.broadcasted_iota(jnp.int32, (c, c), 0)
    ci = lax.broadcasted_iota(jnp.int32, (c, c), 1)
    masks = ((ri >= ci, ri > ci), (ri <= ci, ri < ci))
    eye = (ri == ci).astype(F32)
    blk_sizes = [INV_BASE_BLOCK << i for i in range(int(math.log2(c // INV_BASE_BLOCK)) + 1)]
    same_blk = [(ri // s) == (ci // s) for s in blk_sizes]
    n_sq = int(math.log2(INV_BASE_BLOCK)) - 1

    n_chunks = rows_n // c
    for j0 in range(0, n_chunks, DELTA_CHUNKS_PER_GROUP):
        chains = []
        for j, d, h in [(j, d, h) for j in range(j0, j0 + DELTA_CHUNKS_PER_GROUP) for d in range(2)
                        for h in range(nh)]:
            rs = slice(j * c, (j + 1) * c)
            incl, strict = masks[d]
            last = 0 if d else c - 1
            idx = d * nh + h
            hs = slice(h * DK_B, (h + 1) * DK_B)
            gc = gc_all[rs, idx:idx + 1]
            gr = gct_all[idx:idx + 1, rs]
            ch = dict(j=j, rs=rs, d=d, h=h, idx=idx, hs=hs, strict=strict, gc=gc, g_last=gc[last:last + 1, :],
                      q=q_scr[rs, hs] * DK_B ** -0.5, k=k_scr[rs, hs], v=v_scr[rs, hs],
                      beta=beta_all[rs, idx:idx + 1],
                      decay=jnp.where(incl, jnp.exp(jnp.where(incl, gc - gr, 0.0)), 0.0))
            ch["kb"] = ch["k"] * ch["beta"]
            chains.append(ch)
        for ch in chains:
            scores = _bdot_nt(jnp.concatenate([ch["kb"], ch["q"]], axis=0), ch["k"])
            a = jnp.where(ch["strict"], -(scores[:c] * ch["decay"]), 0.0)
            ch["attn"] = scores[c:] * ch["decay"]
            ch["a"] = a
            ch["apow"] = jnp.where(same_blk[0], a, 0.0)
            ch["p"] = eye + ch["apow"]
        for _ in range(n_sq):
            for ch in chains:
                ch["apow"] = _bdot(ch["apow"], ch["apow"])
            for ch in chains:
                ch["p"] = ch["p"] + _bdot(ch["p"], ch["apow"])
        for lvl in range(1, len(same_blk)):
            for ch in chains:
                a_off = jnp.where(same_blk[lvl] & jnp.logical_not(same_blk[lvl - 1]), ch["a"], 0.0)
                ch["t"] = _bdot(ch["p"], a_off)
            for ch in chains:
                ch["p"] = ch["p"] + _bdot(ch["t"], ch["p"])
        for ch in chains:
            rhs = jnp.concatenate([ch["v"] * ch["beta"], ch["kb"] * jnp.exp(ch["gc"])], axis=-1)
            ch["sol"] = _bdot(ch["p"], rhs)
        for ch in chains:
            j, rs, d, h, hs = ch["j"], ch["rs"], ch["d"], ch["h"], ch["hs"]
            u_ref[d, rs, hs] = ch["sol"][:, :DV_B]
            l1_ref[d, j, h, 0:c, :] = ch["sol"][:, DV_B:].astype(BF16)
            l1_ref[d, j, h, c:2 * c, :] = (ch["q"] * jnp.exp(ch["gc"])).astype(BF16)
            l2_ref[d, j, h, 0:c, :] = ch["attn"].astype(BF16)
            l2_ref[d, j, h, c:, :] = (ch["k"] * jnp.exp(ch["g_last"] - ch["gc"])).T.astype(BF16)
            e_ref[j, ch["idx"]:ch["idx"] + 1, :] = jnp.broadcast_to(jnp.exp(ch["g_last"]), (1, LANES))


def _local_kernel(seq, cur_ref, prev_ref, next_ref, pcur_ref, pprev_ref, pnext_ref, bat_ref,
                  cw_ref, pw_ref, ps_ref, al_row, al_col, dt_row, dt_col,
                  oc_ref, u_ref, l1_ref, l2_ref, e_ref, q_scr, k_scr, v_scr):
    rows_n = cur_ref.shape[0]
    pos0, t_seq = seq(pl.program_id(0) * rows_n)
    has_prev = pos0 > 0
    has_next = pos0 + rows_n < t_seq
    rows = lax.broadcasted_iota(jnp.int32, (rows_n, 1), 0)

    x = cur_ref[...]
    prev_row = jnp.where(has_prev, prev_ref[SUBLANES - 1:SUBLANES, :], 0.0)
    next_row = jnp.where(has_next, next_ref[0:1, :], 0.0)
    xm1 = jnp.where(rows == 0, prev_row, pltpu.roll(x, 1, 0))
    xp1 = jnp.where(rows == rows_n - 1, next_row, pltpu.roll(x, rows_n - 1, 0))
    y = _silu(cw_ref[0:1, :] * xm1 + cw_ref[1:2, :] * x + cw_ref[2:3, :] * xp1)
    for h in range(N_HEADS_B):
        for j, ref in enumerate((q_scr, k_scr)):
            a = y[:, j * W_B + h * DK_B:j * W_B + (h + 1) * DK_B]
            ref[:, h * DK_B:(h + 1) * DK_B] = a * lax.rsqrt(jnp.sum(a * a, axis=-1, keepdims=True) + EPS)
    v_scr[...] = y[:, 2 * W_B:]
    _delta_prepare(q_scr, k_scr, v_scr, pcur_ref[:, 0:LANES], bat_ref[...], al_row, al_col, dt_row, dt_col,
                   u_ref, l1_ref, l2_ref, e_ref)

    def pool_cols(ref):
        t = ref[...]
        return pltpu.roll(t, t.shape[1] - BRANCH_SHIFT, 1)[:, :W_C]

    pz = pool_cols(pcur_ref)
    ext = jnp.concatenate([jnp.where(has_prev, pool_cols(pprev_ref), 0.0), pz,
                           jnp.where(has_next, pool_cols(pnext_ref), 0.0)], axis=0)
    n_ext = rows_n + 2 * SUBLANES
    tpos = pos0 + rows
    for gi, win in enumerate(POOL_WINDOWS):
        sl = slice(gi * POOL_GROUP_W, (gi + 1) * POOL_GROUP_W)
        xg = ext[:, sl]
        acc = xg + pltpu.roll(xg, 1, 0)
        s = 1
        while 2 * s < win:
            acc = pltpu.roll(acc, s, 0) + pltpu.roll(acc, n_ext - s, 0)
            s *= 2
        acc = acc[SUBLANES:SUBLANES + rows_n]
        cnt = (jnp.minimum(tpos + win // 2, t_seq) - jnp.maximum(tpos - win // 2, 0)).astype(F32)
        d = acc / cnt - pz[:, sl]
        yg = _dot(d.astype(BF16), pw_ref[gi].astype(BF16)) * ps_ref[:, sl]
        oc_ref[:, sl] = yg.astype(BF16)


def _local(z, bat, conv_w_l, pool_w_l, pool_scale_l, a_log_l, dt_l, seq):
    m = z.shape[0]
    r, c = ROW_TILE, DELTA_CHUNK
    hb = r // SUBLANES
    last_hb = m // SUBLANES - 1
    c_qkv, c_pool = OFF_QKV_B // (3 * W_B), OFF_BA // IN_TILE_N
    nh2 = 2 * N_HEADS_B
    mc, cpt = m // c, r // c
    prev_map = lambda cb: (lambda i: (jnp.maximum(i * hb - 1, 0), cb))
    next_map = lambda cb: (lambda i: (jnp.minimum((i + 1) * hb, last_hb), cb))
    row = pl.BlockSpec((1, nh2), lambda i: (0, 0))
    col = pl.BlockSpec((nh2, 1), lambda i: (0, 0))
    return pl.pallas_call(
        functools.partial(_local_kernel, seq),
        grid=(m // r,),
        in_specs=[
            pl.BlockSpec((r, 3 * W_B), lambda i: (i, c_qkv)),
            pl.BlockSpec((SUBLANES, 3 * W_B), prev_map(c_qkv)),
            pl.BlockSpec((SUBLANES, 3 * W_B), next_map(c_qkv)),
            pl.BlockSpec((r, IN_TILE_N), lambda i: (i, c_pool)),
            pl.BlockSpec((SUBLANES, IN_TILE_N), prev_map(c_pool)),
            pl.BlockSpec((SUBLANES, IN_TILE_N), next_map(c_pool)),
            pl.BlockSpec((2 * nh2, r), lambda i: (0, i)),
            pl.BlockSpec((3, 3 * W_B), lambda i: (0, 0)),
            pl.BlockSpec((len(POOL_WINDOWS), POOL_GROUP_W, POOL_GROUP_W), lambda i: (0, 0, 0)),
            pl.BlockSpec((1, W_C), lambda i: (0, 0)),
            row, col, row, col,
        ],
        out_specs=[
            pl.BlockSpec((r, W_C), lambda i: (i, 0)),
            pl.BlockSpec((2, r, W_B), lambda i: (0, i, 0)),
            pl.BlockSpec((2, cpt, N_HEADS_B, 2 * c, DV_B), lambda i: (0, i, 0, 0, 0)),
            pl.BlockSpec((2, cpt, N_HEADS_B, c + DK_B, c), lambda i: (0, i, 0, 0, 0)),
            pl.BlockSpec((cpt, nh2, LANES), lambda i: (i, 0, 0)),
        ],
        out_shape=[
            jax.ShapeDtypeStruct((m, W_C), BF16),
            jax.ShapeDtypeStruct((2, m, W_B), F32),
            jax.ShapeDtypeStruct((2, mc, N_HEADS_B, 2 * c, DV_B), BF16),
            jax.ShapeDtypeStruct((2, mc, N_HEADS_B, c + DK_B, c), BF16),
            jax.ShapeDtypeStruct((mc, nh2, LANES), F32),
        ],
        scratch_shapes=[pltpu.VMEM((r, W_B), F32)] * 3,
        compiler_params=_cparams(1),
        name="local_conv_pool",
    )(z, z, z, z, z, z, bat, conv_w_l, pool_w_l, pool_scale_l.reshape(1, W_C),
      a_log_l.reshape(1, nh2), a_log_l.reshape(nh2, 1), dt_l.reshape(1, nh2), dt_l.reshape(nh2, 1))


def _scan_kernel(par, b_lat, n_ctx_ch, *refs):
    c, nh = DELTA_CHUNK, N_HEADS_B
    n_slot = par + b_lat
    ins = refs[:n_slot * 8]
    s0_ref = refs[n_slot * 8]
    of_ctx, ob_ctx, of_lat, ob_lat, sfin_ref, s_scr = refs[n_slot * 8 + 1:]
    step = pl.program_id(0)
    cc = step % n_ctx_ch

    @pl.when(cc == 0)
    def _():
        s_scr[0:par] = jnp.zeros((par,) + s_scr.shape[1:], F32)

    @pl.when(step == 0)
    def _():
        s_scr[par:] = s0_ref[...]

    for slot0 in range(0, n_slot, SCAN_SLOTS_PER_GROUP):
        chains = []
        for slot in range(slot0, min(slot0 + SCAN_SLOTS_PER_GROUP, n_slot)):
            outs = (of_ctx, ob_ctx, slot) if slot < par else (of_lat, ob_lat, slot - par)
            for d in range(2):
                l1_ref, l2_ref, u_ref, e_ref = ins[(slot * 2 + d) * 4:(slot * 2 + d) * 4 + 4]
                for h in range(nh):
                    chains.append(dict(slot=slot, out=outs[d], row=outs[2], h=h, idx=d * nh + h,
                                       l1=l1_ref, l2=l2_ref, u=u_ref, e=e_ref))
        for ch in chains:
            ch["s"] = s_scr[ch["slot"], ch["idx"]]
            ch["r1"] = _dot(ch["l1"][ch["h"]], ch["s"].astype(BF16))
        for ch in chains:
            hs = slice(ch["h"] * DV_B, (ch["h"] + 1) * DV_B)
            v_new = ch["u"][:, hs] - ch["r1"][:c]
            ch["r2"] = _dot(ch["l2"][ch["h"]], v_new.astype(BF16))
        for ch in chains:
            hs = slice(ch["h"] * DV_B, (ch["h"] + 1) * DV_B)
            ch["out"][ch["row"], :, hs] = ch["r1"][c:] + ch["r2"][:c]
            e = ch["e"][ch["idx"]:ch["idx"] + 1, :]
            s_scr[ch["slot"], ch["idx"]] = ch["s"] * e + ch["r2"][c:]

    @pl.when(cc == n_ctx_ch - 1)
    def _():
        sfin_ref[...] = s_scr[0:par]


def _delta_scan(u, l1, l2, e, s0_lat, dims):
    b_ctx, t_ctx, b_lat, t_lat = dims
    c, nh = DELTA_CHUNK, N_HEADS_B
    n_ctx_ch, n_lat_ch = t_ctx // c, t_lat // c
    par = b_ctx * n_ctx_ch // n_lat_ch
    assert par >= 1 and par * n_lat_ch == b_ctx * n_ctx_ch and b_ctx % par == 0
    mc = l1.shape[1]
    ctx_chunks = b_ctx * n_ctx_ch
    u4 = u.reshape(2, mc, c, W_B)

    def chunk_of(slot, d):
        if slot < par:
            return lambda s: ((s // n_ctx_ch) * par + slot) * n_ctx_ch + (
                (n_ctx_ch - 1 - s % n_ctx_ch) if d else s % n_ctx_ch)
        q = slot - par
        return lambda s: ctx_chunks + q * n_lat_ch + ((n_lat_ch - 1 - s) if d else s)

    in_specs, args = [], []
    for slot in range(par + b_lat):
        for d in range(2):
            cg = chunk_of(slot, d)
            in_specs += [
                pl.BlockSpec((None, None, nh, 2 * c, DV_B), lambda s, cg=cg, d=d: (d, cg(s), 0, 0, 0)),
                pl.BlockSpec((None, None, nh, c + DK_B, c), lambda s, cg=cg, d=d: (d, cg(s), 0, 0, 0)),
                pl.BlockSpec((None, None, c, W_B), lambda s, cg=cg, d=d: (d, cg(s), 0, 0)),
                pl.BlockSpec((None, 2 * nh, LANES), lambda s, cg=cg: (cg(s), 0, 0)),
            ]
            args += [l1, l2, u4, e]
    in_specs.append(pl.BlockSpec((b_lat, 2 * nh, DK_B, DV_B), lambda s: (0, 0, 0, 0)))
    args.append(s0_lat)

    ctx_o = jax.ShapeDtypeStruct((b_ctx // par, par, n_ctx_ch, c, W_B), F32)
    lat_o = jax.ShapeDtypeStruct((b_lat, n_lat_ch, c, W_B), F32)
    cf = lambda s: s % n_ctx_ch
    of_ctx, ob_ctx, of_lat, ob_lat, s_fin = pl.pallas_call(
        functools.partial(_scan_kernel, par, b_lat, n_ctx_ch),
        grid=(n_lat_ch,),
        in_specs=in_specs,
        out_specs=[
            pl.BlockSpec((None, par, None, c, W_B), lambda s: (s // n_ctx_ch, 0, cf(s), 0, 0)),
            pl.BlockSpec((None, par, None, c, W_B), lambda s: (s // n_ctx_ch, 0, n_ctx_ch - 1 - cf(s), 0, 0)),
            pl.BlockSpec((b_lat, None, c, W_B), lambda s: (0, s, 0, 0)),
            pl.BlockSpec((b_lat, None, c, W_B), lambda s: (0, n_lat_ch - 1 - s, 0, 0)),
            pl.BlockSpec((par, 2 * nh, DK_B, DV_B), lambda s: (s // n_ctx_ch, 0, 0, 0)),
        ],
        out_shape=[ctx_o, ctx_o, lat_o, lat_o, jax.ShapeDtypeStruct((b_ctx, 2 * nh, DK_B, DV_B), F32)],
        scratch_shapes=[pltpu.VMEM((par + b_lat, 2 * nh, DK_B, DV_B), F32)],
        compiler_params=_cparams(1),
        name="delta_scan",
    )(*args)
    n_ctx, n_lat = b_ctx * t_ctx, b_lat * t_lat
    return (of_ctx.reshape(n_ctx, W_B), ob_ctx.reshape(n_ctx, W_B), of_lat.reshape(n_lat, W_B),
            ob_lat.reshape(n_lat, W_B), s_fin)


def _route_rows(sel, scores):
    g = EXPERTS_PER_GROUP
    gscore = []
    for gi in range(N_EXPERT_GROUPS):
        a, b, c, d = sel[gi * g:(gi + 1) * g]
        hi1, lo1 = jnp.maximum(a, b), jnp.minimum(a, b)
        hi2, lo2 = jnp.maximum(c, d), jnp.minimum(c, d)
        gscore.append(jnp.maximum(hi1, hi2) + jnp.maximum(jnp.minimum(hi1, hi2), jnp.maximum(lo1, lo2)))
    best = jnp.zeros_like(gscore[0], dtype=jnp.int32)
    bestv = gscore[0]
    for gi in range(1, N_EXPERT_GROUPS):
        upd = gscore[gi] > bestv
        best = jnp.where(upd, gi, best)
        bestv = jnp.where(upd, gscore[gi], bestv)

    def in_best(rows, j):
        out = rows[j]
        for gi in range(1, N_EXPERT_GROUPS):
            out = jnp.where(best == gi, rows[gi * g + j], out)
        return out

    e_sel = [in_best(sel, j) for j in range(g)]
    e_sc = [in_best(scores, j) for j in range(g)]

    def first_argmax(vals):
        bi, bv, bs = jnp.zeros_like(best), vals[0], e_sc[0]
        for j in range(1, g):
            upd = vals[j] > bv
            bi = jnp.where(upd, j, bi)
            bv = jnp.where(upd, vals[j], bv)
            bs = jnp.where(upd, e_sc[j], bs)
        return bi, bs

    i0, s0 = first_argmax(e_sel)
    i1, s1 = first_argmax([jnp.where(i0 == j, -jnp.inf, e_sel[j]) for j in range(g)])
    tot = s0 + s1
    return best * g + i0, best * g + i1, s0 / tot, s1 / tot


def _mix_kernel(n_ctx, x_ref, mod_ref, oac_ref, oal_ref, ofc_ref, obc_ref, ofl_ref, obl_ref,
                gb_ref, oc_ref, g0_ref, g1_ref, g2_ref,
                dn_ref, n2_ref, wpa_ref, wpb_ref, wpc_ref, wout_ref, wr_ref, br_ref,
                x1_ref, h2_ref, idx_ref, wt_ref):
    d = x_ref.shape[1]
    is_ctx = pl.program_id(0) * x_ref.shape[0] < n_ctx
    dn = dn_ref[...]
    o = jnp.where(is_ctx, ofc_ref[...] + obc_ref[...], ofl_ref[...] + obl_ref[...])
    o_a = jnp.where(is_ctx, oac_ref[...], oal_ref[...])
    gate = _silu(gb_ref[...])
    parts = []
    for h in range(N_HEADS_B):
        sl = slice(h * DV_B, (h + 1) * DV_B)
        parts.append((_rms(o[:, sl]) * dn * gate[:, sl]).astype(BF16))
    o_b = jnp.concatenate(parts, axis=-1)
    mixed = (_sigmoid(g0_ref[...]) * _dot(o_a, wpa_ref[...])
             + _sigmoid(g1_ref[...]) * _dot(o_b, wpb_ref[...])
             + _sigmoid(g2_ref[...]) * _dot(oc_ref[...], wpc_ref[...]))
    gate1 = mod_ref[:, 2 * d:3 * d]
    x1 = x_ref[...] + gate1 * _dot(mixed.astype(BF16), wout_ref[...])
    x1_ref[...] = x1
    shift2 = mod_ref[:, 3 * d:4 * d]
    scale2 = mod_ref[:, 4 * d:5 * d]
    h2 = _rms(x1) * n2_ref[...] * (1.0 + scale2) + shift2
    h2_ref[...] = h2
    h_hi = h2.astype(BF16)
    h_lo = (h2 - h_hi.astype(F32)).astype(BF16)
    w_r = wr_ref[...]
    w_hi = w_r.astype(BF16)
    w_lo = (w_r - w_hi.astype(F32)).astype(BF16)
    both = _dot_nt(jnp.concatenate([w_hi, w_lo], axis=0), h_hi)
    sc = _sigmoid(both[:N_EXPERTS] + both[N_EXPERTS:] + _dot_nt(w_hi, h_lo))
    sel = sc + br_ref[...]
    i0, i1, w0, w1 = _route_rows([sel[e:e + 1] for e in range(N_EXPERTS)],
                                 [sc[e:e + 1] for e in range(N_EXPERTS)])
    idx_ref[0:1, :] = i0
    idx_ref[1:2, :] = i1
    wt_ref[0:1, :] = w0
    wt_ref[1:2, :] = w1


def _mix(x, mod_l, oa, delta_o, z, oc, delta_norm_l, norm2_l, wpa, wpb, wpc, wout, layer, wr_t, br,
         cond_of_row):
    m, d = x.shape
    r = ROW_TILE
    n_ctx = cond_of_row.n_ctx
    nct = n_ctx // r
    c_gb = OFF_GATE_B // W_B
    once = dict(pipeline_mode=pl.Buffered(1))
    full = lambda shape: pl.BlockSpec(shape, lambda i: (0,) * len(shape), **once)
    of_layer = lambda shape: pl.BlockSpec((None,) + shape, lambda i: (layer,) + (0,) * len(shape), **once)
    ctx_rows = lambda w: pl.BlockSpec((r, w), lambda i: (jnp.minimum(i, nct - 1), 0))
    lat_rows = lambda w: pl.BlockSpec((r, w), lambda i: (jnp.maximum(i - nct, 0), 0))
    of_ctx, ob_ctx, of_lat, ob_lat = delta_o
    return pl.pallas_call(
        functools.partial(_mix_kernel, n_ctx),
        grid=(m // r,),
        in_specs=[
            pl.BlockSpec((r, d), lambda i: (i, 0)),
            pl.BlockSpec((None, 1, 6 * d), lambda i: (cond_of_row(i * r), 0, 0)),
            ctx_rows(W_A), lat_rows(W_A),
            ctx_rows(W_B), ctx_rows(W_B), lat_rows(W_B), lat_rows(W_B),
            pl.BlockSpec((r, W_B), lambda i: (i, c_gb)),
            pl.BlockSpec((r, W_C), lambda i: (i, 0)),
            pl.BlockSpec((r, d), lambda i: (i, 0)),
            pl.BlockSpec((r, d), lambda i: (i, 1)),
            pl.BlockSpec((r, d), lambda i: (i, 2)),
            full((1, DV_B)), full((1, d)),
            of_layer((W_A, d)), of_layer((W_B, d)), of_layer((W_C, d)), of_layer((d, d)),
            full((N_EXPERTS, d)), full((N_EXPERTS, 1)),
        ],
        out_specs=[
            pl.BlockSpec((r, d), lambda i: (i, 0)),
            pl.BlockSpec((r, d), lambda i: (i, 0)),
            pl.BlockSpec((TOP_K, r), lambda i: (0, i)),
            pl.BlockSpec((TOP_K, r), lambda i: (0, i)),
        ],
        out_shape=[
            jax.ShapeDtypeStruct((m, d), F32),
            jax.ShapeDtypeStruct((m, d), F32),
            jax.ShapeDtypeStruct((TOP_K, m), jnp.int32),
            jax.ShapeDtypeStruct((TOP_K, m), F32),
        ],
        compiler_params=_cparams(1),
        name="mix_route",
    )(x, mod_l, oa[0], oa[1], of_ctx, ob_ctx, of_lat, ob_lat, z, oc, z, z, z,
      delta_norm_l.reshape(1, DV_B), norm2_l.reshape(1, d), wpa, wpb, wpc, wout, wr_t, br)


def _dispatch_meta(idx_t, bm):
    m = idx_t.shape[1]
    nk = m * TOP_K
    flat_e = idx_t.T.reshape(nk)
    experts = jnp.arange(N_EXPERTS, dtype=jnp.int32)[None, :]
    counts = jnp.sum((flat_e[:, None] == experts).astype(jnp.int32), axis=0)
    order = jnp.argsort(flat_e, stable=True).astype(jnp.int32)
    start = jnp.cumsum(counts) - counts
    padded = (counts + bm - 1) // bm * bm
    pad_end = jnp.cumsum(padded)
    pad_start = pad_end - padded
    nb = -(-(nk + N_EXPERTS * (bm - 1)) // bm)
    p = jnp.arange(nb * bm, dtype=jnp.int32)
    e_of_p = jnp.minimum(jnp.sum((pad_end[None, :] <= p[:, None]).astype(jnp.int32), axis=1), N_EXPERTS - 1)
    pick = (e_of_p[:, None] == experts).astype(jnp.int32)
    rank = p - jnp.sum(pick * pad_start[None, :], axis=1)
    real = rank < jnp.sum(pick * counts[None, :], axis=1)
    assign = order[jnp.clip(jnp.sum(pick * start[None, :], axis=1) + rank, 0, nk - 1)]
    row_tok = jnp.where(real, assign // TOP_K, 0)
    spare = nk + ((p // bm) % 2) * bm + p % bm
    row_dst = jnp.where(real, (assign % TOP_K) * m + assign // TOP_K, spare)
    prime = nk + jnp.arange(2 * bm, dtype=jnp.int32)
    n_used = (pad_end[-1] // bm).astype(jnp.int32)
    blk_e = e_of_p[::bm]
    n_chunks = 3 * MOE_MAT_CHUNKS
    e_ids = experts[0]
    n_blk = padded // bm
    has = n_blk > 0
    later = lax.cummin(jnp.where(has, e_ids, N_EXPERTS)[::-1])[::-1]
    nxt_of_e = jnp.concatenate([later[1:], jnp.full((1,), N_EXPERTS, jnp.int32)])
    set_of_e = (jnp.cumsum(has.astype(jnp.int32)) - 1) % 2
    b = jnp.arange(nb, dtype=jnp.int32)
    of_blk = (blk_e[:, None] == experts).astype(jnp.int32)
    at_blk = lambda per_expert: jnp.sum(of_blk * per_expert[None, :], axis=1)
    k = b - at_blk(pad_start // bm)
    n_e = jnp.maximum(at_blk(n_blk), 1)
    nxt_b = at_blk(nxt_of_e)
    streams = (nxt_b < N_EXPERTS) & (b < n_used)
    c0 = jnp.where(streams, n_chunks * k // n_e, 0)
    c1 = jnp.where(streams, n_chunks * (k + 1) // n_e, 0)
    nxt_e = jnp.where(streams, nxt_b, 0)
    return (blk_e, n_used.reshape(1), row_tok, jnp.concatenate([prime, row_dst]),
            at_blk(set_of_e), nxt_e, c0, c1)


def _moe_kernel(layer, blk_e_ref, n_used_ref, row_tok_ref, row_dst_ref, set_ref, nxt_ref, c0_ref, c1_ref,
                h2_hbm, wg_hbm, wu_hbm, wd_hbm, out_hbm,
                xbuf, ybuf, wg_s, wu_s, wd_s, st_gu, st_d, gate_scr, act_scr, gsem, ssem, wsem_gu, wsem_d):
    i = pl.program_id(0)
    bm = xbuf.shape[1]
    n_used = n_used_ref[0]
    n_mat = MOE_MAT_CHUNKS
    n_stage = st_gu.shape[0]
    rg, rd = st_gu.shape[1], st_d.shape[1]

    def chunk_ops(c, j, e, s):
        cg, cu, cd = c, c - n_mat, c - 2 * n_mat
        rows_g = pl.ds(pl.multiple_of(cg * rg, rg), rg)
        rows_u = pl.ds(pl.multiple_of(cu * rg, rg), rg)
        rows_d = pl.ds(pl.multiple_of(cd * rd, 2 * SUBLANES), rd)

        def to_set(dst, rows, stage):
            def convert():
                dst[s, rows, :] = stage[j].astype(BF16)
            return convert

        return (
            (c < n_mat, lambda: pltpu.make_async_copy(wg_hbm.at[layer, e, rows_g, :], st_gu.at[j], wsem_gu.at[j]),
             to_set(wg_s, rows_g, st_gu)),
            ((c >= n_mat) & (c < 2 * n_mat),
             lambda: pltpu.make_async_copy(wu_hbm.at[layer, e, rows_u, :], st_gu.at[j], wsem_gu.at[j]),
             to_set(wu_s, rows_u, st_gu)),
            (c >= 2 * n_mat, lambda: pltpu.make_async_copy(wd_hbm.at[layer, e, rows_d, :], st_d.at[j], wsem_d.at[j]),
             to_set(wd_s, rows_d, st_d)),
        )

    def round_chunks(r, c_lo, c_hi):
        first = c_lo + r * n_stage
        return first, jnp.clip(c_hi - first, 0, n_stage)

    def issue_round(r, c_lo, c_hi, e, s):
        first, n = round_chunks(r, c_lo, c_hi)

        def body(j, carry):
            for pred, copy, _ in chunk_ops(first + j, j, e, s):
                pl.when(pred)(lambda copy=copy: copy().start(priority=1))
            return carry
        lax.fori_loop(0, n, body, 0)

    def finish_round(r, c_lo, c_hi, e, s):
        first, n = round_chunks(r, c_lo, c_hi)

        def body(j, carry):
            for pred, copy, convert in chunk_ops(first + j, j, e, s):
                @pl.when(pred)
                def _(copy=copy, convert=convert):
                    copy().wait()
                    convert()
            return carry
        lax.fori_loop(0, n, body, 0)

    def sync_rounds(r_lo, c_lo, c_hi, e, s):
        def body(r, carry):
            issue_round(r, c_lo, c_hi, e, s)
            finish_round(r, c_lo, c_hi, e, s)
            return carry
        n_rounds = (jnp.maximum(c_hi - c_lo, 0) + n_stage - 1) // n_stage
        lax.fori_loop(r_lo, jnp.maximum(n_rounds, r_lo), body, 0)

    def gather_row(blk, slot, r):
        tok = row_tok_ref[blk * bm + r]
        pltpu.make_async_copy(h2_hbm.at[pl.ds(tok, 1)], xbuf.at[slot, pl.ds(r, 1)], gsem.at[slot]).start()

    def scatter_row(tab_blk, slot, r):
        dst = row_dst_ref[tab_blk * bm + r]
        pltpu.make_async_copy(ybuf.at[slot, pl.ds(r, 1)], out_hbm.at[pl.ds(dst, 1)], ssem.at[slot]).start()

    def looped(fn, *args):
        def body(r, carry):
            fn(*args, r)
            return carry
        lax.fori_loop(0, bm, body, 0, unroll=8)

    def wait_gather(slot):
        pltpu.make_async_copy(h2_hbm.at[pl.ds(0, bm)], xbuf.at[slot], gsem.at[slot]).wait()

    def wait_scatter(slot):
        pltpu.make_async_copy(ybuf.at[slot], out_hbm.at[pl.ds(0, bm)], ssem.at[slot]).wait()

    @pl.when(i == 0)
    def _():
        ybuf[...] = jnp.zeros_like(ybuf)
        looped(gather_row, 0, 0)
        looped(scatter_row, 0, 0)
        sync_rounds(0, 0, 3 * n_mat, blk_e_ref[0], set_ref[0])

    def compute_block(slot):
        other = 1 - slot
        nxt = jnp.minimum(i + 1, n_used - 1)
        s, e_next, c_lo, c_hi = set_ref[i], nxt_ref[i], c0_ref[i], c1_ref[i]
        stream = (c_lo, c_hi, e_next, 1 - s)
        wait_gather(slot)
        issue_round(0, *stream)
        for r in range(bm):
            gather_row(nxt, other, r)
            scatter_row(i + 1, other, r)
        xb = xbuf[slot].astype(BF16)
        gate_scr[...] = _dot(xb, wg_s[s])
        finish_round(0, *stream)
        issue_round(1, *stream)
        act_scr[...] = (_silu(gate_scr[...]) * _dot(xb, wu_s[s])).astype(BF16)
        finish_round(1, *stream)
        wait_scatter(slot)
        issue_round(2, *stream)
        ybuf[slot] = _dot(act_scr[...], wd_s[s])
        finish_round(2, *stream)
        sync_rounds(3, *stream)

    for parity in range(2):
        pl.when((i < n_used) & (i % 2 == parity))(functools.partial(compute_block, parity))

    @pl.when(i == n_used)
    def _():
        looped(scatter_row, i + 1, (i - 1) % 2)
        wait_gather(i % 2)
        wait_scatter(0)
        wait_scatter(1)


def _moe(h2, meta, wg, wu, wd, layer):
    m, d = h2.shape
    bm = MOE_ROWS
    nb = meta[0].shape[0]
    f = wg.shape[3]
    rg, rd = d // MOE_MAT_CHUNKS, f // MOE_MAT_CHUNKS
    assert rg % (2 * SUBLANES) == 0 and rd % (2 * SUBLANES) == 0
    any_spec = pl.BlockSpec(memory_space=pl.ANY)
    dma = pltpu.SemaphoreType.DMA
    return pl.pallas_call(
        functools.partial(_moe_kernel, layer),
        grid_spec=pltpu.PrefetchScalarGridSpec(
            num_scalar_prefetch=len(meta),
            grid=(nb + 1,),
            in_specs=[any_spec] * 4,
            out_specs=any_spec,
            scratch_shapes=[
                pltpu.VMEM((2, bm, d), F32), pltpu.VMEM((2, bm, d), F32),
                pltpu.VMEM((2, d, f), BF16), pltpu.VMEM((2, d, f), BF16), pltpu.VMEM((2, f, d), BF16),
                pltpu.VMEM((MOE_STAGE_SLOTS, rg, f), F32), pltpu.VMEM((MOE_STAGE_SLOTS, rd, d), F32),
                pltpu.VMEM((bm, f), F32), pltpu.VMEM((bm, f), BF16),
                dma((2,)), dma((2,)), dma((MOE_STAGE_SLOTS,)), dma((MOE_STAGE_SLOTS,)),
            ],
        ),
        out_shape=jax.ShapeDtypeStruct((TOP_K * m + 2 * bm, d), F32),
        compiler_params=_cparams(1, MOE_VMEM_LIMIT),
        name="moe_experts",
    )(*meta, h2, wg, wu, wd)


def _combine_kernel(n_ctx, final, y0_ref, y1_ref, x1_ref, mod_ref, w_ref, g_ref, modn_ref, o_ref, o2_ref):
    d = x1_ref.shape[1]
    f = w_ref[:, 0:1] * y0_ref[...] + w_ref[:, 1:2] * y1_ref[...]
    x2 = x1_ref[...] + mod_ref[:, 5 * d:6 * d] * f
    if final:
        y = _rms(x2) * g_ref[...]
        is_ctx = pl.program_id(0) * x1_ref.shape[0] < n_ctx

        @pl.when(is_ctx)
        def _():
            o_ref[...] = y

        @pl.when(jnp.logical_not(is_ctx))
        def _():
            o2_ref[...] = y
    else:
        o_ref[...] = x2
        o2_ref[...] = _modulated_norm(x2, g_ref[...], modn_ref[...], d).astype(BF16)


def _combine(y2, x1, mod_l, w_col, gain, mod_next, final, cond_of_row):
    m, d = x1.shape
    r = ROW_TILE
    k1 = m // r
    n_ctx = cond_of_row.n_ctx
    nct = n_ctx // r
    cond_row = pl.BlockSpec((None, 1, 6 * d), lambda i: (cond_of_row(i * r), 0, 0))
    tile = pl.BlockSpec((r, d), lambda i: (i, 0))
    if final:
        out_specs = [pl.BlockSpec((r, d), lambda i: (jnp.minimum(i, nct - 1), 0)),
                     pl.BlockSpec((r, d), lambda i: (jnp.maximum(i - nct, 0), 0))]
        out_shape = [jax.ShapeDtypeStruct((n_ctx, d), F32), jax.ShapeDtypeStruct((m - n_ctx, d), F32)]
    else:
        out_specs = [tile, tile]
        out_shape = [jax.ShapeDtypeStruct((m, d), F32), jax.ShapeDtypeStruct((m, d), BF16)]
    return pl.pallas_call(
        functools.partial(_combine_kernel, n_ctx, final),
        grid=(m // r,),
        in_specs=[
            tile,
            pl.BlockSpec((r, d), lambda i: (k1 + i, 0)),
            tile, cond_row,
            pl.BlockSpec((r, TOP_K), lambda i: (i, 0)),
            pl.BlockSpec((1, d), lambda i: (0, 0)),
            cond_row,
        ],
        out_specs=out_specs,
        out_shape=out_shape,
        compiler_params=_cparams(1),
        name="moe_combine",
    )(y2, y2, x1, mod_l, w_col, gain.reshape(1, d), mod_next)


class _CondOfRow:
    def __init__(self, n_ctx, t_lat):
        self.n_ctx, self.t_lat = n_ctx, t_lat
        self.tile_in = _pick_tile((1024, 512, 256), n_ctx, t_lat)

    def __call__(self, row0):
        return jnp.where(row0 < self.n_ctx, 0, 1 + (row0 - self.n_ctx) // self.t_lat)


def kernel(x_prompt, x_sample, cache_k, cache_v, state_delta, c, c_ctx, w_mod, b_mod, norm1, norm2, w_in,
           lambda_qk, subln, conv_w, a_log, dt_bias, delta_norm, pool_w, pool_scale, w_pa, w_pb, w_pc,
           w_out, w_router, b_router, w_gate, w_up, w_down, norm_final):
    b_ctx, t_ctx, d = x_prompt.shape
    b_lat, t_lat, _ = x_sample.shape
    depth = w_in.shape[0]
    past = cache_k.shape[2]
    n_ctx, n_lat = b_ctx * t_ctx, b_lat * t_lat
    m = n_ctx + n_lat
    assert d == D_MODEL and w_in.shape[2] == D_IN and 1 + b_lat <= SUBLANES
    assert t_ctx % ROW_TILE == 0 and t_lat % ROW_TILE == 0 and n_ctx % t_lat == 0
    cond_of_row = _CondOfRow(n_ctx, t_lat)
    dims = (b_ctx, t_ctx, b_lat, t_lat)

    def seq(row0):
        is_ctx = row0 < n_ctx
        return (jnp.where(is_ctx, row0 % t_ctx, (row0 - n_ctx) % t_lat), jnp.where(is_ctx, t_ctx, t_lat))

    x = jnp.concatenate([x_prompt.reshape(n_ctx, d), x_sample.reshape(n_lat, d)], axis=0)
    cond8 = jnp.zeros((SUBLANES, d), F32).at[0].set(c_ctx).at[1:1 + b_lat].set(c)
    mod = _modulation(cond8, w_mod, b_mod).reshape(depth, SUBLANES, 1, 6 * d)
    rope_tabs = _rope_tables(t_lat)
    wr_t = w_router.T
    br = b_router.reshape(N_EXPERTS, 1)
    ck = cache_k.reshape(b_lat, depth, past, QK_A)
    cv = cache_v.reshape(b_lat, depth, past, W_A)
    s_lat = state_delta.reshape(b_lat, depth, 2 * N_HEADS_B, DK_B, DV_B)

    new_k, new_v, new_s = [], [], []
    h = _norm_mod(x, mod[0], norm1[0], cond_of_row)
    w_in_t = jnp.swapaxes(w_in, 1, 2)
    w_proj = [w.astype(BF16) for w in (w_pa, w_pb, w_pc, w_out)]
    for l in range(depth):
        lam_init = 0.8 - 0.6 * math.exp(-0.3 * l)
        last = l == depth - 1
        z = _in_proj(h, w_in_t, l, cond_of_row.tile_in)
        oa, k_l, v_l = _attention(z, lambda_qk[l], subln[l], ck[:, l], cv[:, l], rope_tabs, lam_init, dims)
        new_k.append(k_l.reshape(b_ctx, t_ctx, N_HEADS_A, 2 * DK_A))
        new_v.append(v_l.reshape(b_ctx, t_ctx, N_HEADS_A, DV_A))
        bat = z[:, OFF_BA:OFF_BA + 4 * N_HEADS_B].T
        oc, u, l1, l2, e = _local(z, bat, conv_w[l], pool_w[l], pool_scale[l], a_log[l], dt_bias[l], seq)
        *delta_o, s_ctx = _delta_scan(u, l1, l2, e, s_lat[:, l], dims)
        new_s.append(s_ctx.reshape(b_ctx, 2, N_HEADS_B, DK_B, DV_B))

        x1, h2, idx_t, wt_t = _mix(x, mod[l], oa, delta_o, z, oc, delta_norm[l], norm2[l], *w_proj, l,
                                   wr_t, br, cond_of_row)
        y2 = _moe(h2, _dispatch_meta(idx_t, MOE_ROWS), w_gate, w_up, w_down, l)
        if last:
            y_ctx, y_lat = _combine(y2, x1, mod[l], wt_t.T, norm_final, mod[l], True, cond_of_row)
        else:
            x, h = _combine(y2, x1, mod[l], wt_t.T, norm1[l + 1], mod[l + 1], False, cond_of_row)

    y_prompt = y_ctx.reshape(b_ctx, t_ctx, d)
    y_sample = y_lat.reshape(b_lat, t_lat, d)
    return (y_prompt, y_sample, jnp.stack(new_k, axis=1), jnp.stack(new_v, axis=1), jnp.stack(new_s, axis=1))
```
